```python
import math
import jax, jax.numpy as jnp
from jax import lax
import numpy as np

D_MODEL = 2048
BATCH = 32
SEQ = 256
DEPTH = 1
DEC_BATCH = 4
DEC_SEQ = 1024
PAST_LEN = 512

GRID_W = 64
Q_BLOCK = 128
ROPE_THETA = 10000.0
EPS = 1e-6
H_A = 8
DK_A = 64
DV_A = 2 * DK_A
H_B = 8
KV_B = 2
DH_B = 128
G_B = H_B // KV_B
A_Q = H_A * 2 * DK_A
A_K = H_A * 2 * DK_A
A_V = H_A * DV_A
A_G = H_A * DV_A
B_Q = H_B * DH_B
B_K = KV_B * DH_B
B_V = KV_B * DH_B
B_G = H_B * DH_B
D_IN = A_Q + A_K + A_V + A_G + B_Q + B_K + B_V + B_G
D_MIX = A_V + B_Q
SPLIT_POINTS = [int(s) for s in np.cumsum([A_Q, A_K, A_V, A_G, B_Q, B_K, B_V])]

kernel_name = "hymba_diff_gqa_diffusion_step"


def rms_norm(x, g):
    xf = x.astype(jnp.float32)
    y = xf * lax.rsqrt(jnp.mean(xf * xf, axis=-1, keepdims=True) + EPS)
    return (y * g.astype(jnp.float32)).astype(x.dtype)


def adaln(cond, w_ada_l, b_ada_l):
    mod = jax.nn.silu(cond) @ w_ada_l + b_ada_l
    return jnp.split(mod, 3, axis=-1)


def axial_rope(n_tok, rot_dim):
    n_rows = n_tok // GRID_W
    rows = jnp.repeat(jnp.arange(n_rows, dtype=jnp.float32), GRID_W)
    cols = jnp.tile(jnp.arange(GRID_W, dtype=jnp.float32), n_rows)
    quarter = rot_dim // 4
    inv_freq = 1.0 / (ROPE_THETA ** (jnp.arange(quarter, dtype=jnp.float32) / quarter))
    ang = jnp.concatenate([rows[:, None] * inv_freq, cols[:, None] * inv_freq], axis=-1)
    return jnp.cos(ang), jnp.sin(ang)


def apply_rope(x, cos, sin):
    shape = (1, cos.shape[0]) + (1,) * (x.ndim - 3) + (cos.shape[1],)
    cos = cos.reshape(shape)
    sin = sin.reshape(shape)
    xf = x.astype(jnp.float32)
    x1, x2 = jnp.split(xf, 2, axis=-1)
    out = jnp.concatenate([x1 * cos - x2 * sin, x1 * sin + x2 * cos], axis=-1)
    return out.astype(x.dtype)


def sweep_query_blocks(attend, q):
    b, lq = q.shape[:2]
    nb = lq // Q_BLOCK
    qb = jnp.moveaxis(q.reshape((b, nb, Q_BLOCK) + q.shape[2:]), 1, 0)
    out = lax.map(attend, qb)
    out = jnp.moveaxis(out, 0, 1)
    return out.reshape((b, lq) + out.shape[3:])


def diff_attention(q, k, v, lam):
    scale = 1.0 / math.sqrt(DK_A)

    def attend(qb):
        s = jnp.einsum('bqhcd,bshcd->bhcqs', qb, k).astype(jnp.float32) * scale
        p = jax.nn.softmax(s, axis=-1)
        p = p[:, :, 0] - lam[None, :, None, None] * p[:, :, 1]
        return jnp.einsum('bhqs,bshe->bqhe', p.astype(v.dtype), v)

    return sweep_query_blocks(attend, q)


def gqa_attention(q, k, v):
    b, lq = q.shape[:2]
    q = q.reshape(b, lq, KV_B, G_B, DH_B)
    scale = 1.0 / math.sqrt(DH_B)

    def attend(qb):
        s = jnp.einsum('bqkgd,bskd->bkgqs', qb, k).astype(jnp.float32) * scale
        p = jax.nn.softmax(s, axis=-1)
        return jnp.einsum('bkgqs,bskd->bqkgd', p.astype(v.dtype), v)

    out = sweep_query_blocks(attend, q)
    return out.reshape(b, lq, H_B, DH_B)


def mixer_inputs(h, w_in_l, q_norm_l, k_norm_l):
    b, n = h.shape[:2]
    z = h @ w_in_l
    qa, ka, va, ga, qb, kb, vb, gb = jnp.split(z, SPLIT_POINTS, axis=-1)
    qa = qa.reshape(b, n, H_A, 2, DK_A)
    ka = ka.reshape(b, n, H_A, 2, DK_A)
    va = va.reshape(b, n, H_A, DV_A)
    qb = rms_norm(qb.reshape(b, n, H_B, DH_B), q_norm_l)
    kb = rms_norm(kb.reshape(b, n, KV_B, DH_B), k_norm_l)
    vb = vb.reshape(b, n, KV_B, DH_B)
    return qa, ka, va, ga, qb, kb, vb, gb


def mixer_outputs(oa, ob, ga, gb, subln_l, lam_init, w_out_l):
    b, n = oa.shape[:2]
    oa = rms_norm(oa, subln_l) * (1.0 - lam_init)
    o = jnp.concatenate([oa.reshape(b, n, A_V) * jax.nn.silu(ga),
                         ob.reshape(b, n, B_Q) * jax.nn.silu(gb)], axis=-1)
    return o @ w_out_l


def diff_lambda(lq1, lk1, lq2, lk2, lam_init):
    f = jnp.float32
    return (jnp.exp(jnp.sum(lq1.astype(f) * lk1.astype(f), axis=-1))
            - jnp.exp(jnp.sum(lq2.astype(f) * lk2.astype(f), axis=-1)) + lam_init)


def setup_inputs(seed: int = 0) -> dict:
    key = jax.random.key(seed)
    ks = jax.random.split(key, 24)
    f = jnp.float32
    nrm = lambda k, s: jax.random.normal(k, s, f)
    d = D_MODEL
    return {
        "x_prompt": nrm(ks[0], (BATCH, SEQ, d)),
        "x_sample": nrm(ks[1], (DEC_BATCH, DEC_SEQ, d)),
        "cache_diff_k": nrm(ks[2], (DEC_BATCH, DEPTH, PAST_LEN, H_A, 2 * DK_A)),
        "cache_diff_v": nrm(ks[3], (DEC_BATCH, DEPTH, PAST_LEN, H_A, DV_A)),
        "cache_gqa_k": nrm(ks[4], (DEC_BATCH, DEPTH, PAST_LEN, KV_B, DH_B)),
        "cache_gqa_v": nrm(ks[5], (DEC_BATCH, DEPTH, PAST_LEN, KV_B, DH_B)),
        "c": nrm(ks[6], (DEC_BATCH, d)),
        "c_ctx": nrm(ks[7], (d,)),
        "w_ada": nrm(ks[8], (DEPTH, d, 3 * d)) * (0.5 * d ** -0.5),
        "b_ada": nrm(ks[9], (DEPTH, 3 * d)) * 0.02,
        "g_norm": 1.0 + 0.05 * nrm(ks[10], (DEPTH, d)),
        "w_in": nrm(ks[11], (DEPTH, d, D_IN)) * d ** -0.5,
        "lam_q1": nrm(ks[12], (DEPTH, H_A, DK_A)) * 0.1,
        "lam_k1": nrm(ks[13], (DEPTH, H_A, DK_A)) * 0.1,
        "lam_q2": nrm(ks[14], (DEPTH, H_A, DK_A)) * 0.1,
        "lam_k2": nrm(ks[15], (DEPTH, H_A, DK_A)) * 0.1,
        "subln_g": 1.0 + 0.05 * nrm(ks[16], (DEPTH, DV_A)),
        "q_norm_g": 1.0 + 0.05 * nrm(ks[17], (DEPTH, DH_B)),
        "k_norm_g": 1.0 + 0.05 * nrm(ks[18], (DEPTH, DH_B)),
        "w_out": nrm(ks[19], (DEPTH, D_MIX, d)) * D_MIX ** -0.5,
        "g_final": 1.0 + 0.05 * nrm(ks[20], (d,)),
    }


def reference(x_prompt, x_sample, cache_diff_k, cache_diff_v, cache_gqa_k, cache_gqa_v,
              c, c_ctx, w_ada, b_ada, g_norm, w_in, lam_q1, lam_k1, lam_q2, lam_k2,
              subln_g, q_norm_g, k_norm_g, w_out, g_final):
    xp = x_prompt
    xs = x_sample
    bp, n_ctx = xp.shape[:2]
    bs, n_lat = xs.shape[:2]
    n_past = cache_diff_k.shape[2]
    cos_a, sin_a = axial_rope(n_lat, DK_A)
    cos_b, sin_b = axial_rope(n_lat, DH_B)
    st_dk, st_dv, st_gk, st_gv = [], [], [], []
    for l in range(DEPTH):
        lam_init = 0.8 - 0.6 * math.exp(-0.3 * l)
        lam = diff_lambda(lam_q1[l], lam_k1[l], lam_q2[l], lam_k2[l], lam_init)

        shift, scale, gate = adaln(c_ctx[None, None, :], w_ada[l], b_ada[l])
        h = rms_norm(xp, g_norm[l]) * (1.0 + scale) + shift
        qa, ka, va, ga, qb, kb, vb, gb = mixer_inputs(h, w_in[l], q_norm_g[l], k_norm_g[l])
        oa = diff_attention(qa, ka, va, lam)
        ob = gqa_attention(qb, kb, vb)
        xp = xp + gate * mixer_outputs(oa, ob, ga, gb, subln_g[l], lam_init, w_out[l])
        st_dk.append(ka.reshape(bp, n_ctx, H_A, 2 * DK_A))
        st_dv.append(va)
        st_gk.append(kb)
        st_gv.append(vb)

        shift, scale, gate = adaln(c[:, None, :], w_ada[l], b_ada[l])
        h = rms_norm(xs, g_norm[l]) * (1.0 + scale) + shift
        qa, ka, va, ga, qb, kb, vb, gb = mixer_inputs(h, w_in[l], q_norm_g[l], k_norm_g[l])
        qa = apply_rope(qa, cos_a, sin_a)
        ka = apply_rope(ka, cos_a, sin_a)
        qb = apply_rope(qb, cos_b, sin_b)
        kb = apply_rope(kb, cos_b, sin_b)
        ka_all = jnp.concatenate([ka, cache_diff_k[:, l].reshape(bs, n_past, H_A, 2, DK_A)], axis=1)
        va_all = jnp.concatenate([va, cache_diff_v[:, l]], axis=1)
        kb_all = jnp.concatenate([kb, cache_gqa_k[:, l]], axis=1)
        vb_all = jnp.concatenate([vb, cache_gqa_v[:, l]], axis=1)
        oa = diff_attention(qa, ka_all, va_all, lam)
        ob = gqa_attention(qb, kb_all, vb_all)
        xs = xs + gate * mixer_outputs(oa, ob, ga, gb, subln_g[l], lam_init, w_out[l])

    y_prompt = rms_norm(xp, g_final)
    y_sample = rms_norm(xs, g_final)
    new_diff_k = jnp.stack(st_dk, axis=1)
    new_diff_v = jnp.stack(st_dv, axis=1)
    new_gqa_k = jnp.stack(st_gk, axis=1)
    new_gqa_v = jnp.stack(st_gv, axis=1)
    return (y_prompt, y_sample, new_diff_k, new_diff_v, new_gqa_k, new_gqa_v)
```

```python
import functools
import math

import jax
import jax.numpy as jnp
import numpy as np
from jax import lax
from jax.experimental import pallas as pl
from jax.experimental.pallas import tpu as pltpu

D_MODEL = 2048
GRID_W = 64
ROPE_THETA = 10000.0
EPS = 1e-6
H_A = 8
DK_A = 64
DV_A = 2 * DK_A
H_B = 8
KV_B = 2
DH_B = 128
G_B = H_B // KV_B
HEAD_W = 128
A_Q = H_A * 2 * DK_A
A_K = H_A * 2 * DK_A
A_V = H_A * DV_A
A_G = H_A * DV_A
B_Q = H_B * DH_B
B_K = KV_B * DH_B
B_V = KV_B * DH_B
B_G = H_B * DH_B
D_IN = A_Q + A_K + A_V + A_G + B_Q + B_K + B_V + B_G
D_MIX = A_V + B_Q
OFF_AQ = 0
OFF_AK = OFF_AQ + A_Q
OFF_AV = OFF_AK + A_K
OFF_AG = OFF_AV + A_V
OFF_BQ = OFF_AG + A_G
OFF_BK = OFF_BQ + B_Q
OFF_BV = OFF_BK + B_K
OFF_BG = OFF_BV + B_V

N_COND = 8
ADA_TN = 512
INPROJ_TM = 256
INPROJ_CHUNK = 256
OUTPROJ_TM = 512
ATTN_TQ = 256
VMEM_LIMIT = 56 * 1024 * 1024

BF16 = jnp.bfloat16
F32 = jnp.float32


def _silu(x):
    return x * jax.nn.sigmoid(x)


def _rms(x):
    return x * lax.rsqrt(jnp.mean(x * x, axis=-1, keepdims=True) + EPS)


def _adaln_body(cond_ref, w_ref, b_ref, o_ref):
    a = _silu(cond_ref[...]).astype(BF16)
    w = w_ref[...].astype(BF16)
    o_ref[...] = jnp.dot(a, w, preferred_element_type=F32) + b_ref[...]


def _adaln(cond, w_ada, b_ada):
    d3 = w_ada.shape[1]
    return pl.pallas_call(
        _adaln_body,
        out_shape=jax.ShapeDtypeStruct((N_COND, d3), F32),
        grid=(d3 // ADA_TN,),
        in_specs=[
            pl.BlockSpec((N_COND, D_MODEL), lambda j: (0, 0)),
            pl.BlockSpec((D_MODEL, ADA_TN), lambda j: (0, j)),
            pl.BlockSpec((1, ADA_TN), lambda j: (0, j)),
        ],
        out_specs=pl.BlockSpec((N_COND, ADA_TN), lambda j: (0, j)),
        compiler_params=pltpu.CompilerParams(
            dimension_semantics=("arbitrary",), vmem_limit_bytes=VMEM_LIMIT),
        name="adaln",
    )(cond, w_ada, b_ada)


def _rope_a(x, c, s_up, s_dn):
    return x * c + pltpu.roll(x, 96, 1) * s_up + pltpu.roll(x, 32, 1) * s_dn


def _rope_b(x, c, s):
    return x * c + pltpu.roll(x, 64, 1) * s


def _inproj_body(rope, x_ref, mod_ref, gn_ref, w_ref, qn_ref, kn_ref, *rest):
    if rope:
        ca_ref, sau_ref, sad_ref, cb_ref, sb_ref = rest[:5]
        rest = rest[5:]
    qa_o, ka_o, va_o, ga_o, qb_o, kb_o, vb_o, gb_o = rest

    x = x_ref[...]
    shift = mod_ref[:, 0:D_MODEL]
    scale = mod_ref[:, D_MODEL:2 * D_MODEL]
    h = ((_rms(x) * gn_ref[...]) * (1.0 + scale) + shift).astype(BF16)

    if rope:
        ca, sau, sad = ca_ref[...], sau_ref[...], sad_ref[...]
        cb, sb = cb_ref[...], sb_ref[...]

    def rope_a(t):
        return _rope_a(t, ca, sau, sad) if rope else t

    def rope_b(t):
        return _rope_b(t, cb, sb) if rope else t

    regions = (
        (OFF_AQ, A_Q, qa_o, rope_a),
        (OFF_AK, A_K, ka_o, rope_a),
        (OFF_AV, A_V, va_o, None),
        (OFF_AG, A_G, ga_o, None),
        (OFF_BQ, B_Q, qb_o, lambda t: rope_b(_rms(t) * qn_ref[...])),
        (OFF_BK, B_K, kb_o, lambda t: rope_b(_rms(t) * kn_ref[...])),
        (OFF_BV, B_V, vb_o, None),
        (OFF_BG, B_G, gb_o, None),
    )
    for start, width, o_ref, epi in regions:
        for c0 in range(0, width, INPROJ_CHUNK):
            z = jnp.dot(h, w_ref[:, start + c0:start + c0 + INPROJ_CHUNK],
                        preferred_element_type=F32)
            if epi is None:
                o_ref[:, c0:c0 + INPROJ_CHUNK] = z.astype(o_ref.dtype)
            else:
                for h0 in range(0, INPROJ_CHUNK, HEAD_W):
                    o_ref[:, c0 + h0:c0 + h0 + HEAD_W] = epi(z[:, h0:h0 + HEAD_W]).astype(o_ref.dtype)


def _inproj(x2d, mod3, row_of_tile, g_norm, w_in_bf16, q_norm_g, k_norm_g, rope_tabs, kv_dtype):
    t = x2d.shape[0]
    tm = INPROJ_TM
    rope = rope_tabs is not None
    row_spec = lambda w: pl.BlockSpec((tm, w), lambda i: (i, 0))
    const = lambda shape: pl.BlockSpec(shape, lambda i: (0,) * len(shape))
    in_specs = [
        row_spec(D_MODEL),
        pl.BlockSpec((None, 1, 3 * D_MODEL), lambda i: (row_of_tile(i), 0, 0)),
        const((1, D_MODEL)),
        pl.BlockSpec((D_MODEL, D_IN), lambda i: (0, 0), pipeline_mode=pl.Buffered(1)),
        const((1, HEAD_W)),
        const((1, HEAD_W)),
    ]
    args = [x2d, mod3, g_norm, w_in_bf16, q_norm_g, k_norm_g]
    if rope:
        n_pos_tiles = rope_tabs[0].shape[0] // tm
        in_specs += [pl.BlockSpec((tm, HEAD_W), lambda i: (i % n_pos_tiles, 0))] * 5
        args += list(rope_tabs)
    widths = (A_Q, A_K, A_V, A_G, B_Q, B_K, B_V, B_G)
    dtypes = (BF16, kv_dtype, kv_dtype, BF16, BF16, kv_dtype, kv_dtype, BF16)
    return pl.pallas_call(
        functools.partial(_inproj_body, rope),
        out_shape=[jax.ShapeDtypeStruct((t, w), dt) for w, dt in zip(widths, dtypes)],
        grid=(t // tm,),
        in_specs=in_specs,
        out_specs=[row_spec(w) for w in widths],
        compiler_params=pltpu.CompilerParams(
            dimension_semantics=("arbitrary",), vmem_limit_bytes=VMEM_LIMIT),
        name="inproj_rope" if rope else "inproj",
    )(*args)


def _diff_lambda_col(lq1_ref, lk1_ref, lq2_ref, lk2_ref, lam_init):
    s1 = jnp.sum(lq1_ref[...] * lk1_ref[...], axis=-1, keepdims=True)
    s2 = jnp.sum(lq2_ref[...] * lk2_ref[...], axis=-1, keepdims=True)
    return jnp.exp(s1) - jnp.exp(s2) + lam_init


def _softmax_parts(s):
    m = jnp.max(s, axis=-1, keepdims=True)
    e = jnp.exp(s - m)
    return e, jnp.sum(e, axis=-1, keepdims=True)


def _qk(q, k):
    return lax.dot_general(q, k, (((1,), (1,)), ((), ())), preferred_element_type=F32)


def _attend(lam_init, lam_col, subln, qa_ref, ga_ref, qb_ref, gb_ref,
            ka, va, kb, vb, o_ref):
    tq = qa_ref.shape[0]
    lane = lax.broadcasted_iota(jnp.int32, (tq, HEAD_W), 1)
    first = lane < DK_A
    scale_a = 1.0 / math.sqrt(DK_A)
    scale_b = 1.0 / math.sqrt(DH_B)

    for hd in range(H_A):
        cols = slice(hd * HEAD_W, (hd + 1) * HEAD_W)
        q = qa_ref[:, cols]
        k = ka(hd)
        zero = jnp.zeros_like(q)
        e1, l1 = _softmax_parts(_qk(jnp.where(first, q, zero), k) * scale_a)
        e2, l2 = _softmax_parts(_qk(jnp.where(first, zero, q), k) * scale_a)
        lam = lam_col[hd:hd + 1, :]
        p = e1 * (1.0 / l1) - e2 * (lam / l2)
        o = jnp.dot(p.astype(BF16), va(hd), preferred_element_type=F32)
        o = (_rms(o) * subln) * (1.0 - lam_init)
        o_ref[:, cols] = (o * _silu(ga_ref[:, cols].astype(F32))).astype(o_ref.dtype)

    for kv in range(KV_B):
        k = kb(kv)
        v = vb(kv)
        q = jnp.concatenate(
            [qb_ref[:, (kv * G_B + g) * HEAD_W:(kv * G_B + g + 1) * HEAD_W] for g in range(G_B)],
            axis=0)
        e, l = _softmax_parts(_qk(q, k) * scale_b)
        p = e * (1.0 / l)
        o = jnp.dot(p.astype(BF16), v, preferred_element_type=F32)
        for g in range(G_B):
            hd = kv * G_B + g
            cols = slice(hd * HEAD_W, (hd + 1) * HEAD_W)
            gate = _silu(gb_ref[:, cols].astype(F32))
            o_ref[:, A_V + hd * HEAD_W:A_V + (hd + 1) * HEAD_W] = (
                o[g * tq:(g + 1) * tq, :] * gate).astype(o_ref.dtype)


def _attn_ctx_body(lam_init, lq1, lk1, lq2, lk2, subln_ref,
                   qa_ref, ka_ref, va_ref, ga_ref, qb_ref, kb_ref, vb_ref, gb_ref, o_ref):
    lam_col = _diff_lambda_col(lq1, lk1, lq2, lk2, lam_init)
    head = lambda ref: (lambda i: ref[:, i * HEAD_W:(i + 1) * HEAD_W].astype(BF16))
    _attend(lam_init, lam_col, subln_ref[...], qa_ref, ga_ref, qb_ref, gb_ref,
            head(ka_ref), head(va_ref), head(kb_ref), head(vb_ref), o_ref)


def _attn_ctx(lam_init, lam_params, subln_g, qa, ka, va, ga, qb, kb, vb, gb, seq):
    t = qa.shape[0]
    const = lambda shape: pl.BlockSpec(shape, lambda b: (0,) * len(shape))
    row_spec = lambda w: pl.BlockSpec((seq, w), lambda b: (b, 0))
    return pl.pallas_call(
        functools.partial(_attn_ctx_body, lam_init),
        out_shape=jax.ShapeDtypeStruct((t, D_MIX), BF16),
        grid=(t // seq,),
        in_specs=[const((H_A, DK_A))] * 4 + [const((1, DV_A))] + [
            row_spec(A_Q), row_spec(A_K), row_spec(A_V), row_spec(A_G),
            row_spec(B_Q), row_spec(B_K), row_spec(B_V), row_spec(B_G)],
        out_specs=row_spec(D_MIX),
        compiler_params=pltpu.CompilerParams(
            dimension_semantics=("arbitrary",), vmem_limit_bytes=VMEM_LIMIT),
        name="attn_ctx",
    )(*lam_params, subln_g, qa, ka, va, ga, qb, kb, vb, gb)


def _attn_lat_body(lam_init, n_lat, lq1, lk1, lq2, lk2, subln_ref,
                   qa_ref, ka_ref, va_ref, cka_ref, cva_ref, ga_ref,
                   qb_ref, kb_ref, vb_ref, ckb_ref, cvb_ref, gb_ref, o_ref,
                   ka_all, va_all, kb_all, vb_all):
    @pl.when(pl.program_id(1) == 0)
    def _():
        for new_ref, cache_ref, all_ref in ((ka_ref, cka_ref, ka_all), (va_ref, cva_ref, va_all),
                                            (kb_ref, ckb_ref, kb_all), (vb_ref, cvb_ref, vb_all)):
            all_ref[0:n_lat, :] = new_ref[...]
            all_ref[n_lat:, :] = cache_ref[...].astype(BF16)

    lam_col = _diff_lambda_col(lq1, lk1, lq2, lk2, lam_init)
    head = lambda ref: (lambda i: ref[:, i * HEAD_W:(i + 1) * HEAD_W])
    _attend(lam_init, lam_col, subln_ref[...], qa_ref, ga_ref, qb_ref, gb_ref,
            head(ka_all), head(va_all), head(kb_all), head(vb_all), o_ref)


def _attn_lat(lam_init, lam_params, subln_g, qa, ka, va, cka, cva, ga, qb, kb, vb, ckb, cvb, gb,
              n_lat, n_past):
    t = qa.shape[0]
    tq = ATTN_TQ
    nq = n_lat // tq
    const = lambda shape: pl.BlockSpec(shape, lambda b, i: (0,) * len(shape))
    q_spec = lambda w: pl.BlockSpec((tq, w), lambda b, i: (b * nq + i, 0))
    new_spec = lambda w: pl.BlockSpec((n_lat, w), lambda b, i: (b, 0))
    cache_spec = lambda w: pl.BlockSpec((n_past, w), lambda b, i: (b, 0))
    n_all = n_lat + n_past
    return pl.pallas_call(
        functools.partial(_attn_lat_body, lam_init, n_lat),
        out_shape=jax.ShapeDtypeStruct((t, D_MIX), BF16),
        grid=(t // n_lat, nq),
        in_specs=[const((H_A, DK_A))] * 4 + [const((1, DV_A))] + [
            q_spec(A_Q), new_spec(A_K), new_spec(A_V), cache_spec(A_K), cache_spec(A_V), q_spec(A_G),
            q_spec(B_Q), new_spec(B_K), new_spec(B_V), cache_spec(B_K), cache_spec(B_V), q_spec(B_G)],
        out_specs=q_spec(D_MIX),
        scratch_shapes=[pltpu.VMEM((n_all, A_K), BF16), pltpu.VMEM((n_all, A_V), BF16),
                        pltpu.VMEM((n_all, B_K), BF16), pltpu.VMEM((n_all, B_V), BF16)],
        compiler_params=pltpu.CompilerParams(
            dimension_semantics=("arbitrary", "arbitrary"), vmem_limit_bytes=VMEM_LIMIT),
        name="attn_lat",
    )(*lam_params, subln_g, qa, ka, va, cka, cva, ga, qb, kb, vb, ckb, cvb, gb)


def _outproj_body(o_ref, w_ref, x_ref, mod_ref, gf_ref, y_ref):
    m = jnp.dot(o_ref[...], w_ref[...], preferred_element_type=F32)
    gate = mod_ref[:, 2 * D_MODEL:3 * D_MODEL]
    y_ref[...] = _rms(x_ref[...] + gate * m) * gf_ref[...]


def _outproj(o, w_out_bf16, x2d, mod3, row_of_tile, g_final):
    t = x2d.shape[0]
    tm = OUTPROJ_TM
    return pl.pallas_call(
        _outproj_body,
        out_shape=jax.ShapeDtypeStruct((t, D_MODEL), F32),
        grid=(t // tm,),
        in_specs=[
            pl.BlockSpec((tm, D_MIX), lambda i: (i, 0)),
            pl.BlockSpec((D_MIX, D_MODEL), lambda i: (0, 0), pipeline_mode=pl.Buffered(1)),
            pl.BlockSpec((tm, D_MODEL), lambda i: (i, 0)),
            pl.BlockSpec((None, 1, 3 * D_MODEL), lambda i: (row_of_tile(i), 0, 0)),
            pl.BlockSpec((1, D_MODEL), lambda i: (0, 0)),
        ],
        out_specs=pl.BlockSpec((tm, D_MODEL), lambda i: (i, 0)),
        compiler_params=pltpu.CompilerParams(
            dimension_semantics=("arbitrary",), vmem_limit_bytes=VMEM_LIMIT),
        name="outproj",
    )(o, w_out_bf16, x2d, mod3, g_final)


def _rope_tables(n_tok):
    n_rows = n_tok // GRID_W
    rows = np.repeat(np.arange(n_rows, dtype=np.float32), GRID_W)
    cols = np.tile(np.arange(GRID_W, dtype=np.float32), n_rows)

    def cos_sin(rot_dim):
        quarter = rot_dim // 4
        inv_freq = (1.0 / (np.float32(ROPE_THETA) ** (np.arange(quarter, dtype=np.float32) / quarter))
                    ).astype(np.float32)
        ang = np.concatenate([rows[:, None] * inv_freq, cols[:, None] * inv_freq], axis=-1)
        return np.cos(ang).astype(np.float32), np.sin(ang).astype(np.float32)

    ca, sa = cos_sin(DK_A)
    za = np.zeros_like(sa)
    cb, sb = cos_sin(DH_B)
    tabs = (
        np.concatenate([ca, ca, ca, ca], axis=-1),
        np.concatenate([-sa, za, -sa, za], axis=-1),
        np.concatenate([za, sa, za, sa], axis=-1),
        np.concatenate([cb, cb], axis=-1),
        np.concatenate([-sb, sb], axis=-1),
    )
    return tuple(jnp.asarray(t) for t in tabs)


def kernel(x_prompt, x_sample, cache_diff_k, cache_diff_v, cache_gqa_k, cache_gqa_v, c, c_ctx,
           w_ada, b_ada, g_norm, w_in, lam_q1, lam_k1, lam_q2, lam_k2, subln_g, q_norm_g, k_norm_g,
           w_out, g_final):
    bp, n_ctx, d = x_prompt.shape
    bs, n_lat, _ = x_sample.shape
    depth = w_in.shape[0]
    n_past = cache_diff_k.shape[2]
    assert depth == 1 and d == D_MODEL and bs + 1 <= N_COND
    assert n_lat % INPROJ_TM == 0 and n_lat % OUTPROJ_TM == 0 and n_lat % ATTN_TQ == 0
    l = 0
    lam_init = 0.8 - 0.6 * math.exp(-0.3 * l)

    cond = jnp.concatenate(
        [c_ctx[None, :], c, jnp.zeros((N_COND - 1 - bs, d), F32)], axis=0)
    mod = _adaln(cond, w_ada[l], b_ada[l][None, :])
    mod3 = mod[:, None, :]

    w_in_b = w_in[l].astype(BF16)
    w_out_b = w_out[l].astype(BF16)
    gn = g_norm[l][None, :]
    qn = q_norm_g[l][None, :]
    kn = k_norm_g[l][None, :]
    lam_params = (lam_q1[l], lam_k1[l], lam_q2[l], lam_k2[l])
    subln = subln_g[l][None, :]
    gf = g_final[None, :]

    xp2 = x_prompt.reshape(bp * n_ctx, d)
    qa, ka, va, ga, qb, kb, vb, gb = _inproj(
        xp2, mod3, lambda i: 0, gn, w_in_b, qn, kn, None, F32)
    o = _attn_ctx(lam_init, lam_params, subln, qa, ka, va, ga, qb, kb, vb, gb, n_ctx)
    y_prompt = _outproj(o, w_out_b, xp2, mod3, lambda i: 0, gf).reshape(bp, n_ctx, d)
    new_diff_k = ka.reshape(bp, 1, n_ctx, H_A, 2 * DK_A)
    new_diff_v = va.reshape(bp, 1, n_ctx, H_A, DV_A)
    new_gqa_k = kb.reshape(bp, 1, n_ctx, KV_B, DH_B)
    new_gqa_v = vb.reshape(bp, 1, n_ctx, KV_B, DH_B)

    xs2 = x_sample.reshape(bs * n_lat, d)
    in_tiles = n_lat // INPROJ_TM
    qa, ka, va, ga, qb, kb, vb, gb = _inproj(
        xs2, mod3, lambda i: 1 + i // in_tiles, gn, w_in_b, qn, kn, _rope_tables(n_lat), BF16)
    cka = cache_diff_k[:, l].reshape(bs * n_past, A_K)
    cva = cache_diff_v[:, l].reshape(bs * n_past, A_V)
    ckb = cache_gqa_k[:, l].reshape(bs * n_past, B_K)
    cvb = cache_gqa_v[:, l].reshape(bs * n_past, B_V)
    o = _attn_lat(lam_init, lam_params, subln, qa, ka, va, cka, cva, ga, qb, kb, vb, ckb, cvb, gb,
                  n_lat, n_past)
    out_tiles = n_lat // OUTPROJ_TM
    y_sample = _outproj(o, w_out_b, xs2, mod3, lambda i: 1 + i // out_tiles, gf).reshape(bs, n_lat, d)

    return (y_prompt, y_sample, new_diff_k, new_diff_v, new_gqa_k, new_gqa_v)
```

```python
import functools
import math

import jax
import jax.numpy as jnp
import numpy as np
from jax import lax
from jax.experimental import pallas as pl
from jax.experimental.pallas import tpu as pltpu

D_MODEL = 2048
GRID_W = 64
ROPE_THETA = 10000.0
EPS = 1e-6
H_A = 8
DK_A = 64
DV_A = 2 * DK_A
H_B = 8
KV_B = 2
DH_B = 128
G_B = H_B // KV_B
HEAD_W = 128
A_Q = H_A * 2 * DK_A
A_K = H_A * 2 * DK_A
A_V = H_A * DV_A
A_G = H_A * DV_A
B_Q = H_B * DH_B
B_K = KV_B * DH_B
B_V = KV_B * DH_B
B_G = H_B * DH_B
D_IN = A_Q + A_K + A_V + A_G + B_Q + B_K + B_V + B_G
D_MIX = A_V + B_Q
OFF_AQ = 0
OFF_AK = OFF_AQ + A_Q
OFF_AV = OFF_AK + A_K
OFF_AG = OFF_AV + A_V
OFF_BQ = OFF_AG + A_G
OFF_BK = OFF_BQ + B_Q
OFF_BV = OFF_BK + B_K
OFF_BG = OFF_BV + B_V

N_COND = 8
ADA_TN = 512
INPROJ_TM = 256
INPROJ_CHUNK = 256
OUTPROJ_TM = 512
ATTN_TQ = 128
GQA_STACK = 2
VMEM_LIMIT = 56 * 1024 * 1024

BF16 = jnp.bfloat16
F32 = jnp.float32


def _silu(x):
    return x * jax.nn.sigmoid(x)


def _rms(x):
    return x * lax.rsqrt(jnp.mean(x * x, axis=-1, keepdims=True) + EPS)


def _adaln_body(cond_ref, w_ref, b_ref, o_ref):
    a = _silu(cond_ref[...]).astype(BF16)
    w = w_ref[...].astype(BF16)
    o_ref[...] = jnp.dot(a, w, preferred_element_type=F32) + b_ref[...]


def _adaln(cond, w_ada, b_ada):
    d3 = w_ada.shape[1]
    return pl.pallas_call(
        _adaln_body,
        out_shape=jax.ShapeDtypeStruct((N_COND, d3), F32),
        grid=(d3 // ADA_TN,),
        in_specs=[
            pl.BlockSpec((N_COND, D_MODEL), lambda j: (0, 0)),
            pl.BlockSpec((D_MODEL, ADA_TN), lambda j: (0, j)),
            pl.BlockSpec((1, ADA_TN), lambda j: (0, j)),
        ],
        out_specs=pl.BlockSpec((N_COND, ADA_TN), lambda j: (0, j)),
        compiler_params=pltpu.CompilerParams(
            dimension_semantics=("arbitrary",), vmem_limit_bytes=VMEM_LIMIT),
        name="adaln",
    )(cond, w_ada, b_ada)


def _rope_a(x, c, s_up, s_dn):
    return x * c + pltpu.roll(x, 96, 1) * s_up + pltpu.roll(x, 32, 1) * s_dn


def _rope_b(x, c, s):
    return x * c + pltpu.roll(x, 64, 1) * s


def _inproj_body(rope, x_ref, mod_ref, gn_ref, w_ref, qn_ref, kn_ref, *rest):
    if rope:
        ca_ref, sau_ref, sad_ref, cb_ref, sb_ref = rest[:5]
        rest = rest[5:]
    qa_o, ka_o, va_o, ga_o, qb_o, kb_o, vb_o, gb_o = rest
    cache_layout = not rope
    tm = x_ref.shape[0]

    x = x_ref[...]
    shift = mod_ref[:, 0:D_MODEL]
    scale = mod_ref[:, D_MODEL:2 * D_MODEL]
    h = ((_rms(x) * gn_ref[...]) * (1.0 + scale) + shift).astype(BF16)

    if rope:
        ca, sau, sad = ca_ref[...], sau_ref[...], sad_ref[...]
        cb, sb = cb_ref[...], sb_ref[...]

    def rope_a(t):
        return _rope_a(t, ca, sau, sad) if rope else t

    def rope_b(t):
        return _rope_b(t, cb, sb) if rope else t

    ident = lambda t: t
    regions = (
        (OFF_AQ, A_Q, qa_o, rope_a, False),
        (OFF_AK, A_K, ka_o, rope_a, True),
        (OFF_AV, A_V, va_o, ident, True),
        (OFF_AG, A_G, ga_o, None, False),
        (OFF_BQ, B_Q, qb_o, lambda t: rope_b(_rms(t) * qn_ref[...]), False),
        (OFF_BK, B_K, kb_o, lambda t: rope_b(_rms(t) * kn_ref[...]), True),
        (OFF_BV, B_V, vb_o, ident, True),
        (OFF_BG, B_G, gb_o, None, False),
    )
    for start, width, o_ref, epi, is_kv in regions:
        n_heads = width // HEAD_W
        for c0 in range(0, width, INPROJ_CHUNK):
            z = jnp.dot(h, w_ref[:, start + c0:start + c0 + INPROJ_CHUNK],
                        preferred_element_type=F32)
            if epi is None:
                o_ref[:, c0:c0 + INPROJ_CHUNK] = z.astype(o_ref.dtype)
                continue
            for h0 in range(0, INPROJ_CHUNK, HEAD_W):
                hd = (c0 + h0) // HEAD_W
                t = epi(z[:, h0:h0 + HEAD_W]).astype(o_ref.dtype)
                if is_kv and cache_layout:
                    o_ref[pl.ds(hd, tm, stride=n_heads), :] = t
                else:
                    o_ref[:, hd * HEAD_W:(hd + 1) * HEAD_W] = t


def _inproj(x2d, mod3, row_of_tile, g_norm, w_in_bf16, q_norm_g, k_norm_g, rope_tabs):
    t = x2d.shape[0]
    tm = INPROJ_TM
    rope = rope_tabs is not None
    row_spec = lambda w: pl.BlockSpec((tm, w), lambda i: (i, 0))
    const = lambda shape: pl.BlockSpec(shape, lambda i: (0,) * len(shape))
    in_specs = [
        row_spec(D_MODEL),
        pl.BlockSpec((None, 1, 3 * D_MODEL), lambda i: (row_of_tile(i), 0, 0)),
        const((1, D_MODEL)),
        pl.BlockSpec((D_MODEL, D_IN), lambda i: (0, 0), pipeline_mode=pl.Buffered(1)),
        const((1, HEAD_W)),
        const((1, HEAD_W)),
    ]
    args = [x2d, mod3, g_norm, w_in_bf16, q_norm_g, k_norm_g]
    if rope:
        n_pos_tiles = rope_tabs[0].shape[0] // tm
        in_specs += [pl.BlockSpec((tm, HEAD_W), lambda i: (i % n_pos_tiles, 0))] * 5
        args += list(rope_tabs)
    widths = (A_Q, A_K, A_V, A_G, B_Q, B_K, B_V, B_G)
    is_kv = (False, True, True, False, False, True, True, False)
    out_shape, out_specs = [], []
    for w, kv in zip(widths, is_kv):
        if kv and not rope:
            n_heads = w // HEAD_W
            out_shape.append(jax.ShapeDtypeStruct((t * n_heads, HEAD_W), F32))
            out_specs.append(pl.BlockSpec((tm * n_heads, HEAD_W), lambda i: (i, 0)))
        else:
            out_shape.append(jax.ShapeDtypeStruct((t, w), BF16))
            out_specs.append(row_spec(w))
    return pl.pallas_call(
        functools.partial(_inproj_body, rope),
        out_shape=out_shape,
        grid=(t // tm,),
        in_specs=in_specs,
        out_specs=out_specs,
        compiler_params=pltpu.CompilerParams(
            dimension_semantics=("arbitrary",), vmem_limit_bytes=VMEM_LIMIT),
        name="inproj_rope" if rope else "inproj",
    )(*args)


def _diff_lambda_col(lq1_ref, lk1_ref, lq2_ref, lk2_ref, lam_init):
    s1 = jnp.sum(lq1_ref[...] * lk1_ref[...], axis=-1, keepdims=True)
    s2 = jnp.sum(lq2_ref[...] * lk2_ref[...], axis=-1, keepdims=True)
    return jnp.exp(s1) - jnp.exp(s2) + lam_init


def _cache_head(ref, head, n_tok, n_heads):
    return ref[pl.ds(head, n_tok, stride=n_heads), :].astype(BF16)


def _softmax_pv(q, k, v, scale):
    s = lax.dot_general(q, k, (((1,), (1,)), ((), ())), preferred_element_type=F32)
    m = jnp.max(s, axis=-1, keepdims=True)
    e = jnp.exp2((s - m) * (scale * math.log2(math.e)))
    l = jnp.sum(e, axis=-1, keepdims=True)
    return jnp.dot(e.astype(BF16), v, preferred_element_type=F32), l


def _attend(lam_init, lam_col, subln, qa_ref, ga_ref, qb_ref, gb_ref,
            ka, va, kb, vb, o_ref):
    tq = qa_ref.shape[0]
    lane = lax.broadcasted_iota(jnp.int32, (tq, HEAD_W), 1)
    first = lane < DK_A

    for hd in range(H_A):
        cols = slice(hd * HEAD_W, (hd + 1) * HEAD_W)
        q = qa_ref[:, cols]
        zero = jnp.zeros_like(q)
        q2 = jnp.concatenate([jnp.where(first, q, zero), jnp.where(first, zero, q)], axis=0)
        o2, l2 = _softmax_pv(q2, ka(hd), va(hd), 1.0 / math.sqrt(DK_A))
        r2 = 1.0 / l2
        lam = lam_col[hd:hd + 1, :]
        o = o2[:tq] * r2[:tq] - o2[tq:] * (lam * r2[tq:])
        o = (_rms(o) * subln) * (1.0 - lam_init)
        o_ref[:, cols] = (o * _silu(ga_ref[:, cols].astype(F32))).astype(o_ref.dtype)

    for h0 in range(0, H_B, GQA_STACK):
        kv = h0 // G_B
        q = jnp.concatenate(
            [qb_ref[:, (h0 + g) * HEAD_W:(h0 + g + 1) * HEAD_W] for g in range(GQA_STACK)], axis=0)
        o, l = _softmax_pv(q, kb(kv), vb(kv), 1.0 / math.sqrt(DH_B))
        o = o * (1.0 / l)
        for g in range(GQA_STACK):
            cols = slice((h0 + g) * HEAD_W, (h0 + g + 1) * HEAD_W)
            gate = _silu(gb_ref[:, cols].astype(F32))
            o_ref[:, A_V + (h0 + g) * HEAD_W:A_V + (h0 + g + 1) * HEAD_W] = (
                o[g * tq:(g + 1) * tq, :] * gate).astype(o_ref.dtype)


def _attn_ctx_body(lam_init, lq1, lk1, lq2, lk2, subln_ref,
                   qa_ref, ka_ref, va_ref, ga_ref, qb_ref, kb_ref, vb_ref, gb_ref, o_ref):
    lam_col = _diff_lambda_col(lq1, lk1, lq2, lk2, lam_init)
    seq = qa_ref.shape[0]
    head = lambda ref, n_heads: (lambda i: _cache_head(ref, i, seq, n_heads))
    _attend(lam_init, lam_col, subln_ref[...], qa_ref, ga_ref, qb_ref, gb_ref,
            head(ka_ref, H_A), head(va_ref, H_A), head(kb_ref, KV_B), head(vb_ref, KV_B), o_ref)


def _attn_ctx(lam_init, lam_params, subln_g, qa, ka, va, ga, qb, kb, vb, gb, seq):
    t = qa.shape[0]
    const = lambda shape: pl.BlockSpec(shape, lambda b: (0,) * len(shape))
    row_spec = lambda w: pl.BlockSpec((seq, w), lambda b: (b, 0))
    cache_spec = lambda n_heads: pl.BlockSpec((seq * n_heads, HEAD_W), lambda b: (b, 0))
    return pl.pallas_call(
        functools.partial(_attn_ctx_body, lam_init),
        out_shape=jax.ShapeDtypeStruct((t, D_MIX), BF16),
        grid=(t // seq,),
        in_specs=[const((H_A, DK_A))] * 4 + [const((1, DV_A))] + [
            row_spec(A_Q), cache_spec(H_A), cache_spec(H_A), row_spec(A_G),
            row_spec(B_Q), cache_spec(KV_B), cache_spec(KV_B), row_spec(B_G)],
        out_specs=row_spec(D_MIX),
        compiler_params=pltpu.CompilerParams(
            dimension_semantics=("arbitrary",), vmem_limit_bytes=VMEM_LIMIT),
        name="attn_ctx",
    )(*lam_params, subln_g, qa, ka, va, ga, qb, kb, vb, gb)


def _attn_lat_body(lam_init, n_lat, lq1, lk1, lq2, lk2, subln_ref,
                   qa_ref, ka_ref, va_ref, cka_ref, cva_ref, ga_ref,
                   qb_ref, kb_ref, vb_ref, ckb_ref, cvb_ref, gb_ref, o_ref,
                   ka_all, va_all, kb_all, vb_all):
    @pl.when(pl.program_id(1) == 0)
    def _():
        for new_ref, cache_ref, all_ref in ((ka_ref, cka_ref, ka_all), (va_ref, cva_ref, va_all),
                                            (kb_ref, ckb_ref, kb_all), (vb_ref, cvb_ref, vb_all)):
            n_heads = new_ref.shape[1] // HEAD_W
            n_past = cache_ref.shape[0] // n_heads
            all_ref[0:n_lat, :] = new_ref[...]
            for hd in range(n_heads):
                all_ref[n_lat:, hd * HEAD_W:(hd + 1) * HEAD_W] = _cache_head(cache_ref, hd, n_past, n_heads)

    lam_col = _diff_lambda_col(lq1, lk1, lq2, lk2, lam_init)
    head = lambda ref: (lambda i: ref[:, i * HEAD_W:(i + 1) * HEAD_W])
    _attend(lam_init, lam_col, subln_ref[...], qa_ref, ga_ref, qb_ref, gb_ref,
            head(ka_all), head(va_all), head(kb_all), head(vb_all), o_ref)


def _attn_lat(lam_init, lam_params, subln_g, qa, ka, va, cka, cva, ga, qb, kb, vb, ckb, cvb, gb,
              n_lat, n_past):
    t = qa.shape[0]
    tq = ATTN_TQ
    nq = n_lat // tq
    const = lambda shape: pl.BlockSpec(shape, lambda b, i: (0,) * len(shape))
    q_spec = lambda w: pl.BlockSpec((tq, w), lambda b, i: (b * nq + i, 0))
    new_spec = lambda w: pl.BlockSpec((n_lat, w), lambda b, i: (b, 0))
    cache_spec = lambda w: pl.BlockSpec((n_past * (w // HEAD_W), HEAD_W), lambda b, i: (b, 0))
    n_all = n_lat + n_past
    return pl.pallas_call(
        functools.partial(_attn_lat_body, lam_init, n_lat),
        out_shape=jax.ShapeDtypeStruct((t, D_MIX), BF16),
        grid=(t // n_lat, nq),
        in_specs=[const((H_A, DK_A))] * 4 + [const((1, DV_A))] + [
            q_spec(A_Q), new_spec(A_K), new_spec(A_V), cache_spec(A_K), cache_spec(A_V), q_spec(A_G),
            q_spec(B_Q), new_spec(B_K), new_spec(B_V), cache_spec(B_K), cache_spec(B_V), q_spec(B_G)],
        out_specs=q_spec(D_MIX),
        scratch_shapes=[pltpu.VMEM((n_all, A_K), BF16), pltpu.VMEM((n_all, A_V), BF16),
                        pltpu.VMEM((n_all, B_K), BF16), pltpu.VMEM((n_all, B_V), BF16)],
        compiler_params=pltpu.CompilerParams(
            dimension_semantics=("arbitrary", "arbitrary"), vmem_limit_bytes=VMEM_LIMIT),
        name="attn_lat",
    )(*lam_params, subln_g, qa, ka, va, cka, cva, ga, qb, kb, vb, ckb, cvb, gb)


def _outproj_body(o_ref, w_ref, x_ref, mod_ref, gf_ref, y_ref):
    m = jnp.dot(o_ref[...], w_ref[...], preferred_element_type=F32)
    gate = mod_ref[:, 2 * D_MODEL:3 * D_MODEL]
    y_ref[...] = _rms(x_ref[...] + gate * m) * gf_ref[...]


def _outproj(o, w_out_bf16, x2d, mod3, row_of_tile, g_final):
    t = x2d.shape[0]
    tm = OUTPROJ_TM
    return pl.pallas_call(
        _outproj_body,
        out_shape=jax.ShapeDtypeStruct((t, D_MODEL), F32),
        grid=(t // tm,),
        in_specs=[
            pl.BlockSpec((tm, D_MIX), lambda i: (i, 0)),
            pl.BlockSpec((D_MIX, D_MODEL), lambda i: (0, 0), pipeline_mode=pl.Buffered(1)),
            pl.BlockSpec((tm, D_MODEL), lambda i: (i, 0)),
            pl.BlockSpec((None, 1, 3 * D_MODEL), lambda i: (row_of_tile(i), 0, 0)),
            pl.BlockSpec((1, D_MODEL), lambda i: (0, 0)),
        ],
        out_specs=pl.BlockSpec((tm, D_MODEL), lambda i: (i, 0)),
        compiler_params=pltpu.CompilerParams(
            dimension_semantics=("arbitrary",), vmem_limit_bytes=VMEM_LIMIT),
        name="outproj",
    )(o, w_out_bf16, x2d, mod3, g_final)


def _rope_tables(n_tok):
    n_rows = n_tok // GRID_W
    rows = np.repeat(np.arange(n_rows, dtype=np.float32), GRID_W)
    cols = np.tile(np.arange(GRID_W, dtype=np.float32), n_rows)

    def cos_sin(rot_dim):
        quarter = rot_dim // 4
        inv_freq = (1.0 / (np.float32(ROPE_THETA) ** (np.arange(quarter, dtype=np.float32) / quarter))
                    ).astype(np.float32)
        ang = np.concatenate([rows[:, None] * inv_freq, cols[:, None] * inv_freq], axis=-1)
        return np.cos(ang).astype(np.float32), np.sin(ang).astype(np.float32)

    ca, sa = cos_sin(DK_A)
    za = np.zeros_like(sa)
    cb, sb = cos_sin(DH_B)
    tabs = (
        np.concatenate([ca, ca, ca, ca], axis=-1),
        np.concatenate([-sa, za, -sa, za], axis=-1),
        np.concatenate([za, sa, za, sa], axis=-1),
        np.concatenate([cb, cb], axis=-1),
        np.concatenate([-sb, sb], axis=-1),
    )
    return tuple(jnp.asarray(t) for t in tabs)


def kernel(x_prompt, x_sample, cache_diff_k, cache_diff_v, cache_gqa_k, cache_gqa_v, c, c_ctx,
           w_ada, b_ada, g_norm, w_in, lam_q1, lam_k1, lam_q2, lam_k2, subln_g, q_norm_g, k_norm_g,
           w_out, g_final):
    bp, n_ctx, d = x_prompt.shape
    bs, n_lat, _ = x_sample.shape
    depth = w_in.shape[0]
    n_past = cache_diff_k.shape[2]
    assert depth == 1 and d == D_MODEL and bs + 1 <= N_COND
    assert n_lat % INPROJ_TM == 0 and n_lat % OUTPROJ_TM == 0 and n_lat % ATTN_TQ == 0
    l = 0
    lam_init = 0.8 - 0.6 * math.exp(-0.3 * l)

    cond = jnp.concatenate(
        [c_ctx[None, :], c, jnp.zeros((N_COND - 1 - bs, d), F32)], axis=0)
    mod = _adaln(cond, w_ada[l], b_ada[l][None, :])
    mod3 = mod[:, None, :]

    w_in_b = w_in[l].astype(BF16)
    w_out_b = w_out[l].astype(BF16)
    gn = g_norm[l][None, :]
    qn = q_norm_g[l][None, :]
    kn = k_norm_g[l][None, :]
    lam_params = (lam_q1[l], lam_k1[l], lam_q2[l], lam_k2[l])
    subln = subln_g[l][None, :]
    gf = g_final[None, :]

    xp2 = x_prompt.reshape(bp * n_ctx, d)
    qa, ka, va, ga, qb, kb, vb, gb = _inproj(
        xp2, mod3, lambda i: 0, gn, w_in_b, qn, kn, None)
    o = _attn_ctx(lam_init, lam_params, subln, qa, ka, va, ga, qb, kb, vb, gb, n_ctx)
    y_prompt = _outproj(o, w_out_b, xp2, mod3, lambda i: 0, gf).reshape(bp, n_ctx, d)
    new_diff_k = ka.reshape(bp, 1, n_ctx, H_A, 2 * DK_A)
    new_diff_v = va.reshape(bp, 1, n_ctx, H_A, DV_A)
    new_gqa_k = kb.reshape(bp, 1, n_ctx, KV_B, DH_B)
    new_gqa_v = vb.reshape(bp, 1, n_ctx, KV_B, DH_B)

    xs2 = x_sample.reshape(bs * n_lat, d)
    in_tiles = n_lat // INPROJ_TM
    qa, ka, va, ga, qb, kb, vb, gb = _inproj(
        xs2, mod3, lambda i: 1 + i // in_tiles, gn, w_in_b, qn, kn, _rope_tables(n_lat))
    cka = cache_diff_k[:, l].reshape(bs * n_past * H_A, HEAD_W)
    cva = cache_diff_v[:, l].reshape(bs * n_past * H_A, HEAD_W)
    ckb = cache_gqa_k[:, l].reshape(bs * n_past * KV_B, HEAD_W)
    cvb = cache_gqa_v[:, l].reshape(bs * n_past * KV_B, HEAD_W)
    o = _attn_lat(lam_init, lam_params, subln, qa, ka, va, cka, cva, ga, qb, kb, vb, ckb, cvb, gb,
                  n_lat, n_past)
    out_tiles = n_lat // OUTPROJ_TM
    y_sample = _outproj(o, w_out_b, xs2, mod3, lambda i: 1 + i // out_tiles, gf).reshape(bs, n_lat, d)

    return (y_prompt, y_sample, new_diff_k, new_diff_v, new_gqa_k, new_gqa_v)
```

```python
import functools
import math

import jax
import jax.numpy as jnp
import numpy as np
from jax import lax
from jax.experimental import pallas as pl
from jax.experimental.pallas import tpu as pltpu

D_MODEL = 2048
GRID_W = 64
ROPE_THETA = 10000.0
EPS = 1e-6
H_A = 8
DK_A = 64
DV_A = 2 * DK_A
H_B = 8
KV_B = 2
DH_B = 128
G_B = H_B // KV_B
HEAD_W = 128
A_Q = H_A * 2 * DK_A
A_K = H_A * 2 * DK_A
A_V = H_A * DV_A
A_G = H_A * DV_A
B_Q = H_B * DH_B
B_K = KV_B * DH_B
B_V = KV_B * DH_B
B_G = H_B * DH_B
D_IN = A_Q + A_K + A_V + A_G + B_Q + B_K + B_V + B_G
D_MIX = A_V + B_Q
OFF_AQ = 0
OFF_AK = OFF_AQ + A_Q
OFF_AV = OFF_AK + A_K
OFF_AG = OFF_AV + A_V
OFF_BQ = OFF_AG + A_G
OFF_BK = OFF_BQ + B_Q
OFF_BV = OFF_BK + B_K
OFF_BG = OFF_BV + B_V

N_COND = 8
ADA_TN = 512
INPROJ_TM = 256
INPROJ_CHUNK = 256
OUTPROJ_TM = 512
ATTN_TQ = 128
GQA_STACK = 2
VMEM_LIMIT = 56 * 1024 * 1024

BF16 = jnp.bfloat16
F32 = jnp.float32


def _silu(x):
    return x * jax.nn.sigmoid(x)


def _rms(x):
    return x * lax.rsqrt(jnp.mean(x * x, axis=-1, keepdims=True) + EPS)


def _adaln_body(cond_ref, w_ref, b_ref, o_ref):
    a = _silu(cond_ref[...]).astype(BF16)
    w = w_ref[...].astype(BF16)
    o_ref[...] = jnp.dot(a, w, preferred_element_type=F32) + b_ref[...]


def _adaln(cond, w_ada, b_ada):
    d3 = w_ada.shape[1]
    return pl.pallas_call(
        _adaln_body,
        out_shape=jax.ShapeDtypeStruct((N_COND, d3), F32),
        grid=(d3 // ADA_TN,),
        in_specs=[
            pl.BlockSpec((N_COND, D_MODEL), lambda j: (0, 0)),
            pl.BlockSpec((D_MODEL, ADA_TN), lambda j: (0, j)),
            pl.BlockSpec((1, ADA_TN), lambda j: (0, j)),
        ],
        out_specs=pl.BlockSpec((N_COND, ADA_TN), lambda j: (0, j)),
        compiler_params=pltpu.CompilerParams(
            dimension_semantics=("arbitrary",), vmem_limit_bytes=VMEM_LIMIT),
        name="adaln",
    )(cond, w_ada, b_ada)


def _rope_a(x, c, s_up, s_dn):
    return x * c + pltpu.roll(x, 96, 1) * s_up + pltpu.roll(x, 32, 1) * s_dn


def _rope_b(x, c, s):
    return x * c + pltpu.roll(x, 64, 1) * s


def _inproj_body(rope, x_ref, mod_ref, gn_ref, w_ref, qn_ref, kn_ref, *rest):
    if rope:
        ca_ref, sau_ref, sad_ref, cb_ref, sb_ref = rest[:5]
        rest = rest[5:]
    qa_o, ka_o, va_o, ga_o, qb_o, kb_o, vb_o, gb_o = rest
    cache_layout = not rope
    tm = x_ref.shape[0]

    x = x_ref[...]
    shift = mod_ref[:, 0:D_MODEL]
    scale = mod_ref[:, D_MODEL:2 * D_MODEL]
    h = ((_rms(x) * gn_ref[...]) * (1.0 + scale) + shift).astype(BF16)

    if rope:
        ca, sau, sad = ca_ref[...], sau_ref[...], sad_ref[...]
        cb, sb = cb_ref[...], sb_ref[...]

    def rope_a(t):
        return _rope_a(t, ca, sau, sad) if rope else t

    def rope_b(t):
        return _rope_b(t, cb, sb) if rope else t

    ident = lambda t: t
    regions = (
        (OFF_AQ, A_Q, qa_o, rope_a, False),
        (OFF_AK, A_K, ka_o, rope_a, True),
        (OFF_AV, A_V, va_o, ident, True),
        (OFF_AG, A_G, ga_o, _silu, False),
        (OFF_BQ, B_Q, qb_o, lambda t: rope_b(_rms(t) * qn_ref[...]), False),
        (OFF_BK, B_K, kb_o, lambda t: rope_b(_rms(t) * kn_ref[...]), True),
        (OFF_BV, B_V, vb_o, ident, True),
        (OFF_BG, B_G, gb_o, _silu, False),
    )
    for start, width, o_ref, epi, is_kv in regions:
        n_heads = width // HEAD_W
        for c0 in range(0, width, INPROJ_CHUNK):
            z = jnp.dot(h, w_ref[:, start + c0:start + c0 + INPROJ_CHUNK],
                        preferred_element_type=F32)
            if epi is None:
                o_ref[:, c0:c0 + INPROJ_CHUNK] = z.astype(o_ref.dtype)
                continue
            for h0 in range(0, INPROJ_CHUNK, HEAD_W):
                hd = (c0 + h0) // HEAD_W
                t = epi(z[:, h0:h0 + HEAD_W]).astype(o_ref.dtype)
                if is_kv and cache_layout:
                    o_ref[pl.ds(hd, tm, stride=n_heads), :] = t
                else:
                    o_ref[:, hd * HEAD_W:(hd + 1) * HEAD_W] = t


def _inproj(x2d, mod3, row_of_tile, g_norm, w_in_bf16, q_norm_g, k_norm_g, rope_tabs):
    t = x2d.shape[0]
    tm = INPROJ_TM
    rope = rope_tabs is not None
    row_spec = lambda w: pl.BlockSpec((tm, w), lambda i: (i, 0))
    const = lambda shape: pl.BlockSpec(shape, lambda i: (0,) * len(shape))
    in_specs = [
        row_spec(D_MODEL),
        pl.BlockSpec((None, 1, 3 * D_MODEL), lambda i: (row_of_tile(i), 0, 0)),
        const((1, D_MODEL)),
        pl.BlockSpec((D_MODEL, D_IN), lambda i: (0, 0), pipeline_mode=pl.Buffered(1)),
        const((1, HEAD_W)),
        const((1, HEAD_W)),
    ]
    args = [x2d, mod3, g_norm, w_in_bf16, q_norm_g, k_norm_g]
    if rope:
        n_pos_tiles = rope_tabs[0].shape[0] // tm
        in_specs += [pl.BlockSpec((tm, HEAD_W), lambda i: (i % n_pos_tiles, 0))] * 5
        args += list(rope_tabs)
    widths = (A_Q, A_K, A_V, A_G, B_Q, B_K, B_V, B_G)
    is_kv = (False, True, True, False, False, True, True, False)
    out_shape, out_specs = [], []
    for w, kv in zip(widths, is_kv):
        if kv and not rope:
            n_heads = w // HEAD_W
            out_shape.append(jax.ShapeDtypeStruct((t * n_heads, HEAD_W), F32))
            out_specs.append(pl.BlockSpec((tm * n_heads, HEAD_W), lambda i: (i, 0)))
        else:
            out_shape.append(jax.ShapeDtypeStruct((t, w), BF16))
            out_specs.append(row_spec(w))
    return pl.pallas_call(
        functools.partial(_inproj_body, rope),
        out_shape=out_shape,
        grid=(t // tm,),
        in_specs=in_specs,
        out_specs=out_specs,
        compiler_params=pltpu.CompilerParams(
            dimension_semantics=("arbitrary",), vmem_limit_bytes=VMEM_LIMIT),
        name="inproj_rope" if rope else "inproj",
    )(*args)


def _diff_lambda_col(lq1_ref, lk1_ref, lq2_ref, lk2_ref, lam_init):
    s1 = jnp.sum(lq1_ref[...] * lk1_ref[...], axis=-1, keepdims=True)
    s2 = jnp.sum(lq2_ref[...] * lk2_ref[...], axis=-1, keepdims=True)
    return jnp.exp(s1) - jnp.exp(s2) + lam_init


def _cache_head(ref, head, n_tok, n_heads):
    return ref[pl.ds(head, n_tok, stride=n_heads), :].astype(BF16)


def _softmax_pv(q, k, v, scale):
    s = lax.dot_general(q, k, (((1,), (1,)), ((), ())), preferred_element_type=F32)
    m = jnp.max(s, axis=-1, keepdims=True)
    e = jnp.exp2((s - m) * (scale * math.log2(math.e)))
    v1 = jnp.concatenate([v, jnp.ones_like(v)], axis=1)
    ol = jnp.dot(e.astype(BF16), v1, preferred_element_type=F32)
    return ol[:, :HEAD_W], ol[:, HEAD_W:]


def _attend(lam_col, subln, qa_ref, ga_ref, qb_ref, gb_ref, ka, va, kb, vb, o_ref):
    tq = qa_ref.shape[0]
    lane = lax.broadcasted_iota(jnp.int32, (tq, HEAD_W), 1)
    first = lane < DK_A

    for hd in range(H_A):
        cols = slice(hd * HEAD_W, (hd + 1) * HEAD_W)
        q = qa_ref[:, cols]
        zero = jnp.zeros_like(q)
        q2 = jnp.concatenate([jnp.where(first, q, zero), jnp.where(first, zero, q)], axis=0)
        o2, l2 = _softmax_pv(q2, ka(hd), va(hd), 1.0 / math.sqrt(DK_A))
        r2 = 1.0 / l2
        lam = lam_col[hd:hd + 1, :]
        o = o2[:tq] * r2[:tq] - o2[tq:] * (lam * r2[tq:])
        o_ref[:, cols] = (_rms(o) * subln * ga_ref[:, cols].astype(F32)).astype(o_ref.dtype)

    for h0 in range(0, H_B, GQA_STACK):
        kv = h0 // G_B
        q = jnp.concatenate(
            [qb_ref[:, (h0 + g) * HEAD_W:(h0 + g + 1) * HEAD_W] for g in range(GQA_STACK)], axis=0)
        o, l = _softmax_pv(q, kb(kv), vb(kv), 1.0 / math.sqrt(DH_B))
        o = o * (1.0 / l)
        for g in range(GQA_STACK):
            cols = slice((h0 + g) * HEAD_W, (h0 + g + 1) * HEAD_W)
            gate = gb_ref[:, cols].astype(F32)
            o_ref[:, A_V + (h0 + g) * HEAD_W:A_V + (h0 + g + 1) * HEAD_W] = (
                o[g * tq:(g + 1) * tq, :] * gate).astype(o_ref.dtype)


def _attn_ctx_body(lam_init, lq1, lk1, lq2, lk2, subln_ref,
                   qa_ref, ka_ref, va_ref, ga_ref, qb_ref, kb_ref, vb_ref, gb_ref, o_ref):
    lam_col = _diff_lambda_col(lq1, lk1, lq2, lk2, lam_init)
    seq = qa_ref.shape[0]
    head = lambda ref, n_heads: (lambda i: _cache_head(ref, i, seq, n_heads))
    _attend(lam_col, subln_ref[...] * (1.0 - lam_init), qa_ref, ga_ref, qb_ref, gb_ref,
            head(ka_ref, H_A), head(va_ref, H_A), head(kb_ref, KV_B), head(vb_ref, KV_B), o_ref)


def _attn_ctx(lam_init, lam_params, subln_g, qa, ka, va, ga, qb, kb, vb, gb, seq):
    t = qa.shape[0]
    const = lambda shape: pl.BlockSpec(shape, lambda b: (0,) * len(shape))
    row_spec = lambda w: pl.BlockSpec((seq, w), lambda b: (b, 0))
    cache_spec = lambda n_heads: pl.BlockSpec((seq * n_heads, HEAD_W), lambda b: (b, 0))
    return pl.pallas_call(
        functools.partial(_attn_ctx_body, lam_init),
        out_shape=jax.ShapeDtypeStruct((t, D_MIX), BF16),
        grid=(t // seq,),
        in_specs=[const((H_A, DK_A))] * 4 + [const((1, DV_A))] + [
            row_spec(A_Q), cache_spec(H_A), cache_spec(H_A), row_spec(A_G),
            row_spec(B_Q), cache_spec(KV_B), cache_spec(KV_B), row_spec(B_G)],
        out_specs=row_spec(D_MIX),
        compiler_params=pltpu.CompilerParams(
            dimension_semantics=("arbitrary",), vmem_limit_bytes=VMEM_LIMIT),
        name="attn_ctx",
    )(*lam_params, subln_g, qa, ka, va, ga, qb, kb, vb, gb)


def _attn_lat_body(lam_init, n_lat, lq1, lk1, lq2, lk2, subln_ref,
                   qa_ref, ka_ref, va_ref, cka_ref, cva_ref, ga_ref,
                   qb_ref, kb_ref, vb_ref, ckb_ref, cvb_ref, gb_ref, o_ref,
                   ka_all, va_all, kb_all, vb_all):
    @pl.when(pl.program_id(1) == 0)
    def _():
        for new_ref, cache_ref, all_ref in ((ka_ref, cka_ref, ka_all), (va_ref, cva_ref, va_all),
                                            (kb_ref, ckb_ref, kb_all), (vb_ref, cvb_ref, vb_all)):
            n_heads = new_ref.shape[1] // HEAD_W
            n_past = cache_ref.shape[0] // n_heads
            all_ref[0:n_lat, :] = new_ref[...]
            for hd in range(n_heads):
                all_ref[n_lat:, hd * HEAD_W:(hd + 1) * HEAD_W] = _cache_head(cache_ref, hd, n_past, n_heads)

    lam_col = _diff_lambda_col(lq1, lk1, lq2, lk2, lam_init)
    head = lambda ref: (lambda i: ref[:, i * HEAD_W:(i + 1) * HEAD_W])
    _attend(lam_col, subln_ref[...] * (1.0 - lam_init), qa_ref, ga_ref, qb_ref, gb_ref,
            head(ka_all), head(va_all), head(kb_all), head(vb_all), o_ref)


def _attn_lat(lam_init, lam_params, subln_g, qa, ka, va, cka, cva, ga, qb, kb, vb, ckb, cvb, gb,
              n_lat, n_past):
    t = qa.shape[0]
    tq = ATTN_TQ
    nq = n_lat // tq
    const = lambda shape: pl.BlockSpec(shape, lambda b, i: (0,) * len(shape))
    q_spec = lambda w: pl.BlockSpec((tq, w), lambda b, i: (b * nq + i, 0))
    new_spec = lambda w: pl.BlockSpec((n_lat, w), lambda b, i: (b, 0))
    cache_spec = lambda w: pl.BlockSpec((n_past * (w // HEAD_W), HEAD_W), lambda b, i: (b, 0))
    n_all = n_lat + n_past
    return pl.pallas_call(
        functools.partial(_attn_lat_body, lam_init, n_lat),
        out_shape=jax.ShapeDtypeStruct((t, D_MIX), BF16),
        grid=(t // n_lat, nq),
        in_specs=[const((H_A, DK_A))] * 4 + [const((1, DV_A))] + [
            q_spec(A_Q), new_spec(A_K), new_spec(A_V), cache_spec(A_K), cache_spec(A_V), q_spec(A_G),
            q_spec(B_Q), new_spec(B_K), new_spec(B_V), cache_spec(B_K), cache_spec(B_V), q_spec(B_G)],
        out_specs=q_spec(D_MIX),
        scratch_shapes=[pltpu.VMEM((n_all, A_K), BF16), pltpu.VMEM((n_all, A_V), BF16),
                        pltpu.VMEM((n_all, B_K), BF16), pltpu.VMEM((n_all, B_V), BF16)],
        compiler_params=pltpu.CompilerParams(
            dimension_semantics=("arbitrary", "arbitrary"), vmem_limit_bytes=VMEM_LIMIT),
        name="attn_lat",
    )(*lam_params, subln_g, qa, ka, va, cka, cva, ga, qb, kb, vb, ckb, cvb, gb)


def _outproj_body(o_ref, w_ref, x_ref, mod_ref, gf_ref, y_ref):
    m = jnp.dot(o_ref[...], w_ref[...], preferred_element_type=F32)
    gate = mod_ref[:, 2 * D_MODEL:3 * D_MODEL]
    y_ref[...] = _rms(x_ref[...] + gate * m) * gf_ref[...]


def _outproj(o, w_out_bf16, x2d, mod3, row_of_tile, g_final):
    t = x2d.shape[0]
    tm = OUTPROJ_TM
    return pl.pallas_call(
        _outproj_body,
        out_shape=jax.ShapeDtypeStruct((t, D_MODEL), F32),
        grid=(t // tm,),
        in_specs=[
            pl.BlockSpec((tm, D_MIX), lambda i: (i, 0)),
            pl.BlockSpec((D_MIX, D_MODEL), lambda i: (0, 0), pipeline_mode=pl.Buffered(1)),
            pl.BlockSpec((tm, D_MODEL), lambda i: (i, 0)),
            pl.BlockSpec((None, 1, 3 * D_MODEL), lambda i: (row_of_tile(i), 0, 0)),
            pl.BlockSpec((1, D_MODEL), lambda i: (0, 0)),
        ],
        out_specs=pl.BlockSpec((tm, D_MODEL), lambda i: (i, 0)),
        compiler_params=pltpu.CompilerParams(
            dimension_semantics=("arbitrary",), vmem_limit_bytes=VMEM_LIMIT),
        name="outproj",
    )(o, w_out_bf16, x2d, mod3, g_final)


def _rope_tables(n_tok):
    n_rows = n_tok // GRID_W
    rows = np.repeat(np.arange(n_rows, dtype=np.float32), GRID_W)
    cols = np.tile(np.arange(GRID_W, dtype=np.float32), n_rows)

    def cos_sin(rot_dim):
        quarter = rot_dim // 4
        inv_freq = (1.0 / (np.float32(ROPE_THETA) ** (np.arange(quarter, dtype=np.float32) / quarter))
                    ).astype(np.float32)
        ang = np.concatenate([rows[:, None] * inv_freq, cols[:, None] * inv_freq], axis=-1)
        return np.cos(ang).astype(np.float32), np.sin(ang).astype(np.float32)

    ca, sa = cos_sin(DK_A)
    za = np.zeros_like(sa)
    cb, sb = cos_sin(DH_B)
    tabs = (
        np.concatenate([ca, ca, ca, ca], axis=-1),
        np.concatenate([-sa, za, -sa, za], axis=-1),
        np.concatenate([za, sa, za, sa], axis=-1),
        np.concatenate([cb, cb], axis=-1),
        np.concatenate([-sb, sb], axis=-1),
    )
    return tuple(jnp.asarray(t) for t in tabs)


def kernel(x_prompt, x_sample, cache_diff_k, cache_diff_v, cache_gqa_k, cache_gqa_v, c, c_ctx,
           w_ada, b_ada, g_norm, w_in, lam_q1, lam_k1, lam_q2, lam_k2, subln_g, q_norm_g, k_norm_g,
           w_out, g_final):
    bp, n_ctx, d = x_prompt.shape
    bs, n_lat, _ = x_sample.shape
    depth = w_in.shape[0]
    n_past = cache_diff_k.shape[2]
    assert depth == 1 and d == D_MODEL and bs + 1 <= N_COND
    assert n_lat % INPROJ_TM == 0 and n_lat % OUTPROJ_TM == 0 and n_lat % ATTN_TQ == 0
    l = 0
    lam_init = 0.8 - 0.6 * math.exp(-0.3 * l)

    cond = jnp.concatenate(
        [c_ctx[None, :], c, jnp.zeros((N_COND - 1 - bs, d), F32)], axis=0)
    mod = _adaln(cond, w_ada[l], b_ada[l][None, :])
    mod3 = mod[:, None, :]

    w_in_b = w_in[l].astype(BF16)
    w_out_b = w_out[l].astype(BF16)
    gn = g_norm[l][None, :]
    qn = q_norm_g[l][None, :]
    kn = k_norm_g[l][None, :]
    lam_params = (lam_q1[l], lam_k1[l], lam_q2[l], lam_k2[l])
    subln = subln_g[l][None, :]
    gf = g_final[None, :]

    xp2 = x_prompt.reshape(bp * n_ctx, d)
    qa, ka, va, ga, qb, kb, vb, gb = _inproj(
        xp2, mod3, lambda i: 0, gn, w_in_b, qn, kn, None)
    o = _attn_ctx(lam_init, lam_params, subln, qa, ka, va, ga, qb, kb, vb, gb, n_ctx)
    y_prompt = _outproj(o, w_out_b, xp2, mod3, lambda i: 0, gf).reshape(bp, n_ctx, d)
    new_diff_k = ka.reshape(bp, 1, n_ctx, H_A, 2 * DK_A)
    new_diff_v = va.reshape(bp, 1, n_ctx, H_A, DV_A)
    new_gqa_k = kb.reshape(bp, 1, n_ctx, KV_B, DH_B)
    new_gqa_v = vb.reshape(bp, 1, n_ctx, KV_B, DH_B)

    xs2 = x_sample.reshape(bs * n_lat, d)
    in_tiles = n_lat // INPROJ_TM
    qa, ka, va, ga, qb, kb, vb, gb = _inproj(
        xs2, mod3, lambda i: 1 + i // in_tiles, gn, w_in_b, qn, kn, _rope_tables(n_lat))
    cka = cache_diff_k[:, l].reshape(bs * n_past * H_A, HEAD_W)
    cva = cache_diff_v[:, l].reshape(bs * n_past * H_A, HEAD_W)
    ckb = cache_gqa_k[:, l].reshape(bs * n_past * KV_B, HEAD_W)
    cvb = cache_gqa_v[:, l].reshape(bs * n_past * KV_B, HEAD_W)
    o = _attn_lat(lam_init, lam_params, subln, qa, ka, va, cka, cva, ga, qb, kb, vb, ckb, cvb, gb,
                  n_lat, n_past)
    out_tiles = n_lat // OUTPROJ_TM
    y_sample = _outproj(o, w_out_b, xs2, mod3, lambda i: 1 + i // out_tiles, gf).reshape(bs, n_lat, d)

    return (y_prompt, y_sample, new_diff_k, new_diff_v, new_gqa_k, new_gqa_v)
```

```python
import functools
import math

import jax
import jax.numpy as jnp
import numpy as np
from jax import lax
from jax.experimental import pallas as pl
from jax.experimental.pallas import tpu as pltpu

D_MODEL = 2048
GRID_W = 64
ROPE_THETA = 10000.0
EPS = 1e-6
H_A = 8
DK_A = 64
DV_A = 2 * DK_A
H_B = 8
KV_B = 2
DH_B = 128
G_B = H_B // KV_B
HEAD_W = 128
A_Q = H_A * 2 * DK_A
A_K = H_A * 2 * DK_A
A_V = H_A * DV_A
A_G = H_A * DV_A
B_Q = H_B * DH_B
B_K = KV_B * DH_B
B_V = KV_B * DH_B
B_G = H_B * DH_B
D_IN = A_Q + A_K + A_V + A_G + B_Q + B_K + B_V + B_G
D_MIX = A_V + B_Q
OFF_AQ = 0
OFF_AK = OFF_AQ + A_Q
OFF_AV = OFF_AK + A_K
OFF_AG = OFF_AV + A_V
OFF_BQ = OFF_AG + A_G
OFF_BK = OFF_BQ + B_Q
OFF_BV = OFF_BK + B_K
OFF_BG = OFF_BV + B_V

N_COND = 8
ADA_TK = 256
INPROJ_TM = 256
INPROJ_CHUNK = 256
OUTPROJ_TM = 512
OUTPROJ_SUB = 256
ATTN_TQ = 256
GQA_STACK = 2
VMEM_LIMIT = 56 * 1024 * 1024

BF16 = jnp.bfloat16
F32 = jnp.float32


def _params(n_grid_axes):
    return pltpu.CompilerParams(
        dimension_semantics=("arbitrary",) * n_grid_axes,
        vmem_limit_bytes=VMEM_LIMIT,
    )


def _silu(x):
    return x * jax.nn.sigmoid(x)


def _rms(x):
    return x * lax.rsqrt(jnp.mean(x * x, axis=-1, keepdims=True) + EPS)


def _adaln_body(cond_ref, w_ref, b_ref, o_ref):
    @pl.when(pl.program_id(0) == 0)
    def _():
        o_ref[...] = jnp.broadcast_to(b_ref[...], o_ref.shape)

    a = _silu(cond_ref[...]).astype(BF16)
    w = w_ref[...].astype(BF16)
    o_ref[...] += jnp.dot(a, w, preferred_element_type=F32)


def _adaln(cond, w_ada, b_ada):
    d3 = w_ada.shape[1]
    return pl.pallas_call(
        _adaln_body,
        out_shape=jax.ShapeDtypeStruct((N_COND, d3), F32),
        grid=(D_MODEL // ADA_TK,),
        in_specs=[
            pl.BlockSpec((N_COND, ADA_TK), lambda k: (0, k)),
            pl.BlockSpec((ADA_TK, d3), lambda k: (k, 0)),
            pl.BlockSpec((1, d3), lambda k: (0, 0)),
        ],
        out_specs=pl.BlockSpec((N_COND, d3), lambda k: (0, 0)),
        compiler_params=_params(1),
        name="adaln",
    )(cond, w_ada, b_ada)


def _rope_a(x, c, s_up, s_dn):
    return x * c + pltpu.roll(x, 96, 1) * s_up + pltpu.roll(x, 32, 1) * s_dn


def _rope_b(x, c, s):
    return x * c + pltpu.roll(x, 64, 1) * s


def _inproj_body(rope, x_ref, mod_ref, gn_ref, w_ref, qn_ref, kn_ref, *rest):
    if rope:
        ca_ref, sau_ref, sad_ref, cb_ref, sb_ref = rest[:5]
        rest = rest[5:]
    qa_o, ka_o, va_o, ga_o, qb_o, kb_o, vb_o, gb_o = rest
    cache_layout = not rope
    tm = x_ref.shape[0]

    x = x_ref[...]
    shift = mod_ref[:, 0:D_MODEL]
    scale = mod_ref[:, D_MODEL:2 * D_MODEL]
    h = ((_rms(x) * gn_ref[...]) * (1.0 + scale) + shift).astype(BF16)

    if rope:
        ca, sau, sad = ca_ref[...], sau_ref[...], sad_ref[...]
        cb, sb = cb_ref[...], sb_ref[...]

    def rope_a(t):
        return _rope_a(t, ca, sau, sad) if rope else t

    def rope_b(t):
        return _rope_b(t, cb, sb) if rope else t

    ident = lambda t: t
    regions = (
        (OFF_AQ, A_Q, qa_o, rope_a, False),
        (OFF_AK, A_K, ka_o, rope_a, True),
        (OFF_AV, A_V, va_o, ident, True),
        (OFF_AG, A_G, ga_o, _silu, False),
        (OFF_BQ, B_Q, qb_o, lambda t: rope_b(_rms(t) * qn_ref[...]), False),
        (OFF_BK, B_K, kb_o, lambda t: rope_b(_rms(t) * kn_ref[...]), True),
        (OFF_BV, B_V, vb_o, ident, True),
        (OFF_BG, B_G, gb_o, _silu, False),
    )
    for start, width, o_ref, epi, is_kv in regions:
        n_heads = width // HEAD_W
        for c0 in range(0, width, INPROJ_CHUNK):
            z = jnp.dot(h, w_ref[:, start + c0:start + c0 + INPROJ_CHUNK],
                        preferred_element_type=F32)
            if epi is None:
                o_ref[:, c0:c0 + INPROJ_CHUNK] = z.astype(o_ref.dtype)
                continue
            for h0 in range(0, INPROJ_CHUNK, HEAD_W):
                hd = (c0 + h0) // HEAD_W
                t = epi(z[:, h0:h0 + HEAD_W]).astype(o_ref.dtype)
                if is_kv and cache_layout:
                    o_ref[pl.ds(hd, tm, stride=n_heads), :] = t
                else:
                    o_ref[:, hd * HEAD_W:(hd + 1) * HEAD_W] = t


def _inproj(x2d, mod3, row_of_tile, g_norm, w_in_bf16, q_norm_g, k_norm_g, rope_tabs):
    t = x2d.shape[0]
    tm = INPROJ_TM
    rope = rope_tabs is not None
    row_spec = lambda w: pl.BlockSpec((tm, w), lambda i: (i, 0))
    const = lambda shape: pl.BlockSpec(shape, lambda i: (0,) * len(shape))
    in_specs = [
        row_spec(D_MODEL),
        pl.BlockSpec((None, 1, 3 * D_MODEL), lambda i: (row_of_tile(i), 0, 0)),
        const((1, D_MODEL)),
        pl.BlockSpec((D_MODEL, D_IN), lambda i: (0, 0), pipeline_mode=pl.Buffered(1)),
        const((1, HEAD_W)),
        const((1, HEAD_W)),
    ]
    args = [x2d, mod3, g_norm, w_in_bf16, q_norm_g, k_norm_g]
    if rope:
        n_pos_tiles = rope_tabs[0].shape[0] // tm
        in_specs += [pl.BlockSpec((tm, HEAD_W), lambda i: (i % n_pos_tiles, 0))] * 5
        args += list(rope_tabs)
    widths = (A_Q, A_K, A_V, A_G, B_Q, B_K, B_V, B_G)
    is_kv = (False, True, True, False, False, True, True, False)
    out_shape, out_specs = [], []
    for w, kv in zip(widths, is_kv):
        if kv and not rope:
            n_heads = w // HEAD_W
            out_shape.append(jax.ShapeDtypeStruct((t * n_heads, HEAD_W), F32))
            out_specs.append(pl.BlockSpec((tm * n_heads, HEAD_W), lambda i: (i, 0)))
        else:
            out_shape.append(jax.ShapeDtypeStruct((t, w), BF16))
            out_specs.append(row_spec(w))
    return pl.pallas_call(
        functools.partial(_inproj_body, rope),
        out_shape=out_shape,
        grid=(t // tm,),
        in_specs=in_specs,
        out_specs=out_specs,
        compiler_params=_params(1),
        name="inproj_rope" if rope else "inproj",
    )(*args)


def _diff_lambda_col(lq1_ref, lk1_ref, lq2_ref, lk2_ref, lam_init):
    s1 = jnp.sum(lq1_ref[...] * lk1_ref[...], axis=-1, keepdims=True)
    s2 = jnp.sum(lq2_ref[...] * lk2_ref[...], axis=-1, keepdims=True)
    return jnp.exp(s1) - jnp.exp(s2) + lam_init


def _cache_head(ref, head, n_tok, n_heads):
    return ref[pl.ds(head, n_tok, stride=n_heads), :].astype(BF16)


def _softmax_pv(q, k, v, scale):
    s = lax.dot_general(q, k, (((1,), (1,)), ((), ())), preferred_element_type=F32)
    m = jnp.max(s, axis=-1, keepdims=True)
    e = jnp.exp2((s - m) * (scale * math.log2(math.e)))
    v1 = jnp.concatenate([v, jnp.ones_like(v)], axis=1)
    ol = jnp.dot(e.astype(BF16), v1, preferred_element_type=F32)
    return ol[:, :HEAD_W], ol[:, HEAD_W:]


def _attend(lam_col, subln, qa_ref, ga_ref, qb_ref, gb_ref, ka, va, kb, vb, o_ref):
    tq = qa_ref.shape[0]
    lane = lax.broadcasted_iota(jnp.int32, (tq, HEAD_W), 1)
    first = lane < DK_A

    for hd in range(H_A):
        cols = slice(hd * HEAD_W, (hd + 1) * HEAD_W)
        q = qa_ref[:, cols]
        zero = jnp.zeros_like(q)
        q2 = jnp.concatenate([jnp.where(first, q, zero), jnp.where(first, zero, q)], axis=0)
        o2, l2 = _softmax_pv(q2, ka(hd), va(hd), 1.0 / math.sqrt(DK_A))
        r2 = 1.0 / l2
        lam = lam_col[hd:hd + 1, :]
        o = o2[:tq] * r2[:tq] - o2[tq:] * (lam * r2[tq:])
        o_ref[:, cols] = (_rms(o) * subln * ga_ref[:, cols].astype(F32)).astype(o_ref.dtype)

    for h0 in range(0, H_B, GQA_STACK):
        kv = h0 // G_B
        q = jnp.concatenate(
            [qb_ref[:, (h0 + g) * HEAD_W:(h0 + g + 1) * HEAD_W] for g in range(GQA_STACK)], axis=0)
        o, l = _softmax_pv(q, kb(kv), vb(kv), 1.0 / math.sqrt(DH_B))
        o = o * (1.0 / l)
        for g in range(GQA_STACK):
            cols = slice((h0 + g) * HEAD_W, (h0 + g + 1) * HEAD_W)
            gate = gb_ref[:, cols].astype(F32)
            o_ref[:, A_V + (h0 + g) * HEAD_W:A_V + (h0 + g + 1) * HEAD_W] = (
                o[g * tq:(g + 1) * tq, :] * gate).astype(o_ref.dtype)


def _attn_ctx_body(lam_init, lq1, lk1, lq2, lk2, subln_ref,
                   qa_ref, ka_ref, va_ref, ga_ref, qb_ref, kb_ref, vb_ref, gb_ref, o_ref):
    lam_col = _diff_lambda_col(lq1, lk1, lq2, lk2, lam_init)
    seq = qa_ref.shape[0]
    head = lambda ref, n_heads: (lambda i: _cache_head(ref, i, seq, n_heads))
    _attend(lam_col, subln_ref[...] * (1.0 - lam_init), qa_ref, ga_ref, qb_ref, gb_ref,
            head(ka_ref, H_A), head(va_ref, H_A), head(kb_ref, KV_B), head(vb_ref, KV_B), o_ref)


def _attn_ctx(lam_init, lam_params, subln_g, qa, ka, va, ga, qb, kb, vb, gb, seq):
    t = qa.shape[0]
    const = lambda shape: pl.BlockSpec(shape, lambda b: (0,) * len(shape))
    row_spec = lambda w: pl.BlockSpec((seq, w), lambda b: (b, 0))
    cache_spec = lambda n_heads: pl.BlockSpec((seq * n_heads, HEAD_W), lambda b: (b, 0))
    return pl.pallas_call(
        functools.partial(_attn_ctx_body, lam_init),
        out_shape=jax.ShapeDtypeStruct((t, D_MIX), BF16),
        grid=(t // seq,),
        in_specs=[const((H_A, DK_A))] * 4 + [const((1, DV_A))] + [
            row_spec(A_Q), cache_spec(H_A), cache_spec(H_A), row_spec(A_G),
            row_spec(B_Q), cache_spec(KV_B), cache_spec(KV_B), row_spec(B_G)],
        out_specs=row_spec(D_MIX),
        compiler_params=_params(1),
        name="attn_ctx",
    )(*lam_params, subln_g, qa, ka, va, ga, qb, kb, vb, gb)


def _attn_lat_body(lam_init, n_lat, lq1, lk1, lq2, lk2, subln_ref,
                   qa_ref, ka_ref, va_ref, cka_ref, cva_ref, ga_ref,
                   qb_ref, kb_ref, vb_ref, ckb_ref, cvb_ref, gb_ref, o_ref,
                   ka_all, va_all, kb_all, vb_all):
    @pl.when(pl.program_id(1) == 0)
    def _():
        for new_ref, cache_ref, all_ref in ((ka_ref, cka_ref, ka_all), (va_ref, cva_ref, va_all),
                                            (kb_ref, ckb_ref, kb_all), (vb_ref, cvb_ref, vb_all)):
            n_heads = new_ref.shape[1] // HEAD_W
            n_past = cache_ref.shape[0] // n_heads
            all_ref[0:n_lat, :] = new_ref[...]
            for hd in range(n_heads):
                all_ref[n_lat:, hd * HEAD_W:(hd + 1) * HEAD_W] = _cache_head(cache_ref, hd, n_past, n_heads)

    lam_col = _diff_lambda_col(lq1, lk1, lq2, lk2, lam_init)
    head = lambda ref: (lambda i: ref[:, i * HEAD_W:(i + 1) * HEAD_W])
    _attend(lam_col, subln_ref[...] * (1.0 - lam_init), qa_ref, ga_ref, qb_ref, gb_ref,
            head(ka_all), head(va_all), head(kb_all), head(vb_all), o_ref)


def _attn_lat(lam_init, lam_params, subln_g, qa, ka, va, cka, cva, ga, qb, kb, vb, ckb, cvb, gb,
              n_lat, n_past):
    t = qa.shape[0]
    tq = ATTN_TQ
    nq = n_lat // tq
    const = lambda shape: pl.BlockSpec(shape, lambda b, i: (0,) * len(shape))
    q_spec = lambda w: pl.BlockSpec((tq, w), lambda b, i: (b * nq + i, 0))
    new_spec = lambda w: pl.BlockSpec((n_lat, w), lambda b, i: (b, 0))
    cache_spec = lambda w: pl.BlockSpec((n_past * (w // HEAD_W), HEAD_W), lambda b, i: (b, 0))
    n_all = n_lat + n_past
    return pl.pallas_call(
        functools.partial(_attn_lat_body, lam_init, n_lat),
        out_shape=jax.ShapeDtypeStruct((t, D_MIX), BF16),
        grid=(t // n_lat, nq),
        in_specs=[const((H_A, DK_A))] * 4 + [const((1, DV_A))] + [
            q_spec(A_Q), new_spec(A_K), new_spec(A_V), cache_spec(A_K), cache_spec(A_V), q_spec(A_G),
            q_spec(B_Q), new_spec(B_K), new_spec(B_V), cache_spec(B_K), cache_spec(B_V), q_spec(B_G)],
        out_specs=q_spec(D_MIX),
        scratch_shapes=[pltpu.VMEM((n_all, A_K), BF16), pltpu.VMEM((n_all, A_V), BF16),
                        pltpu.VMEM((n_all, B_K), BF16), pltpu.VMEM((n_all, B_V), BF16)],
        compiler_params=_params(2),
        name="attn_lat",
    )(*lam_params, subln_g, qa, ka, va, cka, cva, ga, qb, kb, vb, ckb, cvb, gb)


def _outproj_body(o_ref, w_ref, x_ref, mod_ref, gf_ref, y_ref):
    gate = mod_ref[:, 2 * D_MODEL:3 * D_MODEL]
    for r0 in range(0, o_ref.shape[0], OUTPROJ_SUB):
        rows = slice(r0, r0 + OUTPROJ_SUB)
        m = jnp.dot(o_ref[rows, :], w_ref[...], preferred_element_type=F32)
        y_ref[rows, :] = _rms(x_ref[rows, :] + gate * m) * gf_ref[...]


def _outproj(o, w_out_bf16, x2d, mod3, row_of_tile, g_final):
    t = x2d.shape[0]
    tm = OUTPROJ_TM
    return pl.pallas_call(
        _outproj_body,
        out_shape=jax.ShapeDtypeStruct((t, D_MODEL), F32),
        grid=(t // tm,),
        in_specs=[
            pl.BlockSpec((tm, D_MIX), lambda i: (i, 0)),
            pl.BlockSpec((D_MIX, D_MODEL), lambda i: (0, 0), pipeline_mode=pl.Buffered(1)),
            pl.BlockSpec((tm, D_MODEL), lambda i: (i, 0)),
            pl.BlockSpec((None, 1, 3 * D_MODEL), lambda i: (row_of_tile(i), 0, 0)),
            pl.BlockSpec((1, D_MODEL), lambda i: (0, 0)),
        ],
        out_specs=pl.BlockSpec((tm, D_MODEL), lambda i: (i, 0)),
        compiler_params=_params(1),
        name="outproj",
    )(o, w_out_bf16, x2d, mod3, g_final)


def _rope_tables(n_tok):
    n_rows = n_tok // GRID_W
    rows = np.repeat(np.arange(n_rows, dtype=np.float32), GRID_W)
    cols = np.tile(np.arange(GRID_W, dtype=np.float32), n_rows)

    def cos_sin(rot_dim):
        quarter = rot_dim // 4
        inv_freq = (1.0 / (np.float32(ROPE_THETA) ** (np.arange(quarter, dtype=np.float32) / quarter))
                    ).astype(np.float32)
        ang = np.concatenate([rows[:, None] * inv_freq, cols[:, None] * inv_freq], axis=-1)
        return np.cos(ang).astype(np.float32), np.sin(ang).astype(np.float32)

    ca, sa = cos_sin(DK_A)
    za = np.zeros_like(sa)
    cb, sb = cos_sin(DH_B)
    tabs = (
        np.concatenate([ca, ca, ca, ca], axis=-1),
        np.concatenate([-sa, za, -sa, za], axis=-1),
        np.concatenate([za, sa, za, sa], axis=-1),
        np.concatenate([cb, cb], axis=-1),
        np.concatenate([-sb, sb], axis=-1),
    )
    return tuple(jnp.asarray(t) for t in tabs)


def kernel(x_prompt, x_sample, cache_diff_k, cache_diff_v, cache_gqa_k, cache_gqa_v, c, c_ctx,
           w_ada, b_ada, g_norm, w_in, lam_q1, lam_k1, lam_q2, lam_k2, subln_g, q_norm_g, k_norm_g,
           w_out, g_final):
    bp, n_ctx, d = x_prompt.shape
    bs, n_lat, _ = x_sample.shape
    depth = w_in.shape[0]
    n_past = cache_diff_k.shape[2]
    assert depth == 1 and d == D_MODEL and bs + 1 <= N_COND
    assert n_lat % INPROJ_TM == 0 and n_lat % OUTPROJ_TM == 0 and n_lat % ATTN_TQ == 0
    l = 0
    lam_init = 0.8 - 0.6 * math.exp(-0.3 * l)

    cond = jnp.concatenate(
        [c_ctx[None, :], c, jnp.zeros((N_COND - 1 - bs, d), F32)], axis=0)
    mod = _adaln(cond, w_ada[l], b_ada[l][None, :])
    mod3 = mod[:, None, :]

    w_in_b = w_in[l].astype(BF16)
    w_out_b = w_out[l].astype(BF16)
    gn = g_norm[l][None, :]
    qn = q_norm_g[l][None, :]
    kn = k_norm_g[l][None, :]
    lam_params = (lam_q1[l], lam_k1[l], lam_q2[l], lam_k2[l])
    subln = subln_g[l][None, :]
    gf = g_final[None, :]

    xp2 = x_prompt.reshape(bp * n_ctx, d)
    qa, ka, va, ga, qb, kb, vb, gb = _inproj(
        xp2, mod3, lambda i: 0, gn, w_in_b, qn, kn, None)
    o = _attn_ctx(lam_init, lam_params, subln, qa, ka, va, ga, qb, kb, vb, gb, n_ctx)
    y_prompt = _outproj(o, w_out_b, xp2, mod3, lambda i: 0, gf).reshape(bp, n_ctx, d)
    new_diff_k = ka.reshape(bp, 1, n_ctx, H_A, 2 * DK_A)
    new_diff_v = va.reshape(bp, 1, n_ctx, H_A, DV_A)
    new_gqa_k = kb.reshape(bp, 1, n_ctx, KV_B, DH_B)
    new_gqa_v = vb.reshape(bp, 1, n_ctx, KV_B, DH_B)

    xs2 = x_sample.reshape(bs * n_lat, d)
    in_tiles = n_lat // INPROJ_TM
    qa, ka, va, ga, qb, kb, vb, gb = _inproj(
        xs2, mod3, lambda i: 1 + i // in_tiles, gn, w_in_b, qn, kn, _rope_tables(n_lat))
    cka = cache_diff_k[:, l].reshape(bs * n_past * H_A, HEAD_W)
    cva = cache_diff_v[:, l].reshape(bs * n_past * H_A, HEAD_W)
    ckb = cache_gqa_k[:, l].reshape(bs * n_past * KV_B, HEAD_W)
    cvb = cache_gqa_v[:, l].reshape(bs * n_past * KV_B, HEAD_W)
    o = _attn_lat(lam_init, lam_params, subln, qa, ka, va, cka, cva, ga, qb, kb, vb, ckb, cvb, gb,
                  n_lat, n_past)
    out_tiles = n_lat // OUTPROJ_TM
    y_sample = _outproj(o, w_out_b, xs2, mod3, lambda i: 1 + i // out_tiles, gf).reshape(bs, n_lat, d)

    return (y_prompt, y_sample, new_diff_k, new_diff_v, new_gqa_k, new_gqa_v)
```

```python
import functools
import math

import jax
import jax.numpy as jnp
import numpy as np
from jax import lax
from jax.experimental import pallas as pl
from jax.experimental.pallas import tpu as pltpu

D_MODEL = 2048
GRID_W = 64
ROPE_THETA = 10000.0
EPS = 1e-6
H_A = 8
DK_A = 64
DV_A = 2 * DK_A
H_B = 8
KV_B = 2
DH_B = 128
G_B = H_B // KV_B
HEAD_W = 128
A_Q = H_A * 2 * DK_A
A_K = H_A * 2 * DK_A
A_V = H_A * DV_A
A_G = H_A * DV_A
B_Q = H_B * DH_B
B_K = KV_B * DH_B
B_V = KV_B * DH_B
B_G = H_B * DH_B
D_IN = A_Q + A_K + A_V + A_G + B_Q + B_K + B_V + B_G
D_MIX = A_V + B_Q
OFF_AQ = 0
OFF_AK = OFF_AQ + A_Q
OFF_AV = OFF_AK + A_K
OFF_AG = OFF_AV + A_V
OFF_BQ = OFF_AG + A_G
OFF_BK = OFF_BQ + B_Q
OFF_BV = OFF_BK + B_K
OFF_BG = OFF_BV + B_V
N_VHEADS = H_A + KV_B

N_COND = 8
ADA_TK = 256
INPROJ_TM = 256
INPROJ_CHUNK = 256
OUTPROJ_TM = 512
ATTN_TQ = 256
SCORE_LOOKAHEAD = 2
GQA_STACK = 2
SUBLANES = 8
MAX_CHAINS = 8
QSCALE_A = math.log2(math.e) / math.sqrt(DK_A)
QSCALE_B = math.log2(math.e) / math.sqrt(DH_B)
ONES_ROWS = 16
VMEM_LIMIT = 56 * 1024 * 1024

BF16 = jnp.bfloat16
F32 = jnp.float32


def _params(n_grid_axes):
    return pltpu.CompilerParams(
        dimension_semantics=("arbitrary",) * n_grid_axes,
        vmem_limit_bytes=VMEM_LIMIT,
    )


def _silu(x):
    return x * jax.nn.sigmoid(x)


def _rms(x):
    return x * lax.rsqrt(jnp.mean(x * x, axis=-1, keepdims=True) + EPS)


def _adaln_body(cond_ref, w_ref, b_ref, o_ref):
    @pl.when(pl.program_id(0) == 0)
    def _():
        o_ref[...] = jnp.broadcast_to(b_ref[...], o_ref.shape)

    a = _silu(cond_ref[...]).astype(BF16)
    w = w_ref[...].astype(BF16)
    o_ref[...] += jnp.dot(a, w, preferred_element_type=F32)


def _adaln(cond, w_ada, b_ada):
    d3 = w_ada.shape[1]
    return pl.pallas_call(
        _adaln_body,
        out_shape=jax.ShapeDtypeStruct((N_COND, d3), F32),
        grid=(D_MODEL // ADA_TK,),
        in_specs=[
            pl.BlockSpec((N_COND, ADA_TK), lambda k: (0, k)),
            pl.BlockSpec((ADA_TK, d3), lambda k: (k, 0)),
            pl.BlockSpec((1, d3), lambda k: (0, 0)),
        ],
        out_specs=pl.BlockSpec((N_COND, d3), lambda k: (0, 0)),
        compiler_params=_params(1),
        name="adaln",
    )(cond, w_ada, b_ada)


def _rope_a(x, c, s_up, s_dn):
    return x * c + pltpu.roll(x, 96, 1) * s_up + pltpu.roll(x, 32, 1) * s_dn


def _rope_b(x, c, s):
    return x * c + pltpu.roll(x, 64, 1) * s


def _inproj_body(rope, x_ref, mod_ref, gn_ref, w_ref, qn_ref, kn_ref, *rest):
    if rope:
        ca_ref, sau_ref, sad_ref, cb_ref, sb_ref = rest[:5]
        qa_o, ka_o, ga_o, qb_o, kb_o, gb_o, vt_o = rest[5:]
        va_o = vb_o = None
    else:
        qa_o, ka_o, va_o, ga_o, qb_o, kb_o, vb_o, gb_o, vt_o = rest
    tm = x_ref.shape[0]

    x = x_ref[...]
    shift = mod_ref[:, 0:D_MODEL]
    scale = mod_ref[:, D_MODEL:2 * D_MODEL]
    h = ((_rms(x) * gn_ref[...]) * (1.0 + scale) + shift).astype(BF16)

    if rope:
        ca, sau, sad = ca_ref[...], sau_ref[...], sad_ref[...]
        cb, sb = cb_ref[...], sb_ref[...]

    def rope_a(t):
        return _rope_a(t, ca, sau, sad) if rope else t

    def rope_b(t):
        return _rope_b(t, cb, sb) if rope else t

    def cols(o_ref):
        def store(hd, t):
            o_ref[:, hd * HEAD_W:(hd + 1) * HEAD_W] = t.astype(o_ref.dtype)
        return store

    def cache(o_ref, n_heads):
        def store(hd, t):
            o_ref[pl.ds(hd, tm, stride=n_heads), :] = t
        return store

    def transposed(first_head):
        def store(hd, t):
            vt_o[first_head + hd] = t.T.astype(BF16)
        return store

    def key_stores(o_ref, n_heads):
        return [cols(o_ref)] if rope else [cache(o_ref, n_heads)]

    def value_stores(o_ref, n_heads, first_head):
        return [transposed(first_head)] + ([] if rope else [cache(o_ref, n_heads)])

    ident = lambda t: t
    regions = (
        (OFF_AQ, A_Q, lambda t: rope_a(t) * QSCALE_A, [cols(qa_o)]),
        (OFF_AK, A_K, rope_a, key_stores(ka_o, H_A)),
        (OFF_AV, A_V, ident, value_stores(va_o, H_A, 0)),
        (OFF_AG, A_G, _silu, [cols(ga_o)]),
        (OFF_BQ, B_Q, lambda t: rope_b(_rms(t) * qn_ref[...]) * QSCALE_B, [cols(qb_o)]),
        (OFF_BK, B_K, lambda t: rope_b(_rms(t) * kn_ref[...]), key_stores(kb_o, KV_B)),
        (OFF_BV, B_V, ident, value_stores(vb_o, KV_B, H_A)),
        (OFF_BG, B_G, _silu, [cols(gb_o)]),
    )
    for start, width, epi, stores in regions:
        for c0 in range(0, width, INPROJ_CHUNK):
            z = jnp.dot(h, w_ref[:, start + c0:start + c0 + INPROJ_CHUNK],
                        preferred_element_type=F32)
            for h0 in range(0, INPROJ_CHUNK, HEAD_W):
                t = epi(z[:, h0:h0 + HEAD_W])
                for store in stores:
                    store((c0 + h0) // HEAD_W, t)


def _inproj(x2d, mod3, row_of_tile, g_norm, w_in_bf16, q_norm_g, k_norm_g, rope_tabs):
    t = x2d.shape[0]
    tm = INPROJ_TM
    rope = rope_tabs is not None
    row_spec = lambda w: pl.BlockSpec((tm, w), lambda i: (i, 0))
    const = lambda shape: pl.BlockSpec(shape, lambda i: (0,) * len(shape))
    in_specs = [
        row_spec(D_MODEL),
        pl.BlockSpec((None, 1, 3 * D_MODEL), lambda i: (row_of_tile(i), 0, 0)),
        const((1, D_MODEL)),
        pl.BlockSpec((D_MODEL, D_IN), lambda i: (0, 0), pipeline_mode=pl.Buffered(1)),
        const((1, HEAD_W)),
        const((1, HEAD_W)),
    ]
    args = [x2d, mod3, g_norm, w_in_bf16, q_norm_g, k_norm_g]
    if rope:
        n_pos_tiles = rope_tabs[0].shape[0] // tm
        in_specs += [pl.BlockSpec((tm, HEAD_W), lambda i: (i % n_pos_tiles, 0))] * 5
        args += list(rope_tabs)

    out_shape, out_specs = [], []

    def token_major(w):
        out_shape.append(jax.ShapeDtypeStruct((t, w), BF16))
        out_specs.append(row_spec(w))

    def cache_layout(w):
        n_heads = w // HEAD_W
        out_shape.append(jax.ShapeDtypeStruct((t * n_heads, HEAD_W), F32))
        out_specs.append(pl.BlockSpec((tm * n_heads, HEAD_W), lambda i: (i, 0)))

    if rope:
        for w in (A_Q, A_K, A_G, B_Q, B_K, B_G):
            token_major(w)
    else:
        token_major(A_Q), cache_layout(A_K), cache_layout(A_V), token_major(A_G)
        token_major(B_Q), cache_layout(B_K), cache_layout(B_V), token_major(B_G)
    out_shape.append(jax.ShapeDtypeStruct((N_VHEADS, HEAD_W, t), BF16))
    out_specs.append(pl.BlockSpec((N_VHEADS, HEAD_W, tm), lambda i: (0, 0, i)))

    return pl.pallas_call(
        functools.partial(_inproj_body, rope),
        out_shape=out_shape,
        grid=(t // tm,),
        in_specs=in_specs,
        out_specs=out_specs,
        compiler_params=_params(1),
        name="inproj_rope" if rope else "inproj",
    )(*args)


def _diff_lambda_col(lq1_ref, lk1_ref, lq2_ref, lk2_ref, lam_init):
    s1 = jnp.sum(lq1_ref[...] * lk1_ref[...], axis=-1, keepdims=True)
    s2 = jnp.sum(lq2_ref[...] * lk2_ref[...], axis=-1, keepdims=True)
    return jnp.exp(s1) - jnp.exp(s2) + lam_init


def _cache_head(ref, head, n_tok, n_heads):
    return ref[pl.ds(head, n_tok, stride=n_heads), :]


def _col_max(s):
    n_rows, m = s.shape
    s4 = s.reshape(MAX_CHAINS, n_rows // (SUBLANES * MAX_CHAINS), SUBLANES, m)
    return jnp.max(jnp.max(jnp.max(s4, axis=1), axis=0), axis=0, keepdims=True)


def _scores_t(q, k):
    return lax.dot_general(k, q, (((1,), (1,)), ((), ())), preferred_element_type=F32)


def _softmax_pv(st, v1t):
    m = _col_max(st)
    e = jnp.exp2(st - m)
    ot = jnp.dot(v1t, e.astype(BF16), preferred_element_type=F32)
    return ot[:HEAD_W], ot[HEAD_W:HEAD_W + 1]


def _attend(lam_col, subln, qa_ref, ga_ref, qb_ref, gb_ref, ka, kb, v1t, o_ref):
    tq = qa_ref.shape[0]
    lane = lax.broadcasted_iota(jnp.int32, (tq, HEAD_W), 1)
    first = lane < DK_A

    def diff_scores(hd):
        q = qa_ref[:, hd * HEAD_W:(hd + 1) * HEAD_W]
        zero = jnp.zeros_like(q)
        q2 = jnp.concatenate([jnp.where(first, q, zero), jnp.where(first, zero, q)], axis=0)
        return _scores_t(q2, ka(hd))

    def diff_finish(hd, st):
        cols = slice(hd * HEAD_W, (hd + 1) * HEAD_W)
        ot, l = _softmax_pv(st, v1t(hd))
        r = 1.0 / l
        lam = lam_col[hd:hd + 1, :]
        o = (ot[:, :tq] * r[:, :tq] - ot[:, tq:] * (lam * r[:, tq:])).T
        o_ref[:, cols] = (_rms(o) * subln * ga_ref[:, cols].astype(F32)).astype(o_ref.dtype)

    def gqa_scores(h0):
        q = jnp.concatenate(
            [qb_ref[:, (h0 + g) * HEAD_W:(h0 + g + 1) * HEAD_W] for g in range(GQA_STACK)], axis=0)
        return _scores_t(q, kb(h0 // G_B))

    def gqa_finish(h0, st):
        ot, l = _softmax_pv(st, v1t(H_A + h0 // G_B))
        o = (ot * (1.0 / l)).T
        for g in range(GQA_STACK):
            cols = slice((h0 + g) * HEAD_W, (h0 + g + 1) * HEAD_W)
            gate = gb_ref[:, cols].astype(F32)
            o_ref[:, A_V + (h0 + g) * HEAD_W:A_V + (h0 + g + 1) * HEAD_W] = (
                o[g * tq:(g + 1) * tq, :] * gate).astype(o_ref.dtype)

    units = [(diff_scores, diff_finish, hd) for hd in range(H_A)]
    units += [(gqa_scores, gqa_finish, h0) for h0 in range(0, H_B, GQA_STACK)]
    scores = []
    for u, (_, finish, arg) in enumerate(units):
        while len(scores) < min(len(units), u + 1 + SCORE_LOOKAHEAD):
            issue, _, issue_arg = units[len(scores)]
            scores.append(issue(issue_arg))
        finish(arg, scores[u])
        scores[u] = None


def _attn_ctx_body(lam_init, lq1, lk1, lq2, lk2, subln_ref,
                   qa_ref, ka_ref, ga_ref, qb_ref, kb_ref, gb_ref, vt_ref, o_ref):
    lam_col = _diff_lambda_col(lq1, lk1, lq2, lk2, lam_init)
    seq = qa_ref.shape[0]
    head = lambda ref, n_heads: (lambda i: _cache_head(ref, i, seq, n_heads).astype(BF16))
    ones = jnp.ones((ONES_ROWS, seq), BF16)
    v1t = lambda i: jnp.concatenate([vt_ref[i], ones], axis=0)
    _attend(lam_col, subln_ref[...] * (1.0 - lam_init), qa_ref, ga_ref, qb_ref, gb_ref,
            head(ka_ref, H_A), head(kb_ref, KV_B), v1t, o_ref)


def _attn_ctx(lam_init, lam_params, subln_g, qa, ka, ga, qb, kb, gb, vt, seq):
    t = qa.shape[0]
    const = lambda shape: pl.BlockSpec(shape, lambda b: (0,) * len(shape))
    row_spec = lambda w: pl.BlockSpec((seq, w), lambda b: (b, 0))
    cache_spec = lambda n_heads: pl.BlockSpec((seq * n_heads, HEAD_W), lambda b: (b, 0))
    return pl.pallas_call(
        functools.partial(_attn_ctx_body, lam_init),
        out_shape=jax.ShapeDtypeStruct((t, D_MIX), BF16),
        grid=(t // seq,),
        in_specs=[const((H_A, DK_A))] * 4 + [const((1, DV_A))] + [
            row_spec(A_Q), cache_spec(H_A), row_spec(A_G),
            row_spec(B_Q), cache_spec(KV_B), row_spec(B_G),
            pl.BlockSpec((N_VHEADS, HEAD_W, seq), lambda b: (0, 0, b))],
        out_specs=row_spec(D_MIX),
        compiler_params=_params(1),
        name="attn_ctx",
    )(*lam_params, subln_g, qa, ka, ga, qb, kb, gb, vt)


def _attn_lat_body(lam_init, n_lat, lq1, lk1, lq2, lk2, subln_ref,
                   qa_ref, ka_ref, cka_ref, ga_ref, qb_ref, kb_ref, ckb_ref, gb_ref,
                   vt_ref, cva_ref, cvb_ref, o_ref, ka_all, kb_all, v1t_all):
    @pl.when(pl.program_id(1) == 0)
    def _():
        for new_ref, cache_ref, all_ref in ((ka_ref, cka_ref, ka_all), (kb_ref, ckb_ref, kb_all)):
            n_heads = new_ref.shape[1] // HEAD_W
            n_past = cache_ref.shape[0] // n_heads
            all_ref[0:n_lat, :] = new_ref[...]
            for hd in range(n_heads):
                all_ref[n_lat:, hd * HEAD_W:(hd + 1) * HEAD_W] = _cache_head(
                    cache_ref, hd, n_past, n_heads).astype(BF16)
        n_all = v1t_all.shape[2]
        for cache_ref, n_heads, first_head in ((cva_ref, H_A, 0), (cvb_ref, KV_B, H_A)):
            n_past = cache_ref.shape[0] // n_heads
            for hd in range(n_heads):
                v1t_all[first_head + hd, 0:HEAD_W, 0:n_lat] = vt_ref[first_head + hd]
                v1t_all[first_head + hd, 0:HEAD_W, n_lat:] = _cache_head(
                    cache_ref, hd, n_past, n_heads).T.astype(BF16)
                v1t_all[first_head + hd, HEAD_W:, :] = jnp.ones((ONES_ROWS, n_all), BF16)

    lam_col = _diff_lambda_col(lq1, lk1, lq2, lk2, lam_init)
    head = lambda ref: (lambda i: ref[:, i * HEAD_W:(i + 1) * HEAD_W])
    _attend(lam_col, subln_ref[...] * (1.0 - lam_init), qa_ref, ga_ref, qb_ref, gb_ref,
            head(ka_all), head(kb_all), lambda i: v1t_all[i], o_ref)


def _attn_lat(lam_init, lam_params, subln_g, qa, ka, cka, ga, qb, kb, ckb, gb, vt, cva, cvb,
              n_lat, n_past):
    t = qa.shape[0]
    tq = ATTN_TQ
    nq = n_lat // tq
    const = lambda shape: pl.BlockSpec(shape, lambda b, i: (0,) * len(shape))
    q_spec = lambda w: pl.BlockSpec((tq, w), lambda b, i: (b * nq + i, 0))
    new_spec = lambda w: pl.BlockSpec((n_lat, w), lambda b, i: (b, 0))
    cache_spec = lambda w: pl.BlockSpec((n_past * (w // HEAD_W), HEAD_W), lambda b, i: (b, 0))
    n_all = n_lat + n_past
    return pl.pallas_call(
        functools.partial(_attn_lat_body, lam_init, n_lat),
        out_shape=jax.ShapeDtypeStruct((t, D_MIX), BF16),
        grid=(t // n_lat, nq),
        in_specs=[const((H_A, DK_A))] * 4 + [const((1, DV_A))] + [
            q_spec(A_Q), new_spec(A_K), cache_spec(A_K), q_spec(A_G),
            q_spec(B_Q), new_spec(B_K), cache_spec(B_K), q_spec(B_G),
            pl.BlockSpec((N_VHEADS, HEAD_W, n_lat), lambda b, i: (0, 0, b)),
            cache_spec(A_V), cache_spec(B_V)],
        out_specs=q_spec(D_MIX),
        scratch_shapes=[pltpu.VMEM((n_all, A_K), BF16), pltpu.VMEM((n_all, B_K), BF16),
                        pltpu.VMEM((N_VHEADS, HEAD_W + ONES_ROWS, n_all), BF16)],
        compiler_params=_params(2),
        name="attn_lat",
    )(*lam_params, subln_g, qa, ka, cka, ga, qb, kb, ckb, gb, vt, cva, cvb)


def _outproj_body(o_ref, w_ref, x_ref, mod_ref, gf_ref, y_ref):
    m = jnp.dot(o_ref[...], w_ref[...], preferred_element_type=F32)
    gate = mod_ref[:, 2 * D_MODEL:3 * D_MODEL]
    y_ref[...] = _rms(x_ref[...] + gate * m) * gf_ref[...]


def _outproj(o, w_out_bf16, x2d, mod3, row_of_tile, g_final):
    t = x2d.shape[0]
    tm = OUTPROJ_TM
    return pl.pallas_call(
        _outproj_body,
        out_shape=jax.ShapeDtypeStruct((t, D_MODEL), F32),
        grid=(t // tm,),
        in_specs=[
            pl.BlockSpec((tm, D_MIX), lambda i: (i, 0)),
            pl.BlockSpec((D_MIX, D_MODEL), lambda i: (0, 0), pipeline_mode=pl.Buffered(1)),
            pl.BlockSpec((tm, D_MODEL), lambda i: (i, 0)),
            pl.BlockSpec((None, 1, 3 * D_MODEL), lambda i: (row_of_tile(i), 0, 0)),
            pl.BlockSpec((1, D_MODEL), lambda i: (0, 0)),
        ],
        out_specs=pl.BlockSpec((tm, D_MODEL), lambda i: (i, 0)),
        compiler_params=_params(1),
        name="outproj",
    )(o, w_out_bf16, x2d, mod3, g_final)


def _rope_tables(n_tok):
    n_rows = n_tok // GRID_W
    rows = np.repeat(np.arange(n_rows, dtype=np.float32), GRID_W)
    cols = np.tile(np.arange(GRID_W, dtype=np.float32), n_rows)

    def cos_sin(rot_dim):
        quarter = rot_dim // 4
        inv_freq = (1.0 / (np.float32(ROPE_THETA) ** (np.arange(quarter, dtype=np.float32) / quarter))
                    ).astype(np.float32)
        ang = np.concatenate([rows[:, None] * inv_freq, cols[:, None] * inv_freq], axis=-1)
        return np.cos(ang).astype(np.float32), np.sin(ang).astype(np.float32)

    ca, sa = cos_sin(DK_A)
    za = np.zeros_like(sa)
    cb, sb = cos_sin(DH_B)
    tabs = (
        np.concatenate([ca, ca, ca, ca], axis=-1),
        np.concatenate([-sa, za, -sa, za], axis=-1),
        np.concatenate([za, sa, za, sa], axis=-1),
        np.concatenate([cb, cb], axis=-1),
        np.concatenate([-sb, sb], axis=-1),
    )
    return tuple(jnp.asarray(t) for t in tabs)


def kernel(x_prompt, x_sample, cache_diff_k, cache_diff_v, cache_gqa_k, cache_gqa_v, c, c_ctx,
           w_ada, b_ada, g_norm, w_in, lam_q1, lam_k1, lam_q2, lam_k2, subln_g, q_norm_g, k_norm_g,
           w_out, g_final):
    bp, n_ctx, d = x_prompt.shape
    bs, n_lat, _ = x_sample.shape
    depth = w_in.shape[0]
    n_past = cache_diff_k.shape[2]
    assert depth == 1 and d == D_MODEL and bs + 1 <= N_COND
    assert n_lat % INPROJ_TM == 0 and n_lat % OUTPROJ_TM == 0 and n_lat % ATTN_TQ == 0
    l = 0
    lam_init = 0.8 - 0.6 * math.exp(-0.3 * l)

    cond = jnp.concatenate(
        [c_ctx[None, :], c, jnp.zeros((N_COND - 1 - bs, d), F32)], axis=0)
    mod = _adaln(cond, w_ada[l], b_ada[l][None, :])
    mod3 = mod[:, None, :]

    w_in_b = w_in[l].astype(BF16)
    w_out_b = w_out[l].astype(BF16)
    gn = g_norm[l][None, :]
    qn = q_norm_g[l][None, :]
    kn = k_norm_g[l][None, :]
    lam_params = (lam_q1[l], lam_k1[l], lam_q2[l], lam_k2[l])
    subln = subln_g[l][None, :]
    gf = g_final[None, :]

    xp2 = x_prompt.reshape(bp * n_ctx, d)
    qa, ka, va, ga, qb, kb, vb, gb, vt = _inproj(
        xp2, mod3, lambda i: 0, gn, w_in_b, qn, kn, None)
    o = _attn_ctx(lam_init, lam_params, subln, qa, ka, ga, qb, kb, gb, vt, n_ctx)
    y_prompt = _outproj(o, w_out_b, xp2, mod3, lambda i: 0, gf).reshape(bp, n_ctx, d)
    new_diff_k = ka.reshape(bp, 1, n_ctx, H_A, 2 * DK_A)
    new_diff_v = va.reshape(bp, 1, n_ctx, H_A, DV_A)
    new_gqa_k = kb.reshape(bp, 1, n_ctx, KV_B, DH_B)
    new_gqa_v = vb.reshape(bp, 1, n_ctx, KV_B, DH_B)

    xs2 = x_sample.reshape(bs * n_lat, d)
    in_tiles = n_lat // INPROJ_TM
    qa, ka, ga, qb, kb, gb, vt = _inproj(
        xs2, mod3, lambda i: 1 + i // in_tiles, gn, w_in_b, qn, kn, _rope_tables(n_lat))
    cka = cache_diff_k[:, l].reshape(bs * n_past * H_A, HEAD_W)
    cva = cache_diff_v[:, l].reshape(bs * n_past * H_A, HEAD_W)
    ckb = cache_gqa_k[:, l].reshape(bs * n_past * KV_B, HEAD_W)
    cvb = cache_gqa_v[:, l].reshape(bs * n_past * KV_B, HEAD_W)
    o = _attn_lat(lam_init, lam_params, subln, qa, ka, cka, ga, qb, kb, ckb, gb, vt, cva, cvb,
                  n_lat, n_past)
    out_tiles = n_lat // OUTPROJ_TM
    y_sample = _outproj(o, w_out_b, xs2, mod3, lambda i: 1 + i // out_tiles, gf).reshape(bs, n_lat, d)

    return (y_prompt, y_sample, new_diff_k, new_diff_v, new_gqa_k, new_gqa_v)
```

```python
import functools
import math

import jax
import jax.numpy as jnp
import numpy as np
from jax import lax
from jax.experimental import pallas as pl
from jax.experimental.pallas import tpu as pltpu

D_MODEL = 2048
GRID_W = 64
ROPE_THETA = 10000.0
EPS = 1e-6
H_A = 8
DK_A = 64
DV_A = 2 * DK_A
H_B = 8
KV_B = 2
DH_B = 128
G_B = H_B // KV_B
HEAD_W = 128
A_Q = H_A * 2 * DK_A
A_K = H_A * 2 * DK_A
A_V = H_A * DV_A
A_G = H_A * DV_A
B_Q = H_B * DH_B
B_K = KV_B * DH_B
B_V = KV_B * DH_B
B_G = H_B * DH_B
D_IN = A_Q + A_K + A_V + A_G + B_Q + B_K + B_V + B_G
D_MIX = A_V + B_Q
OFF_AQ = 0
OFF_AK = OFF_AQ + A_Q
OFF_AV = OFF_AK + A_K
OFF_AG = OFF_AV + A_V
OFF_BQ = OFF_AG + A_G
OFF_BK = OFF_BQ + B_Q
OFF_BV = OFF_BK + B_K
OFF_BG = OFF_BV + B_V

N_COND = 8
ADA_TK = 256
INPROJ_TM = 256
INPROJ_CHUNK = 256
OUTPROJ_TM = 512
ATTN_TQ = 256
GQA_STACK = 2
SCORE_LOOKAHEAD = 2
QSCALE_A = math.log2(math.e) / math.sqrt(DK_A)
QSCALE_B = math.log2(math.e) / math.sqrt(DH_B)
VMEM_LIMIT = 56 * 1024 * 1024

BF16 = jnp.bfloat16
F32 = jnp.float32


def _params(n_grid_axes):
    return pltpu.CompilerParams(
        dimension_semantics=("arbitrary",) * n_grid_axes,
        vmem_limit_bytes=VMEM_LIMIT,
    )


def _silu(x):
    return x * jax.nn.sigmoid(x)


def _rms(x):
    return x * lax.rsqrt(jnp.mean(x * x, axis=-1, keepdims=True) + EPS)


def _adaln_body(cond_ref, w_ref, b_ref, o_ref):
    @pl.when(pl.program_id(0) == 0)
    def _():
        o_ref[...] = jnp.broadcast_to(b_ref[...], o_ref.shape)

    a = _silu(cond_ref[...]).astype(BF16)
    w = w_ref[...].astype(BF16)
    o_ref[...] += jnp.dot(a, w, preferred_element_type=F32)


def _adaln(cond, w_ada, b_ada):
    d3 = w_ada.shape[1]
    return pl.pallas_call(
        _adaln_body,
        out_shape=jax.ShapeDtypeStruct((N_COND, d3), F32),
        grid=(D_MODEL // ADA_TK,),
        in_specs=[
            pl.BlockSpec((N_COND, ADA_TK), lambda k: (0, k)),
            pl.BlockSpec((ADA_TK, d3), lambda k: (k, 0)),
            pl.BlockSpec((1, d3), lambda k: (0, 0)),
        ],
        out_specs=pl.BlockSpec((N_COND, d3), lambda k: (0, 0)),
        compiler_params=_params(1),
        name="adaln",
    )(cond, w_ada, b_ada)


def _rope_a(x, c, s_up, s_dn):
    return x * c + pltpu.roll(x, 96, 1) * s_up + pltpu.roll(x, 32, 1) * s_dn


def _rope_b(x, c, s):
    return x * c + pltpu.roll(x, 64, 1) * s


def _inproj_body(rope, x_ref, mod_ref, gn_ref, w_ref, qn_ref, kn_ref, *rest):
    if rope:
        ca_ref, sau_ref, sad_ref, cb_ref, sb_ref = rest[:5]
        rest = rest[5:]
    qa_o, ka_o, va_o, ga_o, qb_o, kb_o, vb_o, gb_o = rest
    cache_layout = not rope
    tm = x_ref.shape[0]

    x = x_ref[...]
    shift = mod_ref[:, 0:D_MODEL]
    scale = mod_ref[:, D_MODEL:2 * D_MODEL]
    h = ((_rms(x) * gn_ref[...]) * (1.0 + scale) + shift).astype(BF16)

    if rope:
        ca, sau, sad = ca_ref[...], sau_ref[...], sad_ref[...]
        cb, sb = cb_ref[...], sb_ref[...]

    def rope_a(t):
        return _rope_a(t, ca, sau, sad) if rope else t

    def rope_b(t):
        return _rope_b(t, cb, sb) if rope else t

    ident = lambda t: t
    regions = (
        (OFF_AQ, A_Q, qa_o, lambda t: rope_a(t) * QSCALE_A, False),
        (OFF_AK, A_K, ka_o, rope_a, True),
        (OFF_AV, A_V, va_o, ident, True),
        (OFF_AG, A_G, ga_o, _silu, False),
        (OFF_BQ, B_Q, qb_o, lambda t: rope_b(_rms(t) * qn_ref[...]) * QSCALE_B, False),
        (OFF_BK, B_K, kb_o, lambda t: rope_b(_rms(t) * kn_ref[...]), True),
        (OFF_BV, B_V, vb_o, ident, True),
        (OFF_BG, B_G, gb_o, _silu, False),
    )
    for start, width, o_ref, epi, is_kv in regions:
        n_heads = width // HEAD_W
        for c0 in range(0, width, INPROJ_CHUNK):
            z = jnp.dot(h, w_ref[:, start + c0:start + c0 + INPROJ_CHUNK],
                        preferred_element_type=F32)
            for h0 in range(0, INPROJ_CHUNK, HEAD_W):
                hd = (c0 + h0) // HEAD_W
                t = epi(z[:, h0:h0 + HEAD_W]).astype(o_ref.dtype)
                if is_kv and cache_layout:
                    o_ref[pl.ds(hd, tm, stride=n_heads), :] = t
                else:
                    o_ref[:, hd * HEAD_W:(hd + 1) * HEAD_W] = t


def _inproj(x2d, mod3, row_of_tile, g_norm, w_in_bf16, q_norm_g, k_norm_g, rope_tabs):
    t = x2d.shape[0]
    tm = INPROJ_TM
    rope = rope_tabs is not None
    row_spec = lambda w: pl.BlockSpec((tm, w), lambda i: (i, 0))
    const = lambda shape: pl.BlockSpec(shape, lambda i: (0,) * len(shape))
    in_specs = [
        row_spec(D_MODEL),
        pl.BlockSpec((None, 1, 3 * D_MODEL), lambda i: (row_of_tile(i), 0, 0)),
        const((1, D_MODEL)),
        pl.BlockSpec((D_MODEL, D_IN), lambda i: (0, 0), pipeline_mode=pl.Buffered(1)),
        const((1, HEAD_W)),
        const((1, HEAD_W)),
    ]
    args = [x2d, mod3, g_norm, w_in_bf16, q_norm_g, k_norm_g]
    if rope:
        n_pos_tiles = rope_tabs[0].shape[0] // tm
        in_specs += [pl.BlockSpec((tm, HEAD_W), lambda i: (i % n_pos_tiles, 0))] * 5
        args += list(rope_tabs)
    widths = (A_Q, A_K, A_V, A_G, B_Q, B_K, B_V, B_G)
    is_kv = (False, True, True, False, False, True, True, False)
    out_shape, out_specs = [], []
    for w, kv in zip(widths, is_kv):
        if kv and not rope:
            n_heads = w // HEAD_W
            out_shape.append(jax.ShapeDtypeStruct((t * n_heads, HEAD_W), F32))
            out_specs.append(pl.BlockSpec((tm * n_heads, HEAD_W), lambda i: (i, 0)))
        else:
            out_shape.append(jax.ShapeDtypeStruct((t, w), BF16))
            out_specs.append(row_spec(w))
    return pl.pallas_call(
        functools.partial(_inproj_body, rope),
        out_shape=out_shape,
        grid=(t // tm,),
        in_specs=in_specs,
        out_specs=out_specs,
        compiler_params=_params(1),
        name="inproj_rope" if rope else "inproj",
    )(*args)


def _diff_lambda_col(lq1_ref, lk1_ref, lq2_ref, lk2_ref, lam_init):
    s1 = jnp.sum(lq1_ref[...] * lk1_ref[...], axis=-1, keepdims=True)
    s2 = jnp.sum(lq2_ref[...] * lk2_ref[...], axis=-1, keepdims=True)
    return jnp.exp(s1) - jnp.exp(s2) + lam_init


def _cache_head(ref, head, n_tok, n_heads):
    return ref[pl.ds(head, n_tok, stride=n_heads), :].astype(BF16)


def _scores(q, k):
    return lax.dot_general(q, k, (((1,), (1,)), ((), ())), preferred_element_type=F32)


def _softmax_pv(s, v):
    m = jnp.max(s, axis=-1, keepdims=True)
    e = jnp.exp2(s - m)
    v1 = jnp.concatenate([v, jnp.ones_like(v)], axis=1)
    ol = jnp.dot(e.astype(BF16), v1, preferred_element_type=F32)
    return ol[:, :HEAD_W], ol[:, HEAD_W:]


def _attend(lam_col, subln, qa_ref, ga_ref, qb_ref, gb_ref, ka, va, kb, vb, o_ref):
    tq = qa_ref.shape[0]
    lane = lax.broadcasted_iota(jnp.int32, (tq, HEAD_W), 1)
    first = lane < DK_A

    def diff_scores(hd):
        q = qa_ref[:, hd * HEAD_W:(hd + 1) * HEAD_W]
        zero = jnp.zeros_like(q)
        q2 = jnp.concatenate([jnp.where(first, q, zero), jnp.where(first, zero, q)], axis=0)
        return _scores(q2, ka(hd))

    def diff_finish(hd, s):
        cols = slice(hd * HEAD_W, (hd + 1) * HEAD_W)
        o2, l2 = _softmax_pv(s, va(hd))
        r2 = 1.0 / l2
        lam = lam_col[hd:hd + 1, :]
        o = o2[:tq] * r2[:tq] - o2[tq:] * (lam * r2[tq:])
        o_ref[:, cols] = (_rms(o) * subln * ga_ref[:, cols].astype(F32)).astype(o_ref.dtype)

    def gqa_scores(h0):
        q = jnp.concatenate(
            [qb_ref[:, (h0 + g) * HEAD_W:(h0 + g + 1) * HEAD_W] for g in range(GQA_STACK)], axis=0)
        return _scores(q, kb(h0 // G_B))

    def gqa_finish(h0, s):
        o, l = _softmax_pv(s, vb(h0 // G_B))
        o = o * (1.0 / l)
        for g in range(GQA_STACK):
            cols = slice((h0 + g) * HEAD_W, (h0 + g + 1) * HEAD_W)
            gate = gb_ref[:, cols].astype(F32)
            o_ref[:, A_V + (h0 + g) * HEAD_W:A_V + (h0 + g + 1) * HEAD_W] = (
                o[g * tq:(g + 1) * tq, :] * gate).astype(o_ref.dtype)

    units = [(diff_scores, diff_finish, hd) for hd in range(H_A)]
    units += [(gqa_scores, gqa_finish, h0) for h0 in range(0, H_B, GQA_STACK)]
    scores = []
    for u, (_, finish, arg) in enumerate(units):
        while len(scores) < min(len(units), u + 1 + SCORE_LOOKAHEAD):
            issue, _, issue_arg = units[len(scores)]
            scores.append(issue(issue_arg))
        finish(arg, scores[u])
        scores[u] = None


def _attn_ctx_body(lam_init, lq1, lk1, lq2, lk2, subln_ref,
                   qa_ref, ka_ref, va_ref, ga_ref, qb_ref, kb_ref, vb_ref, gb_ref, o_ref):
    lam_col = _diff_lambda_col(lq1, lk1, lq2, lk2, lam_init)
    seq = qa_ref.shape[0]
    head = lambda ref, n_heads: (lambda i: _cache_head(ref, i, seq, n_heads))
    _attend(lam_col, subln_ref[...] * (1.0 - lam_init), qa_ref, ga_ref, qb_ref, gb_ref,
            head(ka_ref, H_A), head(va_ref, H_A), head(kb_ref, KV_B), head(vb_ref, KV_B), o_ref)


def _attn_ctx(lam_init, lam_params, subln_g, qa, ka, va, ga, qb, kb, vb, gb, seq):
    t = qa.shape[0]
    const = lambda shape: pl.BlockSpec(shape, lambda b: (0,) * len(shape))
    row_spec = lambda w: pl.BlockSpec((seq, w), lambda b: (b, 0))
    cache_spec = lambda n_heads: pl.BlockSpec((seq * n_heads, HEAD_W), lambda b: (b, 0))
    return pl.pallas_call(
        functools.partial(_attn_ctx_body, lam_init),
        out_shape=jax.ShapeDtypeStruct((t, D_MIX), BF16),
        grid=(t // seq,),
        in_specs=[const((H_A, DK_A))] * 4 + [const((1, DV_A))] + [
            row_spec(A_Q), cache_spec(H_A), cache_spec(H_A), row_spec(A_G),
            row_spec(B_Q), cache_spec(KV_B), cache_spec(KV_B), row_spec(B_G)],
        out_specs=row_spec(D_MIX),
        compiler_params=_params(1),
        name="attn_ctx",
    )(*lam_params, subln_g, qa, ka, va, ga, qb, kb, vb, gb)


def _attn_lat_body(lam_init, n_lat, lq1, lk1, lq2, lk2, subln_ref,
                   qa_ref, ka_ref, va_ref, cka_ref, cva_ref, ga_ref,
                   qb_ref, kb_ref, vb_ref, ckb_ref, cvb_ref, gb_ref, o_ref,
                   ka_all, va_all, kb_all, vb_all):
    @pl.when(pl.program_id(1) == 0)
    def _():
        for new_ref, cache_ref, all_ref in ((ka_ref, cka_ref, ka_all), (va_ref, cva_ref, va_all),
                                            (kb_ref, ckb_ref, kb_all), (vb_ref, cvb_ref, vb_all)):
            n_heads = new_ref.shape[1] // HEAD_W
            n_past = cache_ref.shape[0] // n_heads
            all_ref[0:n_lat, :] = new_ref[...]
            for hd in range(n_heads):
                all_ref[n_lat:, hd * HEAD_W:(hd + 1) * HEAD_W] = _cache_head(cache_ref, hd, n_past, n_heads)

    lam_col = _diff_lambda_col(lq1, lk1, lq2, lk2, lam_init)
    head = lambda ref: (lambda i: ref[:, i * HEAD_W:(i + 1) * HEAD_W])
    _attend(lam_col, subln_ref[...] * (1.0 - lam_init), qa_ref, ga_ref, qb_ref, gb_ref,
            head(ka_all), head(va_all), head(kb_all), head(vb_all), o_ref)


def _attn_lat(lam_init, lam_params, subln_g, qa, ka, va, cka, cva, ga, qb, kb, vb, ckb, cvb, gb,
              n_lat, n_past):
    t = qa.shape[0]
    tq = ATTN_TQ
    nq = n_lat // tq
    const = lambda shape: pl.BlockSpec(shape, lambda b, i: (0,) * len(shape))
    q_spec = lambda w: pl.BlockSpec((tq, w), lambda b, i: (b * nq + i, 0))
    new_spec = lambda w: pl.BlockSpec((n_lat, w), lambda b, i: (b, 0))
    cache_spec = lambda w: pl.BlockSpec((n_past * (w // HEAD_W), HEAD_W), lambda b, i: (b, 0))
    n_all = n_lat + n_past
    return pl.pallas_call(
        functools.partial(_attn_lat_body, lam_init, n_lat),
        out_shape=jax.ShapeDtypeStruct((t, D_MIX), BF16),
        grid=(t // n_lat, nq),
        in_specs=[const((H_A, DK_A))] * 4 + [const((1, DV_A))] + [
            q_spec(A_Q), new_spec(A_K), new_spec(A_V), cache_spec(A_K), cache_spec(A_V), q_spec(A_G),
            q_spec(B_Q), new_spec(B_K), new_spec(B_V), cache_spec(B_K), cache_spec(B_V), q_spec(B_G)],
        out_specs=q_spec(D_MIX),
        scratch_shapes=[pltpu.VMEM((n_all, A_K), BF16), pltpu.VMEM((n_all, A_V), BF16),
                        pltpu.VMEM((n_all, B_K), BF16), pltpu.VMEM((n_all, B_V), BF16)],
        compiler_params=_params(2),
        name="attn_lat",
    )(*lam_params, subln_g, qa, ka, va, cka, cva, ga, qb, kb, vb, ckb, cvb, gb)


def _outproj_body(o_ref, w_ref, x_ref, mod_ref, gf_ref, y_ref):
    m = jnp.dot(o_ref[...], w_ref[...], preferred_element_type=F32)
    gate = mod_ref[:, 2 * D_MODEL:3 * D_MODEL]
    y_ref[...] = _rms(x_ref[...] + gate * m) * gf_ref[...]


def _outproj(o, w_out_bf16, x2d, mod3, row_of_tile, g_final):
    t = x2d.shape[0]
    tm = OUTPROJ_TM
    return pl.pallas_call(
        _outproj_body,
        out_shape=jax.ShapeDtypeStruct((t, D_MODEL), F32),
        grid=(t // tm,),
        in_specs=[
            pl.BlockSpec((tm, D_MIX), lambda i: (i, 0)),
            pl.BlockSpec((D_MIX, D_MODEL), lambda i: (0, 0), pipeline_mode=pl.Buffered(1)),
            pl.BlockSpec((tm, D_MODEL), lambda i: (i, 0)),
            pl.BlockSpec((None, 1, 3 * D_MODEL), lambda i: (row_of_tile(i), 0, 0)),
            pl.BlockSpec((1, D_MODEL), lambda i: (0, 0)),
        ],
        out_specs=pl.BlockSpec((tm, D_MODEL), lambda i: (i, 0)),
        compiler_params=_params(1),
        name="outproj",
    )(o, w_out_bf16, x2d, mod3, g_final)


def _rope_tables(n_tok):
    n_rows = n_tok // GRID_W
    rows = np.repeat(np.arange(n_rows, dtype=np.float32), GRID_W)
    cols = np.tile(np.arange(GRID_W, dtype=np.float32), n_rows)

    def cos_sin(rot_dim):
        quarter = rot_dim // 4
        inv_freq = (1.0 / (np.float32(ROPE_THETA) ** (np.arange(quarter, dtype=np.float32) / quarter))
                    ).astype(np.float32)
        ang = np.concatenate([rows[:, None] * inv_freq, cols[:, None] * inv_freq], axis=-1)
        return np.cos(ang).astype(np.float32), np.sin(ang).astype(np.float32)

    ca, sa = cos_sin(DK_A)
    za = np.zeros_like(sa)
    cb, sb = cos_sin(DH_B)
    tabs = (
        np.concatenate([ca, ca, ca, ca], axis=-1),
        np.concatenate([-sa, za, -sa, za], axis=-1),
        np.concatenate([za, sa, za, sa], axis=-1),
        np.concatenate([cb, cb], axis=-1),
        np.concatenate([-sb, sb], axis=-1),
    )
    return tuple(jnp.asarray(t) for t in tabs)


def kernel(x_prompt, x_sample, cache_diff_k, cache_diff_v, cache_gqa_k, cache_gqa_v, c, c_ctx,
           w_ada, b_ada, g_norm, w_in, lam_q1, lam_k1, lam_q2, lam_k2, subln_g, q_norm_g, k_norm_g,
           w_out, g_final):
    bp, n_ctx, d = x_prompt.shape
    bs, n_lat, _ = x_sample.shape
    depth = w_in.shape[0]
    n_past = cache_diff_k.shape[2]
    assert depth == 1 and d == D_MODEL and bs + 1 <= N_COND
    assert n_lat % INPROJ_TM == 0 and n_lat % OUTPROJ_TM == 0 and n_lat % ATTN_TQ == 0
    l = 0
    lam_init = 0.8 - 0.6 * math.exp(-0.3 * l)

    cond = jnp.concatenate(
        [c_ctx[None, :], c, jnp.zeros((N_COND - 1 - bs, d), F32)], axis=0)
    mod = _adaln(cond, w_ada[l], b_ada[l][None, :])
    mod3 = mod[:, None, :]

    w_in_b = w_in[l].astype(BF16)
    w_out_b = w_out[l].astype(BF16)
    gn = g_norm[l][None, :]
    qn = q_norm_g[l][None, :]
    kn = k_norm_g[l][None, :]
    lam_params = (lam_q1[l], lam_k1[l], lam_q2[l], lam_k2[l])
    subln = subln_g[l][None, :]
    gf = g_final[None, :]

    xp2 = x_prompt.reshape(bp * n_ctx, d)
    qa, ka, va, ga, qb, kb, vb, gb = _inproj(
        xp2, mod3, lambda i: 0, gn, w_in_b, qn, kn, None)
    o = _attn_ctx(lam_init, lam_params, subln, qa, ka, va, ga, qb, kb, vb, gb, n_ctx)
    y_prompt = _outproj(o, w_out_b, xp2, mod3, lambda i: 0, gf).reshape(bp, n_ctx, d)
    new_diff_k = ka.reshape(bp, 1, n_ctx, H_A, 2 * DK_A)
    new_diff_v = va.reshape(bp, 1, n_ctx, H_A, DV_A)
    new_gqa_k = kb.reshape(bp, 1, n_ctx, KV_B, DH_B)
    new_gqa_v = vb.reshape(bp, 1, n_ctx, KV_B, DH_B)

    xs2 = x_sample.reshape(bs * n_lat, d)
    in_tiles = n_lat // INPROJ_TM
    qa, ka, va, ga, qb, kb, vb, gb = _inproj(
        xs2, mod3, lambda i: 1 + i // in_tiles, gn, w_in_b, qn, kn, _rope_tables(n_lat))
    cka = cache_diff_k[:, l].reshape(bs * n_past * H_A, HEAD_W)
    cva = cache_diff_v[:, l].reshape(bs * n_past * H_A, HEAD_W)
    ckb = cache_gqa_k[:, l].reshape(bs * n_past * KV_B, HEAD_W)
    cvb = cache_gqa_v[:, l].reshape(bs * n_past * KV_B, HEAD_W)
    o = _attn_lat(lam_init, lam_params, subln, qa, ka, va, cka, cva, ga, qb, kb, vb, ckb, cvb, gb,
                  n_lat, n_past)
    out_tiles = n_lat // OUTPROJ_TM
    y_sample = _outproj(o, w_out_b, xs2, mod3, lambda i: 1 + i // out_tiles, gf).reshape(bs, n_lat, d)

    return (y_prompt, y_sample, new_diff_k, new_diff_v, new_gqa_k, new_gqa_v)
```

```python
import functools
import math

import jax
import jax.numpy as jnp
import numpy as np
from jax import lax
from jax.experimental import pallas as pl
from jax.experimental.pallas import tpu as pltpu

D_MODEL = 2048
GRID_W = 64
ROPE_THETA = 10000.0
EPS = 1e-6
H_A = 8
DK_A = 64
DV_A = 2 * DK_A
H_B = 8
KV_B = 2
DH_B = 128
G_B = H_B // KV_B
HEAD_W = 128
A_Q = H_A * 2 * DK_A
A_K = H_A * 2 * DK_A
A_V = H_A * DV_A
A_G = H_A * DV_A
B_Q = H_B * DH_B
B_K = KV_B * DH_B
B_V = KV_B * DH_B
B_G = H_B * DH_B
D_IN = A_Q + A_K + A_V + A_G + B_Q + B_K + B_V + B_G
D_MIX = A_V + B_Q
OFF_AQ = 0
OFF_AK = OFF_AQ + A_Q
OFF_AV = OFF_AK + A_K
OFF_AG = OFF_AV + A_V
OFF_BQ = OFF_AG + A_G
OFF_BK = OFF_BQ + B_Q
OFF_BV = OFF_BK + B_K
OFF_BG = OFF_BV + B_V

N_COND = 8
ADA_TK = 256
INPROJ_TM = 256
W_CHUNK = 256
OUTPROJ_TM = 512
ATTN_TQ = 256
GQA_STACK = 2
SCORE_LOOKAHEAD = 2
QSCALE_A = math.log2(math.e) / math.sqrt(DK_A)
QSCALE_B = math.log2(math.e) / math.sqrt(DH_B)
VMEM_LIMIT = 56 * 1024 * 1024

BF16 = jnp.bfloat16
F32 = jnp.float32


def _params(n_grid_axes):
    return pltpu.CompilerParams(
        dimension_semantics=("arbitrary",) * n_grid_axes,
        vmem_limit_bytes=VMEM_LIMIT,
    )


def _silu(x):
    return x * jax.nn.sigmoid(x)


def _rms(x):
    return x * lax.rsqrt(jnp.mean(x * x, axis=-1, keepdims=True) + EPS)


def _weight_scratch(n_cols):
    return [pltpu.VMEM((n_cols // W_CHUNK, D_MODEL, W_CHUNK), BF16),
            pltpu.VMEM((2, D_MODEL, W_CHUNK), F32),
            pltpu.SemaphoreType.DMA((2,))]


def _chunk_copy(w_hbm, w_stage, sem, c):
    slot = c % 2
    return pltpu.make_async_copy(
        w_hbm.at[:, pl.ds(c * W_CHUNK, W_CHUNK)], w_stage.at[slot], sem.at[slot])


def _adaln_body(cond_ref, w_ref, b_ref, o_ref):
    @pl.when(pl.program_id(0) == 0)
    def _():
        o_ref[...] = jnp.broadcast_to(b_ref[...], o_ref.shape)

    a = _silu(cond_ref[...]).astype(BF16)
    w = w_ref[...].astype(BF16)
    o_ref[...] += jnp.dot(a, w, preferred_element_type=F32)


def _adaln(cond, w_ada, b_ada):
    d3 = w_ada.shape[1]
    return pl.pallas_call(
        _adaln_body,
        out_shape=jax.ShapeDtypeStruct((N_COND, d3), F32),
        grid=(D_MODEL // ADA_TK,),
        in_specs=[
            pl.BlockSpec((N_COND, ADA_TK), lambda k: (0, k)),
            pl.BlockSpec((ADA_TK, d3), lambda k: (k, 0)),
            pl.BlockSpec((1, d3), lambda k: (0, 0)),
        ],
        out_specs=pl.BlockSpec((N_COND, d3), lambda k: (0, 0)),
        compiler_params=_params(1),
        name="adaln",
    )(cond, w_ada, b_ada)


def _rope_a(x, c, s_up, s_dn):
    return x * c + pltpu.roll(x, 96, 1) * s_up + pltpu.roll(x, 32, 1) * s_dn


def _rope_b(x, c, s):
    return x * c + pltpu.roll(x, 64, 1) * s


def _inproj_body(rope, x_ref, mod_ref, gn_ref, w_hbm, qn_ref, kn_ref, *rest):
    if rope:
        ca_ref, sau_ref, sad_ref, cb_ref, sb_ref = rest[:5]
        rest = rest[5:]
    qa_o, ka_o, va_o, ga_o, qb_o, kb_o, vb_o, gb_o, w_scr, w_stage, sem = rest
    cache_layout = not rope
    tm = x_ref.shape[0]

    x = x_ref[...]
    shift = mod_ref[:, 0:D_MODEL]
    scale = mod_ref[:, D_MODEL:2 * D_MODEL]
    h = ((_rms(x) * gn_ref[...]) * (1.0 + scale) + shift).astype(BF16)

    if rope:
        ca, sau, sad = ca_ref[...], sau_ref[...], sad_ref[...]
        cb, sb = cb_ref[...], sb_ref[...]

    def rope_a(t):
        return _rope_a(t, ca, sau, sad) if rope else t

    def rope_b(t):
        return _rope_b(t, cb, sb) if rope else t

    ident = lambda t: t
    regions = (
        (OFF_AQ, A_Q, qa_o, lambda t: rope_a(t) * QSCALE_A, False),
        (OFF_AK, A_K, ka_o, rope_a, True),
        (OFF_AV, A_V, va_o, ident, True),
        (OFF_AG, A_G, ga_o, _silu, False),
        (OFF_BQ, B_Q, qb_o, lambda t: rope_b(_rms(t) * qn_ref[...]) * QSCALE_B, False),
        (OFF_BK, B_K, kb_o, lambda t: rope_b(_rms(t) * kn_ref[...]), True),
        (OFF_BV, B_V, vb_o, ident, True),
        (OFF_BG, B_G, gb_o, _silu, False),
    )
    def project(stream):
        copy = functools.partial(_chunk_copy, w_hbm, w_stage, sem)
        if stream:
            copy(0).start()
        for start, width, o_ref, epi, is_kv in regions:
            n_heads = width // HEAD_W
            for c0 in range(0, width, W_CHUNK):
                c = (start + c0) // W_CHUNK
                if stream:
                    if c + 1 < D_IN // W_CHUNK:
                        copy(c + 1).start()
                    copy(c).wait()
                    w = w_stage[c % 2].astype(BF16)
                    w_scr[c] = w
                else:
                    w = w_scr[c]
                z = jnp.dot(h, w, preferred_element_type=F32)
                for h0 in range(0, W_CHUNK, HEAD_W):
                    hd = (c0 + h0) // HEAD_W
                    t = epi(z[:, h0:h0 + HEAD_W]).astype(o_ref.dtype)
                    if is_kv and cache_layout:
                        o_ref[pl.ds(hd, tm, stride=n_heads), :] = t
                    else:
                        o_ref[:, hd * HEAD_W:(hd + 1) * HEAD_W] = t

    first_step = pl.program_id(0) == 0
    pl.when(first_step)(functools.partial(project, True))
    pl.when(jnp.logical_not(first_step))(functools.partial(project, False))


def _inproj(x2d, mod3, row_of_tile, g_norm, w_in, q_norm_g, k_norm_g, rope_tabs):
    t = x2d.shape[0]
    tm = INPROJ_TM
    rope = rope_tabs is not None
    row_spec = lambda w: pl.BlockSpec((tm, w), lambda i: (i, 0))
    const = lambda shape: pl.BlockSpec(shape, lambda i: (0,) * len(shape))
    in_specs = [
        row_spec(D_MODEL),
        pl.BlockSpec((None, 1, 3 * D_MODEL), lambda i: (row_of_tile(i), 0, 0)),
        const((1, D_MODEL)),
        pl.BlockSpec(memory_space=pl.ANY),
        const((1, HEAD_W)),
        const((1, HEAD_W)),
    ]
    args = [x2d, mod3, g_norm, w_in, q_norm_g, k_norm_g]
    if rope:
        n_pos_tiles = rope_tabs[0].shape[0] // tm
        in_specs += [pl.BlockSpec((tm, HEAD_W), lambda i: (i % n_pos_tiles, 0))] * 5
        args += list(rope_tabs)
    widths = (A_Q, A_K, A_V, A_G, B_Q, B_K, B_V, B_G)
    is_kv = (False, True, True, False, False, True, True, False)
    out_shape, out_specs = [], []
    for w, kv in zip(widths, is_kv):
        if kv and not rope:
            n_heads = w // HEAD_W
            out_shape.append(jax.ShapeDtypeStruct((t * n_heads, HEAD_W), F32))
            out_specs.append(pl.BlockSpec((tm * n_heads, HEAD_W), lambda i: (i, 0)))
        else:
            out_shape.append(jax.ShapeDtypeStruct((t, w), BF16))
            out_specs.append(row_spec(w))
    return pl.pallas_call(
        functools.partial(_inproj_body, rope),
        out_shape=out_shape,
        grid=(t // tm,),
        in_specs=in_specs,
        out_specs=out_specs,
        scratch_shapes=_weight_scratch(D_IN),
        compiler_params=_params(1),
        name="inproj_rope" if rope else "inproj",
    )(*args)


def _diff_lambda_col(lq1_ref, lk1_ref, lq2_ref, lk2_ref, lam_init):
    s1 = jnp.sum(lq1_ref[...] * lk1_ref[...], axis=-1, keepdims=True)
    s2 = jnp.sum(lq2_ref[...] * lk2_ref[...], axis=-1, keepdims=True)
    return jnp.exp(s1) - jnp.exp(s2) + lam_init


def _cache_head(ref, head, n_tok, n_heads):
    return ref[pl.ds(head, n_tok, stride=n_heads), :].astype(BF16)


def _scores(q, k):
    return lax.dot_general(q, k, (((1,), (1,)), ((), ())), preferred_element_type=F32)


def _softmax_pv(s, v):
    m = jnp.max(s, axis=-1, keepdims=True)
    e = jnp.exp2(s - m)
    v1 = jnp.concatenate([v, jnp.ones_like(v)], axis=1)
    ol = jnp.dot(e.astype(BF16), v1, preferred_element_type=F32)
    return ol[:, :HEAD_W], ol[:, HEAD_W:]


def _attend(lam_col, subln, qa_ref, ga_ref, qb_ref, gb_ref, ka, va, kb, vb, o_ref):
    tq = qa_ref.shape[0]
    lane = lax.broadcasted_iota(jnp.int32, (tq, HEAD_W), 1)
    first = lane < DK_A

    def diff_scores(hd):
        q = qa_ref[:, hd * HEAD_W:(hd + 1) * HEAD_W]
        zero = jnp.zeros_like(q)
        q2 = jnp.concatenate([jnp.where(first, q, zero), jnp.where(first, zero, q)], axis=0)
        return _scores(q2, ka(hd))

    def diff_finish(hd, s):
        cols = slice(hd * HEAD_W, (hd + 1) * HEAD_W)
        o2, l2 = _softmax_pv(s, va(hd))
        r2 = 1.0 / l2
        lam = lam_col[hd:hd + 1, :]
        o = o2[:tq] * r2[:tq] - o2[tq:] * (lam * r2[tq:])
        o_ref[:, cols] = (_rms(o) * subln * ga_ref[:, cols].astype(F32)).astype(o_ref.dtype)

    def gqa_scores(h0):
        q = jnp.concatenate(
            [qb_ref[:, (h0 + g) * HEAD_W:(h0 + g + 1) * HEAD_W] for g in range(GQA_STACK)], axis=0)
        return _scores(q, kb(h0 // G_B))

    def gqa_finish(h0, s):
        o, l = _softmax_pv(s, vb(h0 // G_B))
        o = o * (1.0 / l)
        for g in range(GQA_STACK):
            cols = slice((h0 + g) * HEAD_W, (h0 + g + 1) * HEAD_W)
            gate = gb_ref[:, cols].astype(F32)
            o_ref[:, A_V + (h0 + g) * HEAD_W:A_V + (h0 + g + 1) * HEAD_W] = (
                o[g * tq:(g + 1) * tq, :] * gate).astype(o_ref.dtype)

    units = [(diff_scores, diff_finish, hd) for hd in range(H_A)]
    units += [(gqa_scores, gqa_finish, h0) for h0 in range(0, H_B, GQA_STACK)]
    scores = []
    for u, (_, finish, arg) in enumerate(units):
        while len(scores) < min(len(units), u + 1 + SCORE_LOOKAHEAD):
            issue, _, issue_arg = units[len(scores)]
            scores.append(issue(issue_arg))
        finish(arg, scores[u])
        scores[u] = None


def _attn_ctx_body(lam_init, lq1, lk1, lq2, lk2, subln_ref,
                   qa_ref, ka_ref, va_ref, ga_ref, qb_ref, kb_ref, vb_ref, gb_ref, o_ref):
    lam_col = _diff_lambda_col(lq1, lk1, lq2, lk2, lam_init)
    seq = qa_ref.shape[0]
    head = lambda ref, n_heads: (lambda i: _cache_head(ref, i, seq, n_heads))
    _attend(lam_col, subln_ref[...] * (1.0 - lam_init), qa_ref, ga_ref, qb_ref, gb_ref,
            head(ka_ref, H_A), head(va_ref, H_A), head(kb_ref, KV_B), head(vb_ref, KV_B), o_ref)


def _attn_ctx(lam_init, lam_params, subln_g, qa, ka, va, ga, qb, kb, vb, gb, seq):
    t = qa.shape[0]
    const = lambda shape: pl.BlockSpec(shape, lambda b: (0,) * len(shape))
    row_spec = lambda w: pl.BlockSpec((seq, w), lambda b: (b, 0))
    cache_spec = lambda n_heads: pl.BlockSpec((seq * n_heads, HEAD_W), lambda b: (b, 0))
    return pl.pallas_call(
        functools.partial(_attn_ctx_body, lam_init),
        out_shape=jax.ShapeDtypeStruct((t, D_MIX), BF16),
        grid=(t // seq,),
        in_specs=[const((H_A, DK_A))] * 4 + [const((1, DV_A))] + [
            row_spec(A_Q), cache_spec(H_A), cache_spec(H_A), row_spec(A_G),
            row_spec(B_Q), cache_spec(KV_B), cache_spec(KV_B), row_spec(B_G)],
        out_specs=row_spec(D_MIX),
        compiler_params=_params(1),
        name="attn_ctx",
    )(*lam_params, subln_g, qa, ka, va, ga, qb, kb, vb, gb)


def _attn_lat_body(lam_init, n_lat, lq1, lk1, lq2, lk2, subln_ref,
                   qa_ref, ka_ref, va_ref, cka_ref, cva_ref, ga_ref,
                   qb_ref, kb_ref, vb_ref, ckb_ref, cvb_ref, gb_ref, o_ref,
                   ka_all, va_all, kb_all, vb_all):
    @pl.when(pl.program_id(1) == 0)
    def _():
        for new_ref, cache_ref, all_ref in ((ka_ref, cka_ref, ka_all), (va_ref, cva_ref, va_all),
                                            (kb_ref, ckb_ref, kb_all), (vb_ref, cvb_ref, vb_all)):
            n_heads = new_ref.shape[1] // HEAD_W
            n_past = cache_ref.shape[0] // n_heads
            all_ref[0:n_lat, :] = new_ref[...]
            for hd in range(n_heads):
                all_ref[n_lat:, hd * HEAD_W:(hd + 1) * HEAD_W] = _cache_head(cache_ref, hd, n_past, n_heads)

    lam_col = _diff_lambda_col(lq1, lk1, lq2, lk2, lam_init)
    head = lambda ref: (lambda i: ref[:, i * HEAD_W:(i + 1) * HEAD_W])
    _attend(lam_col, subln_ref[...] * (1.0 - lam_init), qa_ref, ga_ref, qb_ref, gb_ref,
            head(ka_all), head(va_all), head(kb_all), head(vb_all), o_ref)


def _attn_lat(lam_init, lam_params, subln_g, qa, ka, va, cka, cva, ga, qb, kb, vb, ckb, cvb, gb,
              n_lat, n_past):
    t = qa.shape[0]
    tq = ATTN_TQ
    nq = n_lat // tq
    const = lambda shape: pl.BlockSpec(shape, lambda b, i: (0,) * len(shape))
    q_spec = lambda w: pl.BlockSpec((tq, w), lambda b, i: (b * nq + i, 0))
    new_spec = lambda w: pl.BlockSpec((n_lat, w), lambda b, i: (b, 0))
    cache_spec = lambda w: pl.BlockSpec((n_past * (w // HEAD_W), HEAD_W), lambda b, i: (b, 0))
    n_all = n_lat + n_past
    return pl.pallas_call(
        functools.partial(_attn_lat_body, lam_init, n_lat),
        out_shape=jax.ShapeDtypeStruct((t, D_MIX), BF16),
        grid=(t // n_lat, nq),
        in_specs=[const((H_A, DK_A))] * 4 + [const((1, DV_A))] + [
            q_spec(A_Q), new_spec(A_K), new_spec(A_V), cache_spec(A_K), cache_spec(A_V), q_spec(A_G),
            q_spec(B_Q), new_spec(B_K), new_spec(B_V), cache_spec(B_K), cache_spec(B_V), q_spec(B_G)],
        out_specs=q_spec(D_MIX),
        scratch_shapes=[pltpu.VMEM((n_all, A_K), BF16), pltpu.VMEM((n_all, A_V), BF16),
                        pltpu.VMEM((n_all, B_K), BF16), pltpu.VMEM((n_all, B_V), BF16)],
        compiler_params=_params(2),
        name="attn_lat",
    )(*lam_params, subln_g, qa, ka, va, cka, cva, ga, qb, kb, vb, ckb, cvb, gb)


def _outproj_body(o_ref, w_hbm, x_ref, mod_ref, gf_ref, y_ref, w_scr, w_stage, sem):
    def project(stream):
        copy = functools.partial(_chunk_copy, w_hbm, w_stage, sem)
        n_chunks = D_MODEL // W_CHUNK
        o = o_ref[...]
        if stream:
            copy(0).start()
        parts = []
        for c in range(n_chunks):
            if stream:
                if c + 1 < n_chunks:
                    copy(c + 1).start()
                copy(c).wait()
                w = w_stage[c % 2].astype(BF16)
                w_scr[c] = w
            else:
                w = w_scr[c]
            parts.append(jnp.dot(o, w, preferred_element_type=F32))
        m = jnp.concatenate(parts, axis=1)
        gate = mod_ref[:, 2 * D_MODEL:3 * D_MODEL]
        y_ref[...] = _rms(x_ref[...] + gate * m) * gf_ref[...]

    first_step = pl.program_id(0) == 0
    pl.when(first_step)(functools.partial(project, True))
    pl.when(jnp.logical_not(first_step))(functools.partial(project, False))


def _outproj(o, w_out, x2d, mod3, row_of_tile, g_final):
    t = x2d.shape[0]
    tm = OUTPROJ_TM
    return pl.pallas_call(
        _outproj_body,
        out_shape=jax.ShapeDtypeStruct((t, D_MODEL), F32),
        grid=(t // tm,),
        in_specs=[
            pl.BlockSpec((tm, D_MIX), lambda i: (i, 0)),
            pl.BlockSpec(memory_space=pl.ANY),
            pl.BlockSpec((tm, D_MODEL), lambda i: (i, 0)),
            pl.BlockSpec((None, 1, 3 * D_MODEL), lambda i: (row_of_tile(i), 0, 0)),
            pl.BlockSpec((1, D_MODEL), lambda i: (0, 0)),
        ],
        out_specs=pl.BlockSpec((tm, D_MODEL), lambda i: (i, 0)),
        scratch_shapes=_weight_scratch(D_MODEL),
        compiler_params=_params(1),
        name="outproj",
    )(o, w_out, x2d, mod3, g_final)


def _rope_tables(n_tok):
    n_rows = n_tok // GRID_W
    rows = np.repeat(np.arange(n_rows, dtype=np.float32), GRID_W)
    cols = np.tile(np.arange(GRID_W, dtype=np.float32), n_rows)

    def cos_sin(rot_dim):
        quarter = rot_dim // 4
        inv_freq = (1.0 / (np.float32(ROPE_THETA) ** (np.arange(quarter, dtype=np.float32) / quarter))
                    ).astype(np.float32)
        ang = np.concatenate([rows[:, None] * inv_freq, cols[:, None] * inv_freq], axis=-1)
        return np.cos(ang).astype(np.float32), np.sin(ang).astype(np.float32)

    ca, sa = cos_sin(DK_A)
    za = np.zeros_like(sa)
    cb, sb = cos_sin(DH_B)
    tabs = (
        np.concatenate([ca, ca, ca, ca], axis=-1),
        np.concatenate([-sa, za, -sa, za], axis=-1),
        np.concatenate([za, sa, za, sa], axis=-1),
        np.concatenate([cb, cb], axis=-1),
        np.concatenate([-sb, sb], axis=-1),
    )
    return tuple(jnp.asarray(t) for t in tabs)


def kernel(x_prompt, x_sample, cache_diff_k, cache_diff_v, cache_gqa_k, cache_gqa_v, c, c_ctx,
           w_ada, b_ada, g_norm, w_in, lam_q1, lam_k1, lam_q2, lam_k2, subln_g, q_norm_g, k_norm_g,
           w_out, g_final):
    bp, n_ctx, d = x_prompt.shape
    bs, n_lat, _ = x_sample.shape
    depth = w_in.shape[0]
    n_past = cache_diff_k.shape[2]
    assert depth == 1 and d == D_MODEL and bs + 1 <= N_COND
    assert n_lat % INPROJ_TM == 0 and n_lat % OUTPROJ_TM == 0 and n_lat % ATTN_TQ == 0
    l = 0
    lam_init = 0.8 - 0.6 * math.exp(-0.3 * l)

    cond = jnp.concatenate(
        [c_ctx[None, :], c, jnp.zeros((N_COND - 1 - bs, d), F32)], axis=0)
    mod = _adaln(cond, w_ada[l], b_ada[l][None, :])
    mod3 = mod[:, None, :]

    gn = g_norm[l][None, :]
    qn = q_norm_g[l][None, :]
    kn = k_norm_g[l][None, :]
    lam_params = (lam_q1[l], lam_k1[l], lam_q2[l], lam_k2[l])
    subln = subln_g[l][None, :]
    gf = g_final[None, :]

    xp2 = x_prompt.reshape(bp * n_ctx, d)
    qa, ka, va, ga, qb, kb, vb, gb = _inproj(
        xp2, mod3, lambda i: 0, gn, w_in[l], qn, kn, None)
    o = _attn_ctx(lam_init, lam_params, subln, qa, ka, va, ga, qb, kb, vb, gb, n_ctx)
    y_prompt = _outproj(o, w_out[l], xp2, mod3, lambda i: 0, gf).reshape(bp, n_ctx, d)
    new_diff_k = ka.reshape(bp, 1, n_ctx, H_A, 2 * DK_A)
    new_diff_v = va.reshape(bp, 1, n_ctx, H_A, DV_A)
    new_gqa_k = kb.reshape(bp, 1, n_ctx, KV_B, DH_B)
    new_gqa_v = vb.reshape(bp, 1, n_ctx, KV_B, DH_B)

    xs2 = x_sample.reshape(bs * n_lat, d)
    in_tiles = n_lat // INPROJ_TM
    qa, ka, va, ga, qb, kb, vb, gb = _inproj(
        xs2, mod3, lambda i: 1 + i // in_tiles, gn, w_in[l], qn, kn, _rope_tables(n_lat))
    cka = cache_diff_k[:, l].reshape(bs * n_past * H_A, HEAD_W)
    cva = cache_diff_v[:, l].reshape(bs * n_past * H_A, HEAD_W)
    ckb = cache_gqa_k[:, l].reshape(bs * n_past * KV_B, HEAD_W)
    cvb = cache_gqa_v[:, l].reshape(bs * n_past * KV_B, HEAD_W)
    o = _attn_lat(lam_init, lam_params, subln, qa, ka, va, cka, cva, ga, qb, kb, vb, ckb, cvb, gb,
                  n_lat, n_past)
    out_tiles = n_lat // OUTPROJ_TM
    y_sample = _outproj(o, w_out[l], xs2, mod3, lambda i: 1 + i // out_tiles, gf).reshape(bs, n_lat, d)

    return (y_prompt, y_sample, new_diff_k, new_diff_v, new_gqa_k, new_gqa_v)
```

```python
import functools
import math

import jax
import jax.numpy as jnp
import numpy as np
from jax import lax
from jax.experimental import pallas as pl
from jax.experimental.pallas import tpu as pltpu

D_MODEL = 2048
GRID_W = 64
ROPE_THETA = 10000.0
EPS = 1e-6
H_A = 8
DK_A = 64
DV_A = 2 * DK_A
H_B = 8
KV_B = 2
DH_B = 128
G_B = H_B // KV_B
HEAD_W = 128
A_Q = H_A * 2 * DK_A
A_K = H_A * 2 * DK_A
A_V = H_A * DV_A
A_G = H_A * DV_A
B_Q = H_B * DH_B
B_K = KV_B * DH_B
B_V = KV_B * DH_B
B_G = H_B * DH_B
D_IN = A_Q + A_K + A_V + A_G + B_Q + B_K + B_V + B_G
D_MIX = A_V + B_Q
OFF_AQ = 0
OFF_AK = OFF_AQ + A_Q
OFF_AV = OFF_AK + A_K
OFF_AG = OFF_AV + A_V
OFF_BQ = OFF_AG + A_G
OFF_BK = OFF_BQ + B_Q
OFF_BV = OFF_BK + B_K
OFF_BG = OFF_BV + B_V

N_COND = 8
ADA_TK = 256
INPROJ_TM = 256
W_CHUNK = 256
OUTPROJ_TM = 512
ATTN_TQ = 256
GQA_STACK = 2
SCORE_LOOKAHEAD = 2
QSCALE_A = math.log2(math.e) / math.sqrt(DK_A)
QSCALE_B = math.log2(math.e) / math.sqrt(DH_B)
VMEM_LIMIT = 56 * 1024 * 1024

BF16 = jnp.bfloat16
F32 = jnp.float32


def _params(n_grid_axes):
    return pltpu.CompilerParams(
        dimension_semantics=("arbitrary",) * n_grid_axes,
        vmem_limit_bytes=VMEM_LIMIT,
    )


def _silu(x):
    return x * jax.nn.sigmoid(x)


def _rms(x):
    return x * lax.rsqrt(jnp.mean(x * x, axis=-1, keepdims=True) + EPS)


def _weight_scratch(n_cols):
    return [pltpu.VMEM((n_cols // W_CHUNK, D_MODEL, W_CHUNK), BF16),
            pltpu.VMEM((2, D_MODEL, W_CHUNK), F32),
            pltpu.SemaphoreType.DMA((2,)),
            pltpu.SemaphoreType.DMA(())]


def _chunk_copy(w_hbm, w_stage, sem, c):
    slot = c % 2
    return pltpu.make_async_copy(
        w_hbm.at[:, pl.ds(c * W_CHUNK, W_CHUNK)], w_stage.at[slot], sem.at[slot])


def _stage_weights(w_hbm, wbf_hbm, w_scr, w_stage, sem, sem_out):
    n_chunks = w_scr.shape[0]
    copy = functools.partial(_chunk_copy, w_hbm, w_stage, sem)
    handoff = pltpu.make_async_copy(w_scr, wbf_hbm, sem_out)

    @pl.when(pl.program_id(0) == 0)
    def _():
        copy(0).start()
        for c in range(n_chunks):
            if c + 1 < n_chunks:
                copy(c + 1).start()
            copy(c).wait()
            w_scr[c] = w_stage[c % 2].astype(BF16)
        handoff.start()

    return handoff


def _weight_spec(n_cols):
    return pl.BlockSpec((n_cols // W_CHUNK, D_MODEL, W_CHUNK), lambda i: (0, 0, 0),
                        pipeline_mode=pl.Buffered(1))


def _adaln_body(cond_ref, w_ref, b_ref, o_ref):
    @pl.when(pl.program_id(0) == 0)
    def _():
        o_ref[...] = jnp.broadcast_to(b_ref[...], o_ref.shape)

    a = _silu(cond_ref[...]).astype(BF16)
    w = w_ref[...].astype(BF16)
    o_ref[...] += jnp.dot(a, w, preferred_element_type=F32)


def _adaln(cond, w_ada, b_ada):
    d3 = w_ada.shape[1]
    return pl.pallas_call(
        _adaln_body,
        out_shape=jax.ShapeDtypeStruct((N_COND, d3), F32),
        grid=(D_MODEL // ADA_TK,),
        in_specs=[
            pl.BlockSpec((N_COND, ADA_TK), lambda k: (0, k)),
            pl.BlockSpec((ADA_TK, d3), lambda k: (k, 0)),
            pl.BlockSpec((1, d3), lambda k: (0, 0)),
        ],
        out_specs=pl.BlockSpec((N_COND, d3), lambda k: (0, 0)),
        compiler_params=_params(1),
        name="adaln",
    )(cond, w_ada, b_ada)


def _rope_a(x, c, s_up, s_dn):
    return x * c + pltpu.roll(x, 96, 1) * s_up + pltpu.roll(x, 32, 1) * s_dn


def _rope_b(x, c, s):
    return x * c + pltpu.roll(x, 64, 1) * s


def _inproj_body(rope, x_ref, mod_ref, gn_ref, w_in_ref, qn_ref, kn_ref, *rest):
    if rope:
        ca_ref, sau_ref, sad_ref, cb_ref, sb_ref = rest[:5]
        rest = rest[5:]
    qa_o, ka_o, va_o, ga_o, qb_o, kb_o, vb_o, gb_o = rest[:8]
    cache_layout = not rope
    if rope:
        weights = w_in_ref
    else:
        wbf_hbm, weights, w_stage, sem, sem_out = rest[8:]
        handoff = _stage_weights(w_in_ref, wbf_hbm, weights, w_stage, sem, sem_out)
    tm = x_ref.shape[0]

    x = x_ref[...]
    shift = mod_ref[:, 0:D_MODEL]
    scale = mod_ref[:, D_MODEL:2 * D_MODEL]
    h = ((_rms(x) * gn_ref[...]) * (1.0 + scale) + shift).astype(BF16)

    if rope:
        ca, sau, sad = ca_ref[...], sau_ref[...], sad_ref[...]
        cb, sb = cb_ref[...], sb_ref[...]

    def rope_a(t):
        return _rope_a(t, ca, sau, sad) if rope else t

    def rope_b(t):
        return _rope_b(t, cb, sb) if rope else t

    ident = lambda t: t
    regions = (
        (OFF_AQ, A_Q, qa_o, lambda t: rope_a(t) * QSCALE_A, False),
        (OFF_AK, A_K, ka_o, rope_a, True),
        (OFF_AV, A_V, va_o, ident, True),
        (OFF_AG, A_G, ga_o, _silu, False),
        (OFF_BQ, B_Q, qb_o, lambda t: rope_b(_rms(t) * qn_ref[...]) * QSCALE_B, False),
        (OFF_BK, B_K, kb_o, lambda t: rope_b(_rms(t) * kn_ref[...]), True),
        (OFF_BV, B_V, vb_o, ident, True),
        (OFF_BG, B_G, gb_o, _silu, False),
    )
    for start, width, o_ref, epi, is_kv in regions:
        n_heads = width // HEAD_W
        for c0 in range(0, width, W_CHUNK):
            z = jnp.dot(h, weights[(start + c0) // W_CHUNK], preferred_element_type=F32)
            for h0 in range(0, W_CHUNK, HEAD_W):
                hd = (c0 + h0) // HEAD_W
                t = epi(z[:, h0:h0 + HEAD_W]).astype(o_ref.dtype)
                if is_kv and cache_layout:
                    o_ref[pl.ds(hd, tm, stride=n_heads), :] = t
                else:
                    o_ref[:, hd * HEAD_W:(hd + 1) * HEAD_W] = t

    if not rope:
        pl.when(pl.program_id(0) == pl.num_programs(0) - 1)(handoff.wait)


def _inproj(x2d, mod3, row_of_tile, g_norm, w_in, q_norm_g, k_norm_g, rope_tabs):
    t = x2d.shape[0]
    tm = INPROJ_TM
    rope = rope_tabs is not None
    row_spec = lambda w: pl.BlockSpec((tm, w), lambda i: (i, 0))
    const = lambda shape: pl.BlockSpec(shape, lambda i: (0,) * len(shape))
    in_specs = [
        row_spec(D_MODEL),
        pl.BlockSpec((None, 1, 3 * D_MODEL), lambda i: (row_of_tile(i), 0, 0)),
        const((1, D_MODEL)),
        _weight_spec(D_IN) if rope else pl.BlockSpec(memory_space=pl.ANY),
        const((1, HEAD_W)),
        const((1, HEAD_W)),
    ]
    args = [x2d, mod3, g_norm, w_in, q_norm_g, k_norm_g]
    if rope:
        n_pos_tiles = rope_tabs[0].shape[0] // tm
        in_specs += [pl.BlockSpec((tm, HEAD_W), lambda i: (i % n_pos_tiles, 0))] * 5
        args += list(rope_tabs)
    widths = (A_Q, A_K, A_V, A_G, B_Q, B_K, B_V, B_G)
    is_kv = (False, True, True, False, False, True, True, False)
    out_shape, out_specs = [], []
    for w, kv in zip(widths, is_kv):
        if kv and not rope:
            n_heads = w // HEAD_W
            out_shape.append(jax.ShapeDtypeStruct((t * n_heads, HEAD_W), F32))
            out_specs.append(pl.BlockSpec((tm * n_heads, HEAD_W), lambda i: (i, 0)))
        else:
            out_shape.append(jax.ShapeDtypeStruct((t, w), BF16))
            out_specs.append(row_spec(w))
    if not rope:
        out_shape.append(jax.ShapeDtypeStruct((D_IN // W_CHUNK, D_MODEL, W_CHUNK), BF16))
        out_specs.append(pl.BlockSpec(memory_space=pl.ANY))
    return pl.pallas_call(
        functools.partial(_inproj_body, rope),
        out_shape=out_shape,
        grid=(t // tm,),
        in_specs=in_specs,
        out_specs=out_specs,
        scratch_shapes=[] if rope else _weight_scratch(D_IN),
        compiler_params=_params(1),
        name="inproj_rope" if rope else "inproj",
    )(*args)


def _diff_lambda_col(lq1_ref, lk1_ref, lq2_ref, lk2_ref, lam_init):
    s1 = jnp.sum(lq1_ref[...] * lk1_ref[...], axis=-1, keepdims=True)
    s2 = jnp.sum(lq2_ref[...] * lk2_ref[...], axis=-1, keepdims=True)
    return jnp.exp(s1) - jnp.exp(s2) + lam_init


def _cache_head(ref, head, n_tok, n_heads):
    return ref[pl.ds(head, n_tok, stride=n_heads), :].astype(BF16)


def _scores(q, k):
    return lax.dot_general(q, k, (((1,), (1,)), ((), ())), preferred_element_type=F32)


def _softmax_pv(s, v):
    m = jnp.max(s, axis=-1, keepdims=True)
    e = jnp.exp2(s - m)
    v1 = jnp.concatenate([v, jnp.ones_like(v)], axis=1)
    ol = jnp.dot(e.astype(BF16), v1, preferred_element_type=F32)
    return ol[:, :HEAD_W], ol[:, HEAD_W:]


def _attend(lam_col, subln, qa_ref, ga_ref, qb_ref, gb_ref, ka, va, kb, vb, o_ref):
    tq = qa_ref.shape[0]
    lane = lax.broadcasted_iota(jnp.int32, (tq, HEAD_W), 1)
    first = lane < DK_A

    def diff_scores(hd):
        q = qa_ref[:, hd * HEAD_W:(hd + 1) * HEAD_W]
        zero = jnp.zeros_like(q)
        q2 = jnp.concatenate([jnp.where(first, q, zero), jnp.where(first, zero, q)], axis=0)
        return _scores(q2, ka(hd))

    def diff_finish(hd, s):
        cols = slice(hd * HEAD_W, (hd + 1) * HEAD_W)
        o2, l2 = _softmax_pv(s, va(hd))
        r2 = 1.0 / l2
        lam = lam_col[hd:hd + 1, :]
        o = o2[:tq] * r2[:tq] - o2[tq:] * (lam * r2[tq:])
        o_ref[:, cols] = (_rms(o) * subln * ga_ref[:, cols].astype(F32)).astype(o_ref.dtype)

    def gqa_scores(h0):
        q = jnp.concatenate(
            [qb_ref[:, (h0 + g) * HEAD_W:(h0 + g + 1) * HEAD_W] for g in range(GQA_STACK)], axis=0)
        return _scores(q, kb(h0 // G_B))

    def gqa_finish(h0, s):
        o, l = _softmax_pv(s, vb(h0 // G_B))
        o = o * (1.0 / l)
        for g in range(GQA_STACK):
            cols = slice((h0 + g) * HEAD_W, (h0 + g + 1) * HEAD_W)
            gate = gb_ref[:, cols].astype(F32)
            o_ref[:, A_V + (h0 + g) * HEAD_W:A_V + (h0 + g + 1) * HEAD_W] = (
                o[g * tq:(g + 1) * tq, :] * gate).astype(o_ref.dtype)

    units = [(diff_scores, diff_finish, hd) for hd in range(H_A)]
    units += [(gqa_scores, gqa_finish, h0) for h0 in range(0, H_B, GQA_STACK)]
    scores = []
    for u, (_, finish, arg) in enumerate(units):
        while len(scores) < min(len(units), u + 1 + SCORE_LOOKAHEAD):
            issue, _, issue_arg = units[len(scores)]
            scores.append(issue(issue_arg))
        finish(arg, scores[u])
        scores[u] = None


def _attn_ctx_body(lam_init, lq1, lk1, lq2, lk2, subln_ref,
                   qa_ref, ka_ref, va_ref, ga_ref, qb_ref, kb_ref, vb_ref, gb_ref, o_ref):
    lam_col = _diff_lambda_col(lq1, lk1, lq2, lk2, lam_init)
    seq = qa_ref.shape[0]
    head = lambda ref, n_heads: (lambda i: _cache_head(ref, i, seq, n_heads))
    _attend(lam_col, subln_ref[...] * (1.0 - lam_init), qa_ref, ga_ref, qb_ref, gb_ref,
            head(ka_ref, H_A), head(va_ref, H_A), head(kb_ref, KV_B), head(vb_ref, KV_B), o_ref)


def _attn_ctx(lam_init, lam_params, subln_g, qa, ka, va, ga, qb, kb, vb, gb, seq):
    t = qa.shape[0]
    const = lambda shape: pl.BlockSpec(shape, lambda b: (0,) * len(shape))
    row_spec = lambda w: pl.BlockSpec((seq, w), lambda b: (b, 0))
    cache_spec = lambda n_heads: pl.BlockSpec((seq * n_heads, HEAD_W), lambda b: (b, 0))
    return pl.pallas_call(
        functools.partial(_attn_ctx_body, lam_init),
        out_shape=jax.ShapeDtypeStruct((t, D_MIX), BF16),
        grid=(t // seq,),
        in_specs=[const((H_A, DK_A))] * 4 + [const((1, DV_A))] + [
            row_spec(A_Q), cache_spec(H_A), cache_spec(H_A), row_spec(A_G),
            row_spec(B_Q), cache_spec(KV_B), cache_spec(KV_B), row_spec(B_G)],
        out_specs=row_spec(D_MIX),
        compiler_params=_params(1),
        name="attn_ctx",
    )(*lam_params, subln_g, qa, ka, va, ga, qb, kb, vb, gb)


def _attn_lat_body(lam_init, n_lat, lq1, lk1, lq2, lk2, subln_ref,
                   qa_ref, ka_ref, va_ref, cka_ref, cva_ref, ga_ref,
                   qb_ref, kb_ref, vb_ref, ckb_ref, cvb_ref, gb_ref, o_ref,
                   ka_all, va_all, kb_all, vb_all):
    @pl.when(pl.program_id(1) == 0)
    def _():
        for new_ref, cache_ref, all_ref in ((ka_ref, cka_ref, ka_all), (va_ref, cva_ref, va_all),
                                            (kb_ref, ckb_ref, kb_all), (vb_ref, cvb_ref, vb_all)):
            n_heads = new_ref.shape[1] // HEAD_W
            n_past = cache_ref.shape[0] // n_heads
            all_ref[0:n_lat, :] = new_ref[...]
            for hd in range(n_heads):
                all_ref[n_lat:, hd * HEAD_W:(hd + 1) * HEAD_W] = _cache_head(cache_ref, hd, n_past, n_heads)

    lam_col = _diff_lambda_col(lq1, lk1, lq2, lk2, lam_init)
    head = lambda ref: (lambda i: ref[:, i * HEAD_W:(i + 1) * HEAD_W])
    _attend(lam_col, subln_ref[...] * (1.0 - lam_init), qa_ref, ga_ref, qb_ref, gb_ref,
            head(ka_all), head(va_all), head(kb_all), head(vb_all), o_ref)


def _attn_lat(lam_init, lam_params, subln_g, qa, ka, va, cka, cva, ga, qb, kb, vb, ckb, cvb, gb,
              n_lat, n_past):
    t = qa.shape[0]
    tq = ATTN_TQ
    nq = n_lat // tq
    const = lambda shape: pl.BlockSpec(shape, lambda b, i: (0,) * len(shape))
    q_spec = lambda w: pl.BlockSpec((tq, w), lambda b, i: (b * nq + i, 0))
    new_spec = lambda w: pl.BlockSpec((n_lat, w), lambda b, i: (b, 0))
    cache_spec = lambda w: pl.BlockSpec((n_past * (w // HEAD_W), HEAD_W), lambda b, i: (b, 0))
    n_all = n_lat + n_past
    return pl.pallas_call(
        functools.partial(_attn_lat_body, lam_init, n_lat),
        out_shape=jax.ShapeDtypeStruct((t, D_MIX), BF16),
        grid=(t // n_lat, nq),
        in_specs=[const((H_A, DK_A))] * 4 + [const((1, DV_A))] + [
            q_spec(A_Q), new_spec(A_K), new_spec(A_V), cache_spec(A_K), cache_spec(A_V), q_spec(A_G),
            q_spec(B_Q), new_spec(B_K), new_spec(B_V), cache_spec(B_K), cache_spec(B_V), q_spec(B_G)],
        out_specs=q_spec(D_MIX),
        scratch_shapes=[pltpu.VMEM((n_all, A_K), BF16), pltpu.VMEM((n_all, A_V), BF16),
                        pltpu.VMEM((n_all, B_K), BF16), pltpu.VMEM((n_all, B_V), BF16)],
        compiler_params=_params(2),
        name="attn_lat",
    )(*lam_params, subln_g, qa, ka, va, cka, cva, ga, qb, kb, vb, ckb, cvb, gb)


def _outproj_body(staged, o_ref, w_out_ref, x_ref, mod_ref, gf_ref, y_ref, *rest):
    if staged:
        wbf_hbm, weights, w_stage, sem, sem_out = rest
        handoff = _stage_weights(w_out_ref, wbf_hbm, weights, w_stage, sem, sem_out)
    else:
        weights = w_out_ref
    o = o_ref[...]
    m = jnp.concatenate(
        [jnp.dot(o, weights[c], preferred_element_type=F32) for c in range(D_MODEL // W_CHUNK)],
        axis=1)
    gate = mod_ref[:, 2 * D_MODEL:3 * D_MODEL]
    y_ref[...] = _rms(x_ref[...] + gate * m) * gf_ref[...]
    if staged:
        pl.when(pl.program_id(0) == pl.num_programs(0) - 1)(handoff.wait)


def _outproj(o, w_out, x2d, mod3, row_of_tile, g_final, staged):
    t = x2d.shape[0]
    tm = OUTPROJ_TM
    out_shape = [jax.ShapeDtypeStruct((t, D_MODEL), F32)]
    out_specs = [pl.BlockSpec((tm, D_MODEL), lambda i: (i, 0))]
    if staged:
        out_shape.append(jax.ShapeDtypeStruct((D_MODEL // W_CHUNK, D_MIX, W_CHUNK), BF16))
        out_specs.append(pl.BlockSpec(memory_space=pl.ANY))
    return pl.pallas_call(
        functools.partial(_outproj_body, staged),
        out_shape=out_shape,
        grid=(t // tm,),
        in_specs=[
            pl.BlockSpec((tm, D_MIX), lambda i: (i, 0)),
            pl.BlockSpec(memory_space=pl.ANY) if staged else _weight_spec(D_MODEL),
            pl.BlockSpec((tm, D_MODEL), lambda i: (i, 0)),
            pl.BlockSpec((None, 1, 3 * D_MODEL), lambda i: (row_of_tile(i), 0, 0)),
            pl.BlockSpec((1, D_MODEL), lambda i: (0, 0)),
        ],
        out_specs=out_specs,
        scratch_shapes=_weight_scratch(D_MODEL) if staged else [],
        compiler_params=_params(1),
        name="outproj",
    )(o, w_out, x2d, mod3, g_final)


def _rope_tables(n_tok):
    n_rows = n_tok // GRID_W
    rows = np.repeat(np.arange(n_rows, dtype=np.float32), GRID_W)
    cols = np.tile(np.arange(GRID_W, dtype=np.float32), n_rows)

    def cos_sin(rot_dim):
        quarter = rot_dim // 4
        inv_freq = (1.0 / (np.float32(ROPE_THETA) ** (np.arange(quarter, dtype=np.float32) / quarter))
                    ).astype(np.float32)
        ang = np.concatenate([rows[:, None] * inv_freq, cols[:, None] * inv_freq], axis=-1)
        return np.cos(ang).astype(np.float32), np.sin(ang).astype(np.float32)

    ca, sa = cos_sin(DK_A)
    za = np.zeros_like(sa)
    cb, sb = cos_sin(DH_B)
    tabs = (
        np.concatenate([ca, ca, ca, ca], axis=-1),
        np.concatenate([-sa, za, -sa, za], axis=-1),
        np.concatenate([za, sa, za, sa], axis=-1),
        np.concatenate([cb, cb], axis=-1),
        np.concatenate([-sb, sb], axis=-1),
    )
    return tuple(jnp.asarray(t) for t in tabs)


def kernel(x_prompt, x_sample, cache_diff_k, cache_diff_v, cache_gqa_k, cache_gqa_v, c, c_ctx,
           w_ada, b_ada, g_norm, w_in, lam_q1, lam_k1, lam_q2, lam_k2, subln_g, q_norm_g, k_norm_g,
           w_out, g_final):
    bp, n_ctx, d = x_prompt.shape
    bs, n_lat, _ = x_sample.shape
    depth = w_in.shape[0]
    n_past = cache_diff_k.shape[2]
    assert depth == 1 and d == D_MODEL and bs + 1 <= N_COND
    assert n_lat % INPROJ_TM == 0 and n_lat % OUTPROJ_TM == 0 and n_lat % ATTN_TQ == 0
    l = 0
    lam_init = 0.8 - 0.6 * math.exp(-0.3 * l)

    cond = jnp.concatenate(
        [c_ctx[None, :], c, jnp.zeros((N_COND - 1 - bs, d), F32)], axis=0)
    mod = _adaln(cond, w_ada[l], b_ada[l][None, :])
    mod3 = mod[:, None, :]

    gn = g_norm[l][None, :]
    qn = q_norm_g[l][None, :]
    kn = k_norm_g[l][None, :]
    lam_params = (lam_q1[l], lam_k1[l], lam_q2[l], lam_k2[l])
    subln = subln_g[l][None, :]
    gf = g_final[None, :]

    xp2 = x_prompt.reshape(bp * n_ctx, d)
    qa, ka, va, ga, qb, kb, vb, gb, w_in_bf16 = _inproj(
        xp2, mod3, lambda i: 0, gn, w_in[l], qn, kn, None)
    o = _attn_ctx(lam_init, lam_params, subln, qa, ka, va, ga, qb, kb, vb, gb, n_ctx)
    y_prompt, w_out_bf16 = _outproj(o, w_out[l], xp2, mod3, lambda i: 0, gf, True)
    y_prompt = y_prompt.reshape(bp, n_ctx, d)
    new_diff_k = ka.reshape(bp, 1, n_ctx, H_A, 2 * DK_A)
    new_diff_v = va.reshape(bp, 1, n_ctx, H_A, DV_A)
    new_gqa_k = kb.reshape(bp, 1, n_ctx, KV_B, DH_B)
    new_gqa_v = vb.reshape(bp, 1, n_ctx, KV_B, DH_B)

    xs2 = x_sample.reshape(bs * n_lat, d)
    in_tiles = n_lat // INPROJ_TM
    qa, ka, va, ga, qb, kb, vb, gb = _inproj(
        xs2, mod3, lambda i: 1 + i // in_tiles, gn, w_in_bf16, qn, kn, _rope_tables(n_lat))
    cka = cache_diff_k[:, l].reshape(bs * n_past * H_A, HEAD_W)
    cva = cache_diff_v[:, l].reshape(bs * n_past * H_A, HEAD_W)
    ckb = cache_gqa_k[:, l].reshape(bs * n_past * KV_B, HEAD_W)
    cvb = cache_gqa_v[:, l].reshape(bs * n_past * KV_B, HEAD_W)
    o = _attn_lat(lam_init, lam_params, subln, qa, ka, va, cka, cva, ga, qb, kb, vb, ckb, cvb, gb,
                  n_lat, n_past)
    out_tiles = n_lat // OUTPROJ_TM
    (y_sample,) = _outproj(o, w_out_bf16, xs2, mod3, lambda i: 1 + i // out_tiles, gf, False)
    y_sample = y_sample.reshape(bs, n_lat, d)

    return (y_prompt, y_sample, new_diff_k, new_diff_v, new_gqa_k, new_gqa_v)
```

```python
import functools
import math

import jax
import jax.numpy as jnp
import numpy as np
from jax import lax
from jax.experimental import pallas as pl
from jax.experimental.pallas import tpu as pltpu

D_MODEL = 2048
GRID_W = 64
ROPE_THETA = 10000.0
EPS = 1e-6
H_A = 8
DK_A = 64
DV_A = 2 * DK_A
H_B = 8
KV_B = 2
DH_B = 128
G_B = H_B // KV_B
HEAD_W = 128
A_Q = H_A * 2 * DK_A
A_K = H_A * 2 * DK_A
A_V = H_A * DV_A
A_G = H_A * DV_A
B_Q = H_B * DH_B
B_K = KV_B * DH_B
B_V = KV_B * DH_B
B_G = H_B * DH_B
D_IN = A_Q + A_K + A_V + A_G + B_Q + B_K + B_V + B_G
D_MIX = A_V + B_Q
OFF_AQ = 0
OFF_AK = OFF_AQ + A_Q
OFF_AV = OFF_AK + A_K
OFF_AG = OFF_AV + A_V
OFF_BQ = OFF_AG + A_G
OFF_BK = OFF_BQ + B_Q
OFF_BV = OFF_BK + B_K
OFF_BG = OFF_BV + B_V

N_COND = 8
ADA_TK = 256
INPROJ_TM = 256
W_CHUNK = 256
STAGE_ROWS = 128
OUTPROJ_TM = 512
ATTN_TQ = 256
GQA_STACK = 2
SCORE_LOOKAHEAD = 2
QSCALE_A = math.log2(math.e) / math.sqrt(DK_A)
QSCALE_B = math.log2(math.e) / math.sqrt(DH_B)
VMEM_LIMIT = 56 * 1024 * 1024

BF16 = jnp.bfloat16
F32 = jnp.float32


def _params(n_grid_axes):
    return pltpu.CompilerParams(
        dimension_semantics=("arbitrary",) * n_grid_axes,
        vmem_limit_bytes=VMEM_LIMIT,
    )


def _silu(x):
    return x * jax.nn.sigmoid(x)


def _rms(x):
    return x * lax.rsqrt(jnp.mean(x * x, axis=-1, keepdims=True) + EPS)


def _weight_scratch(n_cols):
    return [pltpu.VMEM((n_cols // W_CHUNK, D_MODEL, W_CHUNK), BF16),
            pltpu.VMEM((2, STAGE_ROWS, n_cols), F32),
            pltpu.SemaphoreType.DMA((2,)),
            pltpu.SemaphoreType.DMA(())]


def _slab_copy(w_hbm, w_stage, sem, k):
    slot = k % 2
    return pltpu.make_async_copy(
        w_hbm.at[pl.ds(k * STAGE_ROWS, STAGE_ROWS), :], w_stage.at[slot], sem.at[slot])


def _stage_weights(w_hbm, wbf_hbm, w_scr, w_stage, sem, sem_out):
    n_chunks, n_rows, _ = w_scr.shape
    n_slabs = n_rows // STAGE_ROWS
    copy = functools.partial(_slab_copy, w_hbm, w_stage, sem)
    handoff = pltpu.make_async_copy(w_scr, wbf_hbm, sem_out)

    @pl.when(pl.program_id(0) == 0)
    def _():
        copy(0).start()
        for k in range(n_slabs):
            if k + 1 < n_slabs:
                copy(k + 1).start()
            copy(k).wait()
            rows = slice(k * STAGE_ROWS, (k + 1) * STAGE_ROWS)
            for c in range(n_chunks):
                w_scr[c, rows, :] = w_stage[k % 2, :, c * W_CHUNK:(c + 1) * W_CHUNK].astype(BF16)
        handoff.start()

    return handoff


def _weight_spec(n_cols):
    return pl.BlockSpec((n_cols // W_CHUNK, D_MODEL, W_CHUNK), lambda i: (0, 0, 0),
                        pipeline_mode=pl.Buffered(1))


def _adaln_body(cond_ref, w_ref, b_ref, o_ref):
    @pl.when(pl.program_id(0) == 0)
    def _():
        o_ref[...] = jnp.broadcast_to(b_ref[...], o_ref.shape)

    a = _silu(cond_ref[...]).astype(BF16)
    w = w_ref[...].astype(BF16)
    o_ref[...] += jnp.dot(a, w, preferred_element_type=F32)


def _adaln(cond, w_ada, b_ada):
    d3 = w_ada.shape[1]
    return pl.pallas_call(
        _adaln_body,
        out_shape=jax.ShapeDtypeStruct((N_COND, d3), F32),
        grid=(D_MODEL // ADA_TK,),
        in_specs=[
            pl.BlockSpec((N_COND, ADA_TK), lambda k: (0, k)),
            pl.BlockSpec((ADA_TK, d3), lambda k: (k, 0)),
            pl.BlockSpec((1, d3), lambda k: (0, 0)),
        ],
        out_specs=pl.BlockSpec((N_COND, d3), lambda k: (0, 0)),
        compiler_params=_params(1),
        name="adaln",
    )(cond, w_ada, b_ada)


def _rope_a(x, c, s_up, s_dn):
    return x * c + pltpu.roll(x, 96, 1) * s_up + pltpu.roll(x, 32, 1) * s_dn


def _rope_b(x, c, s):
    return x * c + pltpu.roll(x, 64, 1) * s


def _inproj_body(rope, x_ref, mod_ref, gn_ref, w_in_ref, qn_ref, kn_ref, *rest):
    if rope:
        ca_ref, sau_ref, sad_ref, cb_ref, sb_ref = rest[:5]
        rest = rest[5:]
    qa_o, ka_o, va_o, ga_o, qb_o, kb_o, vb_o, gb_o = rest[:8]
    cache_layout = not rope
    if rope:
        weights = w_in_ref
    else:
        wbf_hbm, weights, w_stage, sem, sem_out = rest[8:]
        handoff = _stage_weights(w_in_ref, wbf_hbm, weights, w_stage, sem, sem_out)
    tm = x_ref.shape[0]

    x = x_ref[...]
    shift = mod_ref[:, 0:D_MODEL]
    scale = mod_ref[:, D_MODEL:2 * D_MODEL]
    h = ((_rms(x) * gn_ref[...]) * (1.0 + scale) + shift).astype(BF16)

    if rope:
        ca, sau, sad = ca_ref[...], sau_ref[...], sad_ref[...]
        cb, sb = cb_ref[...], sb_ref[...]

    def rope_a(t):
        return _rope_a(t, ca, sau, sad) if rope else t

    def rope_b(t):
        return _rope_b(t, cb, sb) if rope else t

    ident = lambda t: t
    regions = (
        (OFF_AQ, A_Q, qa_o, lambda t: rope_a(t) * QSCALE_A, False),
        (OFF_AK, A_K, ka_o, rope_a, True),
        (OFF_AV, A_V, va_o, ident, True),
        (OFF_AG, A_G, ga_o, _silu, False),
        (OFF_BQ, B_Q, qb_o, lambda t: rope_b(_rms(t) * qn_ref[...]) * QSCALE_B, False),
        (OFF_BK, B_K, kb_o, lambda t: rope_b(_rms(t) * kn_ref[...]), True),
        (OFF_BV, B_V, vb_o, ident, True),
        (OFF_BG, B_G, gb_o, _silu, False),
    )
    for start, width, o_ref, epi, is_kv in regions:
        n_heads = width // HEAD_W
        for c0 in range(0, width, W_CHUNK):
            z = jnp.dot(h, weights[(start + c0) // W_CHUNK], preferred_element_type=F32)
            for h0 in range(0, W_CHUNK, HEAD_W):
                hd = (c0 + h0) // HEAD_W
                t = epi(z[:, h0:h0 + HEAD_W]).astype(o_ref.dtype)
                if is_kv and cache_layout:
                    o_ref[pl.ds(hd, tm, stride=n_heads), :] = t
                else:
                    o_ref[:, hd * HEAD_W:(hd + 1) * HEAD_W] = t

    if not rope:
        pl.when(pl.program_id(0) == pl.num_programs(0) - 1)(handoff.wait)


def _inproj(x2d, mod3, row_of_tile, g_norm, w_in, q_norm_g, k_norm_g, rope_tabs):
    t = x2d.shape[0]
    tm = INPROJ_TM
    rope = rope_tabs is not None
    row_spec = lambda w: pl.BlockSpec((tm, w), lambda i: (i, 0))
    const = lambda shape: pl.BlockSpec(shape, lambda i: (0,) * len(shape))
    in_specs = [
        row_spec(D_MODEL),
        pl.BlockSpec((None, 1, 3 * D_MODEL), lambda i: (row_of_tile(i), 0, 0)),
        const((1, D_MODEL)),
        _weight_spec(D_IN) if rope else pl.BlockSpec(memory_space=pl.ANY),
        const((1, HEAD_W)),
        const((1, HEAD_W)),
    ]
    args = [x2d, mod3, g_norm, w_in, q_norm_g, k_norm_g]
    if rope:
        n_pos_tiles = rope_tabs[0].shape[0] // tm
        in_specs += [pl.BlockSpec((tm, HEAD_W), lambda i: (i % n_pos_tiles, 0))] * 5
        args += list(rope_tabs)
    widths = (A_Q, A_K, A_V, A_G, B_Q, B_K, B_V, B_G)
    is_kv = (False, True, True, False, False, True, True, False)
    out_shape, out_specs = [], []
    for w, kv in zip(widths, is_kv):
        if kv and not rope:
            n_heads = w // HEAD_W
            out_shape.append(jax.ShapeDtypeStruct((t * n_heads, HEAD_W), F32))
            out_specs.append(pl.BlockSpec((tm * n_heads, HEAD_W), lambda i: (i, 0)))
        else:
            out_shape.append(jax.ShapeDtypeStruct((t, w), BF16))
            out_specs.append(row_spec(w))
    if not rope:
        out_shape.append(jax.ShapeDtypeStruct((D_IN // W_CHUNK, D_MODEL, W_CHUNK), BF16))
        out_specs.append(pl.BlockSpec(memory_space=pl.ANY))
    return pl.pallas_call(
        functools.partial(_inproj_body, rope),
        out_shape=out_shape,
        grid=(t // tm,),
        in_specs=in_specs,
        out_specs=out_specs,
        scratch_shapes=[] if rope else _weight_scratch(D_IN),
        compiler_params=_params(1),
        name="inproj_rope" if rope else "inproj",
    )(*args)


def _diff_lambda_col(lq1_ref, lk1_ref, lq2_ref, lk2_ref, lam_init):
    s1 = jnp.sum(lq1_ref[...] * lk1_ref[...], axis=-1, keepdims=True)
    s2 = jnp.sum(lq2_ref[...] * lk2_ref[...], axis=-1, keepdims=True)
    return jnp.exp(s1) - jnp.exp(s2) + lam_init


def _cache_head(ref, head, n_tok, n_heads):
    return ref[pl.ds(head, n_tok, stride=n_heads), :].astype(BF16)


def _scores(q, k):
    return lax.dot_general(q, k, (((1,), (1,)), ((), ())), preferred_element_type=F32)


def _softmax_pv(s, v):
    m = jnp.max(s, axis=-1, keepdims=True)
    e = jnp.exp2(s - m)
    v1 = jnp.concatenate([v, jnp.ones_like(v)], axis=1)
    ol = jnp.dot(e.astype(BF16), v1, preferred_element_type=F32)
    return ol[:, :HEAD_W], ol[:, HEAD_W:]


def _attend(lam_col, subln, qa_ref, ga_ref, qb_ref, gb_ref, ka, va, kb, vb, o_ref):
    tq = qa_ref.shape[0]
    lane = lax.broadcasted_iota(jnp.int32, (tq, HEAD_W), 1)
    first = lane < DK_A

    def diff_scores(hd):
        q = qa_ref[:, hd * HEAD_W:(hd + 1) * HEAD_W]
        zero = jnp.zeros_like(q)
        q2 = jnp.concatenate([jnp.where(first, q, zero), jnp.where(first, zero, q)], axis=0)
        return _scores(q2, ka(hd))

    def diff_finish(hd, s):
        cols = slice(hd * HEAD_W, (hd + 1) * HEAD_W)
        o2, l2 = _softmax_pv(s, va(hd))
        r2 = 1.0 / l2
        lam = lam_col[hd:hd + 1, :]
        o = o2[:tq] * r2[:tq] - o2[tq:] * (lam * r2[tq:])
        o_ref[:, cols] = (_rms(o) * subln * ga_ref[:, cols].astype(F32)).astype(o_ref.dtype)

    def gqa_scores(h0):
        q = jnp.concatenate(
            [qb_ref[:, (h0 + g) * HEAD_W:(h0 + g + 1) * HEAD_W] for g in range(GQA_STACK)], axis=0)
        return _scores(q, kb(h0 // G_B))

    def gqa_finish(h0, s):
        o, l = _softmax_pv(s, vb(h0 // G_B))
        o = o * (1.0 / l)
        for g in range(GQA_STACK):
            cols = slice((h0 + g) * HEAD_W, (h0 + g + 1) * HEAD_W)
            gate = gb_ref[:, cols].astype(F32)
            o_ref[:, A_V + (h0 + g) * HEAD_W:A_V + (h0 + g + 1) * HEAD_W] = (
                o[g * tq:(g + 1) * tq, :] * gate).astype(o_ref.dtype)

    units = [(diff_scores, diff_finish, hd) for hd in range(H_A)]
    units += [(gqa_scores, gqa_finish, h0) for h0 in range(0, H_B, GQA_STACK)]
    scores = []
    for u, (_, finish, arg) in enumerate(units):
        while len(scores) < min(len(units), u + 1 + SCORE_LOOKAHEAD):
            issue, _, issue_arg = units[len(scores)]
            scores.append(issue(issue_arg))
        finish(arg, scores[u])
        scores[u] = None


def _attn_ctx_body(lam_init, lq1, lk1, lq2, lk2, subln_ref,
                   qa_ref, ka_ref, va_ref, ga_ref, qb_ref, kb_ref, vb_ref, gb_ref, o_ref):
    lam_col = _diff_lambda_col(lq1, lk1, lq2, lk2, lam_init)
    seq = qa_ref.shape[0]
    head = lambda ref, n_heads: (lambda i: _cache_head(ref, i, seq, n_heads))
    _attend(lam_col, subln_ref[...] * (1.0 - lam_init), qa_ref, ga_ref, qb_ref, gb_ref,
            head(ka_ref, H_A), head(va_ref, H_A), head(kb_ref, KV_B), head(vb_ref, KV_B), o_ref)


def _attn_ctx(lam_init, lam_params, subln_g, qa, ka, va, ga, qb, kb, vb, gb, seq):
    t = qa.shape[0]
    const = lambda shape: pl.BlockSpec(shape, lambda b: (0,) * len(shape))
    row_spec = lambda w: pl.BlockSpec((seq, w), lambda b: (b, 0))
    cache_spec = lambda n_heads: pl.BlockSpec((seq * n_heads, HEAD_W), lambda b: (b, 0))
    return pl.pallas_call(
        functools.partial(_attn_ctx_body, lam_init),
        out_shape=jax.ShapeDtypeStruct((t, D_MIX), BF16),
        grid=(t // seq,),
        in_specs=[const((H_A, DK_A))] * 4 + [const((1, DV_A))] + [
            row_spec(A_Q), cache_spec(H_A), cache_spec(H_A), row_spec(A_G),
            row_spec(B_Q), cache_spec(KV_B), cache_spec(KV_B), row_spec(B_G)],
        out_specs=row_spec(D_MIX),
        compiler_params=_params(1),
        name="attn_ctx",
    )(*lam_params, subln_g, qa, ka, va, ga, qb, kb, vb, gb)


def _attn_lat_body(lam_init, n_lat, lq1, lk1, lq2, lk2, subln_ref,
                   qa_ref, ka_ref, va_ref, cka_ref, cva_ref, ga_ref,
                   qb_ref, kb_ref, vb_ref, ckb_ref, cvb_ref, gb_ref, o_ref,
                   ka_all, va_all, kb_all, vb_all):
    @pl.when(pl.program_id(1) == 0)
    def _():
        for new_ref, cache_ref, all_ref in ((ka_ref, cka_ref, ka_all), (va_ref, cva_ref, va_all),
                                            (kb_ref, ckb_ref, kb_all), (vb_ref, cvb_ref, vb_all)):
            n_heads = new_ref.shape[1] // HEAD_W
            n_past = cache_ref.shape[0] // n_heads
            all_ref[0:n_lat, :] = new_ref[...]
            for hd in range(n_heads):
                all_ref[n_lat:, hd * HEAD_W:(hd + 1) * HEAD_W] = _cache_head(cache_ref, hd, n_past, n_heads)

    lam_col = _diff_lambda_col(lq1, lk1, lq2, lk2, lam_init)
    head = lambda ref: (lambda i: ref[:, i * HEAD_W:(i + 1) * HEAD_W])
    _attend(lam_col, subln_ref[...] * (1.0 - lam_init), qa_ref, ga_ref, qb_ref, gb_ref,
            head(ka_all), head(va_all), head(kb_all), head(vb_all), o_ref)


def _attn_lat(lam_init, lam_params, subln_g, qa, ka, va, cka, cva, ga, qb, kb, vb, ckb, cvb, gb,
              n_lat, n_past):
    t = qa.shape[0]
    tq = ATTN_TQ
    nq = n_lat // tq
    const = lambda shape: pl.BlockSpec(shape, lambda b, i: (0,) * len(shape))
    q_spec = lambda w: pl.BlockSpec((tq, w), lambda b, i: (b * nq + i, 0))
    new_spec = lambda w: pl.BlockSpec((n_lat, w), lambda b, i: (b, 0))
    cache_spec = lambda w: pl.BlockSpec((n_past * (w // HEAD_W), HEAD_W), lambda b, i: (b, 0))
    n_all = n_lat + n_past
    return pl.pallas_call(
        functools.partial(_attn_lat_body, lam_init, n_lat),
        out_shape=jax.ShapeDtypeStruct((t, D_MIX), BF16),
        grid=(t // n_lat, nq),
        in_specs=[const((H_A, DK_A))] * 4 + [const((1, DV_A))] + [
            q_spec(A_Q), new_spec(A_K), new_spec(A_V), cache_spec(A_K), cache_spec(A_V), q_spec(A_G),
            q_spec(B_Q), new_spec(B_K), new_spec(B_V), cache_spec(B_K), cache_spec(B_V), q_spec(B_G)],
        out_specs=q_spec(D_MIX),
        scratch_shapes=[pltpu.VMEM((n_all, A_K), BF16), pltpu.VMEM((n_all, A_V), BF16),
                        pltpu.VMEM((n_all, B_K), BF16), pltpu.VMEM((n_all, B_V), BF16)],
        compiler_params=_params(2),
        name="attn_lat",
    )(*lam_params, subln_g, qa, ka, va, cka, cva, ga, qb, kb, vb, ckb, cvb, gb)


def _outproj_body(staged, o_ref, w_out_ref, x_ref, mod_ref, gf_ref, y_ref, *rest):
    if staged:
        wbf_hbm, weights, w_stage, sem, sem_out = rest
        handoff = _stage_weights(w_out_ref, wbf_hbm, weights, w_stage, sem, sem_out)
    else:
        weights = w_out_ref
    o = o_ref[...]
    m = jnp.concatenate(
        [jnp.dot(o, weights[c], preferred_element_type=F32) for c in range(D_MODEL // W_CHUNK)],
        axis=1)
    gate = mod_ref[:, 2 * D_MODEL:3 * D_MODEL]
    y_ref[...] = _rms(x_ref[...] + gate * m) * gf_ref[...]
    if staged:
        pl.when(pl.program_id(0) == pl.num_programs(0) - 1)(handoff.wait)


def _outproj(o, w_out, x2d, mod3, row_of_tile, g_final, staged):
    t = x2d.shape[0]
    tm = OUTPROJ_TM
    out_shape = [jax.ShapeDtypeStruct((t, D_MODEL), F32)]
    out_specs = [pl.BlockSpec((tm, D_MODEL), lambda i: (i, 0))]
    if staged:
        out_shape.append(jax.ShapeDtypeStruct((D_MODEL // W_CHUNK, D_MIX, W_CHUNK), BF16))
        out_specs.append(pl.BlockSpec(memory_space=pl.ANY))
    return pl.pallas_call(
        functools.partial(_outproj_body, staged),
        out_shape=out_shape,
        grid=(t // tm,),
        in_specs=[
            pl.BlockSpec((tm, D_MIX), lambda i: (i, 0)),
            pl.BlockSpec(memory_space=pl.ANY) if staged else _weight_spec(D_MODEL),
            pl.BlockSpec((tm, D_MODEL), lambda i: (i, 0)),
            pl.BlockSpec((None, 1, 3 * D_MODEL), lambda i: (row_of_tile(i), 0, 0)),
            pl.BlockSpec((1, D_MODEL), lambda i: (0, 0)),
        ],
        out_specs=out_specs,
        scratch_shapes=_weight_scratch(D_MODEL) if staged else [],
        compiler_params=_params(1),
        name="outproj",
    )(o, w_out, x2d, mod3, g_final)


def _rope_tables(n_tok):
    n_rows = n_tok // GRID_W
    rows = np.repeat(np.arange(n_rows, dtype=np.float32), GRID_W)
    cols = np.tile(np.arange(GRID_W, dtype=np.float32), n_rows)

    def cos_sin(rot_dim):
        quarter = rot_dim // 4
        inv_freq = (1.0 / (np.float32(ROPE_THETA) ** (np.arange(quarter, dtype=np.float32) / quarter))
                    ).astype(np.float32)
        ang = np.concatenate([rows[:, None] * inv_freq, cols[:, None] * inv_freq], axis=-1)
        return np.cos(ang).astype(np.float32), np.sin(ang).astype(np.float32)

    ca, sa = cos_sin(DK_A)
    za = np.zeros_like(sa)
    cb, sb = cos_sin(DH_B)
    tabs = (
        np.concatenate([ca, ca, ca, ca], axis=-1),
        np.concatenate([-sa, za, -sa, za], axis=-1),
        np.concatenate([za, sa, za, sa], axis=-1),
        np.concatenate([cb, cb], axis=-1),
        np.concatenate([-sb, sb], axis=-1),
    )
    return tuple(jnp.asarray(t) for t in tabs)


def kernel(x_prompt, x_sample, cache_diff_k, cache_diff_v, cache_gqa_k, cache_gqa_v, c, c_ctx,
           w_ada, b_ada, g_norm, w_in, lam_q1, lam_k1, lam_q2, lam_k2, subln_g, q_norm_g, k_norm_g,
           w_out, g_final):
    bp, n_ctx, d = x_prompt.shape
    bs, n_lat, _ = x_sample.shape
    depth = w_in.shape[0]
    n_past = cache_diff_k.shape[2]
    assert depth == 1 and d == D_MODEL and bs + 1 <= N_COND
    assert n_lat % INPROJ_TM == 0 and n_lat % OUTPROJ_TM == 0 and n_lat % ATTN_TQ == 0
    l = 0
    lam_init = 0.8 - 0.6 * math.exp(-0.3 * l)

    cond = jnp.concatenate(
        [c_ctx[None, :], c, jnp.zeros((N_COND - 1 - bs, d), F32)], axis=0)
    mod = _adaln(cond, w_ada[l], b_ada[l][None, :])
    mod3 = mod[:, None, :]

    gn = g_norm[l][None, :]
    qn = q_norm_g[l][None, :]
    kn = k_norm_g[l][None, :]
    lam_params = (lam_q1[l], lam_k1[l], lam_q2[l], lam_k2[l])
    subln = subln_g[l][None, :]
    gf = g_final[None, :]

    xp2 = x_prompt.reshape(bp * n_ctx, d)
    qa, ka, va, ga, qb, kb, vb, gb, w_in_bf16 = _inproj(
        xp2, mod3, lambda i: 0, gn, w_in[l], qn, kn, None)
    o = _attn_ctx(lam_init, lam_params, subln, qa, ka, va, ga, qb, kb, vb, gb, n_ctx)
    y_prompt, w_out_bf16 = _outproj(o, w_out[l], xp2, mod3, lambda i: 0, gf, True)
    y_prompt = y_prompt.reshape(bp, n_ctx, d)
    new_diff_k = ka.reshape(bp, 1, n_ctx, H_A, 2 * DK_A)
    new_diff_v = va.reshape(bp, 1, n_ctx, H_A, DV_A)
    new_gqa_k = kb.reshape(bp, 1, n_ctx, KV_B, DH_B)
    new_gqa_v = vb.reshape(bp, 1, n_ctx, KV_B, DH_B)

    xs2 = x_sample.reshape(bs * n_lat, d)
    in_tiles = n_lat // INPROJ_TM
    qa, ka, va, ga, qb, kb, vb, gb = _inproj(
        xs2, mod3, lambda i: 1 + i // in_tiles, gn, w_in_bf16, qn, kn, _rope_tables(n_lat))
    cka = cache_diff_k[:, l].reshape(bs * n_past * H_A, HEAD_W)
    cva = cache_diff_v[:, l].reshape(bs * n_past * H_A, HEAD_W)
    ckb = cache_gqa_k[:, l].reshape(bs * n_past * KV_B, HEAD_W)
    cvb = cache_gqa_v[:, l].reshape(bs * n_past * KV_B, HEAD_W)
    o = _attn_lat(lam_init, lam_params, subln, qa, ka, va, cka, cva, ga, qb, kb, vb, ckb, cvb, gb,
                  n_lat, n_past)
    out_tiles = n_lat // OUTPROJ_TM
    (y_sample,) = _outproj(o, w_out_bf16, xs2, mod3, lambda i: 1 + i // out_tiles, gf, False)
    y_sample = y_sample.reshape(bs, n_lat, d)

    return (y_prompt, y_sample, new_diff_k, new_diff_v, new_gqa_k, new_gqa_v)
```

```python
import functools
import math

import jax
import jax.numpy as jnp
import numpy as np
from jax import lax
from jax.experimental import pallas as pl
from jax.experimental.pallas import tpu as pltpu

D_MODEL = 2048
GRID_W = 64
ROPE_THETA = 10000.0
EPS = 1e-6
H_A = 8
DK_A = 64
DV_A = 2 * DK_A
H_B = 8
KV_B = 2
DH_B = 128
G_B = H_B // KV_B
HEAD_W = 128
A_Q = H_A * 2 * DK_A
A_K = H_A * 2 * DK_A
A_V = H_A * DV_A
A_G = H_A * DV_A
B_Q = H_B * DH_B
B_K = KV_B * DH_B
B_V = KV_B * DH_B
B_G = H_B * DH_B
D_IN = A_Q + A_K + A_V + A_G + B_Q + B_K + B_V + B_G
D_MIX = A_V + B_Q
OFF_AQ = 0
OFF_AK = OFF_AQ + A_Q
OFF_AV = OFF_AK + A_K
OFF_AG = OFF_AV + A_V
OFF_BQ = OFF_AG + A_G
OFF_BK = OFF_BQ + B_Q
OFF_BV = OFF_BK + B_K
OFF_BG = OFF_BV + B_V

N_COND = 8
ADA_TK = 256
INPROJ_TM = 256
W_CHUNK = 256
STAGE_BYTES = 2 * 1024 * 1024
STAGE_SLOTS = 4
OUTPROJ_TM = 512
ATTN_TQ = 256
GQA_STACK = 2
SCORE_LOOKAHEAD = 2
QSCALE_A = math.log2(math.e) / math.sqrt(DK_A)
QSCALE_B = math.log2(math.e) / math.sqrt(DH_B)
VMEM_LIMIT = 56 * 1024 * 1024

BF16 = jnp.bfloat16
F32 = jnp.float32


def _params(n_grid_axes):
    return pltpu.CompilerParams(
        dimension_semantics=("arbitrary",) * n_grid_axes,
        vmem_limit_bytes=VMEM_LIMIT,
    )


def _silu(x):
    return x * jax.nn.sigmoid(x)


def _rms(x):
    return x * lax.rsqrt(jnp.mean(x * x, axis=-1, keepdims=True) + EPS)


def _stage_rows(n_cols):
    return 1 << int(math.log2(STAGE_BYTES // (n_cols * 4)))


def _weight_scratch(n_cols):
    return [pltpu.VMEM((n_cols // W_CHUNK, D_MODEL, W_CHUNK), BF16),
            pltpu.VMEM((STAGE_SLOTS, _stage_rows(n_cols), n_cols), F32),
            pltpu.SemaphoreType.DMA((STAGE_SLOTS,)),
            pltpu.SemaphoreType.DMA(())]


def _slab_copy(w_hbm, w_stage, sem, k):
    slot = k % STAGE_SLOTS
    n_rows = w_stage.shape[1]
    return pltpu.make_async_copy(
        w_hbm.at[pl.ds(k * n_rows, n_rows), :], w_stage.at[slot], sem.at[slot])


def _stage_weights(w_hbm, wbf_hbm, w_scr, w_stage, sem, sem_out):
    n_chunks = w_scr.shape[0]
    slab_rows = w_stage.shape[1]
    n_slabs = w_scr.shape[1] // slab_rows
    copy = functools.partial(_slab_copy, w_hbm, w_stage, sem)
    handoff = pltpu.make_async_copy(w_scr, wbf_hbm, sem_out)

    @pl.when(pl.program_id(0) == 0)
    def _():
        for k in range(STAGE_SLOTS - 1):
            copy(k).start()
        for k in range(n_slabs):
            if k + STAGE_SLOTS - 1 < n_slabs:
                copy(k + STAGE_SLOTS - 1).start()
            copy(k).wait()
            rows = slice(k * slab_rows, (k + 1) * slab_rows)
            for c in range(n_chunks):
                w_scr[c, rows, :] = w_stage[
                    k % STAGE_SLOTS, :, c * W_CHUNK:(c + 1) * W_CHUNK].astype(BF16)
        handoff.start()

    return handoff


def _weight_spec(n_cols):
    return pl.BlockSpec((n_cols // W_CHUNK, D_MODEL, W_CHUNK), lambda i: (0, 0, 0),
                        pipeline_mode=pl.Buffered(1))


def _adaln_body(cond_ref, w_ref, b_ref, o_ref):
    @pl.when(pl.program_id(0) == 0)
    def _():
        o_ref[...] = jnp.broadcast_to(b_ref[...], o_ref.shape)

    a = _silu(cond_ref[...]).astype(BF16)
    w = w_ref[...].astype(BF16)
    o_ref[...] += jnp.dot(a, w, preferred_element_type=F32)


def _adaln(cond, w_ada, b_ada):
    d3 = w_ada.shape[1]
    return pl.pallas_call(
        _adaln_body,
        out_shape=jax.ShapeDtypeStruct((N_COND, d3), F32),
        grid=(D_MODEL // ADA_TK,),
        in_specs=[
            pl.BlockSpec((N_COND, ADA_TK), lambda k: (0, k)),
            pl.BlockSpec((ADA_TK, d3), lambda k: (k, 0)),
            pl.BlockSpec((1, d3), lambda k: (0, 0)),
        ],
        out_specs=pl.BlockSpec((N_COND, d3), lambda k: (0, 0)),
        compiler_params=_params(1),
        name="adaln",
    )(cond, w_ada, b_ada)


def _rope_a(x, c, s_up, s_dn):
    return x * c + pltpu.roll(x, 96, 1) * s_up + pltpu.roll(x, 32, 1) * s_dn


def _rope_b(x, c, s):
    return x * c + pltpu.roll(x, 64, 1) * s


def _inproj_body(rope, x_ref, mod_ref, gn_ref, w_in_ref, qn_ref, kn_ref, *rest):
    if rope:
        ca_ref, sau_ref, sad_ref, cb_ref, sb_ref = rest[:5]
        rest = rest[5:]
    qa_o, ka_o, va_o, ga_o, qb_o, kb_o, vb_o, gb_o = rest[:8]
    cache_layout = not rope
    if rope:
        weights = w_in_ref
    else:
        wbf_hbm, weights, w_stage, sem, sem_out = rest[8:]
        handoff = _stage_weights(w_in_ref, wbf_hbm, weights, w_stage, sem, sem_out)
    tm = x_ref.shape[0]

    x = x_ref[...]
    shift = mod_ref[:, 0:D_MODEL]
    scale = mod_ref[:, D_MODEL:2 * D_MODEL]
    h = ((_rms(x) * gn_ref[...]) * (1.0 + scale) + shift).astype(BF16)

    if rope:
        ca, sau, sad = ca_ref[...], sau_ref[...], sad_ref[...]
        cb, sb = cb_ref[...], sb_ref[...]

    def rope_a(t):
        return _rope_a(t, ca, sau, sad) if rope else t

    def rope_b(t):
        return _rope_b(t, cb, sb) if rope else t

    ident = lambda t: t
    regions = (
        (OFF_AQ, A_Q, qa_o, lambda t: rope_a(t) * QSCALE_A, False),
        (OFF_AK, A_K, ka_o, rope_a, True),
        (OFF_AV, A_V, va_o, ident, True),
        (OFF_AG, A_G, ga_o, _silu, False),
        (OFF_BQ, B_Q, qb_o, lambda t: rope_b(_rms(t) * qn_ref[...]) * QSCALE_B, False),
        (OFF_BK, B_K, kb_o, lambda t: rope_b(_rms(t) * kn_ref[...]), True),
        (OFF_BV, B_V, vb_o, ident, True),
        (OFF_BG, B_G, gb_o, _silu, False),
    )
    for start, width, o_ref, epi, is_kv in regions:
        n_heads = width // HEAD_W
        for c0 in range(0, width, W_CHUNK):
            z = jnp.dot(h, weights[(start + c0) // W_CHUNK], preferred_element_type=F32)
            for h0 in range(0, W_CHUNK, HEAD_W):
                hd = (c0 + h0) // HEAD_W
                t = epi(z[:, h0:h0 + HEAD_W]).astype(o_ref.dtype)
                if is_kv and cache_layout:
                    o_ref[pl.ds(hd, tm, stride=n_heads), :] = t
                else:
                    o_ref[:, hd * HEAD_W:(hd + 1) * HEAD_W] = t

    if not rope:
        pl.when(pl.program_id(0) == pl.num_programs(0) - 1)(handoff.wait)


def _inproj(x2d, mod3, row_of_tile, g_norm, w_in, q_norm_g, k_norm_g, rope_tabs):
    t = x2d.shape[0]
    tm = INPROJ_TM
    rope = rope_tabs is not None
    row_spec = lambda w: pl.BlockSpec((tm, w), lambda i: (i, 0))
    const = lambda shape: pl.BlockSpec(shape, lambda i: (0,) * len(shape))
    in_specs = [
        row_spec(D_MODEL),
        pl.BlockSpec((None, 1, 3 * D_MODEL), lambda i: (row_of_tile(i), 0, 0)),
        const((1, D_MODEL)),
        _weight_spec(D_IN) if rope else pl.BlockSpec(memory_space=pl.ANY),
        const((1, HEAD_W)),
        const((1, HEAD_W)),
    ]
    args = [x2d, mod3, g_norm, w_in, q_norm_g, k_norm_g]
    if rope:
        n_pos_tiles = rope_tabs[0].shape[0] // tm
        in_specs += [pl.BlockSpec((tm, HEAD_W), lambda i: (i % n_pos_tiles, 0))] * 5
        args += list(rope_tabs)
    widths = (A_Q, A_K, A_V, A_G, B_Q, B_K, B_V, B_G)
    is_kv = (False, True, True, False, False, True, True, False)
    out_shape, out_specs = [], []
    for w, kv in zip(widths, is_kv):
        if kv and not rope:
            n_heads = w // HEAD_W
            out_shape.append(jax.ShapeDtypeStruct((t * n_heads, HEAD_W), F32))
            out_specs.append(pl.BlockSpec((tm * n_heads, HEAD_W), lambda i: (i, 0)))
        else:
            out_shape.append(jax.ShapeDtypeStruct((t, w), BF16))
            out_specs.append(row_spec(w))
    if not rope:
        out_shape.append(jax.ShapeDtypeStruct((D_IN // W_CHUNK, D_MODEL, W_CHUNK), BF16))
        out_specs.append(pl.BlockSpec(memory_space=pl.ANY))
    return pl.pallas_call(
        functools.partial(_inproj_body, rope),
        out_shape=out_shape,
        grid=(t // tm,),
        in_specs=in_specs,
        out_specs=out_specs,
        scratch_shapes=[] if rope else _weight_scratch(D_IN),
        compiler_params=_params(1),
        name="inproj_rope" if rope else "inproj",
    )(*args)


def _diff_lambda_col(lq1_ref, lk1_ref, lq2_ref, lk2_ref, lam_init):
    s1 = jnp.sum(lq1_ref[...] * lk1_ref[...], axis=-1, keepdims=True)
    s2 = jnp.sum(lq2_ref[...] * lk2_ref[...], axis=-1, keepdims=True)
    return jnp.exp(s1) - jnp.exp(s2) + lam_init


def _cache_head(ref, head, n_tok, n_heads):
    return ref[pl.ds(head, n_tok, stride=n_heads), :].astype(BF16)


def _scores(q, k):
    return lax.dot_general(q, k, (((1,), (1,)), ((), ())), preferred_element_type=F32)


def _softmax_pv(s, v):
    m = jnp.max(s, axis=-1, keepdims=True)
    e = jnp.exp2(s - m)
    v1 = jnp.concatenate([v, jnp.ones_like(v)], axis=1)
    ol = jnp.dot(e.astype(BF16), v1, preferred_element_type=F32)
    return ol[:, :HEAD_W], ol[:, HEAD_W:]


def _attend(lam_col, subln, qa_ref, ga_ref, qb_ref, gb_ref, ka, va, kb, vb, o_ref):
    tq = qa_ref.shape[0]
    lane = lax.broadcasted_iota(jnp.int32, (tq, HEAD_W), 1)
    first = lane < DK_A

    def diff_scores(hd):
        q = qa_ref[:, hd * HEAD_W:(hd + 1) * HEAD_W]
        zero = jnp.zeros_like(q)
        q2 = jnp.concatenate([jnp.where(first, q, zero), jnp.where(first, zero, q)], axis=0)
        return _scores(q2, ka(hd))

    def diff_finish(hd, s):
        cols = slice(hd * HEAD_W, (hd + 1) * HEAD_W)
        o2, l2 = _softmax_pv(s, va(hd))
        r2 = 1.0 / l2
        lam = lam_col[hd:hd + 1, :]
        o = o2[:tq] * r2[:tq] - o2[tq:] * (lam * r2[tq:])
        o_ref[:, cols] = (_rms(o) * subln * ga_ref[:, cols].astype(F32)).astype(o_ref.dtype)

    def gqa_scores(h0):
        q = jnp.concatenate(
            [qb_ref[:, (h0 + g) * HEAD_W:(h0 + g + 1) * HEAD_W] for g in range(GQA_STACK)], axis=0)
        return _scores(q, kb(h0 // G_B))

    def gqa_finish(h0, s):
        o, l = _softmax_pv(s, vb(h0 // G_B))
        o = o * (1.0 / l)
        for g in range(GQA_STACK):
            cols = slice((h0 + g) * HEAD_W, (h0 + g + 1) * HEAD_W)
            gate = gb_ref[:, cols].astype(F32)
            o_ref[:, A_V + (h0 + g) * HEAD_W:A_V + (h0 + g + 1) * HEAD_W] = (
                o[g * tq:(g + 1) * tq, :] * gate).astype(o_ref.dtype)

    units = [(diff_scores, diff_finish, hd) for hd in range(H_A)]
    units += [(gqa_scores, gqa_finish, h0) for h0 in range(0, H_B, GQA_STACK)]
    scores = []
    for u, (_, finish, arg) in enumerate(units):
        while len(scores) < min(len(units), u + 1 + SCORE_LOOKAHEAD):
            issue, _, issue_arg = units[len(scores)]
            scores.append(issue(issue_arg))
        finish(arg, scores[u])
        scores[u] = None


def _attn_ctx_body(lam_init, lq1, lk1, lq2, lk2, subln_ref,
                   qa_ref, ka_ref, va_ref, ga_ref, qb_ref, kb_ref, vb_ref, gb_ref, o_ref):
    lam_col = _diff_lambda_col(lq1, lk1, lq2, lk2, lam_init)
    seq = qa_ref.shape[0]
    head = lambda ref, n_heads: (lambda i: _cache_head(ref, i, seq, n_heads))
    _attend(lam_col, subln_ref[...] * (1.0 - lam_init), qa_ref, ga_ref, qb_ref, gb_ref,
            head(ka_ref, H_A), head(va_ref, H_A), head(kb_ref, KV_B), head(vb_ref, KV_B), o_ref)


def _attn_ctx(lam_init, lam_params, subln_g, qa, ka, va, ga, qb, kb, vb, gb, seq):
    t = qa.shape[0]
    const = lambda shape: pl.BlockSpec(shape, lambda b: (0,) * len(shape))
    row_spec = lambda w: pl.BlockSpec((seq, w), lambda b: (b, 0))
    cache_spec = lambda n_heads: pl.BlockSpec((seq * n_heads, HEAD_W), lambda b: (b, 0))
    return pl.pallas_call(
        functools.partial(_attn_ctx_body, lam_init),
        out_shape=jax.ShapeDtypeStruct((t, D_MIX), BF16),
        grid=(t // seq,),
        in_specs=[const((H_A, DK_A))] * 4 + [const((1, DV_A))] + [
            row_spec(A_Q), cache_spec(H_A), cache_spec(H_A), row_spec(A_G),
            row_spec(B_Q), cache_spec(KV_B), cache_spec(KV_B), row_spec(B_G)],
        out_specs=row_spec(D_MIX),
        compiler_params=_params(1),
        name="attn_ctx",
    )(*lam_params, subln_g, qa, ka, va, ga, qb, kb, vb, gb)


def _attn_lat_body(lam_init, n_lat, lq1, lk1, lq2, lk2, subln_ref,
                   qa_ref, ka_ref, va_ref, cka_ref, cva_ref, ga_ref,
                   qb_ref, kb_ref, vb_ref, ckb_ref, cvb_ref, gb_ref, o_ref,
                   ka_all, va_all, kb_all, vb_all):
    @pl.when(pl.program_id(1) == 0)
    def _():
        for new_ref, cache_ref, all_ref in ((ka_ref, cka_ref, ka_all), (va_ref, cva_ref, va_all),
                                            (kb_ref, ckb_ref, kb_all), (vb_ref, cvb_ref, vb_all)):
            n_heads = new_ref.shape[1] // HEAD_W
            n_past = cache_ref.shape[0] // n_heads
            all_ref[0:n_lat, :] = new_ref[...]
            for hd in range(n_heads):
                all_ref[n_lat:, hd * HEAD_W:(hd + 1) * HEAD_W] = _cache_head(cache_ref, hd, n_past, n_heads)

    lam_col = _diff_lambda_col(lq1, lk1, lq2, lk2, lam_init)
    head = lambda ref: (lambda i: ref[:, i * HEAD_W:(i + 1) * HEAD_W])
    _attend(lam_col, subln_ref[...] * (1.0 - lam_init), qa_ref, ga_ref, qb_ref, gb_ref,
            head(ka_all), head(va_all), head(kb_all), head(vb_all), o_ref)


def _attn_lat(lam_init, lam_params, subln_g, qa, ka, va, cka, cva, ga, qb, kb, vb, ckb, cvb, gb,
              n_lat, n_past):
    t = qa.shape[0]
    tq = ATTN_TQ
    nq = n_lat // tq
    const = lambda shape: pl.BlockSpec(shape, lambda b, i: (0,) * len(shape))
    q_spec = lambda w: pl.BlockSpec((tq, w), lambda b, i: (b * nq + i, 0))
    new_spec = lambda w: pl.BlockSpec((n_lat, w), lambda b, i: (b, 0))
    cache_spec = lambda w: pl.BlockSpec((n_past * (w // HEAD_W), HEAD_W), lambda b, i: (b, 0))
    n_all = n_lat + n_past
    return pl.pallas_call(
        functools.partial(_attn_lat_body, lam_init, n_lat),
        out_shape=jax.ShapeDtypeStruct((t, D_MIX), BF16),
        grid=(t // n_lat, nq),
        in_specs=[const((H_A, DK_A))] * 4 + [const((1, DV_A))] + [
            q_spec(A_Q), new_spec(A_K), new_spec(A_V), cache_spec(A_K), cache_spec(A_V), q_spec(A_G),
            q_spec(B_Q), new_spec(B_K), new_spec(B_V), cache_spec(B_K), cache_spec(B_V), q_spec(B_G)],
        out_specs=q_spec(D_MIX),
        scratch_shapes=[pltpu.VMEM((n_all, A_K), BF16), pltpu.VMEM((n_all, A_V), BF16),
                        pltpu.VMEM((n_all, B_K), BF16), pltpu.VMEM((n_all, B_V), BF16)],
        compiler_params=_params(2),
        name="attn_lat",
    )(*lam_params, subln_g, qa, ka, va, cka, cva, ga, qb, kb, vb, ckb, cvb, gb)


def _outproj_body(staged, o_ref, w_out_ref, x_ref, mod_ref, gf_ref, y_ref, *rest):
    if staged:
        wbf_hbm, weights, w_stage, sem, sem_out = rest
        handoff = _stage_weights(w_out_ref, wbf_hbm, weights, w_stage, sem, sem_out)
    else:
        weights = w_out_ref
    o = o_ref[...]
    m = jnp.concatenate(
        [jnp.dot(o, weights[c], preferred_element_type=F32) for c in range(D_MODEL // W_CHUNK)],
        axis=1)
    gate = mod_ref[:, 2 * D_MODEL:3 * D_MODEL]
    y_ref[...] = _rms(x_ref[...] + gate * m) * gf_ref[...]
    if staged:
        pl.when(pl.program_id(0) == pl.num_programs(0) - 1)(handoff.wait)


def _outproj(o, w_out, x2d, mod3, row_of_tile, g_final, staged):
    t = x2d.shape[0]
    tm = OUTPROJ_TM
    out_shape = [jax.ShapeDtypeStruct((t, D_MODEL), F32)]
    out_specs = [pl.BlockSpec((tm, D_MODEL), lambda i: (i, 0))]
    if staged:
        out_shape.append(jax.ShapeDtypeStruct((D_MODEL // W_CHUNK, D_MIX, W_CHUNK), BF16))
        out_specs.append(pl.BlockSpec(memory_space=pl.ANY))
    return pl.pallas_call(
        functools.partial(_outproj_body, staged),
        out_shape=out_shape,
        grid=(t // tm,),
        in_specs=[
            pl.BlockSpec((tm, D_MIX), lambda i: (i, 0)),
            pl.BlockSpec(memory_space=pl.ANY) if staged else _weight_spec(D_MODEL),
            pl.BlockSpec((tm, D_MODEL), lambda i: (i, 0)),
            pl.BlockSpec((None, 1, 3 * D_MODEL), lambda i: (row_of_tile(i), 0, 0)),
            pl.BlockSpec((1, D_MODEL), lambda i: (0, 0)),
        ],
        out_specs=out_specs,
        scratch_shapes=_weight_scratch(D_MODEL) if staged else [],
        compiler_params=_params(1),
        name="outproj",
    )(o, w_out, x2d, mod3, g_final)


def _rope_tables(n_tok):
    n_rows = n_tok // GRID_W
    rows = np.repeat(np.arange(n_rows, dtype=np.float32), GRID_W)
    cols = np.tile(np.arange(GRID_W, dtype=np.float32), n_rows)

    def cos_sin(rot_dim):
        quarter = rot_dim // 4
        inv_freq = (1.0 / (np.float32(ROPE_THETA) ** (np.arange(quarter, dtype=np.float32) / quarter))
                    ).astype(np.float32)
        ang = np.concatenate([rows[:, None] * inv_freq, cols[:, None] * inv_freq], axis=-1)
        return np.cos(ang).astype(np.float32), np.sin(ang).astype(np.float32)

    ca, sa = cos_sin(DK_A)
    za = np.zeros_like(sa)
    cb, sb = cos_sin(DH_B)
    tabs = (
        np.concatenate([ca, ca, ca, ca], axis=-1),
        np.concatenate([-sa, za, -sa, za], axis=-1),
        np.concatenate([za, sa, za, sa], axis=-1),
        np.concatenate([cb, cb], axis=-1),
        np.concatenate([-sb, sb], axis=-1),
    )
    return tuple(jnp.asarray(t) for t in tabs)


def kernel(x_prompt, x_sample, cache_diff_k, cache_diff_v, cache_gqa_k, cache_gqa_v, c, c_ctx,
           w_ada, b_ada, g_norm, w_in, lam_q1, lam_k1, lam_q2, lam_k2, subln_g, q_norm_g, k_norm_g,
           w_out, g_final):
    bp, n_ctx, d = x_prompt.shape
    bs, n_lat, _ = x_sample.shape
    depth = w_in.shape[0]
    n_past = cache_diff_k.shape[2]
    assert depth == 1 and d == D_MODEL and bs + 1 <= N_COND
    assert n_lat % INPROJ_TM == 0 and n_lat % OUTPROJ_TM == 0 and n_lat % ATTN_TQ == 0
    l = 0
    lam_init = 0.8 - 0.6 * math.exp(-0.3 * l)

    cond = jnp.concatenate(
        [c_ctx[None, :], c, jnp.zeros((N_COND - 1 - bs, d), F32)], axis=0)
    mod = _adaln(cond, w_ada[l], b_ada[l][None, :])
    mod3 = mod[:, None, :]

    gn = g_norm[l][None, :]
    qn = q_norm_g[l][None, :]
    kn = k_norm_g[l][None, :]
    lam_params = (lam_q1[l], lam_k1[l], lam_q2[l], lam_k2[l])
    subln = subln_g[l][None, :]
    gf = g_final[None, :]

    xp2 = x_prompt.reshape(bp * n_ctx, d)
    qa, ka, va, ga, qb, kb, vb, gb, w_in_bf16 = _inproj(
        xp2, mod3, lambda i: 0, gn, w_in[l], qn, kn, None)
    o = _attn_ctx(lam_init, lam_params, subln, qa, ka, va, ga, qb, kb, vb, gb, n_ctx)
    y_prompt, w_out_bf16 = _outproj(o, w_out[l], xp2, mod3, lambda i: 0, gf, True)
    y_prompt = y_prompt.reshape(bp, n_ctx, d)
    new_diff_k = ka.reshape(bp, 1, n_ctx, H_A, 2 * DK_A)
    new_diff_v = va.reshape(bp, 1, n_ctx, H_A, DV_A)
    new_gqa_k = kb.reshape(bp, 1, n_ctx, KV_B, DH_B)
    new_gqa_v = vb.reshape(bp, 1, n_ctx, KV_B, DH_B)

    xs2 = x_sample.reshape(bs * n_lat, d)
    in_tiles = n_lat // INPROJ_TM
    qa, ka, va, ga, qb, kb, vb, gb = _inproj(
        xs2, mod3, lambda i: 1 + i // in_tiles, gn, w_in_bf16, qn, kn, _rope_tables(n_lat))
    cka = cache_diff_k[:, l].reshape(bs * n_past * H_A, HEAD_W)
    cva = cache_diff_v[:, l].reshape(bs * n_past * H_A, HEAD_W)
    ckb = cache_gqa_k[:, l].reshape(bs * n_past * KV_B, HEAD_W)
    cvb = cache_gqa_v[:, l].reshape(bs * n_past * KV_B, HEAD_W)
    o = _attn_lat(lam_init, lam_params, subln, qa, ka, va, cka, cva, ga, qb, kb, vb, ckb, cvb, gb,
                  n_lat, n_past)
    out_tiles = n_lat // OUTPROJ_TM
    (y_sample,) = _outproj(o, w_out_bf16, xs2, mod3, lambda i: 1 + i // out_tiles, gf, False)
    y_sample = y_sample.reshape(bs, n_lat, d)

    return (y_prompt, y_sample, new_diff_k, new_diff_v, new_gqa_k, new_gqa_v)
```

```python
import functools
import math

import jax
import jax.numpy as jnp
import numpy as np
from jax import lax
from jax.experimental import pallas as pl
from jax.experimental.pallas import tpu as pltpu

D_MODEL = 2048
GRID_W = 64
ROPE_THETA = 10000.0
EPS = 1e-6
H_A = 8
DK_A = 64
DV_A = 2 * DK_A
H_B = 8
KV_B = 2
DH_B = 128
G_B = H_B // KV_B
HEAD_W = 128
A_Q = H_A * 2 * DK_A
A_K = H_A * 2 * DK_A
A_V = H_A * DV_A
A_G = H_A * DV_A
B_Q = H_B * DH_B
B_K = KV_B * DH_B
B_V = KV_B * DH_B
B_G = H_B * DH_B
D_IN = A_Q + A_K + A_V + A_G + B_Q + B_K + B_V + B_G
D_MIX = A_V + B_Q
OFF_AQ = 0
OFF_AK = OFF_AQ + A_Q
OFF_AV = OFF_AK + A_K
OFF_AG = OFF_AV + A_V
OFF_BQ = OFF_AG + A_G
OFF_BK = OFF_BQ + B_Q
OFF_BV = OFF_BK + B_K
OFF_BG = OFF_BV + B_V

N_COND = 8
ADA_TK = 256
INPROJ_TM = 256
W_CHUNK = 256
STAGE_BYTES = 2 * 1024 * 1024
STAGE_SLOTS = 4
OUTPROJ_TM = 512
ATTN_TQ = 256
GQA_STACK = 2
SCORE_LOOKAHEAD = 2
QSCALE_A = math.log2(math.e) / math.sqrt(DK_A)
QSCALE_B = math.log2(math.e) / math.sqrt(DH_B)
VMEM_LIMIT = 56 * 1024 * 1024

BF16 = jnp.bfloat16
F32 = jnp.float32


def _params(n_grid_axes):
    return pltpu.CompilerParams(
        dimension_semantics=("arbitrary",) * n_grid_axes,
        vmem_limit_bytes=VMEM_LIMIT,
    )


def _silu(x):
    return x * jax.nn.sigmoid(x)


def _rms(x):
    return x * lax.rsqrt(jnp.mean(x * x, axis=-1, keepdims=True) + EPS)


def _stage_rows(n_cols):
    return 1 << int(math.log2(STAGE_BYTES // (n_cols * 4)))


def _weight_scratch(n_cols):
    return [pltpu.VMEM((n_cols // W_CHUNK, D_MODEL, W_CHUNK), BF16),
            pltpu.VMEM((STAGE_SLOTS, _stage_rows(n_cols), n_cols), F32),
            pltpu.SemaphoreType.DMA((STAGE_SLOTS,)),
            pltpu.SemaphoreType.DMA(())]


def _slab_copy(w_hbm, w_stage, sem, k):
    slot = k % STAGE_SLOTS
    n_rows = w_stage.shape[1]
    return pltpu.make_async_copy(
        w_hbm.at[pl.ds(k * n_rows, n_rows), :], w_stage.at[slot], sem.at[slot])


def _stage_weights(w_hbm, wbf_hbm, w_scr, w_stage, sem, sem_out):
    n_chunks = w_scr.shape[0]
    slab_rows = w_stage.shape[1]
    n_slabs = w_scr.shape[1] // slab_rows
    copy = functools.partial(_slab_copy, w_hbm, w_stage, sem)
    handoff = pltpu.make_async_copy(w_scr, wbf_hbm, sem_out)

    @pl.when(pl.program_id(0) == 0)
    def _():
        for k in range(STAGE_SLOTS - 1):
            copy(k).start()
        for k in range(n_slabs):
            if k + STAGE_SLOTS - 1 < n_slabs:
                copy(k + STAGE_SLOTS - 1).start()
            copy(k).wait()
            rows = slice(k * slab_rows, (k + 1) * slab_rows)
            for c in range(n_chunks):
                w_scr[c, rows, :] = w_stage[
                    k % STAGE_SLOTS, :, c * W_CHUNK:(c + 1) * W_CHUNK].astype(BF16)
        handoff.start()

    return handoff


def _weight_spec(n_cols):
    return pl.BlockSpec((n_cols // W_CHUNK, D_MODEL, W_CHUNK), lambda i: (0, 0, 0),
                        pipeline_mode=pl.Buffered(1))


def _adaln_body(cond_ref, w_ref, b_ref, o_ref):
    @pl.when(pl.program_id(0) == 0)
    def _():
        o_ref[...] = jnp.broadcast_to(b_ref[...], o_ref.shape)

    a = _silu(cond_ref[...]).astype(BF16)
    w = w_ref[...].astype(BF16)
    o_ref[...] += jnp.dot(a, w, preferred_element_type=F32)


def _adaln(cond, w_ada, b_ada):
    d3 = w_ada.shape[1]
    return pl.pallas_call(
        _adaln_body,
        out_shape=jax.ShapeDtypeStruct((N_COND, d3), F32),
        grid=(D_MODEL // ADA_TK,),
        in_specs=[
            pl.BlockSpec((N_COND, ADA_TK), lambda k: (0, k)),
            pl.BlockSpec((ADA_TK, d3), lambda k: (k, 0)),
            pl.BlockSpec((1, d3), lambda k: (0, 0)),
        ],
        out_specs=pl.BlockSpec((N_COND, d3), lambda k: (0, 0)),
        compiler_params=_params(1),
        name="adaln",
    )(cond, w_ada, b_ada)


def _rope_a(x, c, s_up, s_dn):
    return x * c + pltpu.roll(x, 96, 1) * s_up + pltpu.roll(x, 32, 1) * s_dn


def _rope_b(x, c, s):
    return x * c + pltpu.roll(x, 64, 1) * s


def _inproj_body(rope, x_ref, mod_ref, gn_ref, w_in_ref, qn_ref, kn_ref, *rest):
    if rope:
        ca_ref, sau_ref, sad_ref, cb_ref, sb_ref = rest[:5]
        rest = rest[5:]
    qa_o, ka_o, va_o, ga_o, qb_o, kb_o, vb_o, gb_o = rest[:8]
    cache_layout = not rope
    if rope:
        weights = w_in_ref
    else:
        wbf_hbm, weights, w_stage, sem, sem_out = rest[8:]
        handoff = _stage_weights(w_in_ref, wbf_hbm, weights, w_stage, sem, sem_out)
    tm = x_ref.shape[0]

    x = x_ref[...]
    shift = mod_ref[:, 0:D_MODEL]
    scale = mod_ref[:, D_MODEL:2 * D_MODEL]
    h = ((_rms(x) * gn_ref[...]) * (1.0 + scale) + shift).astype(BF16)

    if rope:
        ca, sau, sad = ca_ref[...], sau_ref[...], sad_ref[...]
        cb, sb = cb_ref[...], sb_ref[...]

    def rope_a(t):
        return _rope_a(t, ca, sau, sad) if rope else t

    def rope_b(t):
        return _rope_b(t, cb, sb) if rope else t

    ident = lambda t: t
    regions = (
        (OFF_AQ, A_Q, qa_o, lambda t: rope_a(t) * QSCALE_A, False),
        (OFF_AK, A_K, ka_o, rope_a, True),
        (OFF_AV, A_V, va_o, ident, True),
        (OFF_AG, A_G, ga_o, _silu, False),
        (OFF_BQ, B_Q, qb_o, lambda t: rope_b(_rms(t) * qn_ref[...]) * QSCALE_B, False),
        (OFF_BK, B_K, kb_o, lambda t: rope_b(_rms(t) * kn_ref[...]), True),
        (OFF_BV, B_V, vb_o, ident, True),
        (OFF_BG, B_G, gb_o, _silu, False),
    )
    for start, width, o_ref, epi, is_kv in regions:
        n_heads = width // HEAD_W
        for c0 in range(0, width, W_CHUNK):
            z = jnp.dot(h, weights[(start + c0) // W_CHUNK], preferred_element_type=F32)
            for h0 in range(0, W_CHUNK, HEAD_W):
                hd = (c0 + h0) // HEAD_W
                t = epi(z[:, h0:h0 + HEAD_W]).astype(o_ref.dtype)
                if is_kv and cache_layout:
                    o_ref[pl.ds(hd, tm, stride=n_heads), :] = t
                else:
                    o_ref[:, hd * HEAD_W:(hd + 1) * HEAD_W] = t

    if not rope:
        pl.when(pl.program_id(0) == pl.num_programs(0) - 1)(handoff.wait)


def _inproj(x2d, mod3, row_of_tile, g_norm, w_in, q_norm_g, k_norm_g, rope_tabs):
    t = x2d.shape[0]
    tm = INPROJ_TM
    rope = rope_tabs is not None
    row_spec = lambda w: pl.BlockSpec((tm, w), lambda i: (i, 0))
    const = lambda shape: pl.BlockSpec(shape, lambda i: (0,) * len(shape))
    in_specs = [
        row_spec(D_MODEL),
        pl.BlockSpec((None, 1, 3 * D_MODEL), lambda i: (row_of_tile(i), 0, 0)),
        const((1, D_MODEL)),
        _weight_spec(D_IN) if rope else pl.BlockSpec(memory_space=pl.ANY),
        const((1, HEAD_W)),
        const((1, HEAD_W)),
    ]
    args = [x2d, mod3, g_norm, w_in, q_norm_g, k_norm_g]
    if rope:
        n_pos_tiles = rope_tabs[0].shape[0] // tm
        in_specs += [pl.BlockSpec((tm, HEAD_W), lambda i: (i % n_pos_tiles, 0))] * 5
        args += list(rope_tabs)
    widths = (A_Q, A_K, A_V, A_G, B_Q, B_K, B_V, B_G)
    is_kv = (False, True, True, False, False, True, True, False)
    out_shape, out_specs = [], []
    for w, kv in zip(widths, is_kv):
        if kv and not rope:
            n_heads = w // HEAD_W
            out_shape.append(jax.ShapeDtypeStruct((t * n_heads, HEAD_W), F32))
            out_specs.append(pl.BlockSpec((tm * n_heads, HEAD_W), lambda i: (i, 0)))
        else:
            out_shape.append(jax.ShapeDtypeStruct((t, w), BF16))
            out_specs.append(row_spec(w))
    if not rope:
        out_shape.append(jax.ShapeDtypeStruct((D_IN // W_CHUNK, D_MODEL, W_CHUNK), BF16))
        out_specs.append(pl.BlockSpec(memory_space=pl.ANY))
    return pl.pallas_call(
        functools.partial(_inproj_body, rope),
        out_shape=out_shape,
        grid=(t // tm,),
        in_specs=in_specs,
        out_specs=out_specs,
        scratch_shapes=[] if rope else _weight_scratch(D_IN),
        compiler_params=_params(1),
        name="inproj_rope" if rope else "inproj",
    )(*args)


def _diff_lambda_col(lq1_ref, lk1_ref, lq2_ref, lk2_ref, lam_init):
    s1 = jnp.sum(lq1_ref[...] * lk1_ref[...], axis=-1, keepdims=True)
    s2 = jnp.sum(lq2_ref[...] * lk2_ref[...], axis=-1, keepdims=True)
    return jnp.exp(s1) - jnp.exp(s2) + lam_init


def _cache_head(ref, head, n_tok, n_heads):
    return ref[pl.ds(head, n_tok, stride=n_heads), :].astype(BF16)


def _scores(q, k):
    return lax.dot_general(q, k, (((1,), (1,)), ((), ())), preferred_element_type=F32)


def _softmax_pv(s, v):
    m = jnp.max(s, axis=-1, keepdims=True)
    e = jnp.exp2(s - m)
    v1 = jnp.concatenate([v, jnp.ones_like(v)], axis=1)
    ol = jnp.dot(e.astype(BF16), v1, preferred_element_type=F32)
    return ol[:, :HEAD_W], ol[:, HEAD_W:]


def _attend(lam_col, subln, qa_ref, ga_ref, qb_ref, gb_ref, ka, va, kb, vb, o_ref):
    tq = qa_ref.shape[0]
    lane = lax.broadcasted_iota(jnp.int32, (tq, HEAD_W), 1)
    first = lane < DK_A

    def diff_scores(hd):
        q = qa_ref[:, hd * HEAD_W:(hd + 1) * HEAD_W]
        zero = jnp.zeros_like(q)
        q2 = jnp.concatenate([jnp.where(first, q, zero), jnp.where(first, zero, q)], axis=0)
        return _scores(q2, ka(hd))

    def diff_finish(hd, s):
        cols = slice(hd * HEAD_W, (hd + 1) * HEAD_W)
        o2, l2 = _softmax_pv(s, va(hd))
        r2 = 1.0 / l2
        lam = lam_col[hd:hd + 1, :]
        o = o2[:tq] * r2[:tq] - o2[tq:] * (lam * r2[tq:])
        o_ref[:, cols] = (_rms(o) * subln * ga_ref[:, cols].astype(F32)).astype(o_ref.dtype)

    def gqa_scores(h0):
        q = jnp.concatenate(
            [qb_ref[:, (h0 + g) * HEAD_W:(h0 + g + 1) * HEAD_W] for g in range(GQA_STACK)], axis=0)
        return _scores(q, kb(h0 // G_B))

    def gqa_finish(h0, s):
        o, l = _softmax_pv(s, vb(h0 // G_B))
        o = o * (1.0 / l)
        for g in range(GQA_STACK):
            cols = slice((h0 + g) * HEAD_W, (h0 + g + 1) * HEAD_W)
            gate = gb_ref[:, cols].astype(F32)
            o_ref[:, A_V + (h0 + g) * HEAD_W:A_V + (h0 + g + 1) * HEAD_W] = (
                o[g * tq:(g + 1) * tq, :] * gate).astype(o_ref.dtype)

    units = [(diff_scores, diff_finish, hd) for hd in range(H_A)]
    units += [(gqa_scores, gqa_finish, h0) for h0 in range(0, H_B, GQA_STACK)]
    scores = []
    for u, (_, finish, arg) in enumerate(units):
        while len(scores) < min(len(units), u + 1 + SCORE_LOOKAHEAD):
            issue, _, issue_arg = units[len(scores)]
            scores.append(issue(issue_arg))
        finish(arg, scores[u])
        scores[u] = None


def _project_out(o, weights, x_ref, mod_ref, gf_ref, y_ref):
    m = jnp.concatenate(
        [jnp.dot(o, weights[c], preferred_element_type=F32) for c in range(D_MODEL // W_CHUNK)],
        axis=1)
    gate = mod_ref[:, 2 * D_MODEL:3 * D_MODEL]
    y_ref[...] = _rms(x_ref[...] + gate * m) * gf_ref[...]


def _attn_ctx_body(lam_init, lq1, lk1, lq2, lk2, subln_ref,
                   qa_ref, ka_ref, va_ref, ga_ref, qb_ref, kb_ref, vb_ref, gb_ref,
                   w_out_hbm, x_ref, mod_ref, gf_ref, y_ref, wbf_hbm,
                   w_scr, w_stage, sem, sem_out, o_scr):
    handoff = _stage_weights(w_out_hbm, wbf_hbm, w_scr, w_stage, sem, sem_out)
    lam_col = _diff_lambda_col(lq1, lk1, lq2, lk2, lam_init)
    seq = qa_ref.shape[0]
    head = lambda ref, n_heads: (lambda i: _cache_head(ref, i, seq, n_heads))
    _attend(lam_col, subln_ref[...] * (1.0 - lam_init), qa_ref, ga_ref, qb_ref, gb_ref,
            head(ka_ref, H_A), head(va_ref, H_A), head(kb_ref, KV_B), head(vb_ref, KV_B), o_scr)
    _project_out(o_scr[...], w_scr, x_ref, mod_ref, gf_ref, y_ref)
    pl.when(pl.program_id(0) == pl.num_programs(0) - 1)(handoff.wait)


def _attn_ctx(lam_init, lam_params, subln_g, qa, ka, va, ga, qb, kb, vb, gb, w_out, x2d, mod3,
              g_final, seq):
    t = qa.shape[0]
    const = lambda shape: pl.BlockSpec(shape, lambda b: (0,) * len(shape))
    row_spec = lambda w: pl.BlockSpec((seq, w), lambda b: (b, 0))
    cache_spec = lambda n_heads: pl.BlockSpec((seq * n_heads, HEAD_W), lambda b: (b, 0))
    return pl.pallas_call(
        functools.partial(_attn_ctx_body, lam_init),
        out_shape=[jax.ShapeDtypeStruct((t, D_MODEL), F32),
                   jax.ShapeDtypeStruct((D_MODEL // W_CHUNK, D_MIX, W_CHUNK), BF16)],
        grid=(t // seq,),
        in_specs=[const((H_A, DK_A))] * 4 + [const((1, DV_A))] + [
            row_spec(A_Q), cache_spec(H_A), cache_spec(H_A), row_spec(A_G),
            row_spec(B_Q), cache_spec(KV_B), cache_spec(KV_B), row_spec(B_G),
            pl.BlockSpec(memory_space=pl.ANY),
            row_spec(D_MODEL),
            pl.BlockSpec((None, 1, 3 * D_MODEL), lambda b: (0, 0, 0)),
            const((1, D_MODEL))],
        out_specs=[row_spec(D_MODEL), pl.BlockSpec(memory_space=pl.ANY)],
        scratch_shapes=_weight_scratch(D_MODEL) + [pltpu.VMEM((seq, D_MIX), BF16)],
        compiler_params=_params(1),
        name="attn_ctx",
    )(*lam_params, subln_g, qa, ka, va, ga, qb, kb, vb, gb, w_out, x2d, mod3, g_final)


def _attn_lat_body(lam_init, n_lat, lq1, lk1, lq2, lk2, subln_ref,
                   qa_ref, ka_ref, va_ref, cka_ref, cva_ref, ga_ref,
                   qb_ref, kb_ref, vb_ref, ckb_ref, cvb_ref, gb_ref, o_ref,
                   ka_all, va_all, kb_all, vb_all):
    @pl.when(pl.program_id(1) == 0)
    def _():
        for new_ref, cache_ref, all_ref in ((ka_ref, cka_ref, ka_all), (va_ref, cva_ref, va_all),
                                            (kb_ref, ckb_ref, kb_all), (vb_ref, cvb_ref, vb_all)):
            n_heads = new_ref.shape[1] // HEAD_W
            n_past = cache_ref.shape[0] // n_heads
            all_ref[0:n_lat, :] = new_ref[...]
            for hd in range(n_heads):
                all_ref[n_lat:, hd * HEAD_W:(hd + 1) * HEAD_W] = _cache_head(cache_ref, hd, n_past, n_heads)

    lam_col = _diff_lambda_col(lq1, lk1, lq2, lk2, lam_init)
    head = lambda ref: (lambda i: ref[:, i * HEAD_W:(i + 1) * HEAD_W])
    _attend(lam_col, subln_ref[...] * (1.0 - lam_init), qa_ref, ga_ref, qb_ref, gb_ref,
            head(ka_all), head(va_all), head(kb_all), head(vb_all), o_ref)


def _attn_lat(lam_init, lam_params, subln_g, qa, ka, va, cka, cva, ga, qb, kb, vb, ckb, cvb, gb,
              n_lat, n_past):
    t = qa.shape[0]
    tq = ATTN_TQ
    nq = n_lat // tq
    const = lambda shape: pl.BlockSpec(shape, lambda b, i: (0,) * len(shape))
    q_spec = lambda w: pl.BlockSpec((tq, w), lambda b, i: (b * nq + i, 0))
    new_spec = lambda w: pl.BlockSpec((n_lat, w), lambda b, i: (b, 0))
    cache_spec = lambda w: pl.BlockSpec((n_past * (w // HEAD_W), HEAD_W), lambda b, i: (b, 0))
    n_all = n_lat + n_past
    return pl.pallas_call(
        functools.partial(_attn_lat_body, lam_init, n_lat),
        out_shape=jax.ShapeDtypeStruct((t, D_MIX), BF16),
        grid=(t // n_lat, nq),
        in_specs=[const((H_A, DK_A))] * 4 + [const((1, DV_A))] + [
            q_spec(A_Q), new_spec(A_K), new_spec(A_V), cache_spec(A_K), cache_spec(A_V), q_spec(A_G),
            q_spec(B_Q), new_spec(B_K), new_spec(B_V), cache_spec(B_K), cache_spec(B_V), q_spec(B_G)],
        out_specs=q_spec(D_MIX),
        scratch_shapes=[pltpu.VMEM((n_all, A_K), BF16), pltpu.VMEM((n_all, A_V), BF16),
                        pltpu.VMEM((n_all, B_K), BF16), pltpu.VMEM((n_all, B_V), BF16)],
        compiler_params=_params(2),
        name="attn_lat",
    )(*lam_params, subln_g, qa, ka, va, cka, cva, ga, qb, kb, vb, ckb, cvb, gb)


def _outproj_body(o_ref, w_ref, x_ref, mod_ref, gf_ref, y_ref):
    _project_out(o_ref[...], w_ref, x_ref, mod_ref, gf_ref, y_ref)


def _outproj(o, w_out_bf16, x2d, mod3, row_of_tile, g_final):
    t = x2d.shape[0]
    tm = OUTPROJ_TM
    return pl.pallas_call(
        _outproj_body,
        out_shape=jax.ShapeDtypeStruct((t, D_MODEL), F32),
        grid=(t // tm,),
        in_specs=[
            pl.BlockSpec((tm, D_MIX), lambda i: (i, 0)),
            _weight_spec(D_MODEL),
            pl.BlockSpec((tm, D_MODEL), lambda i: (i, 0)),
            pl.BlockSpec((None, 1, 3 * D_MODEL), lambda i: (row_of_tile(i), 0, 0)),
            pl.BlockSpec((1, D_MODEL), lambda i: (0, 0)),
        ],
        out_specs=pl.BlockSpec((tm, D_MODEL), lambda i: (i, 0)),
        compiler_params=_params(1),
        name="outproj",
    )(o, w_out_bf16, x2d, mod3, g_final)


def _rope_tables(n_tok):
    n_rows = n_tok // GRID_W
    rows = np.repeat(np.arange(n_rows, dtype=np.float32), GRID_W)
    cols = np.tile(np.arange(GRID_W, dtype=np.float32), n_rows)

    def cos_sin(rot_dim):
        quarter = rot_dim // 4
        inv_freq = (1.0 / (np.float32(ROPE_THETA) ** (np.arange(quarter, dtype=np.float32) / quarter))
                    ).astype(np.float32)
        ang = np.concatenate([rows[:, None] * inv_freq, cols[:, None] * inv_freq], axis=-1)
        return np.cos(ang).astype(np.float32), np.sin(ang).astype(np.float32)

    ca, sa = cos_sin(DK_A)
    za = np.zeros_like(sa)
    cb, sb = cos_sin(DH_B)
    tabs = (
        np.concatenate([ca, ca, ca, ca], axis=-1),
        np.concatenate([-sa, za, -sa, za], axis=-1),
        np.concatenate([za, sa, za, sa], axis=-1),
        np.concatenate([cb, cb], axis=-1),
        np.concatenate([-sb, sb], axis=-1),
    )
    return tuple(jnp.asarray(t) for t in tabs)


def kernel(x_prompt, x_sample, cache_diff_k, cache_diff_v, cache_gqa_k, cache_gqa_v, c, c_ctx,
           w_ada, b_ada, g_norm, w_in, lam_q1, lam_k1, lam_q2, lam_k2, subln_g, q_norm_g, k_norm_g,
           w_out, g_final):
    bp, n_ctx, d = x_prompt.shape
    bs, n_lat, _ = x_sample.shape
    depth = w_in.shape[0]
    n_past = cache_diff_k.shape[2]
    assert depth == 1 and d == D_MODEL and bs + 1 <= N_COND
    assert n_lat % INPROJ_TM == 0 and n_lat % OUTPROJ_TM == 0 and n_lat % ATTN_TQ == 0
    l = 0
    lam_init = 0.8 - 0.6 * math.exp(-0.3 * l)

    cond = jnp.concatenate(
        [c_ctx[None, :], c, jnp.zeros((N_COND - 1 - bs, d), F32)], axis=0)
    mod = _adaln(cond, w_ada[l], b_ada[l][None, :])
    mod3 = mod[:, None, :]

    gn = g_norm[l][None, :]
    qn = q_norm_g[l][None, :]
    kn = k_norm_g[l][None, :]
    lam_params = (lam_q1[l], lam_k1[l], lam_q2[l], lam_k2[l])
    subln = subln_g[l][None, :]
    gf = g_final[None, :]

    xp2 = x_prompt.reshape(bp * n_ctx, d)
    qa, ka, va, ga, qb, kb, vb, gb, w_in_bf16 = _inproj(
        xp2, mod3, lambda i: 0, gn, w_in[l], qn, kn, None)
    y_prompt, w_out_bf16 = _attn_ctx(lam_init, lam_params, subln, qa, ka, va, ga, qb, kb, vb, gb,
                                     w_out[l], xp2, mod3, gf, n_ctx)
    y_prompt = y_prompt.reshape(bp, n_ctx, d)
    new_diff_k = ka.reshape(bp, 1, n_ctx, H_A, 2 * DK_A)
    new_diff_v = va.reshape(bp, 1, n_ctx, H_A, DV_A)
    new_gqa_k = kb.reshape(bp, 1, n_ctx, KV_B, DH_B)
    new_gqa_v = vb.reshape(bp, 1, n_ctx, KV_B, DH_B)

    xs2 = x_sample.reshape(bs * n_lat, d)
    in_tiles = n_lat // INPROJ_TM
    qa, ka, va, ga, qb, kb, vb, gb = _inproj(
        xs2, mod3, lambda i: 1 + i // in_tiles, gn, w_in_bf16, qn, kn, _rope_tables(n_lat))
    cka = cache_diff_k[:, l].reshape(bs * n_past * H_A, HEAD_W)
    cva = cache_diff_v[:, l].reshape(bs * n_past * H_A, HEAD_W)
    ckb = cache_gqa_k[:, l].reshape(bs * n_past * KV_B, HEAD_W)
    cvb = cache_gqa_v[:, l].reshape(bs * n_past * KV_B, HEAD_W)
    o = _attn_lat(lam_init, lam_params, subln, qa, ka, va, cka, cva, ga, qb, kb, vb, ckb, cvb, gb,
                  n_lat, n_past)
    out_tiles = n_lat // OUTPROJ_TM
    y_sample = _outproj(o, w_out_bf16, xs2, mod3, lambda i: 1 + i // out_tiles, gf)
    y_sample = y_sample.reshape(bs, n_lat, d)

    return (y_prompt, y_sample, new_diff_k, new_diff_v, new_gqa_k, new_gqa_v)
```

```python
import functools
import math

import jax
import jax.numpy as jnp
import numpy as np
from jax import lax
from jax.experimental import pallas as pl
from jax.experimental.pallas import tpu as pltpu

D_MODEL = 2048
GRID_W = 64
ROPE_THETA = 10000.0
EPS = 1e-6
H_A = 8
DK_A = 64
DV_A = 2 * DK_A
H_B = 8
KV_B = 2
DH_B = 128
G_B = H_B // KV_B
HEAD_W = 128
A_Q = H_A * 2 * DK_A
A_K = H_A * 2 * DK_A
A_V = H_A * DV_A
A_G = H_A * DV_A
B_Q = H_B * DH_B
B_K = KV_B * DH_B
B_V = KV_B * DH_B
B_G = H_B * DH_B
D_IN = A_Q + A_K + A_V + A_G + B_Q + B_K + B_V + B_G
D_MIX = A_V + B_Q
OFF_AQ = 0
OFF_AK = OFF_AQ + A_Q
OFF_AV = OFF_AK + A_K
OFF_AG = OFF_AV + A_V
OFF_BQ = OFF_AG + A_G
OFF_BK = OFF_BQ + B_Q
OFF_BV = OFF_BK + B_K
OFF_BG = OFF_BV + B_V

N_COND = 8
ADA_TK = 512
INPROJ_TM = 256
W_CHUNK = 256
STAGE_BYTES = 2 * 1024 * 1024
STAGE_SLOTS = 4
OUTPROJ_TM = 512
ATTN_TQ = 256
GQA_STACK = 2
SCORE_LOOKAHEAD = 2
QSCALE_A = math.log2(math.e) / math.sqrt(DK_A)
QSCALE_B = math.log2(math.e) / math.sqrt(DH_B)
VMEM_LIMIT = 56 * 1024 * 1024

BF16 = jnp.bfloat16
F32 = jnp.float32


def _params(n_grid_axes):
    return pltpu.CompilerParams(
        dimension_semantics=("arbitrary",) * n_grid_axes,
        vmem_limit_bytes=VMEM_LIMIT,
    )


def _silu(x):
    return x * jax.nn.sigmoid(x)


def _rms(x):
    return x * lax.rsqrt(jnp.mean(x * x, axis=-1, keepdims=True) + EPS)


def _stage_rows(n_cols):
    return 1 << int(math.log2(STAGE_BYTES // (n_cols * 4)))


def _weight_scratch(n_cols):
    return [pltpu.VMEM((n_cols // W_CHUNK, D_MODEL, W_CHUNK), BF16),
            pltpu.VMEM((STAGE_SLOTS, _stage_rows(n_cols), n_cols), F32),
            pltpu.SemaphoreType.DMA((STAGE_SLOTS,)),
            pltpu.SemaphoreType.DMA(())]


def _slab_copy(w_hbm, w_stage, sem, k):
    slot = k % STAGE_SLOTS
    n_rows = w_stage.shape[1]
    return pltpu.make_async_copy(
        w_hbm.at[pl.ds(k * n_rows, n_rows), :], w_stage.at[slot], sem.at[slot])


def _stage_weights(w_hbm, wbf_hbm, w_scr, w_stage, sem, sem_out):
    n_chunks = w_scr.shape[0]
    slab_rows = w_stage.shape[1]
    n_slabs = w_scr.shape[1] // slab_rows
    copy = functools.partial(_slab_copy, w_hbm, w_stage, sem)
    handoff = pltpu.make_async_copy(w_scr, wbf_hbm, sem_out)

    @pl.when(pl.program_id(0) == 0)
    def _():
        for k in range(STAGE_SLOTS - 1):
            copy(k).start()
        for k in range(n_slabs):
            if k + STAGE_SLOTS - 1 < n_slabs:
                copy(k + STAGE_SLOTS - 1).start()
            copy(k).wait()
            rows = slice(k * slab_rows, (k + 1) * slab_rows)
            for c in range(n_chunks):
                w_scr[c, rows, :] = w_stage[
                    k % STAGE_SLOTS, :, c * W_CHUNK:(c + 1) * W_CHUNK].astype(BF16)
        handoff.start()

    return handoff


def _weight_spec(n_cols):
    return pl.BlockSpec((n_cols // W_CHUNK, D_MODEL, W_CHUNK), lambda i: (0, 0, 0),
                        pipeline_mode=pl.Buffered(1))


def _adaln_body(cond_ref, w_ref, b_ref, o_ref):
    @pl.when(pl.program_id(0) == 0)
    def _():
        o_ref[...] = jnp.broadcast_to(b_ref[...], o_ref.shape)

    a = _silu(cond_ref[...]).astype(BF16)
    w = w_ref[...].astype(BF16)
    o_ref[...] += jnp.dot(a, w, preferred_element_type=F32)


def _adaln(cond, w_ada, b_ada):
    d3 = w_ada.shape[1]
    return pl.pallas_call(
        _adaln_body,
        out_shape=jax.ShapeDtypeStruct((N_COND, d3), F32),
        grid=(D_MODEL // ADA_TK,),
        in_specs=[
            pl.BlockSpec((N_COND, ADA_TK), lambda k: (0, k)),
            pl.BlockSpec((ADA_TK, d3), lambda k: (k, 0)),
            pl.BlockSpec((1, d3), lambda k: (0, 0)),
        ],
        out_specs=pl.BlockSpec((N_COND, d3), lambda k: (0, 0)),
        compiler_params=_params(1),
        name="adaln",
    )(cond, w_ada, b_ada)


def _rope_a(x, c, s_up, s_dn):
    return x * c + pltpu.roll(x, 96, 1) * s_up + pltpu.roll(x, 32, 1) * s_dn


def _rope_b(x, c, s):
    return x * c + pltpu.roll(x, 64, 1) * s


def _inproj_body(rope, x_ref, mod_ref, gn_ref, w_in_ref, qn_ref, kn_ref, *rest):
    if rope:
        ca_ref, sau_ref, sad_ref, cb_ref, sb_ref = rest[:5]
        rest = rest[5:]
    qa_o, ka_o, va_o, ga_o, qb_o, kb_o, vb_o, gb_o = rest[:8]
    cache_layout = not rope
    if rope:
        weights = w_in_ref
    else:
        wbf_hbm, weights, w_stage, sem, sem_out = rest[8:]
        handoff = _stage_weights(w_in_ref, wbf_hbm, weights, w_stage, sem, sem_out)
    tm = x_ref.shape[0]

    x = x_ref[...]
    shift = mod_ref[:, 0:D_MODEL]
    scale = mod_ref[:, D_MODEL:2 * D_MODEL]
    h = (_rms(x) * (gn_ref[...] * (1.0 + scale)) + shift).astype(BF16)

    if rope:
        ca, sau, sad = ca_ref[...], sau_ref[...], sad_ref[...]
        cb, sb = cb_ref[...], sb_ref[...]

    def rope_a(t):
        return _rope_a(t, ca, sau, sad) if rope else t

    def rope_b(t):
        return _rope_b(t, cb, sb) if rope else t

    ident = lambda t: t
    regions = (
        (OFF_AQ, A_Q, qa_o, lambda t: rope_a(t) * QSCALE_A, False),
        (OFF_AK, A_K, ka_o, rope_a, True),
        (OFF_AV, A_V, va_o, ident, True),
        (OFF_AG, A_G, ga_o, _silu, False),
        (OFF_BQ, B_Q, qb_o, lambda t: rope_b(_rms(t) * qn_ref[...]) * QSCALE_B, False),
        (OFF_BK, B_K, kb_o, lambda t: rope_b(_rms(t) * kn_ref[...]), True),
        (OFF_BV, B_V, vb_o, ident, True),
        (OFF_BG, B_G, gb_o, _silu, False),
    )
    for start, width, o_ref, epi, is_kv in regions:
        n_heads = width // HEAD_W
        for c0 in range(0, width, W_CHUNK):
            z = jnp.dot(h, weights[(start + c0) // W_CHUNK], preferred_element_type=F32)
            for h0 in range(0, W_CHUNK, HEAD_W):
                hd = (c0 + h0) // HEAD_W
                t = epi(z[:, h0:h0 + HEAD_W]).astype(o_ref.dtype)
                if is_kv and cache_layout:
                    o_ref[pl.ds(hd, tm, stride=n_heads), :] = t
                else:
                    o_ref[:, hd * HEAD_W:(hd + 1) * HEAD_W] = t

    if not rope:
        pl.when(pl.program_id(0) == pl.num_programs(0) - 1)(handoff.wait)


def _inproj(x2d, mod3, row_of_tile, g_norm, w_in, q_norm_g, k_norm_g, rope_tabs):
    t = x2d.shape[0]
    tm = INPROJ_TM
    rope = rope_tabs is not None
    row_spec = lambda w: pl.BlockSpec((tm, w), lambda i: (i, 0))
    const = lambda shape: pl.BlockSpec(shape, lambda i: (0,) * len(shape))
    in_specs = [
        row_spec(D_MODEL),
        pl.BlockSpec((None, 1, 3 * D_MODEL), lambda i: (row_of_tile(i), 0, 0)),
        const((1, D_MODEL)),
        _weight_spec(D_IN) if rope else pl.BlockSpec(memory_space=pl.ANY),
        const((1, HEAD_W)),
        const((1, HEAD_W)),
    ]
    args = [x2d, mod3, g_norm, w_in, q_norm_g, k_norm_g]
    if rope:
        n_pos_tiles = rope_tabs[0].shape[0] // tm
        in_specs += [pl.BlockSpec((tm, HEAD_W), lambda i: (i % n_pos_tiles, 0))] * 5
        args += list(rope_tabs)
    widths = (A_Q, A_K, A_V, A_G, B_Q, B_K, B_V, B_G)
    is_kv = (False, True, True, False, False, True, True, False)
    out_shape, out_specs = [], []
    for w, kv in zip(widths, is_kv):
        if kv and not rope:
            n_heads = w // HEAD_W
            out_shape.append(jax.ShapeDtypeStruct((t * n_heads, HEAD_W), F32))
            out_specs.append(pl.BlockSpec((tm * n_heads, HEAD_W), lambda i: (i, 0)))
        else:
            out_shape.append(jax.ShapeDtypeStruct((t, w), BF16))
            out_specs.append(row_spec(w))
    if not rope:
        out_shape.append(jax.ShapeDtypeStruct((D_IN // W_CHUNK, D_MODEL, W_CHUNK), BF16))
        out_specs.append(pl.BlockSpec(memory_space=pl.ANY))
    return pl.pallas_call(
        functools.partial(_inproj_body, rope),
        out_shape=out_shape,
        grid=(t // tm,),
        in_specs=in_specs,
        out_specs=out_specs,
        scratch_shapes=[] if rope else _weight_scratch(D_IN),
        compiler_params=_params(1),
        name="inproj_rope" if rope else "inproj",
    )(*args)


def _diff_lambda_col(lq1_ref, lk1_ref, lq2_ref, lk2_ref, lam_init):
    s1 = jnp.sum(lq1_ref[...] * lk1_ref[...], axis=-1, keepdims=True)
    s2 = jnp.sum(lq2_ref[...] * lk2_ref[...], axis=-1, keepdims=True)
    return jnp.exp(s1) - jnp.exp(s2) + lam_init


def _cache_head(ref, head, n_tok, n_heads):
    return ref[pl.ds(head, n_tok, stride=n_heads), :].astype(BF16)


def _scores(q, k):
    return lax.dot_general(q, k, (((1,), (1,)), ((), ())), preferred_element_type=F32)


def _softmax_pv(s, v):
    m = jnp.max(s, axis=-1, keepdims=True)
    e = jnp.exp2(s - m)
    v1 = jnp.concatenate([v, jnp.ones_like(v)], axis=1)
    ol = jnp.dot(e.astype(BF16), v1, preferred_element_type=F32)
    return ol[:, :HEAD_W], ol[:, HEAD_W:]


def _attend(lam_col, subln, qa_ref, ga_ref, qb_ref, gb_ref, ka, va, kb, vb, o_ref):
    tq = qa_ref.shape[0]
    lane = lax.broadcasted_iota(jnp.int32, (tq, HEAD_W), 1)
    first = lane < DK_A

    def diff_scores(hd):
        q = qa_ref[:, hd * HEAD_W:(hd + 1) * HEAD_W]
        zero = jnp.zeros_like(q)
        q2 = jnp.concatenate([jnp.where(first, q, zero), jnp.where(first, zero, q)], axis=0)
        return _scores(q2, ka(hd))

    def diff_finish(hd, s):
        cols = slice(hd * HEAD_W, (hd + 1) * HEAD_W)
        o2, l2 = _softmax_pv(s, va(hd))
        r2 = 1.0 / l2
        lam = lam_col[hd:hd + 1, :]
        o = o2[:tq] * r2[:tq] - o2[tq:] * (lam * r2[tq:])
        o_ref[:, cols] = (_rms(o) * subln * ga_ref[:, cols].astype(F32)).astype(o_ref.dtype)

    def gqa_scores(h0):
        q = jnp.concatenate(
            [qb_ref[:, (h0 + g) * HEAD_W:(h0 + g + 1) * HEAD_W] for g in range(GQA_STACK)], axis=0)
        return _scores(q, kb(h0 // G_B))

    def gqa_finish(h0, s):
        o, l = _softmax_pv(s, vb(h0 // G_B))
        o = o * (1.0 / l)
        for g in range(GQA_STACK):
            cols = slice((h0 + g) * HEAD_W, (h0 + g + 1) * HEAD_W)
            gate = gb_ref[:, cols].astype(F32)
            o_ref[:, A_V + (h0 + g) * HEAD_W:A_V + (h0 + g + 1) * HEAD_W] = (
                o[g * tq:(g + 1) * tq, :] * gate).astype(o_ref.dtype)

    units = [(diff_scores, diff_finish, hd) for hd in range(H_A)]
    units += [(gqa_scores, gqa_finish, h0) for h0 in range(0, H_B, GQA_STACK)]
    scores = []
    for u, (_, finish, arg) in enumerate(units):
        while len(scores) < min(len(units), u + 1 + SCORE_LOOKAHEAD):
            issue, _, issue_arg = units[len(scores)]
            scores.append(issue(issue_arg))
        finish(arg, scores[u])
        scores[u] = None


def _project_out(o, weights, x_ref, mod_ref, gf_ref, y_ref):
    m = jnp.concatenate(
        [jnp.dot(o, weights[c], preferred_element_type=F32) for c in range(D_MODEL // W_CHUNK)],
        axis=1)
    gate = mod_ref[:, 2 * D_MODEL:3 * D_MODEL]
    y_ref[...] = _rms(x_ref[...] + gate * m) * gf_ref[...]


def _attn_ctx_body(lam_init, lq1, lk1, lq2, lk2, subln_ref,
                   qa_ref, ka_ref, va_ref, ga_ref, qb_ref, kb_ref, vb_ref, gb_ref,
                   w_out_hbm, x_ref, mod_ref, gf_ref, y_ref, wbf_hbm,
                   w_scr, w_stage, sem, sem_out, o_scr):
    handoff = _stage_weights(w_out_hbm, wbf_hbm, w_scr, w_stage, sem, sem_out)
    lam_col = _diff_lambda_col(lq1, lk1, lq2, lk2, lam_init)
    seq = qa_ref.shape[0]
    head = lambda ref, n_heads: (lambda i: _cache_head(ref, i, seq, n_heads))
    _attend(lam_col, subln_ref[...] * (1.0 - lam_init), qa_ref, ga_ref, qb_ref, gb_ref,
            head(ka_ref, H_A), head(va_ref, H_A), head(kb_ref, KV_B), head(vb_ref, KV_B), o_scr)
    _project_out(o_scr[...], w_scr, x_ref, mod_ref, gf_ref, y_ref)
    pl.when(pl.program_id(0) == pl.num_programs(0) - 1)(handoff.wait)


def _attn_ctx(lam_init, lam_params, subln_g, qa, ka, va, ga, qb, kb, vb, gb, w_out, x2d, mod3,
              g_final, seq):
    t = qa.shape[0]
    const = lambda shape: pl.BlockSpec(shape, lambda b: (0,) * len(shape))
    row_spec = lambda w: pl.BlockSpec((seq, w), lambda b: (b, 0))
    cache_spec = lambda n_heads: pl.BlockSpec((seq * n_heads, HEAD_W), lambda b: (b, 0))
    return pl.pallas_call(
        functools.partial(_attn_ctx_body, lam_init),
        out_shape=[jax.ShapeDtypeStruct((t, D_MODEL), F32),
                   jax.ShapeDtypeStruct((D_MODEL // W_CHUNK, D_MIX, W_CHUNK), BF16)],
        grid=(t // seq,),
        in_specs=[const((H_A, DK_A))] * 4 + [const((1, DV_A))] + [
            row_spec(A_Q), cache_spec(H_A), cache_spec(H_A), row_spec(A_G),
            row_spec(B_Q), cache_spec(KV_B), cache_spec(KV_B), row_spec(B_G),
            pl.BlockSpec(memory_space=pl.ANY),
            row_spec(D_MODEL),
            pl.BlockSpec((None, 1, 3 * D_MODEL), lambda b: (0, 0, 0)),
            const((1, D_MODEL))],
        out_specs=[row_spec(D_MODEL), pl.BlockSpec(memory_space=pl.ANY)],
        scratch_shapes=_weight_scratch(D_MODEL) + [pltpu.VMEM((seq, D_MIX), BF16)],
        compiler_params=_params(1),
        name="attn_ctx",
    )(*lam_params, subln_g, qa, ka, va, ga, qb, kb, vb, gb, w_out, x2d, mod3, g_final)


def _attn_lat_body(lam_init, n_lat, lq1, lk1, lq2, lk2, subln_ref,
                   qa_ref, ka_ref, va_ref, cka_ref, cva_ref, ga_ref,
                   qb_ref, kb_ref, vb_ref, ckb_ref, cvb_ref, gb_ref, o_ref,
                   ka_all, va_all, kb_all, vb_all):
    @pl.when(pl.program_id(1) == 0)
    def _():
        for new_ref, cache_ref, all_ref in ((ka_ref, cka_ref, ka_all), (va_ref, cva_ref, va_all),
                                            (kb_ref, ckb_ref, kb_all), (vb_ref, cvb_ref, vb_all)):
            n_heads = new_ref.shape[1] // HEAD_W
            n_past = cache_ref.shape[0] // n_heads
            all_ref[0:n_lat, :] = new_ref[...]
            for hd in range(n_heads):
                all_ref[n_lat:, hd * HEAD_W:(hd + 1) * HEAD_W] = _cache_head(cache_ref, hd, n_past, n_heads)

    lam_col = _diff_lambda_col(lq1, lk1, lq2, lk2, lam_init)
    head = lambda ref: (lambda i: ref[:, i * HEAD_W:(i + 1) * HEAD_W])
    _attend(lam_col, subln_ref[...] * (1.0 - lam_init), qa_ref, ga_ref, qb_ref, gb_ref,
            head(ka_all), head(va_all), head(kb_all), head(vb_all), o_ref)


def _attn_lat(lam_init, lam_params, subln_g, qa, ka, va, cka, cva, ga, qb, kb, vb, ckb, cvb, gb,
              n_lat, n_past):
    t = qa.shape[0]
    tq = ATTN_TQ
    nq = n_lat // tq
    const = lambda shape: pl.BlockSpec(shape, lambda b, i: (0,) * len(shape))
    q_spec = lambda w: pl.BlockSpec((tq, w), lambda b, i: (b * nq + i, 0))
    new_spec = lambda w: pl.BlockSpec((n_lat, w), lambda b, i: (b, 0))
    cache_spec = lambda w: pl.BlockSpec((n_past * (w // HEAD_W), HEAD_W), lambda b, i: (b, 0))
    n_all = n_lat + n_past
    return pl.pallas_call(
        functools.partial(_attn_lat_body, lam_init, n_lat),
        out_shape=jax.ShapeDtypeStruct((t, D_MIX), BF16),
        grid=(t // n_lat, nq),
        in_specs=[const((H_A, DK_A))] * 4 + [const((1, DV_A))] + [
            q_spec(A_Q), new_spec(A_K), new_spec(A_V), cache_spec(A_K), cache_spec(A_V), q_spec(A_G),
            q_spec(B_Q), new_spec(B_K), new_spec(B_V), cache_spec(B_K), cache_spec(B_V), q_spec(B_G)],
        out_specs=q_spec(D_MIX),
        scratch_shapes=[pltpu.VMEM((n_all, A_K), BF16), pltpu.VMEM((n_all, A_V), BF16),
                        pltpu.VMEM((n_all, B_K), BF16), pltpu.VMEM((n_all, B_V), BF16)],
        compiler_params=_params(2),
        name="attn_lat",
    )(*lam_params, subln_g, qa, ka, va, cka, cva, ga, qb, kb, vb, ckb, cvb, gb)


def _outproj_body(o_ref, w_ref, x_ref, mod_ref, gf_ref, y_ref):
    _project_out(o_ref[...], w_ref, x_ref, mod_ref, gf_ref, y_ref)


def _outproj(o, w_out_bf16, x2d, mod3, row_of_tile, g_final):
    t = x2d.shape[0]
    tm = OUTPROJ_TM
    return pl.pallas_call(
        _outproj_body,
        out_shape=jax.ShapeDtypeStruct((t, D_MODEL), F32),
        grid=(t // tm,),
        in_specs=[
            pl.BlockSpec((tm, D_MIX), lambda i: (i, 0)),
            _weight_spec(D_MODEL),
            pl.BlockSpec((tm, D_MODEL), lambda i: (i, 0)),
            pl.BlockSpec((None, 1, 3 * D_MODEL), lambda i: (row_of_tile(i), 0, 0)),
            pl.BlockSpec((1, D_MODEL), lambda i: (0, 0)),
        ],
        out_specs=pl.BlockSpec((tm, D_MODEL), lambda i: (i, 0)),
        compiler_params=_params(1),
        name="outproj",
    )(o, w_out_bf16, x2d, mod3, g_final)


def _rope_tables(n_tok):
    n_rows = n_tok // GRID_W
    rows = np.repeat(np.arange(n_rows, dtype=np.float32), GRID_W)
    cols = np.tile(np.arange(GRID_W, dtype=np.float32), n_rows)

    def cos_sin(rot_dim):
        quarter = rot_dim // 4
        inv_freq = (1.0 / (np.float32(ROPE_THETA) ** (np.arange(quarter, dtype=np.float32) / quarter))
                    ).astype(np.float32)
        ang = np.concatenate([rows[:, None] * inv_freq, cols[:, None] * inv_freq], axis=-1)
        return np.cos(ang).astype(np.float32), np.sin(ang).astype(np.float32)

    ca, sa = cos_sin(DK_A)
    za = np.zeros_like(sa)
    cb, sb = cos_sin(DH_B)
    tabs = (
        np.concatenate([ca, ca, ca, ca], axis=-1),
        np.concatenate([-sa, za, -sa, za], axis=-1),
        np.concatenate([za, sa, za, sa], axis=-1),
        np.concatenate([cb, cb], axis=-1),
        np.concatenate([-sb, sb], axis=-1),
    )
    return tuple(jnp.asarray(t) for t in tabs)


def kernel(x_prompt, x_sample, cache_diff_k, cache_diff_v, cache_gqa_k, cache_gqa_v, c, c_ctx,
           w_ada, b_ada, g_norm, w_in, lam_q1, lam_k1, lam_q2, lam_k2, subln_g, q_norm_g, k_norm_g,
           w_out, g_final):
    bp, n_ctx, d = x_prompt.shape
    bs, n_lat, _ = x_sample.shape
    depth = w_in.shape[0]
    n_past = cache_diff_k.shape[2]
    assert depth == 1 and d == D_MODEL and bs + 1 <= N_COND
    assert n_lat % INPROJ_TM == 0 and n_lat % OUTPROJ_TM == 0 and n_lat % ATTN_TQ == 0
    l = 0
    lam_init = 0.8 - 0.6 * math.exp(-0.3 * l)

    cond = jnp.concatenate(
        [c_ctx[None, :], c, jnp.zeros((N_COND - 1 - bs, d), F32)], axis=0)
    mod = _adaln(cond, w_ada[l], b_ada[l][None, :])
    mod3 = mod[:, None, :]

    gn = g_norm[l][None, :]
    qn = q_norm_g[l][None, :]
    kn = k_norm_g[l][None, :]
    lam_params = (lam_q1[l], lam_k1[l], lam_q2[l], lam_k2[l])
    subln = subln_g[l][None, :]
    gf = g_final[None, :]

    xp2 = x_prompt.reshape(bp * n_ctx, d)
    qa, ka, va, ga, qb, kb, vb, gb, w_in_bf16 = _inproj(
        xp2, mod3, lambda i: 0, gn, w_in[l], qn, kn, None)
    y_prompt, w_out_bf16 = _attn_ctx(lam_init, lam_params, subln, qa, ka, va, ga, qb, kb, vb, gb,
                                     w_out[l], xp2, mod3, gf, n_ctx)
    y_prompt = y_prompt.reshape(bp, n_ctx, d)
    new_diff_k = ka.reshape(bp, 1, n_ctx, H_A, 2 * DK_A)
    new_diff_v = va.reshape(bp, 1, n_ctx, H_A, DV_A)
    new_gqa_k = kb.reshape(bp, 1, n_ctx, KV_B, DH_B)
    new_gqa_v = vb.reshape(bp, 1, n_ctx, KV_B, DH_B)

    xs2 = x_sample.reshape(bs * n_lat, d)
    in_tiles = n_lat // INPROJ_TM
    qa, ka, va, ga, qb, kb, vb, gb = _inproj(
        xs2, mod3, lambda i: 1 + i // in_tiles, gn, w_in_bf16, qn, kn, _rope_tables(n_lat))
    cka = cache_diff_k[:, l].reshape(bs * n_past * H_A, HEAD_W)
    cva = cache_diff_v[:, l].reshape(bs * n_past * H_A, HEAD_W)
    ckb = cache_gqa_k[:, l].reshape(bs * n_past * KV_B, HEAD_W)
    cvb = cache_gqa_v[:, l].reshape(bs * n_past * KV_B, HEAD_W)
    o = _attn_lat(lam_init, lam_params, subln, qa, ka, va, cka, cva, ga, qb, kb, vb, ckb, cvb, gb,
                  n_lat, n_past)
    out_tiles = n_lat // OUTPROJ_TM
    y_sample = _outproj(o, w_out_bf16, xs2, mod3, lambda i: 1 + i // out_tiles, gf)
    y_sample = y_sample.reshape(bs, n_lat, d)

    return (y_prompt, y_sample, new_diff_k, new_diff_v, new_gqa_k, new_gqa_v)
```

```python
import functools
import math

import jax
import jax.numpy as jnp
import numpy as np
from jax import lax
from jax.experimental import pallas as pl
from jax.experimental.pallas import tpu as pltpu

D_MODEL = 2048
GRID_W = 64
ROPE_THETA = 10000.0
EPS = 1e-6
H_A = 8
DK_A = 64
DV_A = 2 * DK_A
H_B = 8
KV_B = 2
DH_B = 128
G_B = H_B // KV_B
HEAD_W = 128
A_Q = H_A * 2 * DK_A
A_K = H_A * 2 * DK_A
A_V = H_A * DV_A
A_G = H_A * DV_A
B_Q = H_B * DH_B
B_K = KV_B * DH_B
B_V = KV_B * DH_B
B_G = H_B * DH_B
D_IN = A_Q + A_K + A_V + A_G + B_Q + B_K + B_V + B_G
D_MIX = A_V + B_Q
OFF_AQ = 0
OFF_AK = OFF_AQ + A_Q
OFF_AV = OFF_AK + A_K
OFF_AG = OFF_AV + A_V
OFF_BQ = OFF_AG + A_G
OFF_BK = OFF_BQ + B_Q
OFF_BV = OFF_BK + B_K
OFF_BG = OFF_BV + B_V

N_COND = 8
ADA_TK = 256
INPROJ_TM = 256
W_CHUNK = 256
STAGE_BYTES = 2 * 1024 * 1024
STAGE_SLOTS = 4
OUTPROJ_TM = 512
ATTN_TQ = 256
GQA_STACK = 2
SCORE_LOOKAHEAD = 2
QSCALE_A = math.log2(math.e) / math.sqrt(DK_A)
QSCALE_B = math.log2(math.e) / math.sqrt(DH_B)
VMEM_LIMIT = 56 * 1024 * 1024

BF16 = jnp.bfloat16
F32 = jnp.float32


def _params(n_grid_axes):
    return pltpu.CompilerParams(
        dimension_semantics=("arbitrary",) * n_grid_axes,
        vmem_limit_bytes=VMEM_LIMIT,
    )


def _silu(x):
    return x * jax.nn.sigmoid(x)


def _rms(x):
    return x * lax.rsqrt(jnp.mean(x * x, axis=-1, keepdims=True) + EPS)


def _stage_rows(n_cols):
    return 1 << int(math.log2(STAGE_BYTES // (n_cols * 4)))


def _weight_scratch(n_cols):
    return [pltpu.VMEM((n_cols // W_CHUNK, D_MODEL, W_CHUNK), BF16),
            pltpu.VMEM((STAGE_SLOTS, _stage_rows(n_cols), n_cols), F32),
            pltpu.SemaphoreType.DMA((STAGE_SLOTS,)),
            pltpu.SemaphoreType.DMA(())]


def _slab_copy(w_hbm, w_stage, sem, k):
    slot = k % STAGE_SLOTS
    n_rows = w_stage.shape[1]
    return pltpu.make_async_copy(
        w_hbm.at[pl.ds(k * n_rows, n_rows), :], w_stage.at[slot], sem.at[slot])


def _stage_weights(w_hbm, wbf_hbm, w_scr, w_stage, sem, sem_out, on_slab=None):
    n_chunks = w_scr.shape[0]
    slab_rows = w_stage.shape[1]
    n_slabs = w_scr.shape[1] // slab_rows
    copy = functools.partial(_slab_copy, w_hbm, w_stage, sem)
    handoff = pltpu.make_async_copy(w_scr, wbf_hbm, sem_out)

    @pl.when(pl.program_id(0) == 0)
    def _():
        for k in range(STAGE_SLOTS - 1):
            copy(k).start()
        for k in range(n_slabs):
            if k + STAGE_SLOTS - 1 < n_slabs:
                copy(k + STAGE_SLOTS - 1).start()
            copy(k).wait()
            rows = slice(k * slab_rows, (k + 1) * slab_rows)
            slab = w_stage[k % STAGE_SLOTS].astype(BF16)
            for c in range(n_chunks):
                w_scr[c, rows, :] = slab[:, c * W_CHUNK:(c + 1) * W_CHUNK]
            if on_slab is not None:
                on_slab(rows, slab)
        handoff.start()

    return handoff


def _weight_spec(n_cols):
    return pl.BlockSpec((n_cols // W_CHUNK, D_MODEL, W_CHUNK), lambda i: (0, 0, 0),
                        pipeline_mode=pl.Buffered(1))


def _adaln_body(cond_ref, w_ref, b_ref, o_ref):
    @pl.when(pl.program_id(0) == 0)
    def _():
        o_ref[...] = jnp.broadcast_to(b_ref[...], o_ref.shape)

    a = _silu(cond_ref[...]).astype(BF16)
    w = w_ref[...].astype(BF16)
    o_ref[...] += jnp.dot(a, w, preferred_element_type=F32)


def _adaln(cond, w_ada, b_ada):
    d3 = w_ada.shape[1]
    return pl.pallas_call(
        _adaln_body,
        out_shape=jax.ShapeDtypeStruct((N_COND, d3), F32),
        grid=(D_MODEL // ADA_TK,),
        in_specs=[
            pl.BlockSpec((N_COND, ADA_TK), lambda k: (0, k)),
            pl.BlockSpec((ADA_TK, d3), lambda k: (k, 0)),
            pl.BlockSpec((1, d3), lambda k: (0, 0)),
        ],
        out_specs=pl.BlockSpec((N_COND, d3), lambda k: (0, 0)),
        compiler_params=_params(1),
        name="adaln",
    )(cond, w_ada, b_ada)


def _rope_a(x, c, s_up, s_dn):
    return x * c + pltpu.roll(x, 96, 1) * s_up + pltpu.roll(x, 32, 1) * s_dn


def _rope_b(x, c, s):
    return x * c + pltpu.roll(x, 64, 1) * s


def _inproj_body(rope, row_of_tile, x_ref, mod_ref, mods_ref, gn_ref, w_in_ref, qn_ref, kn_ref,
                 *rest):
    if rope:
        ca_ref, sau_ref, sad_ref, cb_ref, sb_ref = rest[:5]
        rest = rest[5:]
    qa_o, ka_o, va_o, ga_o, qb_o, kb_o, vb_o, gb_o = rest[:8]
    cache_layout = not rope
    if rope:
        weights, shift_w = w_in_ref, mods_ref
    else:
        shift_w, wbf_hbm, weights, w_stage, sem, sem_out = rest[8:]

        @pl.when(pl.program_id(0) == 0)
        def _():
            shift_w[...] = jnp.zeros_like(shift_w)

        def add_shift_rows(rows, slab):
            shift_w[...] += jnp.dot(mods_ref[:, rows].astype(BF16), slab,
                                    preferred_element_type=F32)

        handoff = _stage_weights(w_in_ref, wbf_hbm, weights, w_stage, sem, sem_out,
                                 add_shift_rows)
    tm = x_ref.shape[0]

    x = x_ref[...]
    scale = mod_ref[:, D_MODEL:2 * D_MODEL]
    h = (x * (gn_ref[...] * (1.0 + scale))).astype(BF16)
    inv = lax.rsqrt(jnp.mean(x * x, axis=-1, keepdims=True) + EPS)
    shift_row = shift_w[pl.ds(row_of_tile(pl.program_id(0)), 1), :]

    if rope:
        ca, sau, sad = ca_ref[...], sau_ref[...], sad_ref[...]
        cb, sb = cb_ref[...], sb_ref[...]

    def rope_a(t):
        return _rope_a(t, ca, sau, sad) if rope else t

    def rope_b(t):
        return _rope_b(t, cb, sb) if rope else t

    ident = lambda t: t
    regions = (
        (OFF_AQ, A_Q, qa_o, lambda t: rope_a(t) * QSCALE_A, False),
        (OFF_AK, A_K, ka_o, rope_a, True),
        (OFF_AV, A_V, va_o, ident, True),
        (OFF_AG, A_G, ga_o, _silu, False),
        (OFF_BQ, B_Q, qb_o, lambda t: rope_b(_rms(t) * qn_ref[...]) * QSCALE_B, False),
        (OFF_BK, B_K, kb_o, lambda t: rope_b(_rms(t) * kn_ref[...]), True),
        (OFF_BV, B_V, vb_o, ident, True),
        (OFF_BG, B_G, gb_o, _silu, False),
    )
    for start, width, o_ref, epi, is_kv in regions:
        n_heads = width // HEAD_W
        for c0 in range(0, width, W_CHUNK):
            z = jnp.dot(h, weights[(start + c0) // W_CHUNK], preferred_element_type=F32)
            for h0 in range(0, W_CHUNK, HEAD_W):
                hd = (c0 + h0) // HEAD_W
                col = start + c0 + h0
                t = z[:, h0:h0 + HEAD_W] * inv + shift_row[:, col:col + HEAD_W]
                t = epi(t).astype(o_ref.dtype)
                if is_kv and cache_layout:
                    o_ref[pl.ds(hd, tm, stride=n_heads), :] = t
                else:
                    o_ref[:, hd * HEAD_W:(hd + 1) * HEAD_W] = t

    if not rope:
        pl.when(pl.program_id(0) == pl.num_programs(0) - 1)(handoff.wait)


def _inproj(x2d, mod3, mods, row_of_tile, g_norm, w_in, q_norm_g, k_norm_g, rope_tabs):
    t = x2d.shape[0]
    tm = INPROJ_TM
    rope = rope_tabs is not None
    row_spec = lambda w: pl.BlockSpec((tm, w), lambda i: (i, 0))
    const = lambda shape: pl.BlockSpec(shape, lambda i: (0,) * len(shape))
    in_specs = [
        row_spec(D_MODEL),
        pl.BlockSpec((None, 1, 3 * D_MODEL), lambda i: (row_of_tile(i), 0, 0)),
        const((N_COND, D_IN)) if rope else const((N_COND, 3 * D_MODEL)),
        const((1, D_MODEL)),
        _weight_spec(D_IN) if rope else pl.BlockSpec(memory_space=pl.ANY),
        const((1, HEAD_W)),
        const((1, HEAD_W)),
    ]
    args = [x2d, mod3, mods, g_norm, w_in, q_norm_g, k_norm_g]
    if rope:
        n_pos_tiles = rope_tabs[0].shape[0] // tm
        in_specs += [pl.BlockSpec((tm, HEAD_W), lambda i: (i % n_pos_tiles, 0))] * 5
        args += list(rope_tabs)
    widths = (A_Q, A_K, A_V, A_G, B_Q, B_K, B_V, B_G)
    is_kv = (False, True, True, False, False, True, True, False)
    out_shape, out_specs = [], []
    for w, kv in zip(widths, is_kv):
        if kv and not rope:
            n_heads = w // HEAD_W
            out_shape.append(jax.ShapeDtypeStruct((t * n_heads, HEAD_W), F32))
            out_specs.append(pl.BlockSpec((tm * n_heads, HEAD_W), lambda i: (i, 0)))
        else:
            out_shape.append(jax.ShapeDtypeStruct((t, w), BF16))
            out_specs.append(row_spec(w))
    if not rope:
        out_shape.append(jax.ShapeDtypeStruct((N_COND, D_IN), F32))
        out_specs.append(const((N_COND, D_IN)))
        out_shape.append(jax.ShapeDtypeStruct((D_IN // W_CHUNK, D_MODEL, W_CHUNK), BF16))
        out_specs.append(pl.BlockSpec(memory_space=pl.ANY))
    return pl.pallas_call(
        functools.partial(_inproj_body, rope, row_of_tile),
        out_shape=out_shape,
        grid=(t // tm,),
        in_specs=in_specs,
        out_specs=out_specs,
        scratch_shapes=[] if rope else _weight_scratch(D_IN),
        compiler_params=_params(1),
        name="inproj_rope" if rope else "inproj",
    )(*args)


def _diff_lambda_col(lq1_ref, lk1_ref, lq2_ref, lk2_ref, lam_init):
    s1 = jnp.sum(lq1_ref[...] * lk1_ref[...], axis=-1, keepdims=True)
    s2 = jnp.sum(lq2_ref[...] * lk2_ref[...], axis=-1, keepdims=True)
    return jnp.exp(s1) - jnp.exp(s2) + lam_init


def _cache_head(ref, head, n_tok, n_heads):
    return ref[pl.ds(head, n_tok, stride=n_heads), :].astype(BF16)


def _scores(q, k):
    return lax.dot_general(q, k, (((1,), (1,)), ((), ())), preferred_element_type=F32)


def _softmax_pv(s, v):
    m = jnp.max(s, axis=-1, keepdims=True)
    e = jnp.exp2(s - m)
    v1 = jnp.concatenate([v, jnp.ones_like(v)], axis=1)
    ol = jnp.dot(e.astype(BF16), v1, preferred_element_type=F32)
    return ol[:, :HEAD_W], ol[:, HEAD_W:]


def _attend(lam_col, subln, qa_ref, ga_ref, qb_ref, gb_ref, ka, va, kb, vb, o_ref):
    tq = qa_ref.shape[0]
    lane = lax.broadcasted_iota(jnp.int32, (tq, HEAD_W), 1)
    first = lane < DK_A

    def diff_scores(hd):
        q = qa_ref[:, hd * HEAD_W:(hd + 1) * HEAD_W]
        zero = jnp.zeros_like(q)
        q2 = jnp.concatenate([jnp.where(first, q, zero), jnp.where(first, zero, q)], axis=0)
        return _scores(q2, ka(hd))

    def diff_finish(hd, s):
        cols = slice(hd * HEAD_W, (hd + 1) * HEAD_W)
        o2, l2 = _softmax_pv(s, va(hd))
        r2 = 1.0 / l2
        lam = lam_col[hd:hd + 1, :]
        o = o2[:tq] * r2[:tq] - o2[tq:] * (lam * r2[tq:])
        o_ref[:, cols] = (_rms(o) * subln * ga_ref[:, cols].astype(F32)).astype(o_ref.dtype)

    def gqa_scores(h0):
        q = jnp.concatenate(
            [qb_ref[:, (h0 + g) * HEAD_W:(h0 + g + 1) * HEAD_W] for g in range(GQA_STACK)], axis=0)
        return _scores(q, kb(h0 // G_B))

    def gqa_finish(h0, s):
        o, l = _softmax_pv(s, vb(h0 // G_B))
        o = o * (1.0 / l)
        for g in range(GQA_STACK):
            cols = slice((h0 + g) * HEAD_W, (h0 + g + 1) * HEAD_W)
            gate = gb_ref[:, cols].astype(F32)
            o_ref[:, A_V + (h0 + g) * HEAD_W:A_V + (h0 + g + 1) * HEAD_W] = (
                o[g * tq:(g + 1) * tq, :] * gate).astype(o_ref.dtype)

    units = [(diff_scores, diff_finish, hd) for hd in range(H_A)]
    units += [(gqa_scores, gqa_finish, h0) for h0 in range(0, H_B, GQA_STACK)]
    scores = []
    for u, (_, finish, arg) in enumerate(units):
        while len(scores) < min(len(units), u + 1 + SCORE_LOOKAHEAD):
            issue, _, issue_arg = units[len(scores)]
            scores.append(issue(issue_arg))
        finish(arg, scores[u])
        scores[u] = None


def _project_out(o, weights, x_ref, mod_ref, gf_ref, y_ref):
    m = jnp.concatenate(
        [jnp.dot(o, weights[c], preferred_element_type=F32) for c in range(D_MODEL // W_CHUNK)],
        axis=1)
    gate = mod_ref[:, 2 * D_MODEL:3 * D_MODEL]
    y_ref[...] = _rms(x_ref[...] + gate * m) * gf_ref[...]


def _attn_ctx_body(lam_init, lq1, lk1, lq2, lk2, subln_ref,
                   qa_ref, ka_ref, va_ref, ga_ref, qb_ref, kb_ref, vb_ref, gb_ref,
                   w_out_hbm, x_ref, mod_ref, gf_ref, y_ref, wbf_hbm,
                   w_scr, w_stage, sem, sem_out, o_scr):
    handoff = _stage_weights(w_out_hbm, wbf_hbm, w_scr, w_stage, sem, sem_out)
    lam_col = _diff_lambda_col(lq1, lk1, lq2, lk2, lam_init)
    seq = qa_ref.shape[0]
    head = lambda ref, n_heads: (lambda i: _cache_head(ref, i, seq, n_heads))
    _attend(lam_col, subln_ref[...] * (1.0 - lam_init), qa_ref, ga_ref, qb_ref, gb_ref,
            head(ka_ref, H_A), head(va_ref, H_A), head(kb_ref, KV_B), head(vb_ref, KV_B), o_scr)
    _project_out(o_scr[...], w_scr, x_ref, mod_ref, gf_ref, y_ref)
    pl.when(pl.program_id(0) == pl.num_programs(0) - 1)(handoff.wait)


def _attn_ctx(lam_init, lam_params, subln_g, qa, ka, va, ga, qb, kb, vb, gb, w_out, x2d, mod3,
              g_final, seq):
    t = qa.shape[0]
    const = lambda shape: pl.BlockSpec(shape, lambda b: (0,) * len(shape))
    row_spec = lambda w: pl.BlockSpec((seq, w), lambda b: (b, 0))
    cache_spec = lambda n_heads: pl.BlockSpec((seq * n_heads, HEAD_W), lambda b: (b, 0))
    return pl.pallas_call(
        functools.partial(_attn_ctx_body, lam_init),
        out_shape=[jax.ShapeDtypeStruct((t, D_MODEL), F32),
                   jax.ShapeDtypeStruct((D_MODEL // W_CHUNK, D_MIX, W_CHUNK), BF16)],
        grid=(t // seq,),
        in_specs=[const((H_A, DK_A))] * 4 + [const((1, DV_A))] + [
            row_spec(A_Q), cache_spec(H_A), cache_spec(H_A), row_spec(A_G),
            row_spec(B_Q), cache_spec(KV_B), cache_spec(KV_B), row_spec(B_G),
            pl.BlockSpec(memory_space=pl.ANY),
            row_spec(D_MODEL),
            pl.BlockSpec((None, 1, 3 * D_MODEL), lambda b: (0, 0, 0)),
            const((1, D_MODEL))],
        out_specs=[row_spec(D_MODEL), pl.BlockSpec(memory_space=pl.ANY)],
        scratch_shapes=_weight_scratch(D_MODEL) + [pltpu.VMEM((seq, D_MIX), BF16)],
        compiler_params=_params(1),
        name="attn_ctx",
    )(*lam_params, subln_g, qa, ka, va, ga, qb, kb, vb, gb, w_out, x2d, mod3, g_final)


def _attn_lat_body(lam_init, n_lat, lq1, lk1, lq2, lk2, subln_ref,
                   qa_ref, ka_ref, va_ref, cka_ref, cva_ref, ga_ref,
                   qb_ref, kb_ref, vb_ref, ckb_ref, cvb_ref, gb_ref, o_ref,
                   ka_all, va_all, kb_all, vb_all):
    @pl.when(pl.program_id(1) == 0)
    def _():
        for new_ref, cache_ref, all_ref in ((ka_ref, cka_ref, ka_all), (va_ref, cva_ref, va_all),
                                            (kb_ref, ckb_ref, kb_all), (vb_ref, cvb_ref, vb_all)):
            n_heads = new_ref.shape[1] // HEAD_W
            n_past = cache_ref.shape[0] // n_heads
            all_ref[0:n_lat, :] = new_ref[...]
            for hd in range(n_heads):
                all_ref[n_lat:, hd * HEAD_W:(hd + 1) * HEAD_W] = _cache_head(cache_ref, hd, n_past, n_heads)

    lam_col = _diff_lambda_col(lq1, lk1, lq2, lk2, lam_init)
    head = lambda ref: (lambda i: ref[:, i * HEAD_W:(i + 1) * HEAD_W])
    _attend(lam_col, subln_ref[...] * (1.0 - lam_init), qa_ref, ga_ref, qb_ref, gb_ref,
            head(ka_all), head(va_all), head(kb_all), head(vb_all), o_ref)


def _attn_lat(lam_init, lam_params, subln_g, qa, ka, va, cka, cva, ga, qb, kb, vb, ckb, cvb, gb,
              n_lat, n_past):
    t = qa.shape[0]
    tq = ATTN_TQ
    nq = n_lat // tq
    const = lambda shape: pl.BlockSpec(shape, lambda b, i: (0,) * len(shape))
    q_spec = lambda w: pl.BlockSpec((tq, w), lambda b, i: (b * nq + i, 0))
    new_spec = lambda w: pl.BlockSpec((n_lat, w), lambda b, i: (b, 0))
    cache_spec = lambda w: pl.BlockSpec((n_past * (w // HEAD_W), HEAD_W), lambda b, i: (b, 0))
    n_all = n_lat + n_past
    return pl.pallas_call(
        functools.partial(_attn_lat_body, lam_init, n_lat),
        out_shape=jax.ShapeDtypeStruct((t, D_MIX), BF16),
        grid=(t // n_lat, nq),
        in_specs=[const((H_A, DK_A))] * 4 + [const((1, DV_A))] + [
            q_spec(A_Q), new_spec(A_K), new_spec(A_V), cache_spec(A_K), cache_spec(A_V), q_spec(A_G),
            q_spec(B_Q), new_spec(B_K), new_spec(B_V), cache_spec(B_K), cache_spec(B_V), q_spec(B_G)],
        out_specs=q_spec(D_MIX),
        scratch_shapes=[pltpu.VMEM((n_all, A_K), BF16), pltpu.VMEM((n_all, A_V), BF16),
                        pltpu.VMEM((n_all, B_K), BF16), pltpu.VMEM((n_all, B_V), BF16)],
        compiler_params=_params(2),
        name="attn_lat",
    )(*lam_params, subln_g, qa, ka, va, cka, cva, ga, qb, kb, vb, ckb, cvb, gb)


def _outproj_body(o_ref, w_ref, x_ref, mod_ref, gf_ref, y_ref):
    _project_out(o_ref[...], w_ref, x_ref, mod_ref, gf_ref, y_ref)


def _outproj(o, w_out_bf16, x2d, mod3, row_of_tile, g_final):
    t = x2d.shape[0]
    tm = OUTPROJ_TM
    return pl.pallas_call(
        _outproj_body,
        out_shape=jax.ShapeDtypeStruct((t, D_MODEL), F32),
        grid=(t // tm,),
        in_specs=[
            pl.BlockSpec((tm, D_MIX), lambda i: (i, 0)),
            _weight_spec(D_MODEL),
            pl.BlockSpec((tm, D_MODEL), lambda i: (i, 0)),
            pl.BlockSpec((None, 1, 3 * D_MODEL), lambda i: (row_of_tile(i), 0, 0)),
            pl.BlockSpec((1, D_MODEL), lambda i: (0, 0)),
        ],
        out_specs=pl.BlockSpec((tm, D_MODEL), lambda i: (i, 0)),
        compiler_params=_params(1),
        name="outproj",
    )(o, w_out_bf16, x2d, mod3, g_final)


def _rope_tables(n_tok):
    n_rows = n_tok // GRID_W
    rows = np.repeat(np.arange(n_rows, dtype=np.float32), GRID_W)
    cols = np.tile(np.arange(GRID_W, dtype=np.float32), n_rows)

    def cos_sin(rot_dim):
        quarter = rot_dim // 4
        inv_freq = (1.0 / (np.float32(ROPE_THETA) ** (np.arange(quarter, dtype=np.float32) / quarter))
                    ).astype(np.float32)
        ang = np.concatenate([rows[:, None] * inv_freq, cols[:, None] * inv_freq], axis=-1)
        return np.cos(ang).astype(np.float32), np.sin(ang).astype(np.float32)

    ca, sa = cos_sin(DK_A)
    za = np.zeros_like(sa)
    cb, sb = cos_sin(DH_B)
    tabs = (
        np.concatenate([ca, ca, ca, ca], axis=-1),
        np.concatenate([-sa, za, -sa, za], axis=-1),
        np.concatenate([za, sa, za, sa], axis=-1),
        np.concatenate([cb, cb], axis=-1),
        np.concatenate([-sb, sb], axis=-1),
    )
    return tuple(jnp.asarray(t) for t in tabs)


def kernel(x_prompt, x_sample, cache_diff_k, cache_diff_v, cache_gqa_k, cache_gqa_v, c, c_ctx,
           w_ada, b_ada, g_norm, w_in, lam_q1, lam_k1, lam_q2, lam_k2, subln_g, q_norm_g, k_norm_g,
           w_out, g_final):
    bp, n_ctx, d = x_prompt.shape
    bs, n_lat, _ = x_sample.shape
    depth = w_in.shape[0]
    n_past = cache_diff_k.shape[2]
    assert depth == 1 and d == D_MODEL and bs + 1 <= N_COND
    assert n_lat % INPROJ_TM == 0 and n_lat % OUTPROJ_TM == 0 and n_lat % ATTN_TQ == 0
    l = 0
    lam_init = 0.8 - 0.6 * math.exp(-0.3 * l)

    cond = jnp.concatenate(
        [c_ctx[None, :], c, jnp.zeros((N_COND - 1 - bs, d), F32)], axis=0)
    mod = _adaln(cond, w_ada[l], b_ada[l][None, :])
    mod3 = mod[:, None, :]

    gn = g_norm[l][None, :]
    qn = q_norm_g[l][None, :]
    kn = k_norm_g[l][None, :]
    lam_params = (lam_q1[l], lam_k1[l], lam_q2[l], lam_k2[l])
    subln = subln_g[l][None, :]
    gf = g_final[None, :]

    xp2 = x_prompt.reshape(bp * n_ctx, d)
    qa, ka, va, ga, qb, kb, vb, gb, shift_w, w_in_bf16 = _inproj(
        xp2, mod3, mod, lambda i: 0, gn, w_in[l], qn, kn, None)
    y_prompt, w_out_bf16 = _attn_ctx(lam_init, lam_params, subln, qa, ka, va, ga, qb, kb, vb, gb,
                                     w_out[l], xp2, mod3, gf, n_ctx)
    y_prompt = y_prompt.reshape(bp, n_ctx, d)
    new_diff_k = ka.reshape(bp, 1, n_ctx, H_A, 2 * DK_A)
    new_diff_v = va.reshape(bp, 1, n_ctx, H_A, DV_A)
    new_gqa_k = kb.reshape(bp, 1, n_ctx, KV_B, DH_B)
    new_gqa_v = vb.reshape(bp, 1, n_ctx, KV_B, DH_B)

    xs2 = x_sample.reshape(bs * n_lat, d)
    in_tiles = n_lat // INPROJ_TM
    qa, ka, va, ga, qb, kb, vb, gb = _inproj(
        xs2, mod3, shift_w, lambda i: 1 + i // in_tiles, gn, w_in_bf16, qn, kn, _rope_tables(n_lat))
    cka = cache_diff_k[:, l].reshape(bs * n_past * H_A, HEAD_W)
    cva = cache_diff_v[:, l].reshape(bs * n_past * H_A, HEAD_W)
    ckb = cache_gqa_k[:, l].reshape(bs * n_past * KV_B, HEAD_W)
    cvb = cache_gqa_v[:, l].reshape(bs * n_past * KV_B, HEAD_W)
    o = _attn_lat(lam_init, lam_params, subln, qa, ka, va, cka, cva, ga, qb, kb, vb, ckb, cvb, gb,
                  n_lat, n_past)
    out_tiles = n_lat // OUTPROJ_TM
    y_sample = _outproj(o, w_out_bf16, xs2, mod3, lambda i: 1 + i // out_tiles, gf)
    y_sample = y_sample.reshape(bs, n_lat, d)

    return (y_prompt, y_sample, new_diff_k, new_diff_v, new_gqa_k, new_gqa_v)
```

```python
import functools
import math

import jax
import jax.numpy as jnp
import numpy as np
from jax import lax
from jax.experimental import pallas as pl
from jax.experimental.pallas import tpu as pltpu

D_MODEL = 2048
GRID_W = 64
ROPE_THETA = 10000.0
EPS = 1e-6
H_A = 8
DK_A = 64
DV_A = 2 * DK_A
H_B = 8
KV_B = 2
DH_B = 128
G_B = H_B // KV_B
HEAD_W = 128
A_Q = H_A * 2 * DK_A
A_K = H_A * 2 * DK_A
A_V = H_A * DV_A
A_G = H_A * DV_A
B_Q = H_B * DH_B
B_K = KV_B * DH_B
B_V = KV_B * DH_B
B_G = H_B * DH_B
D_IN = A_Q + A_K + A_V + A_G + B_Q + B_K + B_V + B_G
D_MIX = A_V + B_Q
OFF_AQ = 0
OFF_AK = OFF_AQ + A_Q
OFF_AV = OFF_AK + A_K
OFF_AG = OFF_AV + A_V
OFF_BQ = OFF_AG + A_G
OFF_BK = OFF_BQ + B_Q
OFF_BV = OFF_BK + B_K
OFF_BG = OFF_BV + B_V

N_COND = 8
ADA_TK = 256
INPROJ_TM = 256
W_CHUNK = 256
STAGE_BYTES = 2 * 1024 * 1024
STAGE_SLOTS = 4
OUTPROJ_TM = 512
ATTN_TQ = 256
GQA_STACK = 2
SCORE_LOOKAHEAD = 2
QSCALE_A = math.log2(math.e) / math.sqrt(DK_A)
QSCALE_B = math.log2(math.e) / math.sqrt(DH_B)
VMEM_LIMIT = 56 * 1024 * 1024

BF16 = jnp.bfloat16
F32 = jnp.float32


def _params(n_grid_axes):
    return pltpu.CompilerParams(
        dimension_semantics=("arbitrary",) * n_grid_axes,
        vmem_limit_bytes=VMEM_LIMIT,
    )


def _silu(x):
    return x * jax.nn.sigmoid(x)


def _rms(x):
    return x * lax.rsqrt(jnp.mean(x * x, axis=-1, keepdims=True) + EPS)


def _stage_rows(n_cols):
    return 1 << int(math.log2(STAGE_BYTES // (n_cols * 4)))


def _weight_scratch(n_cols):
    return [pltpu.VMEM((n_cols // W_CHUNK, D_MODEL, W_CHUNK), BF16),
            pltpu.VMEM((STAGE_SLOTS, _stage_rows(n_cols), n_cols), F32),
            pltpu.SemaphoreType.DMA((STAGE_SLOTS,)),
            pltpu.SemaphoreType.DMA(())]


def _slab_copy(w_hbm, w_stage, sem, k):
    slot = k % STAGE_SLOTS
    n_rows = w_stage.shape[1]
    return pltpu.make_async_copy(
        w_hbm.at[pl.ds(k * n_rows, n_rows), :], w_stage.at[slot], sem.at[slot])


def _stage_weights(w_hbm, wbf_hbm, w_scr, w_stage, sem, sem_out):
    n_chunks = w_scr.shape[0]
    slab_rows = w_stage.shape[1]
    n_slabs = w_scr.shape[1] // slab_rows
    copy = functools.partial(_slab_copy, w_hbm, w_stage, sem)
    handoff = pltpu.make_async_copy(w_scr, wbf_hbm, sem_out)

    @pl.when(pl.program_id(0) == 0)
    def _():
        for k in range(STAGE_SLOTS - 1):
            copy(k).start()
        for k in range(n_slabs):
            if k + STAGE_SLOTS - 1 < n_slabs:
                copy(k + STAGE_SLOTS - 1).start()
            copy(k).wait()
            rows = slice(k * slab_rows, (k + 1) * slab_rows)
            for c in range(n_chunks):
                w_scr[c, rows, :] = w_stage[
                    k % STAGE_SLOTS, :, c * W_CHUNK:(c + 1) * W_CHUNK].astype(BF16)
        handoff.start()

    return handoff


def _weight_spec(n_cols):
    return pl.BlockSpec((n_cols // W_CHUNK, D_MODEL, W_CHUNK), lambda i: (0, 0, 0),
                        pipeline_mode=pl.Buffered(1))


def _adaln_body(cond_ref, w_ref, b_ref, o_ref):
    @pl.when(pl.program_id(0) == 0)
    def _():
        o_ref[...] = jnp.broadcast_to(b_ref[...], o_ref.shape)

    a = _silu(cond_ref[...]).astype(BF16)
    w = w_ref[...].astype(BF16)
    o_ref[...] += jnp.dot(a, w, preferred_element_type=F32)


def _adaln(cond, w_ada, b_ada):
    d3 = w_ada.shape[1]
    return pl.pallas_call(
        _adaln_body,
        out_shape=jax.ShapeDtypeStruct((N_COND, d3), F32),
        grid=(D_MODEL // ADA_TK,),
        in_specs=[
            pl.BlockSpec((N_COND, ADA_TK), lambda k: (0, k)),
            pl.BlockSpec((ADA_TK, d3), lambda k: (k, 0)),
            pl.BlockSpec((1, d3), lambda k: (0, 0)),
        ],
        out_specs=pl.BlockSpec((N_COND, d3), lambda k: (0, 0)),
        compiler_params=_params(1),
        name="adaln",
    )(cond, w_ada, b_ada)


def _rope_a(x, c, s_up, s_dn):
    return x * c + pltpu.roll(x, 96, 1) * s_up + pltpu.roll(x, 32, 1) * s_dn


def _rope_b(x, c, s):
    return x * c + pltpu.roll(x, 64, 1) * s


def _inproj_body(rope, x_ref, mod_ref, gn_ref, w_in_ref, qn_ref, kn_ref, *rest):
    if rope:
        ca_ref, sau_ref, sad_ref, cb_ref, sb_ref = rest[:5]
        rest = rest[5:]
    qa_o, ka_o, va_o, ga_o, qb_o, kb_o, vb_o, gb_o = rest[:8]
    cache_layout = not rope
    if rope:
        weights = w_in_ref
    else:
        wbf_hbm, weights, w_stage, sem, sem_out = rest[8:]
        handoff = _stage_weights(w_in_ref, wbf_hbm, weights, w_stage, sem, sem_out)
    tm = x_ref.shape[0]

    x = x_ref[...]
    shift = mod_ref[:, 0:D_MODEL]
    scale = mod_ref[:, D_MODEL:2 * D_MODEL]
    h = ((_rms(x) * gn_ref[...]) * (1.0 + scale) + shift).astype(BF16)

    if rope:
        ca, sau, sad = ca_ref[...], sau_ref[...], sad_ref[...]
        cb, sb = cb_ref[...], sb_ref[...]

    def rope_a(t):
        return _rope_a(t, ca, sau, sad) if rope else t

    def rope_b(t):
        return _rope_b(t, cb, sb) if rope else t

    ident = lambda t: t
    regions = (
        (OFF_AQ, A_Q, qa_o, lambda t: rope_a(t) * QSCALE_A, False),
        (OFF_AK, A_K, ka_o, rope_a, True),
        (OFF_AV, A_V, va_o, ident, True),
        (OFF_AG, A_G, ga_o, _silu, False),
        (OFF_BQ, B_Q, qb_o, lambda t: rope_b(_rms(t) * qn_ref[...]) * QSCALE_B, False),
        (OFF_BK, B_K, kb_o, lambda t: rope_b(_rms(t) * kn_ref[...]), True),
        (OFF_BV, B_V, vb_o, ident, True),
        (OFF_BG, B_G, gb_o, _silu, False),
    )
    for start, width, o_ref, epi, is_kv in regions:
        n_heads = width // HEAD_W
        for c0 in range(0, width, W_CHUNK):
            z = jnp.dot(h, weights[(start + c0) // W_CHUNK], preferred_element_type=F32)
            for h0 in range(0, W_CHUNK, HEAD_W):
                hd = (c0 + h0) // HEAD_W
                t = epi(z[:, h0:h0 + HEAD_W]).astype(o_ref.dtype)
                if is_kv and cache_layout:
                    o_ref[pl.ds(hd, tm, stride=n_heads), :] = t
                else:
                    o_ref[:, hd * HEAD_W:(hd + 1) * HEAD_W] = t

    if not rope:
        pl.when(pl.program_id(0) == pl.num_programs(0) - 1)(handoff.wait)


def _inproj(x2d, mod3, row_of_tile, g_norm, w_in, q_norm_g, k_norm_g, rope_tabs):
    t = x2d.shape[0]
    tm = INPROJ_TM
    rope = rope_tabs is not None
    row_spec = lambda w: pl.BlockSpec((tm, w), lambda i: (i, 0))
    const = lambda shape: pl.BlockSpec(shape, lambda i: (0,) * len(shape))
    in_specs = [
        row_spec(D_MODEL),
        pl.BlockSpec((None, 1, 3 * D_MODEL), lambda i: (row_of_tile(i), 0, 0)),
        const((1, D_MODEL)),
        _weight_spec(D_IN) if rope else pl.BlockSpec(memory_space=pl.ANY),
        const((1, HEAD_W)),
        const((1, HEAD_W)),
    ]
    args = [x2d, mod3, g_norm, w_in, q_norm_g, k_norm_g]
    if rope:
        n_pos_tiles = rope_tabs[0].shape[0] // tm
        in_specs += [pl.BlockSpec((tm, HEAD_W), lambda i: (i % n_pos_tiles, 0))] * 5
        args += list(rope_tabs)
    widths = (A_Q, A_K, A_V, A_G, B_Q, B_K, B_V, B_G)
    is_kv = (False, True, True, False, False, True, True, False)
    out_shape, out_specs = [], []
    for w, kv in zip(widths, is_kv):
        if kv and not rope:
            n_heads = w // HEAD_W
            out_shape.append(jax.ShapeDtypeStruct((t * n_heads, HEAD_W), F32))
            out_specs.append(pl.BlockSpec((tm * n_heads, HEAD_W), lambda i: (i, 0)))
        else:
            out_shape.append(jax.ShapeDtypeStruct((t, w), BF16))
            out_specs.append(row_spec(w))
    if not rope:
        out_shape.append(jax.ShapeDtypeStruct((D_IN // W_CHUNK, D_MODEL, W_CHUNK), BF16))
        out_specs.append(pl.BlockSpec(memory_space=pl.ANY))
    return pl.pallas_call(
        functools.partial(_inproj_body, rope),
        out_shape=out_shape,
        grid=(t // tm,),
        in_specs=in_specs,
        out_specs=out_specs,
        scratch_shapes=[] if rope else _weight_scratch(D_IN),
        compiler_params=_params(1),
        name="inproj_rope" if rope else "inproj",
    )(*args)


def _diff_lambda_col(lq1_ref, lk1_ref, lq2_ref, lk2_ref, lam_init):
    s1 = jnp.sum(lq1_ref[...] * lk1_ref[...], axis=-1, keepdims=True)
    s2 = jnp.sum(lq2_ref[...] * lk2_ref[...], axis=-1, keepdims=True)
    return jnp.exp(s1) - jnp.exp(s2) + lam_init


def _cache_head(ref, head, n_tok, n_heads):
    return ref[pl.ds(head, n_tok, stride=n_heads), :].astype(BF16)


def _scores(q, k):
    return lax.dot_general(q, k, (((1,), (1,)), ((), ())), preferred_element_type=F32)


def _softmax_pv(s, v):
    m = jnp.max(s, axis=-1, keepdims=True)
    e = jnp.exp2(s - m)
    v1 = jnp.concatenate([v, jnp.ones_like(v)], axis=1)
    ol = jnp.dot(e.astype(BF16), v1, preferred_element_type=F32)
    return ol[:, :HEAD_W], ol[:, HEAD_W:]


def _attend(lam_col, subln, qa_ref, ga_ref, qb_ref, gb_ref, ka, va, kb, vb, o_ref):
    tq = qa_ref.shape[0]
    lane = lax.broadcasted_iota(jnp.int32, (tq, HEAD_W), 1)
    first = lane < DK_A

    def diff_scores(hd):
        q = qa_ref[:, hd * HEAD_W:(hd + 1) * HEAD_W]
        zero = jnp.zeros_like(q)
        q2 = jnp.concatenate([jnp.where(first, q, zero), jnp.where(first, zero, q)], axis=0)
        return _scores(q2, ka(hd))

    def diff_finish(hd, s):
        cols = slice(hd * HEAD_W, (hd + 1) * HEAD_W)
        o2, l2 = _softmax_pv(s, va(hd))
        r2 = 1.0 / l2
        lam = lam_col[hd:hd + 1, :]
        o = o2[:tq] * r2[:tq] - o2[tq:] * (lam * r2[tq:])
        o_ref[:, cols] = (_rms(o) * subln * ga_ref[:, cols].astype(F32)).astype(o_ref.dtype)

    def gqa_scores(h0):
        q = jnp.concatenate(
            [qb_ref[:, (h0 + g) * HEAD_W:(h0 + g + 1) * HEAD_W] for g in range(GQA_STACK)], axis=0)
        return _scores(q, kb(h0 // G_B))

    def gqa_finish(h0, s):
        o, l = _softmax_pv(s, vb(h0 // G_B))
        o = o * (1.0 / l)
        for g in range(GQA_STACK):
            cols = slice((h0 + g) * HEAD_W, (h0 + g + 1) * HEAD_W)
            gate = gb_ref[:, cols].astype(F32)
            o_ref[:, A_V + (h0 + g) * HEAD_W:A_V + (h0 + g + 1) * HEAD_W] = (
                o[g * tq:(g + 1) * tq, :] * gate).astype(o_ref.dtype)

    units = [(diff_scores, diff_finish, hd) for hd in range(H_A)]
    units += [(gqa_scores, gqa_finish, h0) for h0 in range(0, H_B, GQA_STACK)]
    scores = []
    for u, (_, finish, arg) in enumerate(units):
        while len(scores) < min(len(units), u + 1 + SCORE_LOOKAHEAD):
            issue, _, issue_arg = units[len(scores)]
            scores.append(issue(issue_arg))
        finish(arg, scores[u])
        scores[u] = None


def _project_out(o, weights, x_ref, mod_ref, gf_ref, y_ref):
    m = jnp.concatenate(
        [jnp.dot(o, weights[c], preferred_element_type=F32) for c in range(D_MODEL // W_CHUNK)],
        axis=1)
    gate = mod_ref[:, 2 * D_MODEL:3 * D_MODEL]
    y_ref[...] = _rms(x_ref[...] + gate * m) * gf_ref[...]


def _attn_ctx_body(lam_init, lq1, lk1, lq2, lk2, subln_ref,
                   qa_ref, ka_ref, va_ref, ga_ref, qb_ref, kb_ref, vb_ref, gb_ref,
                   w_out_hbm, x_ref, mod_ref, gf_ref, y_ref, wbf_hbm,
                   w_scr, w_stage, sem, sem_out, o_scr):
    handoff = _stage_weights(w_out_hbm, wbf_hbm, w_scr, w_stage, sem, sem_out)
    lam_col = _diff_lambda_col(lq1, lk1, lq2, lk2, lam_init)
    seq = qa_ref.shape[0]
    head = lambda ref, n_heads: (lambda i: _cache_head(ref, i, seq, n_heads))
    _attend(lam_col, subln_ref[...] * (1.0 - lam_init), qa_ref, ga_ref, qb_ref, gb_ref,
            head(ka_ref, H_A), head(va_ref, H_A), head(kb_ref, KV_B), head(vb_ref, KV_B), o_scr)
    _project_out(o_scr[...], w_scr, x_ref, mod_ref, gf_ref, y_ref)
    pl.when(pl.program_id(0) == pl.num_programs(0) - 1)(handoff.wait)


def _attn_ctx(lam_init, lam_params, subln_g, qa, ka, va, ga, qb, kb, vb, gb, w_out, x2d, mod3,
              g_final, seq):
    t = qa.shape[0]
    const = lambda shape: pl.BlockSpec(shape, lambda b: (0,) * len(shape))
    row_spec = lambda w: pl.BlockSpec((seq, w), lambda b: (b, 0))
    cache_spec = lambda n_heads: pl.BlockSpec((seq * n_heads, HEAD_W), lambda b: (b, 0))
    return pl.pallas_call(
        functools.partial(_attn_ctx_body, lam_init),
        out_shape=[jax.ShapeDtypeStruct((t, D_MODEL), F32),
                   jax.ShapeDtypeStruct((D_MODEL // W_CHUNK, D_MIX, W_CHUNK), BF16)],
        grid=(t // seq,),
        in_specs=[const((H_A, DK_A))] * 4 + [const((1, DV_A))] + [
            row_spec(A_Q), cache_spec(H_A), cache_spec(H_A), row_spec(A_G),
            row_spec(B_Q), cache_spec(KV_B), cache_spec(KV_B), row_spec(B_G),
            pl.BlockSpec(memory_space=pl.ANY),
            row_spec(D_MODEL),
            pl.BlockSpec((None, 1, 3 * D_MODEL), lambda b: (0, 0, 0)),
            const((1, D_MODEL))],
        out_specs=[row_spec(D_MODEL), pl.BlockSpec(memory_space=pl.ANY)],
        scratch_shapes=_weight_scratch(D_MODEL) + [pltpu.VMEM((seq, D_MIX), BF16)],
        compiler_params=_params(1),
        name="attn_ctx",
    )(*lam_params, subln_g, qa, ka, va, ga, qb, kb, vb, gb, w_out, x2d, mod3, g_final)


def _attn_lat_body(lam_init, n_lat, lq1, lk1, lq2, lk2, subln_ref,
                   qa_ref, ka_ref, va_ref, cka_ref, cva_ref, ga_ref,
                   qb_ref, kb_ref, vb_ref, ckb_ref, cvb_ref, gb_ref, o_ref,
                   ka_all, va_all, kb_all, vb_all):
    @pl.when(pl.program_id(1) == 0)
    def _():
        for new_ref, cache_ref, all_ref in ((ka_ref, cka_ref, ka_all), (va_ref, cva_ref, va_all),
                                            (kb_ref, ckb_ref, kb_all), (vb_ref, cvb_ref, vb_all)):
            n_heads = new_ref.shape[1] // HEAD_W
            n_past = cache_ref.shape[0] // n_heads
            for hd in range(n_heads):
                all_ref[hd, 0:n_lat, :] = new_ref[:, hd * HEAD_W:(hd + 1) * HEAD_W]
                all_ref[hd, n_lat:, :] = _cache_head(cache_ref, hd, n_past, n_heads)

    lam_col = _diff_lambda_col(lq1, lk1, lq2, lk2, lam_init)
    head = lambda ref: (lambda i: ref[i])
    _attend(lam_col, subln_ref[...] * (1.0 - lam_init), qa_ref, ga_ref, qb_ref, gb_ref,
            head(ka_all), head(va_all), head(kb_all), head(vb_all), o_ref)


def _attn_lat(lam_init, lam_params, subln_g, qa, ka, va, cka, cva, ga, qb, kb, vb, ckb, cvb, gb,
              n_lat, n_past):
    t = qa.shape[0]
    tq = ATTN_TQ
    nq = n_lat // tq
    const = lambda shape: pl.BlockSpec(shape, lambda b, i: (0,) * len(shape))
    q_spec = lambda w: pl.BlockSpec((tq, w), lambda b, i: (b * nq + i, 0))
    new_spec = lambda w: pl.BlockSpec((n_lat, w), lambda b, i: (b, 0))
    cache_spec = lambda w: pl.BlockSpec((n_past * (w // HEAD_W), HEAD_W), lambda b, i: (b, 0))
    n_all = n_lat + n_past
    return pl.pallas_call(
        functools.partial(_attn_lat_body, lam_init, n_lat),
        out_shape=jax.ShapeDtypeStruct((t, D_MIX), BF16),
        grid=(t // n_lat, nq),
        in_specs=[const((H_A, DK_A))] * 4 + [const((1, DV_A))] + [
            q_spec(A_Q), new_spec(A_K), new_spec(A_V), cache_spec(A_K), cache_spec(A_V), q_spec(A_G),
            q_spec(B_Q), new_spec(B_K), new_spec(B_V), cache_spec(B_K), cache_spec(B_V), q_spec(B_G)],
        out_specs=q_spec(D_MIX),
        scratch_shapes=[pltpu.VMEM((H_A, n_all, HEAD_W), BF16), pltpu.VMEM((H_A, n_all, HEAD_W), BF16),
                        pltpu.VMEM((KV_B, n_all, HEAD_W), BF16), pltpu.VMEM((KV_B, n_all, HEAD_W), BF16)],
        compiler_params=_params(2),
        name="attn_lat",
    )(*lam_params, subln_g, qa, ka, va, cka, cva, ga, qb, kb, vb, ckb, cvb, gb)


def _outproj_body(o_ref, w_ref, x_ref, mod_ref, gf_ref, y_ref):
    _project_out(o_ref[...], w_ref, x_ref, mod_ref, gf_ref, y_ref)


def _outproj(o, w_out_bf16, x2d, mod3, row_of_tile, g_final):
    t = x2d.shape[0]
    tm = OUTPROJ_TM
    return pl.pallas_call(
        _outproj_body,
        out_shape=jax.ShapeDtypeStruct((t, D_MODEL), F32),
        grid=(t // tm,),
        in_specs=[
            pl.BlockSpec((tm, D_MIX), lambda i: (i, 0)),
            _weight_spec(D_MODEL),
            pl.BlockSpec((tm, D_MODEL), lambda i: (i, 0)),
            pl.BlockSpec((None, 1, 3 * D_MODEL), lambda i: (row_of_tile(i), 0, 0)),
            pl.BlockSpec((1, D_MODEL), lambda i: (0, 0)),
        ],
        out_specs=pl.BlockSpec((tm, D_MODEL), lambda i: (i, 0)),
        compiler_params=_params(1),
        name="outproj",
    )(o, w_out_bf16, x2d, mod3, g_final)


def _rope_tables(n_tok):
    n_rows = n_tok // GRID_W
    rows = np.repeat(np.arange(n_rows, dtype=np.float32), GRID_W)
    cols = np.tile(np.arange(GRID_W, dtype=np.float32), n_rows)

    def cos_sin(rot_dim):
        quarter = rot_dim // 4
        inv_freq = (1.0 / (np.float32(ROPE_THETA) ** (np.arange(quarter, dtype=np.float32) / quarter))
                    ).astype(np.float32)
        ang = np.concatenate([rows[:, None] * inv_freq, cols[:, None] * inv_freq], axis=-1)
        return np.cos(ang).astype(np.float32), np.sin(ang).astype(np.float32)

    ca, sa = cos_sin(DK_A)
    za = np.zeros_like(sa)
    cb, sb = cos_sin(DH_B)
    tabs = (
        np.concatenate([ca, ca, ca, ca], axis=-1),
        np.concatenate([-sa, za, -sa, za], axis=-1),
        np.concatenate([za, sa, za, sa], axis=-1),
        np.concatenate([cb, cb], axis=-1),
        np.concatenate([-sb, sb], axis=-1),
    )
    return tuple(jnp.asarray(t) for t in tabs)


def kernel(x_prompt, x_sample, cache_diff_k, cache_diff_v, cache_gqa_k, cache_gqa_v, c, c_ctx,
           w_ada, b_ada, g_norm, w_in, lam_q1, lam_k1, lam_q2, lam_k2, subln_g, q_norm_g, k_norm_g,
           w_out, g_final):
    bp, n_ctx, d = x_prompt.shape
    bs, n_lat, _ = x_sample.shape
    depth = w_in.shape[0]
    n_past = cache_diff_k.shape[2]
    assert depth == 1 and d == D_MODEL and bs + 1 <= N_COND
    assert n_lat % INPROJ_TM == 0 and n_lat % OUTPROJ_TM == 0 and n_lat % ATTN_TQ == 0
    l = 0
    lam_init = 0.8 - 0.6 * math.exp(-0.3 * l)

    cond = jnp.concatenate(
        [c_ctx[None, :], c, jnp.zeros((N_COND - 1 - bs, d), F32)], axis=0)
    mod = _adaln(cond, w_ada[l], b_ada[l][None, :])
    mod3 = mod[:, None, :]

    gn = g_norm[l][None, :]
    qn = q_norm_g[l][None, :]
    kn = k_norm_g[l][None, :]
    lam_params = (lam_q1[l], lam_k1[l], lam_q2[l], lam_k2[l])
    subln = subln_g[l][None, :]
    gf = g_final[None, :]

    xp2 = x_prompt.reshape(bp * n_ctx, d)
    qa, ka, va, ga, qb, kb, vb, gb, w_in_bf16 = _inproj(
        xp2, mod3, lambda i: 0, gn, w_in[l], qn, kn, None)
    y_prompt, w_out_bf16 = _attn_ctx(lam_init, lam_params, subln, qa, ka, va, ga, qb, kb, vb, gb,
                                     w_out[l], xp2, mod3, gf, n_ctx)
    y_prompt = y_prompt.reshape(bp, n_ctx, d)
    new_diff_k = ka.reshape(bp, 1, n_ctx, H_A, 2 * DK_A)
    new_diff_v = va.reshape(bp, 1, n_ctx, H_A, DV_A)
    new_gqa_k = kb.reshape(bp, 1, n_ctx, KV_B, DH_B)
    new_gqa_v = vb.reshape(bp, 1, n_ctx, KV_B, DH_B)

    xs2 = x_sample.reshape(bs * n_lat, d)
    in_tiles = n_lat // INPROJ_TM
    qa, ka, va, ga, qb, kb, vb, gb = _inproj(
        xs2, mod3, lambda i: 1 + i // in_tiles, gn, w_in_bf16, qn, kn, _rope_tables(n_lat))
    cka = cache_diff_k[:, l].reshape(bs * n_past * H_A, HEAD_W)
    cva = cache_diff_v[:, l].reshape(bs * n_past * H_A, HEAD_W)
    ckb = cache_gqa_k[:, l].reshape(bs * n_past * KV_B, HEAD_W)
    cvb = cache_gqa_v[:, l].reshape(bs * n_past * KV_B, HEAD_W)
    o = _attn_lat(lam_init, lam_params, subln, qa, ka, va, cka, cva, ga, qb, kb, vb, ckb, cvb, gb,
                  n_lat, n_past)
    out_tiles = n_lat // OUTPROJ_TM
    y_sample = _outproj(o, w_out_bf16, xs2, mod3, lambda i: 1 + i // out_tiles, gf)
    y_sample = y_sample.reshape(bs, n_lat, d)

    return (y_prompt, y_sample, new_diff_k, new_diff_v, new_gqa_k, new_gqa_v)
```

```python
import functools
import math

import jax
import jax.numpy as jnp
import numpy as np
from jax import lax
from jax.experimental import pallas as pl
from jax.experimental.pallas import tpu as pltpu

D_MODEL = 2048
GRID_W = 64
ROPE_THETA = 10000.0
EPS = 1e-6
H_A = 8
DK_A = 64
DV_A = 2 * DK_A
H_B = 8
KV_B = 2
DH_B = 128
G_B = H_B // KV_B
HEAD_W = 128
A_Q = H_A * 2 * DK_A
A_K = H_A * 2 * DK_A
A_V = H_A * DV_A
A_G = H_A * DV_A
B_Q = H_B * DH_B
B_K = KV_B * DH_B
B_V = KV_B * DH_B
B_G = H_B * DH_B
D_IN = A_Q + A_K + A_V + A_G + B_Q + B_K + B_V + B_G
D_MIX = A_V + B_Q
OFF_AQ = 0
OFF_AK = OFF_AQ + A_Q
OFF_AV = OFF_AK + A_K
OFF_AG = OFF_AV + A_V
OFF_BQ = OFF_AG + A_G
OFF_BK = OFF_BQ + B_Q
OFF_BV = OFF_BK + B_K
OFF_BG = OFF_BV + B_V

N_COND = 8
ADA_TK = 256
INPROJ_TM = 256
W_CHUNK = 256
STAGE_BYTES = 2 * 1024 * 1024
STAGE_SLOTS = 4
OUTPROJ_TM = 512
ATTN_TQ = 256
GQA_STACK = 2
SCORE_LOOKAHEAD = 2
QSCALE_A = math.log2(math.e) / math.sqrt(DK_A)
QSCALE_B = math.log2(math.e) / math.sqrt(DH_B)
VMEM_LIMIT = 56 * 1024 * 1024

BF16 = jnp.bfloat16
F32 = jnp.float32


def _params(n_grid_axes):
    return pltpu.CompilerParams(
        dimension_semantics=("arbitrary",) * n_grid_axes,
        vmem_limit_bytes=VMEM_LIMIT,
    )


def _silu(x):
    return x * jax.nn.sigmoid(x)


def _rms(x):
    return x * lax.rsqrt(jnp.mean(x * x, axis=-1, keepdims=True) + EPS)


def _stage_rows(n_cols):
    return 1 << int(math.log2(STAGE_BYTES // (n_cols * 4)))


def _weight_scratch(n_cols):
    return [pltpu.VMEM((n_cols // W_CHUNK, D_MODEL, W_CHUNK), BF16),
            pltpu.VMEM((STAGE_SLOTS, _stage_rows(n_cols), n_cols), F32),
            pltpu.SemaphoreType.DMA((STAGE_SLOTS,)),
            pltpu.SemaphoreType.DMA(())]


def _slab_copy(w_hbm, w_stage, sem, k):
    slot = k % STAGE_SLOTS
    n_rows = w_stage.shape[1]
    return pltpu.make_async_copy(
        w_hbm.at[pl.ds(k * n_rows, n_rows), :], w_stage.at[slot], sem.at[slot])


def _stage_weights(w_hbm, wbf_hbm, w_scr, w_stage, sem, sem_out):
    n_chunks = w_scr.shape[0]
    slab_rows = w_stage.shape[1]
    n_slabs = w_scr.shape[1] // slab_rows
    copy = functools.partial(_slab_copy, w_hbm, w_stage, sem)
    handoff = pltpu.make_async_copy(w_scr, wbf_hbm, sem_out)

    @pl.when(pl.program_id(0) == 0)
    def _():
        for k in range(STAGE_SLOTS - 1):
            copy(k).start()
        for k in range(n_slabs):
            if k + STAGE_SLOTS - 1 < n_slabs:
                copy(k + STAGE_SLOTS - 1).start()
            copy(k).wait()
            rows = slice(k * slab_rows, (k + 1) * slab_rows)
            for c in range(n_chunks):
                w_scr[c, rows, :] = w_stage[
                    k % STAGE_SLOTS, :, c * W_CHUNK:(c + 1) * W_CHUNK].astype(BF16)
        handoff.start()

    return handoff


def _weight_spec(n_cols):
    return pl.BlockSpec((n_cols // W_CHUNK, D_MODEL, W_CHUNK), lambda i: (0, 0, 0),
                        pipeline_mode=pl.Buffered(1))


def _adaln_body(cond_ref, w_ref, b_ref, o_ref):
    @pl.when(pl.program_id(0) == 0)
    def _():
        o_ref[...] = jnp.broadcast_to(b_ref[...], o_ref.shape)

    a = _silu(cond_ref[...]).astype(BF16)
    w = w_ref[...].astype(BF16)
    o_ref[...] += jnp.dot(a, w, preferred_element_type=F32)


def _adaln(cond, w_ada, b_ada):
    d3 = w_ada.shape[1]
    return pl.pallas_call(
        _adaln_body,
        out_shape=jax.ShapeDtypeStruct((N_COND, d3), F32),
        grid=(D_MODEL // ADA_TK,),
        in_specs=[
            pl.BlockSpec((N_COND, ADA_TK), lambda k: (0, k)),
            pl.BlockSpec((ADA_TK, d3), lambda k: (k, 0)),
            pl.BlockSpec((1, d3), lambda k: (0, 0)),
        ],
        out_specs=pl.BlockSpec((N_COND, d3), lambda k: (0, 0)),
        compiler_params=_params(1),
        name="adaln",
    )(cond, w_ada, b_ada)


def _rope_a(x, c, s_up, s_dn):
    return x * c + pltpu.roll(x, 96, 1) * s_up + pltpu.roll(x, 32, 1) * s_dn


def _rope_b(x, c, s):
    return x * c + pltpu.roll(x, 64, 1) * s


def _inproj_body(rope, x_ref, mod_ref, gn_ref, w_in_ref, qn_ref, kn_ref, *rest):
    if rope:
        ca_ref, sau_ref, sad_ref, cb_ref, sb_ref = rest[:5]
        rest = rest[5:]
    qa_o, ka_o, va_o, ga_o, qb_o, kb_o, vb_o, gb_o = rest[:8]
    cache_layout = not rope
    if rope:
        weights = w_in_ref
    else:
        wbf_hbm, weights, w_stage, sem, sem_out = rest[8:]
        handoff = _stage_weights(w_in_ref, wbf_hbm, weights, w_stage, sem, sem_out)
    tm = x_ref.shape[0]

    x = x_ref[...]
    shift = mod_ref[:, 0:D_MODEL]
    scale = mod_ref[:, D_MODEL:2 * D_MODEL]
    h = ((_rms(x) * gn_ref[...]) * (1.0 + scale) + shift).astype(BF16)

    if rope:
        ca, sau, sad = ca_ref[...], sau_ref[...], sad_ref[...]
        cb, sb = cb_ref[...], sb_ref[...]

    def rope_a(t):
        return _rope_a(t, ca, sau, sad) if rope else t

    def rope_b(t):
        return _rope_b(t, cb, sb) if rope else t

    ident = lambda t: t
    regions = (
        (OFF_AQ, A_Q, qa_o, lambda t: rope_a(t) * QSCALE_A, False),
        (OFF_AK, A_K, ka_o, rope_a, True),
        (OFF_AG, A_G, ga_o, _silu, False),
        (OFF_BQ, B_Q, qb_o, lambda t: rope_b(_rms(t) * qn_ref[...]) * QSCALE_B, False),
        (OFF_BK, B_K, kb_o, lambda t: rope_b(_rms(t) * kn_ref[...]), True),
        (OFF_BG, B_G, gb_o, _silu, False),
        (OFF_AV, A_V, va_o, ident, True),
        (OFF_BV, B_V, vb_o, ident, True),
    )
    for start, width, o_ref, epi, is_kv in regions:
        n_heads = width // HEAD_W
        for c0 in range(0, width, W_CHUNK):
            z = jnp.dot(h, weights[(start + c0) // W_CHUNK], preferred_element_type=F32)
            for h0 in range(0, W_CHUNK, HEAD_W):
                hd = (c0 + h0) // HEAD_W
                t = epi(z[:, h0:h0 + HEAD_W]).astype(o_ref.dtype)
                if is_kv and cache_layout:
                    o_ref[pl.ds(hd, tm, stride=n_heads), :] = t
                else:
                    o_ref[:, hd * HEAD_W:(hd + 1) * HEAD_W] = t

    if not rope:
        pl.when(pl.program_id(0) == pl.num_programs(0) - 1)(handoff.wait)


def _inproj(x2d, mod3, row_of_tile, g_norm, w_in, q_norm_g, k_norm_g, rope_tabs):
    t = x2d.shape[0]
    tm = INPROJ_TM
    rope = rope_tabs is not None
    row_spec = lambda w: pl.BlockSpec((tm, w), lambda i: (i, 0))
    const = lambda shape: pl.BlockSpec(shape, lambda i: (0,) * len(shape))
    in_specs = [
        row_spec(D_MODEL),
        pl.BlockSpec((None, 1, 3 * D_MODEL), lambda i: (row_of_tile(i), 0, 0)),
        const((1, D_MODEL)),
        _weight_spec(D_IN) if rope else pl.BlockSpec(memory_space=pl.ANY),
        const((1, HEAD_W)),
        const((1, HEAD_W)),
    ]
    args = [x2d, mod3, g_norm, w_in, q_norm_g, k_norm_g]
    if rope:
        n_pos_tiles = rope_tabs[0].shape[0] // tm
        in_specs += [pl.BlockSpec((tm, HEAD_W), lambda i: (i % n_pos_tiles, 0))] * 5
        args += list(rope_tabs)
    widths = (A_Q, A_K, A_V, A_G, B_Q, B_K, B_V, B_G)
    is_kv = (False, True, True, False, False, True, True, False)
    out_shape, out_specs = [], []
    for w, kv in zip(widths, is_kv):
        if kv and not rope:
            n_heads = w // HEAD_W
            out_shape.append(jax.ShapeDtypeStruct((t * n_heads, HEAD_W), F32))
            out_specs.append(pl.BlockSpec((tm * n_heads, HEAD_W), lambda i: (i, 0)))
        else:
            out_shape.append(jax.ShapeDtypeStruct((t, w), BF16))
            out_specs.append(row_spec(w))
    if not rope:
        out_shape.append(jax.ShapeDtypeStruct((D_IN // W_CHUNK, D_MODEL, W_CHUNK), BF16))
        out_specs.append(pl.BlockSpec(memory_space=pl.ANY))
    return pl.pallas_call(
        functools.partial(_inproj_body, rope),
        out_shape=out_shape,
        grid=(t // tm,),
        in_specs=in_specs,
        out_specs=out_specs,
        scratch_shapes=[] if rope else _weight_scratch(D_IN),
        compiler_params=_params(1),
        name="inproj_rope" if rope else "inproj",
    )(*args)


def _diff_lambda_col(lq1_ref, lk1_ref, lq2_ref, lk2_ref, lam_init):
    s1 = jnp.sum(lq1_ref[...] * lk1_ref[...], axis=-1, keepdims=True)
    s2 = jnp.sum(lq2_ref[...] * lk2_ref[...], axis=-1, keepdims=True)
    return jnp.exp(s1) - jnp.exp(s2) + lam_init


def _cache_head(ref, head, n_tok, n_heads):
    return ref[pl.ds(head, n_tok, stride=n_heads), :].astype(BF16)


def _scores(q, k):
    return lax.dot_general(q, k, (((1,), (1,)), ((), ())), preferred_element_type=F32)


def _softmax_pv(s, v):
    m = jnp.max(s, axis=-1, keepdims=True)
    e = jnp.exp2(s - m)
    v1 = jnp.concatenate([v, jnp.ones_like(v)], axis=1)
    ol = jnp.dot(e.astype(BF16), v1, preferred_element_type=F32)
    return ol[:, :HEAD_W], ol[:, HEAD_W:]


def _attend(lam_col, subln, qa_ref, ga_ref, qb_ref, gb_ref, ka, va, kb, vb, o_ref):
    tq = qa_ref.shape[0]
    lane = lax.broadcasted_iota(jnp.int32, (tq, HEAD_W), 1)
    first = lane < DK_A

    def diff_scores(hd):
        q = qa_ref[:, hd * HEAD_W:(hd + 1) * HEAD_W]
        zero = jnp.zeros_like(q)
        q2 = jnp.concatenate([jnp.where(first, q, zero), jnp.where(first, zero, q)], axis=0)
        return _scores(q2, ka(hd))

    def diff_finish(hd, s):
        cols = slice(hd * HEAD_W, (hd + 1) * HEAD_W)
        o2, l2 = _softmax_pv(s, va(hd))
        r2 = 1.0 / l2
        lam = lam_col[hd:hd + 1, :]
        o = o2[:tq] * r2[:tq] - o2[tq:] * (lam * r2[tq:])
        o_ref[:, cols] = (_rms(o) * subln * ga_ref[:, cols].astype(F32)).astype(o_ref.dtype)

    def gqa_scores(h0):
        q = jnp.concatenate(
            [qb_ref[:, (h0 + g) * HEAD_W:(h0 + g + 1) * HEAD_W] for g in range(GQA_STACK)], axis=0)
        return _scores(q, kb(h0 // G_B))

    def gqa_finish(h0, s):
        o, l = _softmax_pv(s, vb(h0 // G_B))
        o = o * (1.0 / l)
        for g in range(GQA_STACK):
            cols = slice((h0 + g) * HEAD_W, (h0 + g + 1) * HEAD_W)
            gate = gb_ref[:, cols].astype(F32)
            o_ref[:, A_V + (h0 + g) * HEAD_W:A_V + (h0 + g + 1) * HEAD_W] = (
                o[g * tq:(g + 1) * tq, :] * gate).astype(o_ref.dtype)

    units = [(diff_scores, diff_finish, hd) for hd in range(H_A)]
    units += [(gqa_scores, gqa_finish, h0) for h0 in range(0, H_B, GQA_STACK)]
    scores = []
    for u, (_, finish, arg) in enumerate(units):
        while len(scores) < min(len(units), u + 1 + SCORE_LOOKAHEAD):
            issue, _, issue_arg = units[len(scores)]
            scores.append(issue(issue_arg))
        finish(arg, scores[u])
        scores[u] = None


def _project_out(o, weights, x_ref, mod_ref, gf_ref, y_ref):
    m = jnp.concatenate(
        [jnp.dot(o, weights[c], preferred_element_type=F32) for c in range(D_MODEL // W_CHUNK)],
        axis=1)
    gate = mod_ref[:, 2 * D_MODEL:3 * D_MODEL]
    y_ref[...] = _rms(x_ref[...] + gate * m) * gf_ref[...]


def _attn_ctx_body(lam_init, lq1, lk1, lq2, lk2, subln_ref,
                   qa_ref, ka_ref, va_ref, ga_ref, qb_ref, kb_ref, vb_ref, gb_ref,
                   w_out_hbm, x_ref, mod_ref, gf_ref, y_ref, wbf_hbm,
                   w_scr, w_stage, sem, sem_out, o_scr):
    handoff = _stage_weights(w_out_hbm, wbf_hbm, w_scr, w_stage, sem, sem_out)
    lam_col = _diff_lambda_col(lq1, lk1, lq2, lk2, lam_init)
    seq = qa_ref.shape[0]
    head = lambda ref, n_heads: (lambda i: _cache_head(ref, i, seq, n_heads))
    _attend(lam_col, subln_ref[...] * (1.0 - lam_init), qa_ref, ga_ref, qb_ref, gb_ref,
            head(ka_ref, H_A), head(va_ref, H_A), head(kb_ref, KV_B), head(vb_ref, KV_B), o_scr)
    _project_out(o_scr[...], w_scr, x_ref, mod_ref, gf_ref, y_ref)
    pl.when(pl.program_id(0) == pl.num_programs(0) - 1)(handoff.wait)


def _attn_ctx(lam_init, lam_params, subln_g, qa, ka, va, ga, qb, kb, vb, gb, w_out, x2d, mod3,
              g_final, seq):
    t = qa.shape[0]
    const = lambda shape: pl.BlockSpec(shape, lambda b: (0,) * len(shape))
    row_spec = lambda w: pl.BlockSpec((seq, w), lambda b: (b, 0))
    cache_spec = lambda n_heads: pl.BlockSpec((seq * n_heads, HEAD_W), lambda b: (b, 0))
    return pl.pallas_call(
        functools.partial(_attn_ctx_body, lam_init),
        out_shape=[jax.ShapeDtypeStruct((t, D_MODEL), F32),
                   jax.ShapeDtypeStruct((D_MODEL // W_CHUNK, D_MIX, W_CHUNK), BF16)],
        grid=(t // seq,),
        in_specs=[const((H_A, DK_A))] * 4 + [const((1, DV_A))] + [
            row_spec(A_Q), cache_spec(H_A), cache_spec(H_A), row_spec(A_G),
            row_spec(B_Q), cache_spec(KV_B), cache_spec(KV_B), row_spec(B_G),
            pl.BlockSpec(memory_space=pl.ANY),
            row_spec(D_MODEL),
            pl.BlockSpec((None, 1, 3 * D_MODEL), lambda b: (0, 0, 0)),
            const((1, D_MODEL))],
        out_specs=[row_spec(D_MODEL), pl.BlockSpec(memory_space=pl.ANY)],
        scratch_shapes=_weight_scratch(D_MODEL) + [pltpu.VMEM((seq, D_MIX), BF16)],
        compiler_params=_params(1),
        name="attn_ctx",
    )(*lam_params, subln_g, qa, ka, va, ga, qb, kb, vb, gb, w_out, x2d, mod3, g_final)


def _attn_lat_body(lam_init, n_lat, lq1, lk1, lq2, lk2, subln_ref,
                   qa_ref, ka_ref, va_ref, cka_ref, cva_ref, ga_ref,
                   qb_ref, kb_ref, vb_ref, ckb_ref, cvb_ref, gb_ref, o_ref,
                   ka_all, va_all, kb_all, vb_all):
    @pl.when(pl.program_id(1) == 0)
    def _():
        for new_ref, cache_ref, all_ref in ((ka_ref, cka_ref, ka_all), (va_ref, cva_ref, va_all),
                                            (kb_ref, ckb_ref, kb_all), (vb_ref, cvb_ref, vb_all)):
            n_heads = new_ref.shape[1] // HEAD_W
            n_past = cache_ref.shape[0] // n_heads
            all_ref[0:n_lat, :] = new_ref[...]
            for hd in range(n_heads):
                all_ref[n_lat:, hd * HEAD_W:(hd + 1) * HEAD_W] = _cache_head(cache_ref, hd, n_past, n_heads)

    lam_col = _diff_lambda_col(lq1, lk1, lq2, lk2, lam_init)
    head = lambda ref: (lambda i: ref[:, i * HEAD_W:(i + 1) * HEAD_W])
    _attend(lam_col, subln_ref[...] * (1.0 - lam_init), qa_ref, ga_ref, qb_ref, gb_ref,
            head(ka_all), head(va_all), head(kb_all), head(vb_all), o_ref)


def _attn_lat(lam_init, lam_params, subln_g, qa, ka, va, cka, cva, ga, qb, kb, vb, ckb, cvb, gb,
              n_lat, n_past):
    t = qa.shape[0]
    tq = ATTN_TQ
    nq = n_lat // tq
    const = lambda shape: pl.BlockSpec(shape, lambda b, i: (0,) * len(shape))
    q_spec = lambda w: pl.BlockSpec((tq, w), lambda b, i: (b * nq + i, 0))
    new_spec = lambda w: pl.BlockSpec((n_lat, w), lambda b, i: (b, 0))
    cache_spec = lambda w: pl.BlockSpec((n_past * (w // HEAD_W), HEAD_W), lambda b, i: (b, 0))
    n_all = n_lat + n_past
    return pl.pallas_call(
        functools.partial(_attn_lat_body, lam_init, n_lat),
        out_shape=jax.ShapeDtypeStruct((t, D_MIX), BF16),
        grid=(t // n_lat, nq),
        in_specs=[const((H_A, DK_A))] * 4 + [const((1, DV_A))] + [
            q_spec(A_Q), new_spec(A_K), new_spec(A_V), cache_spec(A_K), cache_spec(A_V), q_spec(A_G),
            q_spec(B_Q), new_spec(B_K), new_spec(B_V), cache_spec(B_K), cache_spec(B_V), q_spec(B_G)],
        out_specs=q_spec(D_MIX),
        scratch_shapes=[pltpu.VMEM((n_all, A_K), BF16), pltpu.VMEM((n_all, A_V), BF16),
                        pltpu.VMEM((n_all, B_K), BF16), pltpu.VMEM((n_all, B_V), BF16)],
        compiler_params=_params(2),
        name="attn_lat",
    )(*lam_params, subln_g, qa, ka, va, cka, cva, ga, qb, kb, vb, ckb, cvb, gb)


def _outproj_body(o_ref, w_ref, x_ref, mod_ref, gf_ref, y_ref):
    _project_out(o_ref[...], w_ref, x_ref, mod_ref, gf_ref, y_ref)


def _outproj(o, w_out_bf16, x2d, mod3, row_of_tile, g_final):
    t = x2d.shape[0]
    tm = OUTPROJ_TM
    return pl.pallas_call(
        _outproj_body,
        out_shape=jax.ShapeDtypeStruct((t, D_MODEL), F32),
        grid=(t // tm,),
        in_specs=[
            pl.BlockSpec((tm, D_MIX), lambda i: (i, 0)),
            _weight_spec(D_MODEL),
            pl.BlockSpec((tm, D_MODEL), lambda i: (i, 0)),
            pl.BlockSpec((None, 1, 3 * D_MODEL), lambda i: (row_of_tile(i), 0, 0)),
            pl.BlockSpec((1, D_MODEL), lambda i: (0, 0)),
        ],
        out_specs=pl.BlockSpec((tm, D_MODEL), lambda i: (i, 0)),
        compiler_params=_params(1),
        name="outproj",
    )(o, w_out_bf16, x2d, mod3, g_final)


def _rope_tables(n_tok):
    n_rows = n_tok // GRID_W
    rows = np.repeat(np.arange(n_rows, dtype=np.float32), GRID_W)
    cols = np.tile(np.arange(GRID_W, dtype=np.float32), n_rows)

    def cos_sin(rot_dim):
        quarter = rot_dim // 4
        inv_freq = (1.0 / (np.float32(ROPE_THETA) ** (np.arange(quarter, dtype=np.float32) / quarter))
                    ).astype(np.float32)
        ang = np.concatenate([rows[:, None] * inv_freq, cols[:, None] * inv_freq], axis=-1)
        return np.cos(ang).astype(np.float32), np.sin(ang).astype(np.float32)

    ca, sa = cos_sin(DK_A)
    za = np.zeros_like(sa)
    cb, sb = cos_sin(DH_B)
    tabs = (
        np.concatenate([ca, ca, ca, ca], axis=-1),
        np.concatenate([-sa, za, -sa, za], axis=-1),
        np.concatenate([za, sa, za, sa], axis=-1),
        np.concatenate([cb, cb], axis=-1),
        np.concatenate([-sb, sb], axis=-1),
    )
    return tuple(jnp.asarray(t) for t in tabs)


def kernel(x_prompt, x_sample, cache_diff_k, cache_diff_v, cache_gqa_k, cache_gqa_v, c, c_ctx,
           w_ada, b_ada, g_norm, w_in, lam_q1, lam_k1, lam_q2, lam_k2, subln_g, q_norm_g, k_norm_g,
           w_out, g_final):
    bp, n_ctx, d = x_prompt.shape
    bs, n_lat, _ = x_sample.shape
    depth = w_in.shape[0]
    n_past = cache_diff_k.shape[2]
    assert depth == 1 and d == D_MODEL and bs + 1 <= N_COND
    assert n_lat % INPROJ_TM == 0 and n_lat % OUTPROJ_TM == 0 and n_lat % ATTN_TQ == 0
    l = 0
    lam_init = 0.8 - 0.6 * math.exp(-0.3 * l)

    cond = jnp.concatenate(
        [c_ctx[None, :], c, jnp.zeros((N_COND - 1 - bs, d), F32)], axis=0)
    mod = _adaln(cond, w_ada[l], b_ada[l][None, :])
    mod3 = mod[:, None, :]

    gn = g_norm[l][None, :]
    qn = q_norm_g[l][None, :]
    kn = k_norm_g[l][None, :]
    lam_params = (lam_q1[l], lam_k1[l], lam_q2[l], lam_k2[l])
    subln = subln_g[l][None, :]
    gf = g_final[None, :]

    xp2 = x_prompt.reshape(bp * n_ctx, d)
    qa, ka, va, ga, qb, kb, vb, gb, w_in_bf16 = _inproj(
        xp2, mod3, lambda i: 0, gn, w_in[l], qn, kn, None)
    y_prompt, w_out_bf16 = _attn_ctx(lam_init, lam_params, subln, qa, ka, va, ga, qb, kb, vb, gb,
                                     w_out[l], xp2, mod3, gf, n_ctx)
    y_prompt = y_prompt.reshape(bp, n_ctx, d)
    new_diff_k = ka.reshape(bp, 1, n_ctx, H_A, 2 * DK_A)
    new_diff_v = va.reshape(bp, 1, n_ctx, H_A, DV_A)
    new_gqa_k = kb.reshape(bp, 1, n_ctx, KV_B, DH_B)
    new_gqa_v = vb.reshape(bp, 1, n_ctx, KV_B, DH_B)

    xs2 = x_sample.reshape(bs * n_lat, d)
    in_tiles = n_lat // INPROJ_TM
    qa, ka, va, ga, qb, kb, vb, gb = _inproj(
        xs2, mod3, lambda i: 1 + i // in_tiles, gn, w_in_bf16, qn, kn, _rope_tables(n_lat))
    cka = cache_diff_k[:, l].reshape(bs * n_past * H_A, HEAD_W)
    cva = cache_diff_v[:, l].reshape(bs * n_past * H_A, HEAD_W)
    ckb = cache_gqa_k[:, l].reshape(bs * n_past * KV_B, HEAD_W)
    cvb = cache_gqa_v[:, l].reshape(bs * n_past * KV_B, HEAD_W)
    o = _attn_lat(lam_init, lam_params, subln, qa, ka, va, cka, cva, ga, qb, kb, vb, ckb, cvb, gb,
                  n_lat, n_past)
    out_tiles = n_lat // OUTPROJ_TM
    y_sample = _outproj(o, w_out_bf16, xs2, mod3, lambda i: 1 + i // out_tiles, gf)
    y_sample = y_sample.reshape(bs, n_lat, d)

    return (y_prompt, y_sample, new_diff_k, new_diff_v, new_gqa_k, new_gqa_v)
```

```python
import functools
import math

import jax
import jax.numpy as jnp
import numpy as np
from jax import lax
from jax.experimental import pallas as pl
from jax.experimental.pallas import tpu as pltpu

D_MODEL = 2048
GRID_W = 64
ROPE_THETA = 10000.0
EPS = 1e-6
H_A = 8
DK_A = 64
DV_A = 2 * DK_A
H_B = 8
KV_B = 2
DH_B = 128
G_B = H_B // KV_B
HEAD_W = 128
A_Q = H_A * 2 * DK_A
A_K = H_A * 2 * DK_A
A_V = H_A * DV_A
A_G = H_A * DV_A
B_Q = H_B * DH_B
B_K = KV_B * DH_B
B_V = KV_B * DH_B
B_G = H_B * DH_B
D_IN = A_Q + A_K + A_V + A_G + B_Q + B_K + B_V + B_G
D_MIX = A_V + B_Q
OFF_AQ = 0
OFF_AK = OFF_AQ + A_Q
OFF_AV = OFF_AK + A_K
OFF_AG = OFF_AV + A_V
OFF_BQ = OFF_AG + A_G
OFF_BK = OFF_BQ + B_Q
OFF_BV = OFF_BK + B_K
OFF_BG = OFF_BV + B_V

N_COND = 8
ADA_TK = 256
INPROJ_TM = 256
INPROJ_TM_ROPE = 512
W_CHUNK = 256
STAGE_BYTES = 2 * 1024 * 1024
STAGE_SLOTS = 4
OUTPROJ_TM = 512
ATTN_TQ = 256
GQA_STACK = 2
SCORE_LOOKAHEAD = 2
QSCALE_A = math.log2(math.e) / math.sqrt(DK_A)
QSCALE_B = math.log2(math.e) / math.sqrt(DH_B)
VMEM_LIMIT = 56 * 1024 * 1024

BF16 = jnp.bfloat16
F32 = jnp.float32


def _params(n_grid_axes):
    return pltpu.CompilerParams(
        dimension_semantics=("arbitrary",) * n_grid_axes,
        vmem_limit_bytes=VMEM_LIMIT,
    )


def _silu(x):
    return x * jax.nn.sigmoid(x)


def _rms(x):
    return x * lax.rsqrt(jnp.mean(x * x, axis=-1, keepdims=True) + EPS)


def _stage_rows(n_cols):
    return 1 << int(math.log2(STAGE_BYTES // (n_cols * 4)))


def _weight_scratch(n_cols):
    return [pltpu.VMEM((n_cols // W_CHUNK, D_MODEL, W_CHUNK), BF16),
            pltpu.VMEM((STAGE_SLOTS, _stage_rows(n_cols), n_cols), F32),
            pltpu.SemaphoreType.DMA((STAGE_SLOTS,)),
            pltpu.SemaphoreType.DMA(())]


def _slab_copy(w_hbm, w_stage, sem, k):
    slot = k % STAGE_SLOTS
    n_rows = w_stage.shape[1]
    return pltpu.make_async_copy(
        w_hbm.at[pl.ds(k * n_rows, n_rows), :], w_stage.at[slot], sem.at[slot])


def _stage_weights(w_hbm, wbf_hbm, w_scr, w_stage, sem, sem_out):
    n_chunks = w_scr.shape[0]
    slab_rows = w_stage.shape[1]
    n_slabs = w_scr.shape[1] // slab_rows
    copy = functools.partial(_slab_copy, w_hbm, w_stage, sem)
    handoff = pltpu.make_async_copy(w_scr, wbf_hbm, sem_out)

    @pl.when(pl.program_id(0) == 0)
    def _():
        for k in range(STAGE_SLOTS - 1):
            copy(k).start()
        for k in range(n_slabs):
            if k + STAGE_SLOTS - 1 < n_slabs:
                copy(k + STAGE_SLOTS - 1).start()
            copy(k).wait()
            rows = slice(k * slab_rows, (k + 1) * slab_rows)
            for c in range(n_chunks):
                w_scr[c, rows, :] = w_stage[
                    k % STAGE_SLOTS, :, c * W_CHUNK:(c + 1) * W_CHUNK].astype(BF16)
        handoff.start()

    return handoff


def _weight_spec(n_cols):
    return pl.BlockSpec((n_cols // W_CHUNK, D_MODEL, W_CHUNK), lambda i: (0, 0, 0),
                        pipeline_mode=pl.Buffered(1))


def _adaln_body(cond_ref, w_ref, b_ref, o_ref):
    @pl.when(pl.program_id(0) == 0)
    def _():
        o_ref[...] = jnp.broadcast_to(b_ref[...], o_ref.shape)

    a = _silu(cond_ref[...]).astype(BF16)
    w = w_ref[...].astype(BF16)
    o_ref[...] += jnp.dot(a, w, preferred_element_type=F32)


def _adaln(cond, w_ada, b_ada):
    d3 = w_ada.shape[1]
    return pl.pallas_call(
        _adaln_body,
        out_shape=jax.ShapeDtypeStruct((N_COND, d3), F32),
        grid=(D_MODEL // ADA_TK,),
        in_specs=[
            pl.BlockSpec((N_COND, ADA_TK), lambda k: (0, k)),
            pl.BlockSpec((ADA_TK, d3), lambda k: (k, 0)),
            pl.BlockSpec((1, d3), lambda k: (0, 0)),
        ],
        out_specs=pl.BlockSpec((N_COND, d3), lambda k: (0, 0)),
        compiler_params=_params(1),
        name="adaln",
    )(cond, w_ada, b_ada)


def _rope_a(x, c, s_up, s_dn):
    return x * c + pltpu.roll(x, 96, 1) * s_up + pltpu.roll(x, 32, 1) * s_dn


def _rope_b(x, c, s):
    return x * c + pltpu.roll(x, 64, 1) * s


def _inproj_body(rope, x_ref, mod_ref, gn_ref, w_in_ref, qn_ref, kn_ref, *rest):
    if rope:
        ca_ref, sau_ref, sad_ref, cb_ref, sb_ref = rest[:5]
        rest = rest[5:]
    qa_o, ka_o, va_o, ga_o, qb_o, kb_o, vb_o, gb_o = rest[:8]
    cache_layout = not rope
    if rope:
        weights = w_in_ref
    else:
        wbf_hbm, weights, w_stage, sem, sem_out = rest[8:]
        handoff = _stage_weights(w_in_ref, wbf_hbm, weights, w_stage, sem, sem_out)
    tm = x_ref.shape[0]

    x = x_ref[...]
    shift = mod_ref[:, 0:D_MODEL]
    scale = mod_ref[:, D_MODEL:2 * D_MODEL]
    h = ((_rms(x) * gn_ref[...]) * (1.0 + scale) + shift).astype(BF16)

    if rope:
        ca, sau, sad = ca_ref[...], sau_ref[...], sad_ref[...]
        cb, sb = cb_ref[...], sb_ref[...]

    def rope_a(t):
        return _rope_a(t, ca, sau, sad) if rope else t

    def rope_b(t):
        return _rope_b(t, cb, sb) if rope else t

    ident = lambda t: t
    regions = (
        (OFF_AQ, A_Q, qa_o, lambda t: rope_a(t) * QSCALE_A, False),
        (OFF_AK, A_K, ka_o, rope_a, True),
        (OFF_AG, A_G, ga_o, _silu, False),
        (OFF_BQ, B_Q, qb_o, lambda t: rope_b(_rms(t) * qn_ref[...]) * QSCALE_B, False),
        (OFF_BK, B_K, kb_o, lambda t: rope_b(_rms(t) * kn_ref[...]), True),
        (OFF_BG, B_G, gb_o, _silu, False),
        (OFF_AV, A_V, va_o, ident, True),
        (OFF_BV, B_V, vb_o, ident, True),
    )
    for start, width, o_ref, epi, is_kv in regions:
        n_heads = width // HEAD_W
        for c0 in range(0, width, W_CHUNK):
            z = jnp.dot(h, weights[(start + c0) // W_CHUNK], preferred_element_type=F32)
            for h0 in range(0, W_CHUNK, HEAD_W):
                hd = (c0 + h0) // HEAD_W
                t = epi(z[:, h0:h0 + HEAD_W]).astype(o_ref.dtype)
                if is_kv and cache_layout:
                    o_ref[pl.ds(hd, tm, stride=n_heads), :] = t
                else:
                    o_ref[:, hd * HEAD_W:(hd + 1) * HEAD_W] = t

    if not rope:
        pl.when(pl.program_id(0) == pl.num_programs(0) - 1)(handoff.wait)


def _inproj(x2d, mod3, row_of_tile, g_norm, w_in, q_norm_g, k_norm_g, rope_tabs):
    t = x2d.shape[0]
    rope = rope_tabs is not None
    tm = INPROJ_TM_ROPE if rope else INPROJ_TM
    row_spec = lambda w: pl.BlockSpec((tm, w), lambda i: (i, 0))
    const = lambda shape: pl.BlockSpec(shape, lambda i: (0,) * len(shape))
    in_specs = [
        row_spec(D_MODEL),
        pl.BlockSpec((None, 1, 3 * D_MODEL), lambda i: (row_of_tile(i), 0, 0)),
        const((1, D_MODEL)),
        _weight_spec(D_IN) if rope else pl.BlockSpec(memory_space=pl.ANY),
        const((1, HEAD_W)),
        const((1, HEAD_W)),
    ]
    args = [x2d, mod3, g_norm, w_in, q_norm_g, k_norm_g]
    if rope:
        n_pos_tiles = rope_tabs[0].shape[0] // tm
        in_specs += [pl.BlockSpec((tm, HEAD_W), lambda i: (i % n_pos_tiles, 0))] * 5
        args += list(rope_tabs)
    widths = (A_Q, A_K, A_V, A_G, B_Q, B_K, B_V, B_G)
    is_kv = (False, True, True, False, False, True, True, False)
    out_shape, out_specs = [], []
    for w, kv in zip(widths, is_kv):
        if kv and not rope:
            n_heads = w // HEAD_W
            out_shape.append(jax.ShapeDtypeStruct((t * n_heads, HEAD_W), F32))
            out_specs.append(pl.BlockSpec((tm * n_heads, HEAD_W), lambda i: (i, 0)))
        else:
            out_shape.append(jax.ShapeDtypeStruct((t, w), BF16))
            out_specs.append(row_spec(w))
    if not rope:
        out_shape.append(jax.ShapeDtypeStruct((D_IN // W_CHUNK, D_MODEL, W_CHUNK), BF16))
        out_specs.append(pl.BlockSpec(memory_space=pl.ANY))
    return pl.pallas_call(
        functools.partial(_inproj_body, rope),
        out_shape=out_shape,
        grid=(t // tm,),
        in_specs=in_specs,
        out_specs=out_specs,
        scratch_shapes=[] if rope else _weight_scratch(D_IN),
        compiler_params=_params(1),
        name="inproj_rope" if rope else "inproj",
    )(*args)


def _diff_lambda_col(lq1_ref, lk1_ref, lq2_ref, lk2_ref, lam_init):
    s1 = jnp.sum(lq1_ref[...] * lk1_ref[...], axis=-1, keepdims=True)
    s2 = jnp.sum(lq2_ref[...] * lk2_ref[...], axis=-1, keepdims=True)
    return jnp.exp(s1) - jnp.exp(s2) + lam_init


def _cache_head(ref, head, n_tok, n_heads):
    return ref[pl.ds(head, n_tok, stride=n_heads), :].astype(BF16)


def _scores(q, k):
    return lax.dot_general(q, k, (((1,), (1,)), ((), ())), preferred_element_type=F32)


def _softmax_pv(s, v):
    m = jnp.max(s, axis=-1, keepdims=True)
    e = jnp.exp2(s - m)
    v1 = jnp.concatenate([v, jnp.ones_like(v)], axis=1)
    ol = jnp.dot(e.astype(BF16), v1, preferred_element_type=F32)
    return ol[:, :HEAD_W], ol[:, HEAD_W:]


def _attend(lam_col, subln, qa_ref, ga_ref, qb_ref, gb_ref, ka, va, kb, vb, o_ref):
    tq = qa_ref.shape[0]
    lane = lax.broadcasted_iota(jnp.int32, (tq, HEAD_W), 1)
    first = lane < DK_A

    def diff_scores(hd):
        q = qa_ref[:, hd * HEAD_W:(hd + 1) * HEAD_W]
        zero = jnp.zeros_like(q)
        q2 = jnp.concatenate([jnp.where(first, q, zero), jnp.where(first, zero, q)], axis=0)
        return _scores(q2, ka(hd))

    def diff_finish(hd, s):
        cols = slice(hd * HEAD_W, (hd + 1) * HEAD_W)
        o2, l2 = _softmax_pv(s, va(hd))
        r2 = 1.0 / l2
        lam = lam_col[hd:hd + 1, :]
        o = o2[:tq] * r2[:tq] - o2[tq:] * (lam * r2[tq:])
        o_ref[:, cols] = (_rms(o) * subln * ga_ref[:, cols].astype(F32)).astype(o_ref.dtype)

    def gqa_scores(h0):
        q = jnp.concatenate(
            [qb_ref[:, (h0 + g) * HEAD_W:(h0 + g + 1) * HEAD_W] for g in range(GQA_STACK)], axis=0)
        return _scores(q, kb(h0 // G_B))

    def gqa_finish(h0, s):
        o, l = _softmax_pv(s, vb(h0 // G_B))
        o = o * (1.0 / l)
        for g in range(GQA_STACK):
            cols = slice((h0 + g) * HEAD_W, (h0 + g + 1) * HEAD_W)
            gate = gb_ref[:, cols].astype(F32)
            o_ref[:, A_V + (h0 + g) * HEAD_W:A_V + (h0 + g + 1) * HEAD_W] = (
                o[g * tq:(g + 1) * tq, :] * gate).astype(o_ref.dtype)

    units = [(diff_scores, diff_finish, hd) for hd in range(H_A)]
    units += [(gqa_scores, gqa_finish, h0) for h0 in range(0, H_B, GQA_STACK)]
    scores = []
    for u, (_, finish, arg) in enumerate(units):
        while len(scores) < min(len(units), u + 1 + SCORE_LOOKAHEAD):
            issue, _, issue_arg = units[len(scores)]
            scores.append(issue(issue_arg))
        finish(arg, scores[u])
        scores[u] = None


def _project_out(o, weights, x_ref, mod_ref, gf_ref, y_ref):
    m = jnp.concatenate(
        [jnp.dot(o, weights[c], preferred_element_type=F32) for c in range(D_MODEL // W_CHUNK)],
        axis=1)
    gate = mod_ref[:, 2 * D_MODEL:3 * D_MODEL]
    y_ref[...] = _rms(x_ref[...] + gate * m) * gf_ref[...]


def _attn_ctx_body(lam_init, lq1, lk1, lq2, lk2, subln_ref,
                   qa_ref, ka_ref, va_ref, ga_ref, qb_ref, kb_ref, vb_ref, gb_ref,
                   w_out_hbm, x_ref, mod_ref, gf_ref, y_ref, wbf_hbm,
                   w_scr, w_stage, sem, sem_out, o_scr):
    handoff = _stage_weights(w_out_hbm, wbf_hbm, w_scr, w_stage, sem, sem_out)
    lam_col = _diff_lambda_col(lq1, lk1, lq2, lk2, lam_init)
    seq = qa_ref.shape[0]
    head = lambda ref, n_heads: (lambda i: _cache_head(ref, i, seq, n_heads))
    _attend(lam_col, subln_ref[...] * (1.0 - lam_init), qa_ref, ga_ref, qb_ref, gb_ref,
            head(ka_ref, H_A), head(va_ref, H_A), head(kb_ref, KV_B), head(vb_ref, KV_B), o_scr)
    _project_out(o_scr[...], w_scr, x_ref, mod_ref, gf_ref, y_ref)
    pl.when(pl.program_id(0) == pl.num_programs(0) - 1)(handoff.wait)


def _attn_ctx(lam_init, lam_params, subln_g, qa, ka, va, ga, qb, kb, vb, gb, w_out, x2d, mod3,
              g_final, seq):
    t = qa.shape[0]
    const = lambda shape: pl.BlockSpec(shape, lambda b: (0,) * len(shape))
    row_spec = lambda w: pl.BlockSpec((seq, w), lambda b: (b, 0))
    cache_spec = lambda n_heads: pl.BlockSpec((seq * n_heads, HEAD_W), lambda b: (b, 0))
    return pl.pallas_call(
        functools.partial(_attn_ctx_body, lam_init),
        out_shape=[jax.ShapeDtypeStruct((t, D_MODEL), F32),
                   jax.ShapeDtypeStruct((D_MODEL // W_CHUNK, D_MIX, W_CHUNK), BF16)],
        grid=(t // seq,),
        in_specs=[const((H_A, DK_A))] * 4 + [const((1, DV_A))] + [
            row_spec(A_Q), cache_spec(H_A), cache_spec(H_A), row_spec(A_G),
            row_spec(B_Q), cache_spec(KV_B), cache_spec(KV_B), row_spec(B_G),
            pl.BlockSpec(memory_space=pl.ANY),
            row_spec(D_MODEL),
            pl.BlockSpec((None, 1, 3 * D_MODEL), lambda b: (0, 0, 0)),
            const((1, D_MODEL))],
        out_specs=[row_spec(D_MODEL), pl.BlockSpec(memory_space=pl.ANY)],
        scratch_shapes=_weight_scratch(D_MODEL) + [pltpu.VMEM((seq, D_MIX), BF16)],
        compiler_params=_params(1),
        name="attn_ctx",
    )(*lam_params, subln_g, qa, ka, va, ga, qb, kb, vb, gb, w_out, x2d, mod3, g_final)


def _attn_lat_body(lam_init, n_lat, lq1, lk1, lq2, lk2, subln_ref,
                   qa_ref, ka_ref, va_ref, cka_ref, cva_ref, ga_ref,
                   qb_ref, kb_ref, vb_ref, ckb_ref, cvb_ref, gb_ref, o_ref,
                   ka_all, va_all, kb_all, vb_all):
    @pl.when(pl.program_id(1) == 0)
    def _():
        for new_ref, cache_ref, all_ref in ((ka_ref, cka_ref, ka_all), (va_ref, cva_ref, va_all),
                                            (kb_ref, ckb_ref, kb_all), (vb_ref, cvb_ref, vb_all)):
            n_heads = new_ref.shape[1] // HEAD_W
            n_past = cache_ref.shape[0] // n_heads
            all_ref[0:n_lat, :] = new_ref[...]
            for hd in range(n_heads):
                all_ref[n_lat:, hd * HEAD_W:(hd + 1) * HEAD_W] = _cache_head(cache_ref, hd, n_past, n_heads)

    lam_col = _diff_lambda_col(lq1, lk1, lq2, lk2, lam_init)
    head = lambda ref: (lambda i: ref[:, i * HEAD_W:(i + 1) * HEAD_W])
    _attend(lam_col, subln_ref[...] * (1.0 - lam_init), qa_ref, ga_ref, qb_ref, gb_ref,
            head(ka_all), head(va_all), head(kb_all), head(vb_all), o_ref)


def _attn_lat(lam_init, lam_params, subln_g, qa, ka, va, cka, cva, ga, qb, kb, vb, ckb, cvb, gb,
              n_lat, n_past):
    t = qa.shape[0]
    tq = ATTN_TQ
    nq = n_lat // tq
    const = lambda shape: pl.BlockSpec(shape, lambda b, i: (0,) * len(shape))
    q_spec = lambda w: pl.BlockSpec((tq, w), lambda b, i: (b * nq + i, 0))
    new_spec = lambda w: pl.BlockSpec((n_lat, w), lambda b, i: (b, 0))
    cache_spec = lambda w: pl.BlockSpec((n_past * (w // HEAD_W), HEAD_W), lambda b, i: (b, 0))
    n_all = n_lat + n_past
    return pl.pallas_call(
        functools.partial(_attn_lat_body, lam_init, n_lat),
        out_shape=jax.ShapeDtypeStruct((t, D_MIX), BF16),
        grid=(t // n_lat, nq),
        in_specs=[const((H_A, DK_A))] * 4 + [const((1, DV_A))] + [
            q_spec(A_Q), new_spec(A_K), new_spec(A_V), cache_spec(A_K), cache_spec(A_V), q_spec(A_G),
            q_spec(B_Q), new_spec(B_K), new_spec(B_V), cache_spec(B_K), cache_spec(B_V), q_spec(B_G)],
        out_specs=q_spec(D_MIX),
        scratch_shapes=[pltpu.VMEM((n_all, A_K), BF16), pltpu.VMEM((n_all, A_V), BF16),
                        pltpu.VMEM((n_all, B_K), BF16), pltpu.VMEM((n_all, B_V), BF16)],
        compiler_params=_params(2),
        name="attn_lat",
    )(*lam_params, subln_g, qa, ka, va, cka, cva, ga, qb, kb, vb, ckb, cvb, gb)


def _outproj_body(o_ref, w_ref, x_ref, mod_ref, gf_ref, y_ref):
    _project_out(o_ref[...], w_ref, x_ref, mod_ref, gf_ref, y_ref)


def _outproj(o, w_out_bf16, x2d, mod3, row_of_tile, g_final):
    t = x2d.shape[0]
    tm = OUTPROJ_TM
    return pl.pallas_call(
        _outproj_body,
        out_shape=jax.ShapeDtypeStruct((t, D_MODEL), F32),
        grid=(t // tm,),
        in_specs=[
            pl.BlockSpec((tm, D_MIX), lambda i: (i, 0)),
            _weight_spec(D_MODEL),
            pl.BlockSpec((tm, D_MODEL), lambda i: (i, 0)),
            pl.BlockSpec((None, 1, 3 * D_MODEL), lambda i: (row_of_tile(i), 0, 0)),
            pl.BlockSpec((1, D_MODEL), lambda i: (0, 0)),
        ],
        out_specs=pl.BlockSpec((tm, D_MODEL), lambda i: (i, 0)),
        compiler_params=_params(1),
        name="outproj",
    )(o, w_out_bf16, x2d, mod3, g_final)


def _rope_tables(n_tok):
    n_rows = n_tok // GRID_W
    rows = np.repeat(np.arange(n_rows, dtype=np.float32), GRID_W)
    cols = np.tile(np.arange(GRID_W, dtype=np.float32), n_rows)

    def cos_sin(rot_dim):
        quarter = rot_dim // 4
        inv_freq = (1.0 / (np.float32(ROPE_THETA) ** (np.arange(quarter, dtype=np.float32) / quarter))
                    ).astype(np.float32)
        ang = np.concatenate([rows[:, None] * inv_freq, cols[:, None] * inv_freq], axis=-1)
        return np.cos(ang).astype(np.float32), np.sin(ang).astype(np.float32)

    ca, sa = cos_sin(DK_A)
    za = np.zeros_like(sa)
    cb, sb = cos_sin(DH_B)
    tabs = (
        np.concatenate([ca, ca, ca, ca], axis=-1),
        np.concatenate([-sa, za, -sa, za], axis=-1),
        np.concatenate([za, sa, za, sa], axis=-1),
        np.concatenate([cb, cb], axis=-1),
        np.concatenate([-sb, sb], axis=-1),
    )
    return tuple(jnp.asarray(t) for t in tabs)


def kernel(x_prompt, x_sample, cache_diff_k, cache_diff_v, cache_gqa_k, cache_gqa_v, c, c_ctx,
           w_ada, b_ada, g_norm, w_in, lam_q1, lam_k1, lam_q2, lam_k2, subln_g, q_norm_g, k_norm_g,
           w_out, g_final):
    bp, n_ctx, d = x_prompt.shape
    bs, n_lat, _ = x_sample.shape
    depth = w_in.shape[0]
    n_past = cache_diff_k.shape[2]
    assert depth == 1 and d == D_MODEL and bs + 1 <= N_COND
    assert n_lat % INPROJ_TM_ROPE == 0 and n_lat % OUTPROJ_TM == 0 and n_lat % ATTN_TQ == 0
    l = 0
    lam_init = 0.8 - 0.6 * math.exp(-0.3 * l)

    cond = jnp.concatenate(
        [c_ctx[None, :], c, jnp.zeros((N_COND - 1 - bs, d), F32)], axis=0)
    mod = _adaln(cond, w_ada[l], b_ada[l][None, :])
    mod3 = mod[:, None, :]

    gn = g_norm[l][None, :]
    qn = q_norm_g[l][None, :]
    kn = k_norm_g[l][None, :]
    lam_params = (lam_q1[l], lam_k1[l], lam_q2[l], lam_k2[l])
    subln = subln_g[l][None, :]
    gf = g_final[None, :]

    xp2 = x_prompt.reshape(bp * n_ctx, d)
    qa, ka, va, ga, qb, kb, vb, gb, w_in_bf16 = _inproj(
        xp2, mod3, lambda i: 0, gn, w_in[l], qn, kn, None)
    y_prompt, w_out_bf16 = _attn_ctx(lam_init, lam_params, subln, qa, ka, va, ga, qb, kb, vb, gb,
                                     w_out[l], xp2, mod3, gf, n_ctx)
    y_prompt = y_prompt.reshape(bp, n_ctx, d)
    new_diff_k = ka.reshape(bp, 1, n_ctx, H_A, 2 * DK_A)
    new_diff_v = va.reshape(bp, 1, n_ctx, H_A, DV_A)
    new_gqa_k = kb.reshape(bp, 1, n_ctx, KV_B, DH_B)
    new_gqa_v = vb.reshape(bp, 1, n_ctx, KV_B, DH_B)

    xs2 = x_sample.reshape(bs * n_lat, d)
    in_tiles = n_lat // INPROJ_TM_ROPE
    qa, ka, va, ga, qb, kb, vb, gb = _inproj(
        xs2, mod3, lambda i: 1 + i // in_tiles, gn, w_in_bf16, qn, kn, _rope_tables(n_lat))
    cka = cache_diff_k[:, l].reshape(bs * n_past * H_A, HEAD_W)
    cva = cache_diff_v[:, l].reshape(bs * n_past * H_A, HEAD_W)
    ckb = cache_gqa_k[:, l].reshape(bs * n_past * KV_B, HEAD_W)
    cvb = cache_gqa_v[:, l].reshape(bs * n_past * KV_B, HEAD_W)
    o = _attn_lat(lam_init, lam_params, subln, qa, ka, va, cka, cva, ga, qb, kb, vb, ckb, cvb, gb,
                  n_lat, n_past)
    out_tiles = n_lat // OUTPROJ_TM
    y_sample = _outproj(o, w_out_bf16, xs2, mod3, lambda i: 1 + i // out_tiles, gf)
    y_sample = y_sample.reshape(bs, n_lat, d)

    return (y_prompt, y_sample, new_diff_k, new_diff_v, new_gqa_k, new_gqa_v)
```

```python
import functools
import math

import jax
import jax.numpy as jnp
import numpy as np
from jax import lax
from jax.experimental import pallas as pl
from jax.experimental.pallas import tpu as pltpu

D_MODEL = 2048
GRID_W = 64
ROPE_THETA = 10000.0
EPS = 1e-6
H_A = 8
DK_A = 64
DV_A = 2 * DK_A
H_B = 8
KV_B = 2
DH_B = 128
G_B = H_B // KV_B
HEAD_W = 128
A_Q = H_A * 2 * DK_A
A_K = H_A * 2 * DK_A
A_V = H_A * DV_A
A_G = H_A * DV_A
B_Q = H_B * DH_B
B_K = KV_B * DH_B
B_V = KV_B * DH_B
B_G = H_B * DH_B
D_IN = A_Q + A_K + A_V + A_G + B_Q + B_K + B_V + B_G
D_MIX = A_V + B_Q
OFF_AQ = 0
OFF_AK = OFF_AQ + A_Q
OFF_AV = OFF_AK + A_K
OFF_AG = OFF_AV + A_V
OFF_BQ = OFF_AG + A_G
OFF_BK = OFF_BQ + B_Q
OFF_BV = OFF_BK + B_K
OFF_BG = OFF_BV + B_V

N_COND = 8
ADA_TK = 256
INPROJ_TM = 256
W_CHUNK = 256
STAGE_BYTES = 2 * 1024 * 1024
LOAD_GROUP = 2
STAGE_SLOTS = 4
OUTPROJ_TM = 512
ATTN_TQ = 256
GQA_STACK = 2
SCORE_LOOKAHEAD = 2
QSCALE_A = math.log2(math.e) / math.sqrt(DK_A)
QSCALE_B = math.log2(math.e) / math.sqrt(DH_B)
VMEM_LIMIT = 56 * 1024 * 1024

BF16 = jnp.bfloat16
F32 = jnp.float32


def _params(n_grid_axes):
    return pltpu.CompilerParams(
        dimension_semantics=("arbitrary",) * n_grid_axes,
        vmem_limit_bytes=VMEM_LIMIT,
    )


def _silu(x):
    return x * jax.nn.sigmoid(x)


def _rms(x):
    return x * lax.rsqrt(jnp.mean(x * x, axis=-1, keepdims=True) + EPS)


def _stage_rows(n_cols):
    return 1 << int(math.log2(STAGE_BYTES // (n_cols * 4)))


def _weight_scratch(n_cols):
    return [pltpu.VMEM((n_cols // W_CHUNK, D_MODEL, W_CHUNK), BF16),
            pltpu.VMEM((STAGE_SLOTS, _stage_rows(n_cols), n_cols), F32),
            pltpu.SemaphoreType.DMA((STAGE_SLOTS,)),
            pltpu.SemaphoreType.DMA(())]


def _slab_copy(w_hbm, w_stage, sem, k):
    slot = k % STAGE_SLOTS
    n_rows = w_stage.shape[1]
    return pltpu.make_async_copy(
        w_hbm.at[pl.ds(k * n_rows, n_rows), :], w_stage.at[slot], sem.at[slot])


def _stage_weights(w_hbm, wbf_hbm, w_scr, w_stage, sem, sem_out):
    n_chunks = w_scr.shape[0]
    slab_rows = w_stage.shape[1]
    n_slabs = w_scr.shape[1] // slab_rows
    copy = functools.partial(_slab_copy, w_hbm, w_stage, sem)
    handoff = pltpu.make_async_copy(w_scr, wbf_hbm, sem_out)

    @pl.when(pl.program_id(0) == 0)
    def _():
        for k in range(STAGE_SLOTS - 1):
            copy(k).start()
        for k in range(n_slabs):
            if k + STAGE_SLOTS - 1 < n_slabs:
                copy(k + STAGE_SLOTS - 1).start()
            copy(k).wait()
            rows = slice(k * slab_rows, (k + 1) * slab_rows)
            for c in range(n_chunks):
                w_scr[c, rows, :] = w_stage[
                    k % STAGE_SLOTS, :, c * W_CHUNK:(c + 1) * W_CHUNK].astype(BF16)
        handoff.start()

    return handoff


def _group_copy(w_hbm, w_scr, sem, g):
    rows = pl.ds(g * LOAD_GROUP, LOAD_GROUP)
    return pltpu.make_async_copy(w_hbm.at[rows], w_scr.at[rows], sem.at[g])


def _weight_spec(n_cols):
    return pl.BlockSpec((n_cols // W_CHUNK, D_MODEL, W_CHUNK), lambda i: (0, 0, 0),
                        pipeline_mode=pl.Buffered(1))


def _adaln_body(cond_ref, w_ref, b_ref, o_ref):
    @pl.when(pl.program_id(0) == 0)
    def _():
        o_ref[...] = jnp.broadcast_to(b_ref[...], o_ref.shape)

    a = _silu(cond_ref[...]).astype(BF16)
    w = w_ref[...].astype(BF16)
    o_ref[...] += jnp.dot(a, w, preferred_element_type=F32)


def _adaln(cond, w_ada, b_ada):
    d3 = w_ada.shape[1]
    return pl.pallas_call(
        _adaln_body,
        out_shape=jax.ShapeDtypeStruct((N_COND, d3), F32),
        grid=(D_MODEL // ADA_TK,),
        in_specs=[
            pl.BlockSpec((N_COND, ADA_TK), lambda k: (0, k)),
            pl.BlockSpec((ADA_TK, d3), lambda k: (k, 0)),
            pl.BlockSpec((1, d3), lambda k: (0, 0)),
        ],
        out_specs=pl.BlockSpec((N_COND, d3), lambda k: (0, 0)),
        compiler_params=_params(1),
        name="adaln",
    )(cond, w_ada, b_ada)


def _rope_a(x, c, s_up, s_dn):
    return x * c + pltpu.roll(x, 96, 1) * s_up + pltpu.roll(x, 32, 1) * s_dn


def _rope_b(x, c, s):
    return x * c + pltpu.roll(x, 64, 1) * s


def _inproj_body(rope, x_ref, mod_ref, gn_ref, w_in_ref, qn_ref, kn_ref, *rest):
    if rope:
        ca_ref, sau_ref, sad_ref, cb_ref, sb_ref = rest[:5]
        rest = rest[5:]
    qa_o, ka_o, va_o, ga_o, qb_o, kb_o, vb_o, gb_o = rest[:8]
    cache_layout = not rope
    if rope:
        weights, load_sem = rest[8:]
    else:
        wbf_hbm, weights, w_stage, sem, sem_out = rest[8:]
        handoff = _stage_weights(w_in_ref, wbf_hbm, weights, w_stage, sem, sem_out)
    tm = x_ref.shape[0]

    x = x_ref[...]
    shift = mod_ref[:, 0:D_MODEL]
    scale = mod_ref[:, D_MODEL:2 * D_MODEL]
    h = ((_rms(x) * gn_ref[...]) * (1.0 + scale) + shift).astype(BF16)

    if rope:
        ca, sau, sad = ca_ref[...], sau_ref[...], sad_ref[...]
        cb, sb = cb_ref[...], sb_ref[...]

    def rope_a(t):
        return _rope_a(t, ca, sau, sad) if rope else t

    def rope_b(t):
        return _rope_b(t, cb, sb) if rope else t

    ident = lambda t: t
    regions = (
        (OFF_AQ, A_Q, qa_o, lambda t: rope_a(t) * QSCALE_A, False),
        (OFF_AK, A_K, ka_o, rope_a, True),
        (OFF_AG, A_G, ga_o, _silu, False),
        (OFF_BQ, B_Q, qb_o, lambda t: rope_b(_rms(t) * qn_ref[...]) * QSCALE_B, False),
        (OFF_BK, B_K, kb_o, lambda t: rope_b(_rms(t) * kn_ref[...]), True),
        (OFF_BG, B_G, gb_o, _silu, False),
        (OFF_AV, A_V, va_o, ident, True),
        (OFF_BV, B_V, vb_o, ident, True),
    )
    chunks = [(start + c0) // W_CHUNK for start, width, *_ in regions
              for c0 in range(0, width, W_CHUNK)]

    def project(load):
        copy = functools.partial(_group_copy, w_in_ref, weights, load_sem) if load else None
        groups = list(dict.fromkeys(c // LOAD_GROUP for c in chunks))
        if load:
            for g in groups:
                copy(g).start()
        pending = set(groups) if load else set()
        for start, width, o_ref, epi, is_kv in regions:
            n_heads = width // HEAD_W
            for c0 in range(0, width, W_CHUNK):
                c = (start + c0) // W_CHUNK
                if c // LOAD_GROUP in pending:
                    copy(c // LOAD_GROUP).wait()
                    pending.remove(c // LOAD_GROUP)
                z = jnp.dot(h, weights[c], preferred_element_type=F32)
                for h0 in range(0, W_CHUNK, HEAD_W):
                    hd = (c0 + h0) // HEAD_W
                    t = epi(z[:, h0:h0 + HEAD_W]).astype(o_ref.dtype)
                    if is_kv and cache_layout:
                        o_ref[pl.ds(hd, tm, stride=n_heads), :] = t
                    else:
                        o_ref[:, hd * HEAD_W:(hd + 1) * HEAD_W] = t

    if rope:
        first_step = pl.program_id(0) == 0
        pl.when(first_step)(functools.partial(project, True))
        pl.when(jnp.logical_not(first_step))(functools.partial(project, False))
    else:
        project(False)
        pl.when(pl.program_id(0) == pl.num_programs(0) - 1)(handoff.wait)


def _inproj(x2d, mod3, row_of_tile, g_norm, w_in, q_norm_g, k_norm_g, rope_tabs):
    t = x2d.shape[0]
    tm = INPROJ_TM
    rope = rope_tabs is not None
    row_spec = lambda w: pl.BlockSpec((tm, w), lambda i: (i, 0))
    const = lambda shape: pl.BlockSpec(shape, lambda i: (0,) * len(shape))
    in_specs = [
        row_spec(D_MODEL),
        pl.BlockSpec((None, 1, 3 * D_MODEL), lambda i: (row_of_tile(i), 0, 0)),
        const((1, D_MODEL)),
        pl.BlockSpec(memory_space=pl.ANY),
        const((1, HEAD_W)),
        const((1, HEAD_W)),
    ]
    args = [x2d, mod3, g_norm, w_in, q_norm_g, k_norm_g]
    if rope:
        n_pos_tiles = rope_tabs[0].shape[0] // tm
        in_specs += [pl.BlockSpec((tm, HEAD_W), lambda i: (i % n_pos_tiles, 0))] * 5
        args += list(rope_tabs)
    widths = (A_Q, A_K, A_V, A_G, B_Q, B_K, B_V, B_G)
    is_kv = (False, True, True, False, False, True, True, False)
    out_shape, out_specs = [], []
    for w, kv in zip(widths, is_kv):
        if kv and not rope:
            n_heads = w // HEAD_W
            out_shape.append(jax.ShapeDtypeStruct((t * n_heads, HEAD_W), F32))
            out_specs.append(pl.BlockSpec((tm * n_heads, HEAD_W), lambda i: (i, 0)))
        else:
            out_shape.append(jax.ShapeDtypeStruct((t, w), BF16))
            out_specs.append(row_spec(w))
    if not rope:
        out_shape.append(jax.ShapeDtypeStruct((D_IN // W_CHUNK, D_MODEL, W_CHUNK), BF16))
        out_specs.append(pl.BlockSpec(memory_space=pl.ANY))
    return pl.pallas_call(
        functools.partial(_inproj_body, rope),
        out_shape=out_shape,
        grid=(t // tm,),
        in_specs=in_specs,
        out_specs=out_specs,
        scratch_shapes=_weight_scratch(D_IN) if not rope else [
            pltpu.VMEM((D_IN // W_CHUNK, D_MODEL, W_CHUNK), BF16),
            pltpu.SemaphoreType.DMA((D_IN // W_CHUNK // LOAD_GROUP,))],
        compiler_params=_params(1),
        name="inproj_rope" if rope else "inproj",
    )(*args)


def _diff_lambda_col(lq1_ref, lk1_ref, lq2_ref, lk2_ref, lam_init):
    s1 = jnp.sum(lq1_ref[...] * lk1_ref[...], axis=-1, keepdims=True)
    s2 = jnp.sum(lq2_ref[...] * lk2_ref[...], axis=-1, keepdims=True)
    return jnp.exp(s1) - jnp.exp(s2) + lam_init


def _cache_head(ref, head, n_tok, n_heads):
    return ref[pl.ds(head, n_tok, stride=n_heads), :].astype(BF16)


def _scores(q, k):
    return lax.dot_general(q, k, (((1,), (1,)), ((), ())), preferred_element_type=F32)


def _softmax_pv(s, v):
    m = jnp.max(s, axis=-1, keepdims=True)
    e = jnp.exp2(s - m)
    v1 = jnp.concatenate([v, jnp.ones_like(v)], axis=1)
    ol = jnp.dot(e.astype(BF16), v1, preferred_element_type=F32)
    return ol[:, :HEAD_W], ol[:, HEAD_W:]


def _attend(lam_col, subln, qa_ref, ga_ref, qb_ref, gb_ref, ka, va, kb, vb, o_ref):
    tq = qa_ref.shape[0]
    lane = lax.broadcasted_iota(jnp.int32, (tq, HEAD_W), 1)
    first = lane < DK_A

    def diff_scores(hd):
        q = qa_ref[:, hd * HEAD_W:(hd + 1) * HEAD_W]
        zero = jnp.zeros_like(q)
        q2 = jnp.concatenate([jnp.where(first, q, zero), jnp.where(first, zero, q)], axis=0)
        return _scores(q2, ka(hd))

    def diff_finish(hd, s):
        cols = slice(hd * HEAD_W, (hd + 1) * HEAD_W)
        o2, l2 = _softmax_pv(s, va(hd))
        r2 = 1.0 / l2
        lam = lam_col[hd:hd + 1, :]
        o = o2[:tq] * r2[:tq] - o2[tq:] * (lam * r2[tq:])
        o_ref[:, cols] = (_rms(o) * subln * ga_ref[:, cols].astype(F32)).astype(o_ref.dtype)

    def gqa_scores(h0):
        q = jnp.concatenate(
            [qb_ref[:, (h0 + g) * HEAD_W:(h0 + g + 1) * HEAD_W] for g in range(GQA_STACK)], axis=0)
        return _scores(q, kb(h0 // G_B))

    def gqa_finish(h0, s):
        o, l = _softmax_pv(s, vb(h0 // G_B))
        o = o * (1.0 / l)
        for g in range(GQA_STACK):
            cols = slice((h0 + g) * HEAD_W, (h0 + g + 1) * HEAD_W)
            gate = gb_ref[:, cols].astype(F32)
            o_ref[:, A_V + (h0 + g) * HEAD_W:A_V + (h0 + g + 1) * HEAD_W] = (
                o[g * tq:(g + 1) * tq, :] * gate).astype(o_ref.dtype)

    units = [(diff_scores, diff_finish, hd) for hd in range(H_A)]
    units += [(gqa_scores, gqa_finish, h0) for h0 in range(0, H_B, GQA_STACK)]
    scores = []
    for u, (_, finish, arg) in enumerate(units):
        while len(scores) < min(len(units), u + 1 + SCORE_LOOKAHEAD):
            issue, _, issue_arg = units[len(scores)]
            scores.append(issue(issue_arg))
        finish(arg, scores[u])
        scores[u] = None


def _project_out(o, weights, x_ref, mod_ref, gf_ref, y_ref):
    m = jnp.concatenate(
        [jnp.dot(o, weights[c], preferred_element_type=F32) for c in range(D_MODEL // W_CHUNK)],
        axis=1)
    gate = mod_ref[:, 2 * D_MODEL:3 * D_MODEL]
    y_ref[...] = _rms(x_ref[...] + gate * m) * gf_ref[...]


def _attn_ctx_body(lam_init, lq1, lk1, lq2, lk2, subln_ref,
                   qa_ref, ka_ref, va_ref, ga_ref, qb_ref, kb_ref, vb_ref, gb_ref,
                   w_out_hbm, x_ref, mod_ref, gf_ref, y_ref, wbf_hbm,
                   w_scr, w_stage, sem, sem_out, o_scr):
    handoff = _stage_weights(w_out_hbm, wbf_hbm, w_scr, w_stage, sem, sem_out)
    lam_col = _diff_lambda_col(lq1, lk1, lq2, lk2, lam_init)
    seq = qa_ref.shape[0]
    head = lambda ref, n_heads: (lambda i: _cache_head(ref, i, seq, n_heads))
    _attend(lam_col, subln_ref[...] * (1.0 - lam_init), qa_ref, ga_ref, qb_ref, gb_ref,
            head(ka_ref, H_A), head(va_ref, H_A), head(kb_ref, KV_B), head(vb_ref, KV_B), o_scr)
    _project_out(o_scr[...], w_scr, x_ref, mod_ref, gf_ref, y_ref)
    pl.when(pl.program_id(0) == pl.num_programs(0) - 1)(handoff.wait)


def _attn_ctx(lam_init, lam_params, subln_g, qa, ka, va, ga, qb, kb, vb, gb, w_out, x2d, mod3,
              g_final, seq):
    t = qa.shape[0]
    const = lambda shape: pl.BlockSpec(shape, lambda b: (0,) * len(shape))
    row_spec = lambda w: pl.BlockSpec((seq, w), lambda b: (b, 0))
    cache_spec = lambda n_heads: pl.BlockSpec((seq * n_heads, HEAD_W), lambda b: (b, 0))
    return pl.pallas_call(
        functools.partial(_attn_ctx_body, lam_init),
        out_shape=[jax.ShapeDtypeStruct((t, D_MODEL), F32),
                   jax.ShapeDtypeStruct((D_MODEL // W_CHUNK, D_MIX, W_CHUNK), BF16)],
        grid=(t // seq,),
        in_specs=[const((H_A, DK_A))] * 4 + [const((1, DV_A))] + [
            row_spec(A_Q), cache_spec(H_A), cache_spec(H_A), row_spec(A_G),
            row_spec(B_Q), cache_spec(KV_B), cache_spec(KV_B), row_spec(B_G),
            pl.BlockSpec(memory_space=pl.ANY),
            row_spec(D_MODEL),
            pl.BlockSpec((None, 1, 3 * D_MODEL), lambda b: (0, 0, 0)),
            const((1, D_MODEL))],
        out_specs=[row_spec(D_MODEL), pl.BlockSpec(memory_space=pl.ANY)],
        scratch_shapes=_weight_scratch(D_MODEL) + [pltpu.VMEM((seq, D_MIX), BF16)],
        compiler_params=_params(1),
        name="attn_ctx",
    )(*lam_params, subln_g, qa, ka, va, ga, qb, kb, vb, gb, w_out, x2d, mod3, g_final)


def _attn_lat_body(lam_init, n_lat, lq1, lk1, lq2, lk2, subln_ref,
                   qa_ref, ka_ref, va_ref, cka_ref, cva_ref, ga_ref,
                   qb_ref, kb_ref, vb_ref, ckb_ref, cvb_ref, gb_ref, o_ref,
                   ka_all, va_all, kb_all, vb_all):
    @pl.when(pl.program_id(1) == 0)
    def _():
        for new_ref, cache_ref, all_ref in ((ka_ref, cka_ref, ka_all), (va_ref, cva_ref, va_all),
                                            (kb_ref, ckb_ref, kb_all), (vb_ref, cvb_ref, vb_all)):
            n_heads = new_ref.shape[1] // HEAD_W
            n_past = cache_ref.shape[0] // n_heads
            all_ref[0:n_lat, :] = new_ref[...]
            for hd in range(n_heads):
                all_ref[n_lat:, hd * HEAD_W:(hd + 1) * HEAD_W] = _cache_head(cache_ref, hd, n_past, n_heads)

    lam_col = _diff_lambda_col(lq1, lk1, lq2, lk2, lam_init)
    head = lambda ref: (lambda i: ref[:, i * HEAD_W:(i + 1) * HEAD_W])
    _attend(lam_col, subln_ref[...] * (1.0 - lam_init), qa_ref, ga_ref, qb_ref, gb_ref,
            head(ka_all), head(va_all), head(kb_all), head(vb_all), o_ref)


def _attn_lat(lam_init, lam_params, subln_g, qa, ka, va, cka, cva, ga, qb, kb, vb, ckb, cvb, gb,
              n_lat, n_past):
    t = qa.shape[0]
    tq = ATTN_TQ
    nq = n_lat // tq
    const = lambda shape: pl.BlockSpec(shape, lambda b, i: (0,) * len(shape))
    q_spec = lambda w: pl.BlockSpec((tq, w), lambda b, i: (b * nq + i, 0))
    new_spec = lambda w: pl.BlockSpec((n_lat, w), lambda b, i: (b, 0))
    cache_spec = lambda w: pl.BlockSpec((n_past * (w // HEAD_W), HEAD_W), lambda b, i: (b, 0))
    n_all = n_lat + n_past
    return pl.pallas_call(
        functools.partial(_attn_lat_body, lam_init, n_lat),
        out_shape=jax.ShapeDtypeStruct((t, D_MIX), BF16),
        grid=(t // n_lat, nq),
        in_specs=[const((H_A, DK_A))] * 4 + [const((1, DV_A))] + [
            q_spec(A_Q), new_spec(A_K), new_spec(A_V), cache_spec(A_K), cache_spec(A_V), q_spec(A_G),
            q_spec(B_Q), new_spec(B_K), new_spec(B_V), cache_spec(B_K), cache_spec(B_V), q_spec(B_G)],
        out_specs=q_spec(D_MIX),
        scratch_shapes=[pltpu.VMEM((n_all, A_K), BF16), pltpu.VMEM((n_all, A_V), BF16),
                        pltpu.VMEM((n_all, B_K), BF16), pltpu.VMEM((n_all, B_V), BF16)],
        compiler_params=_params(2),
        name="attn_lat",
    )(*lam_params, subln_g, qa, ka, va, cka, cva, ga, qb, kb, vb, ckb, cvb, gb)


def _outproj_body(o_ref, w_ref, x_ref, mod_ref, gf_ref, y_ref):
    _project_out(o_ref[...], w_ref, x_ref, mod_ref, gf_ref, y_ref)


def _outproj(o, w_out_bf16, x2d, mod3, row_of_tile, g_final):
    t = x2d.shape[0]
    tm = OUTPROJ_TM
    return pl.pallas_call(
        _outproj_body,
        out_shape=jax.ShapeDtypeStruct((t, D_MODEL), F32),
        grid=(t // tm,),
        in_specs=[
            pl.BlockSpec((tm, D_MIX), lambda i: (i, 0)),
            _weight_spec(D_MODEL),
            pl.BlockSpec((tm, D_MODEL), lambda i: (i, 0)),
            pl.BlockSpec((None, 1, 3 * D_MODEL), lambda i: (row_of_tile(i), 0, 0)),
            pl.BlockSpec((1, D_MODEL), lambda i: (0, 0)),
        ],
        out_specs=pl.BlockSpec((tm, D_MODEL), lambda i: (i, 0)),
        compiler_params=_params(1),
        name="outproj",
    )(o, w_out_bf16, x2d, mod3, g_final)


def _rope_tables(n_tok):
    n_rows = n_tok // GRID_W
    rows = np.repeat(np.arange(n_rows, dtype=np.float32), GRID_W)
    cols = np.tile(np.arange(GRID_W, dtype=np.float32), n_rows)

    def cos_sin(rot_dim):
        quarter = rot_dim // 4
        inv_freq = (1.0 / (np.float32(ROPE_THETA) ** (np.arange(quarter, dtype=np.float32) / quarter))
                    ).astype(np.float32)
        ang = np.concatenate([rows[:, None] * inv_freq, cols[:, None] * inv_freq], axis=-1)
        return np.cos(ang).astype(np.float32), np.sin(ang).astype(np.float32)

    ca, sa = cos_sin(DK_A)
    za = np.zeros_like(sa)
    cb, sb = cos_sin(DH_B)
    tabs = (
        np.concatenate([ca, ca, ca, ca], axis=-1),
        np.concatenate([-sa, za, -sa, za], axis=-1),
        np.concatenate([za, sa, za, sa], axis=-1),
        np.concatenate([cb, cb], axis=-1),
        np.concatenate([-sb, sb], axis=-1),
    )
    return tuple(jnp.asarray(t) for t in tabs)


def kernel(x_prompt, x_sample, cache_diff_k, cache_diff_v, cache_gqa_k, cache_gqa_v, c, c_ctx,
           w_ada, b_ada, g_norm, w_in, lam_q1, lam_k1, lam_q2, lam_k2, subln_g, q_norm_g, k_norm_g,
           w_out, g_final):
    bp, n_ctx, d = x_prompt.shape
    bs, n_lat, _ = x_sample.shape
    depth = w_in.shape[0]
    n_past = cache_diff_k.shape[2]
    assert depth == 1 and d == D_MODEL and bs + 1 <= N_COND
    assert n_lat % INPROJ_TM == 0 and n_lat % OUTPROJ_TM == 0 and n_lat % ATTN_TQ == 0
    l = 0
    lam_init = 0.8 - 0.6 * math.exp(-0.3 * l)

    cond = jnp.concatenate(
        [c_ctx[None, :], c, jnp.zeros((N_COND - 1 - bs, d), F32)], axis=0)
    mod = _adaln(cond, w_ada[l], b_ada[l][None, :])
    mod3 = mod[:, None, :]

    gn = g_norm[l][None, :]
    qn = q_norm_g[l][None, :]
    kn = k_norm_g[l][None, :]
    lam_params = (lam_q1[l], lam_k1[l], lam_q2[l], lam_k2[l])
    subln = subln_g[l][None, :]
    gf = g_final[None, :]

    xp2 = x_prompt.reshape(bp * n_ctx, d)
    qa, ka, va, ga, qb, kb, vb, gb, w_in_bf16 = _inproj(
        xp2, mod3, lambda i: 0, gn, w_in[l], qn, kn, None)
    y_prompt, w_out_bf16 = _attn_ctx(lam_init, lam_params, subln, qa, ka, va, ga, qb, kb, vb, gb,
                                     w_out[l], xp2, mod3, gf, n_ctx)
    y_prompt = y_prompt.reshape(bp, n_ctx, d)
    new_diff_k = ka.reshape(bp, 1, n_ctx, H_A, 2 * DK_A)
    new_diff_v = va.reshape(bp, 1, n_ctx, H_A, DV_A)
    new_gqa_k = kb.reshape(bp, 1, n_ctx, KV_B, DH_B)
    new_gqa_v = vb.reshape(bp, 1, n_ctx, KV_B, DH_B)

    xs2 = x_sample.reshape(bs * n_lat, d)
    in_tiles = n_lat // INPROJ_TM
    qa, ka, va, ga, qb, kb, vb, gb = _inproj(
        xs2, mod3, lambda i: 1 + i // in_tiles, gn, w_in_bf16, qn, kn, _rope_tables(n_lat))
    cka = cache_diff_k[:, l].reshape(bs * n_past * H_A, HEAD_W)
    cva = cache_diff_v[:, l].reshape(bs * n_past * H_A, HEAD_W)
    ckb = cache_gqa_k[:, l].reshape(bs * n_past * KV_B, HEAD_W)
    cvb = cache_gqa_v[:, l].reshape(bs * n_past * KV_B, HEAD_W)
    o = _attn_lat(lam_init, lam_params, subln, qa, ka, va, cka, cva, ga, qb, kb, vb, ckb, cvb, gb,
                  n_lat, n_past)
    out_tiles = n_lat // OUTPROJ_TM
    y_sample = _outproj(o, w_out_bf16, xs2, mod3, lambda i: 1 + i // out_tiles, gf)
    y_sample = y_sample.reshape(bs, n_lat, d)

    return (y_prompt, y_sample, new_diff_k, new_diff_v, new_gqa_k, new_gqa_v)
```

```python
import functools
import math

import jax
import jax.numpy as jnp
import numpy as np
from jax import lax
from jax.experimental import pallas as pl
from jax.experimental.pallas import tpu as pltpu

D_MODEL = 2048
GRID_W = 64
ROPE_THETA = 10000.0
EPS = 1e-6
H_A = 8
DK_A = 64
DV_A = 2 * DK_A
H_B = 8
KV_B = 2
DH_B = 128
G_B = H_B // KV_B
HEAD_W = 128
A_Q = H_A * 2 * DK_A
A_K = H_A * 2 * DK_A
A_V = H_A * DV_A
A_G = H_A * DV_A
B_Q = H_B * DH_B
B_K = KV_B * DH_B
B_V = KV_B * DH_B
B_G = H_B * DH_B
D_IN = A_Q + A_K + A_V + A_G + B_Q + B_K + B_V + B_G
D_MIX = A_V + B_Q
OFF_AQ = 0
OFF_AK = OFF_AQ + A_Q
OFF_AV = OFF_AK + A_K
OFF_AG = OFF_AV + A_V
OFF_BQ = OFF_AG + A_G
OFF_BK = OFF_BQ + B_Q
OFF_BV = OFF_BK + B_K
OFF_BG = OFF_BV + B_V

N_COND = 8
ADA_TK = 256
INPROJ_TM = 256
W_CHUNK = 256
STAGE_BYTES = 2 * 1024 * 1024
STAGE_SLOTS = 4
OUTPROJ_TM = 512
ATTN_TQ = 256
GQA_STACK = 2
SCORE_LOOKAHEAD = 2
QSCALE_A = math.log2(math.e) / math.sqrt(DK_A)
QSCALE_B = math.log2(math.e) / math.sqrt(DH_B)
VMEM_LIMIT = 56 * 1024 * 1024

BF16 = jnp.bfloat16
F32 = jnp.float32


def _params(n_grid_axes):
    return pltpu.CompilerParams(
        dimension_semantics=("arbitrary",) * n_grid_axes,
        vmem_limit_bytes=VMEM_LIMIT,
    )


def _silu(x):
    return x * jax.nn.sigmoid(x)


def _rms(x):
    return x * lax.rsqrt(jnp.mean(x * x, axis=-1, keepdims=True) + EPS)


def _stage_rows(n_cols):
    return 1 << int(math.log2(STAGE_BYTES // (n_cols * 4)))


def _weight_scratch(n_cols):
    return [pltpu.VMEM((n_cols // W_CHUNK, D_MODEL, W_CHUNK), BF16),
            pltpu.VMEM((STAGE_SLOTS, _stage_rows(n_cols), n_cols), F32),
            pltpu.SemaphoreType.DMA((STAGE_SLOTS,)),
            pltpu.SemaphoreType.DMA(())]


def _slab_copy(w_hbm, w_stage, sem, k):
    slot = k % STAGE_SLOTS
    n_rows = w_stage.shape[1]
    return pltpu.make_async_copy(
        w_hbm.at[pl.ds(k * n_rows, n_rows), :], w_stage.at[slot], sem.at[slot])


def _stage_weights(w_hbm, wbf_hbm, w_scr, w_stage, sem, sem_out):
    n_chunks = w_scr.shape[0]
    slab_rows = w_stage.shape[1]
    n_slabs = w_scr.shape[1] // slab_rows
    copy = functools.partial(_slab_copy, w_hbm, w_stage, sem)
    handoff = pltpu.make_async_copy(w_scr, wbf_hbm, sem_out)

    @pl.when(pl.program_id(0) == 0)
    def _():
        for k in range(STAGE_SLOTS - 1):
            copy(k).start()
        for k in range(n_slabs):
            if k + STAGE_SLOTS - 1 < n_slabs:
                copy(k + STAGE_SLOTS - 1).start()
            copy(k).wait()
            rows = slice(k * slab_rows, (k + 1) * slab_rows)
            for c in range(n_chunks):
                w_scr[c, rows, :] = w_stage[
                    k % STAGE_SLOTS, :, c * W_CHUNK:(c + 1) * W_CHUNK].astype(BF16)
        handoff.start()

    return handoff


def _weight_spec(n_cols):
    return pl.BlockSpec((n_cols // W_CHUNK, D_MODEL, W_CHUNK), lambda i: (0, 0, 0),
                        pipeline_mode=pl.Buffered(1))


def _adaln_body(cctx_ref, c_ref, w_ref, b_ref, o_ref):
    @pl.when(pl.program_id(0) == 0)
    def _():
        o_ref[:, 0, :] = jnp.broadcast_to(b_ref[...], (N_COND, b_ref.shape[1]))

    pad = jnp.zeros((N_COND - 1 - c_ref.shape[0], c_ref.shape[1]), F32)
    cond = jnp.concatenate([cctx_ref[...], c_ref[...], pad], axis=0)
    a = _silu(cond).astype(BF16)
    w = w_ref[...].astype(BF16)
    o_ref[:, 0, :] += jnp.dot(a, w, preferred_element_type=F32)


def _adaln(c_ctx, c, w_ada, b_ada):
    d3 = w_ada.shape[1]
    return pl.pallas_call(
        _adaln_body,
        out_shape=jax.ShapeDtypeStruct((N_COND, 1, d3), F32),
        grid=(D_MODEL // ADA_TK,),
        in_specs=[
            pl.BlockSpec((1, ADA_TK), lambda k: (0, k)),
            pl.BlockSpec((c.shape[0], ADA_TK), lambda k: (0, k)),
            pl.BlockSpec((ADA_TK, d3), lambda k: (k, 0)),
            pl.BlockSpec((1, d3), lambda k: (0, 0)),
        ],
        out_specs=pl.BlockSpec((N_COND, 1, d3), lambda k: (0, 0, 0)),
        compiler_params=_params(1),
        name="adaln",
    )(c_ctx, c, w_ada, b_ada)


def _rope_a(x, c, s_up, s_dn):
    return x * c + pltpu.roll(x, 96, 1) * s_up + pltpu.roll(x, 32, 1) * s_dn


def _rope_b(x, c, s):
    return x * c + pltpu.roll(x, 64, 1) * s


def _inproj_body(rope, x_ref, mod_ref, gn_ref, w_in_ref, qn_ref, kn_ref, *rest):
    if rope:
        ca_ref, sau_ref, sad_ref, cb_ref, sb_ref = rest[:5]
        rest = rest[5:]
    qa_o, ka_o, va_o, ga_o, qb_o, kb_o, vb_o, gb_o = rest[:8]
    cache_layout = not rope
    if rope:
        weights = w_in_ref
    else:
        wbf_hbm, weights, w_stage, sem, sem_out = rest[8:]
        handoff = _stage_weights(w_in_ref, wbf_hbm, weights, w_stage, sem, sem_out)
    tm = x_ref.shape[0]

    x = x_ref[...]
    shift = mod_ref[:, 0:D_MODEL]
    scale = mod_ref[:, D_MODEL:2 * D_MODEL]
    h = ((_rms(x) * gn_ref[...]) * (1.0 + scale) + shift).astype(BF16)

    if rope:
        ca, sau, sad = ca_ref[...], sau_ref[...], sad_ref[...]
        cb, sb = cb_ref[...], sb_ref[...]

    def rope_a(t):
        return _rope_a(t, ca, sau, sad) if rope else t

    def rope_b(t):
        return _rope_b(t, cb, sb) if rope else t

    ident = lambda t: t
    regions = (
        (OFF_AQ, A_Q, qa_o, lambda t: rope_a(t) * QSCALE_A, False),
        (OFF_AK, A_K, ka_o, rope_a, True),
        (OFF_AG, A_G, ga_o, _silu, False),
        (OFF_BQ, B_Q, qb_o, lambda t: rope_b(_rms(t) * qn_ref[...]) * QSCALE_B, False),
        (OFF_BK, B_K, kb_o, lambda t: rope_b(_rms(t) * kn_ref[...]), True),
        (OFF_BG, B_G, gb_o, _silu, False),
        (OFF_AV, A_V, va_o, ident, True),
        (OFF_BV, B_V, vb_o, ident, True),
    )
    for start, width, o_ref, epi, is_kv in regions:
        n_heads = width // HEAD_W
        for c0 in range(0, width, W_CHUNK):
            z = jnp.dot(h, weights[(start + c0) // W_CHUNK], preferred_element_type=F32)
            for h0 in range(0, W_CHUNK, HEAD_W):
                hd = (c0 + h0) // HEAD_W
                t = epi(z[:, h0:h0 + HEAD_W]).astype(o_ref.dtype)
                if is_kv and cache_layout:
                    o_ref[pl.ds(hd, tm, stride=n_heads), :] = t
                else:
                    o_ref[:, hd * HEAD_W:(hd + 1) * HEAD_W] = t

    if not rope:
        pl.when(pl.program_id(0) == pl.num_programs(0) - 1)(handoff.wait)


def _inproj(x2d, mod3, row_of_tile, g_norm, w_in, q_norm_g, k_norm_g, rope_tabs):
    t = x2d.shape[0]
    tm = INPROJ_TM
    rope = rope_tabs is not None
    row_spec = lambda w: pl.BlockSpec((tm, w), lambda i: (i, 0))
    const = lambda shape: pl.BlockSpec(shape, lambda i: (0,) * len(shape))
    in_specs = [
        row_spec(D_MODEL),
        pl.BlockSpec((None, 1, 3 * D_MODEL), lambda i: (row_of_tile(i), 0, 0)),
        const((1, D_MODEL)),
        _weight_spec(D_IN) if rope else pl.BlockSpec(memory_space=pl.ANY),
        const((1, HEAD_W)),
        const((1, HEAD_W)),
    ]
    args = [x2d, mod3, g_norm, w_in, q_norm_g, k_norm_g]
    if rope:
        n_pos_tiles = rope_tabs[0].shape[0] // tm
        in_specs += [pl.BlockSpec((tm, HEAD_W), lambda i: (i % n_pos_tiles, 0))] * 5
        args += list(rope_tabs)
    widths = (A_Q, A_K, A_V, A_G, B_Q, B_K, B_V, B_G)
    is_kv = (False, True, True, False, False, True, True, False)
    out_shape, out_specs = [], []
    for w, kv in zip(widths, is_kv):
        if kv and not rope:
            n_heads = w // HEAD_W
            out_shape.append(jax.ShapeDtypeStruct((t * n_heads, HEAD_W), F32))
            out_specs.append(pl.BlockSpec((tm * n_heads, HEAD_W), lambda i: (i, 0)))
        else:
            out_shape.append(jax.ShapeDtypeStruct((t, w), BF16))
            out_specs.append(row_spec(w))
    if not rope:
        out_shape.append(jax.ShapeDtypeStruct((D_IN // W_CHUNK, D_MODEL, W_CHUNK), BF16))
        out_specs.append(pl.BlockSpec(memory_space=pl.ANY))
    return pl.pallas_call(
        functools.partial(_inproj_body, rope),
        out_shape=out_shape,
        grid=(t // tm,),
        in_specs=in_specs,
        out_specs=out_specs,
        scratch_shapes=[] if rope else _weight_scratch(D_IN),
        compiler_params=_params(1),
        name="inproj_rope" if rope else "inproj",
    )(*args)


def _diff_lambda_col(lq1_ref, lk1_ref, lq2_ref, lk2_ref, lam_init):
    s1 = jnp.sum(lq1_ref[...] * lk1_ref[...], axis=-1, keepdims=True)
    s2 = jnp.sum(lq2_ref[...] * lk2_ref[...], axis=-1, keepdims=True)
    return jnp.exp(s1) - jnp.exp(s2) + lam_init


def _cache_head(ref, head, n_tok, n_heads):
    return ref[pl.ds(head, n_tok, stride=n_heads), :].astype(BF16)


def _scores(q, k):
    return lax.dot_general(q, k, (((1,), (1,)), ((), ())), preferred_element_type=F32)


def _softmax_pv(s, v):
    m = jnp.max(s, axis=-1, keepdims=True)
    e = jnp.exp2(s - m)
    v1 = jnp.concatenate([v, jnp.ones_like(v)], axis=1)
    ol = jnp.dot(e.astype(BF16), v1, preferred_element_type=F32)
    return ol[:, :HEAD_W], ol[:, HEAD_W:]


def _attend(lam_col, subln, qa_ref, ga_ref, qb_ref, gb_ref, ka, va, kb, vb, o_ref):
    tq = qa_ref.shape[0]
    lane = lax.broadcasted_iota(jnp.int32, (tq, HEAD_W), 1)
    first = lane < DK_A

    def diff_scores(hd):
        q = qa_ref[:, hd * HEAD_W:(hd + 1) * HEAD_W]
        zero = jnp.zeros_like(q)
        q2 = jnp.concatenate([jnp.where(first, q, zero), jnp.where(first, zero, q)], axis=0)
        return _scores(q2, ka(hd))

    def diff_finish(hd, s):
        cols = slice(hd * HEAD_W, (hd + 1) * HEAD_W)
        o2, l2 = _softmax_pv(s, va(hd))
        r2 = 1.0 / l2
        lam = lam_col[hd:hd + 1, :]
        o = o2[:tq] * r2[:tq] - o2[tq:] * (lam * r2[tq:])
        o_ref[:, cols] = (_rms(o) * subln * ga_ref[:, cols].astype(F32)).astype(o_ref.dtype)

    def gqa_scores(h0):
        q = jnp.concatenate(
            [qb_ref[:, (h0 + g) * HEAD_W:(h0 + g + 1) * HEAD_W] for g in range(GQA_STACK)], axis=0)
        return _scores(q, kb(h0 // G_B))

    def gqa_finish(h0, s):
        o, l = _softmax_pv(s, vb(h0 // G_B))
        o = o * (1.0 / l)
        for g in range(GQA_STACK):
            cols = slice((h0 + g) * HEAD_W, (h0 + g + 1) * HEAD_W)
            gate = gb_ref[:, cols].astype(F32)
            o_ref[:, A_V + (h0 + g) * HEAD_W:A_V + (h0 + g + 1) * HEAD_W] = (
                o[g * tq:(g + 1) * tq, :] * gate).astype(o_ref.dtype)

    units = [(diff_scores, diff_finish, hd) for hd in range(H_A)]
    units += [(gqa_scores, gqa_finish, h0) for h0 in range(0, H_B, GQA_STACK)]
    scores = []
    for u, (_, finish, arg) in enumerate(units):
        while len(scores) < min(len(units), u + 1 + SCORE_LOOKAHEAD):
            issue, _, issue_arg = units[len(scores)]
            scores.append(issue(issue_arg))
        finish(arg, scores[u])
        scores[u] = None


def _project_out(o, weights, x_ref, mod_ref, gf_ref, y_ref):
    m = jnp.concatenate(
        [jnp.dot(o, weights[c], preferred_element_type=F32) for c in range(D_MODEL // W_CHUNK)],
        axis=1)
    gate = mod_ref[:, 2 * D_MODEL:3 * D_MODEL]
    y_ref[...] = _rms(x_ref[...] + gate * m) * gf_ref[...]


def _attn_ctx_body(lam_init, lq1, lk1, lq2, lk2, subln_ref,
                   qa_ref, ka_ref, va_ref, ga_ref, qb_ref, kb_ref, vb_ref, gb_ref,
                   w_out_hbm, x_ref, mod_ref, gf_ref, y_ref, wbf_hbm,
                   w_scr, w_stage, sem, sem_out, o_scr):
    handoff = _stage_weights(w_out_hbm, wbf_hbm, w_scr, w_stage, sem, sem_out)
    lam_col = _diff_lambda_col(lq1, lk1, lq2, lk2, lam_init)
    seq = qa_ref.shape[0]
    head = lambda ref, n_heads: (lambda i: _cache_head(ref, i, seq, n_heads))
    _attend(lam_col, subln_ref[...] * (1.0 - lam_init), qa_ref, ga_ref, qb_ref, gb_ref,
            head(ka_ref, H_A), head(va_ref, H_A), head(kb_ref, KV_B), head(vb_ref, KV_B), o_scr)
    _project_out(o_scr[...], w_scr, x_ref, mod_ref, gf_ref, y_ref)
    pl.when(pl.program_id(0) == pl.num_programs(0) - 1)(handoff.wait)


def _attn_ctx(lam_init, lam_params, subln_g, qa, ka, va, ga, qb, kb, vb, gb, w_out, x2d, mod3,
              g_final, seq):
    t = qa.shape[0]
    const = lambda shape: pl.BlockSpec(shape, lambda b: (0,) * len(shape))
    row_spec = lambda w: pl.BlockSpec((seq, w), lambda b: (b, 0))
    cache_spec = lambda n_heads: pl.BlockSpec((seq * n_heads, HEAD_W), lambda b: (b, 0))
    return pl.pallas_call(
        functools.partial(_attn_ctx_body, lam_init),
        out_shape=[jax.ShapeDtypeStruct((t, D_MODEL), F32),
                   jax.ShapeDtypeStruct((D_MODEL // W_CHUNK, D_MIX, W_CHUNK), BF16)],
        grid=(t // seq,),
        in_specs=[const((H_A, DK_A))] * 4 + [const((1, DV_A))] + [
            row_spec(A_Q), cache_spec(H_A), cache_spec(H_A), row_spec(A_G),
            row_spec(B_Q), cache_spec(KV_B), cache_spec(KV_B), row_spec(B_G),
            pl.BlockSpec(memory_space=pl.ANY),
            row_spec(D_MODEL),
            pl.BlockSpec((None, 1, 3 * D_MODEL), lambda b: (0, 0, 0)),
            const((1, D_MODEL))],
        out_specs=[row_spec(D_MODEL), pl.BlockSpec(memory_space=pl.ANY)],
        scratch_shapes=_weight_scratch(D_MODEL) + [pltpu.VMEM((seq, D_MIX), BF16)],
        compiler_params=_params(1),
        name="attn_ctx",
    )(*lam_params, subln_g, qa, ka, va, ga, qb, kb, vb, gb, w_out, x2d, mod3, g_final)


def _attn_lat_body(lam_init, n_lat, lq1, lk1, lq2, lk2, subln_ref,
                   qa_ref, ka_ref, va_ref, cka_ref, cva_ref, ga_ref,
                   qb_ref, kb_ref, vb_ref, ckb_ref, cvb_ref, gb_ref, o_ref,
                   ka_all, va_all, kb_all, vb_all):
    @pl.when(pl.program_id(1) == 0)
    def _():
        for new_ref, cache_ref, all_ref in ((ka_ref, cka_ref, ka_all), (va_ref, cva_ref, va_all),
                                            (kb_ref, ckb_ref, kb_all), (vb_ref, cvb_ref, vb_all)):
            n_heads = new_ref.shape[1] // HEAD_W
            n_past = cache_ref.shape[0] // n_heads
            all_ref[0:n_lat, :] = new_ref[...]
            for hd in range(n_heads):
                all_ref[n_lat:, hd * HEAD_W:(hd + 1) * HEAD_W] = _cache_head(cache_ref, hd, n_past, n_heads)

    lam_col = _diff_lambda_col(lq1, lk1, lq2, lk2, lam_init)
    head = lambda ref: (lambda i: ref[:, i * HEAD_W:(i + 1) * HEAD_W])
    _attend(lam_col, subln_ref[...] * (1.0 - lam_init), qa_ref, ga_ref, qb_ref, gb_ref,
            head(ka_all), head(va_all), head(kb_all), head(vb_all), o_ref)


def _attn_lat(lam_init, lam_params, subln_g, qa, ka, va, cka, cva, ga, qb, kb, vb, ckb, cvb, gb,
              n_lat, n_past):
    t = qa.shape[0]
    tq = ATTN_TQ
    nq = n_lat // tq
    const = lambda shape: pl.BlockSpec(shape, lambda b, i: (0,) * len(shape))
    q_spec = lambda w: pl.BlockSpec((tq, w), lambda b, i: (b * nq + i, 0))
    new_spec = lambda w: pl.BlockSpec((n_lat, w), lambda b, i: (b, 0))
    cache_spec = lambda w: pl.BlockSpec((n_past * (w // HEAD_W), HEAD_W), lambda b, i: (b, 0))
    n_all = n_lat + n_past
    return pl.pallas_call(
        functools.partial(_attn_lat_body, lam_init, n_lat),
        out_shape=jax.ShapeDtypeStruct((t, D_MIX), BF16),
        grid=(t // n_lat, nq),
        in_specs=[const((H_A, DK_A))] * 4 + [const((1, DV_A))] + [
            q_spec(A_Q), new_spec(A_K), new_spec(A_V), cache_spec(A_K), cache_spec(A_V), q_spec(A_G),
            q_spec(B_Q), new_spec(B_K), new_spec(B_V), cache_spec(B_K), cache_spec(B_V), q_spec(B_G)],
        out_specs=q_spec(D_MIX),
        scratch_shapes=[pltpu.VMEM((n_all, A_K), BF16), pltpu.VMEM((n_all, A_V), BF16),
                        pltpu.VMEM((n_all, B_K), BF16), pltpu.VMEM((n_all, B_V), BF16)],
        compiler_params=_params(2),
        name="attn_lat",
    )(*lam_params, subln_g, qa, ka, va, cka, cva, ga, qb, kb, vb, ckb, cvb, gb)


def _outproj_body(o_ref, w_ref, x_ref, mod_ref, gf_ref, y_ref):
    _project_out(o_ref[...], w_ref, x_ref, mod_ref, gf_ref, y_ref)


def _outproj(o, w_out_bf16, x2d, mod3, row_of_tile, g_final):
    t = x2d.shape[0]
    tm = OUTPROJ_TM
    return pl.pallas_call(
        _outproj_body,
        out_shape=jax.ShapeDtypeStruct((t, D_MODEL), F32),
        grid=(t // tm,),
        in_specs=[
            pl.BlockSpec((tm, D_MIX), lambda i: (i, 0)),
            _weight_spec(D_MODEL),
            pl.BlockSpec((tm, D_MODEL), lambda i: (i, 0)),
            pl.BlockSpec((None, 1, 3 * D_MODEL), lambda i: (row_of_tile(i), 0, 0)),
            pl.BlockSpec((1, D_MODEL), lambda i: (0, 0)),
        ],
        out_specs=pl.BlockSpec((tm, D_MODEL), lambda i: (i, 0)),
        compiler_params=_params(1),
        name="outproj",
    )(o, w_out_bf16, x2d, mod3, g_final)


def _rope_tables(n_tok):
    n_rows = n_tok // GRID_W
    rows = np.repeat(np.arange(n_rows, dtype=np.float32), GRID_W)
    cols = np.tile(np.arange(GRID_W, dtype=np.float32), n_rows)

    def cos_sin(rot_dim):
        quarter = rot_dim // 4
        inv_freq = (1.0 / (np.float32(ROPE_THETA) ** (np.arange(quarter, dtype=np.float32) / quarter))
                    ).astype(np.float32)
        ang = np.concatenate([rows[:, None] * inv_freq, cols[:, None] * inv_freq], axis=-1)
        return np.cos(ang).astype(np.float32), np.sin(ang).astype(np.float32)

    ca, sa = cos_sin(DK_A)
    za = np.zeros_like(sa)
    cb, sb = cos_sin(DH_B)
    tabs = (
        np.concatenate([ca, ca, ca, ca], axis=-1),
        np.concatenate([-sa, za, -sa, za], axis=-1),
        np.concatenate([za, sa, za, sa], axis=-1),
        np.concatenate([cb, cb], axis=-1),
        np.concatenate([-sb, sb], axis=-1),
    )
    return tuple(jnp.asarray(t) for t in tabs)


def kernel(x_prompt, x_sample, cache_diff_k, cache_diff_v, cache_gqa_k, cache_gqa_v, c, c_ctx,
           w_ada, b_ada, g_norm, w_in, lam_q1, lam_k1, lam_q2, lam_k2, subln_g, q_norm_g, k_norm_g,
           w_out, g_final):
    bp, n_ctx, d = x_prompt.shape
    bs, n_lat, _ = x_sample.shape
    depth = w_in.shape[0]
    n_past = cache_diff_k.shape[2]
    assert depth == 1 and d == D_MODEL and bs + 1 <= N_COND
    assert n_lat % INPROJ_TM == 0 and n_lat % OUTPROJ_TM == 0 and n_lat % ATTN_TQ == 0
    l = 0
    lam_init = 0.8 - 0.6 * math.exp(-0.3 * l)

    mod3 = _adaln(c_ctx[None, :], c, w_ada[l], b_ada[l][None, :])

    gn = g_norm[l][None, :]
    qn = q_norm_g[l][None, :]
    kn = k_norm_g[l][None, :]
    lam_params = (lam_q1[l], lam_k1[l], lam_q2[l], lam_k2[l])
    subln = subln_g[l][None, :]
    gf = g_final[None, :]

    xp2 = x_prompt.reshape(bp * n_ctx, d)
    qa, ka, va, ga, qb, kb, vb, gb, w_in_bf16 = _inproj(
        xp2, mod3, lambda i: 0, gn, w_in[l], qn, kn, None)
    y_prompt, w_out_bf16 = _attn_ctx(lam_init, lam_params, subln, qa, ka, va, ga, qb, kb, vb, gb,
                                     w_out[l], xp2, mod3, gf, n_ctx)
    y_prompt = y_prompt.reshape(bp, n_ctx, d)
    new_diff_k = ka.reshape(bp, 1, n_ctx, H_A, 2 * DK_A)
    new_diff_v = va.reshape(bp, 1, n_ctx, H_A, DV_A)
    new_gqa_k = kb.reshape(bp, 1, n_ctx, KV_B, DH_B)
    new_gqa_v = vb.reshape(bp, 1, n_ctx, KV_B, DH_B)

    xs2 = x_sample.reshape(bs * n_lat, d)
    in_tiles = n_lat // INPROJ_TM
    qa, ka, va, ga, qb, kb, vb, gb = _inproj(
        xs2, mod3, lambda i: 1 + i // in_tiles, gn, w_in_bf16, qn, kn, _rope_tables(n_lat))
    cka = cache_diff_k[:, l].reshape(bs * n_past * H_A, HEAD_W)
    cva = cache_diff_v[:, l].reshape(bs * n_past * H_A, HEAD_W)
    ckb = cache_gqa_k[:, l].reshape(bs * n_past * KV_B, HEAD_W)
    cvb = cache_gqa_v[:, l].reshape(bs * n_past * KV_B, HEAD_W)
    o = _attn_lat(lam_init, lam_params, subln, qa, ka, va, cka, cva, ga, qb, kb, vb, ckb, cvb, gb,
                  n_lat, n_past)
    out_tiles = n_lat // OUTPROJ_TM
    y_sample = _outproj(o, w_out_bf16, xs2, mod3, lambda i: 1 + i // out_tiles, gf)
    y_sample = y_sample.reshape(bs, n_lat, d)

    return (y_prompt, y_sample, new_diff_k, new_diff_v, new_gqa_k, new_gqa_v)
```

```python
import functools
import math

import jax
import jax.numpy as jnp
import numpy as np
from jax import lax
from jax.experimental import pallas as pl
from jax.experimental.pallas import tpu as pltpu

D_MODEL = 2048
GRID_W = 64
ROPE_THETA = 10000.0
EPS = 1e-6
H_A = 8
DK_A = 64
DV_A = 2 * DK_A
H_B = 8
KV_B = 2
DH_B = 128
G_B = H_B // KV_B
HEAD_W = 128
A_Q = H_A * 2 * DK_A
A_K = H_A * 2 * DK_A
A_V = H_A * DV_A
A_G = H_A * DV_A
B_Q = H_B * DH_B
B_K = KV_B * DH_B
B_V = KV_B * DH_B
B_G = H_B * DH_B
D_IN = A_Q + A_K + A_V + A_G + B_Q + B_K + B_V + B_G
D_MIX = A_V + B_Q
OFF_AQ = 0
OFF_AK = OFF_AQ + A_Q
OFF_AV = OFF_AK + A_K
OFF_AG = OFF_AV + A_V
OFF_BQ = OFF_AG + A_G
OFF_BK = OFF_BQ + B_Q
OFF_BV = OFF_BK + B_K
OFF_BG = OFF_BV + B_V

N_COND = 8
ADA_TK = 256
INPROJ_TM = 256
W_CHUNK = 256
STAGE_BYTES = 2 * 1024 * 1024
STAGE_SLOTS = 6
OUTPROJ_TM = 512
ATTN_TQ = 256
GQA_STACK = 2
SCORE_LOOKAHEAD = 2
QSCALE_A = math.log2(math.e) / math.sqrt(DK_A)
QSCALE_B = math.log2(math.e) / math.sqrt(DH_B)
VMEM_LIMIT = 56 * 1024 * 1024

BF16 = jnp.bfloat16
F32 = jnp.float32


def _params(n_grid_axes):
    return pltpu.CompilerParams(
        dimension_semantics=("arbitrary",) * n_grid_axes,
        vmem_limit_bytes=VMEM_LIMIT,
    )


def _silu(x):
    return x * jax.nn.sigmoid(x)


def _rms(x):
    return x * lax.rsqrt(jnp.mean(x * x, axis=-1, keepdims=True) + EPS)


def _stage_rows(n_cols):
    return 1 << int(math.log2(STAGE_BYTES // (n_cols * 4)))


def _weight_scratch(n_cols):
    return [pltpu.VMEM((n_cols // W_CHUNK, D_MODEL, W_CHUNK), BF16),
            pltpu.VMEM((STAGE_SLOTS, _stage_rows(n_cols), n_cols), F32),
            pltpu.SemaphoreType.DMA((STAGE_SLOTS,)),
            pltpu.SemaphoreType.DMA(())]


def _slab_copy(w_hbm, w_stage, sem, k):
    slot = k % STAGE_SLOTS
    n_rows = w_stage.shape[1]
    return pltpu.make_async_copy(
        w_hbm.at[pl.ds(k * n_rows, n_rows), :], w_stage.at[slot], sem.at[slot])


def _stage_weights(w_hbm, wbf_hbm, w_scr, w_stage, sem, sem_out):
    n_chunks = w_scr.shape[0]
    slab_rows = w_stage.shape[1]
    n_slabs = w_scr.shape[1] // slab_rows
    copy = functools.partial(_slab_copy, w_hbm, w_stage, sem)
    handoff = pltpu.make_async_copy(w_scr, wbf_hbm, sem_out)

    @pl.when(pl.program_id(0) == 0)
    def _():
        for k in range(STAGE_SLOTS - 1):
            copy(k).start()
        for k in range(n_slabs):
            if k + STAGE_SLOTS - 1 < n_slabs:
                copy(k + STAGE_SLOTS - 1).start()
            copy(k).wait()
            rows = slice(k * slab_rows, (k + 1) * slab_rows)
            for c in range(n_chunks):
                w_scr[c, rows, :] = w_stage[
                    k % STAGE_SLOTS, :, c * W_CHUNK:(c + 1) * W_CHUNK].astype(BF16)
        handoff.start()

    return handoff


def _weight_spec(n_cols):
    return pl.BlockSpec((n_cols // W_CHUNK, D_MODEL, W_CHUNK), lambda i: (0, 0, 0),
                        pipeline_mode=pl.Buffered(1))


def _adaln_body(cctx_ref, c_ref, w_ref, b_ref, o_ref):
    @pl.when(pl.program_id(0) == 0)
    def _():
        o_ref[:, 0, :] = jnp.broadcast_to(b_ref[...], (N_COND, b_ref.shape[1]))

    pad = jnp.zeros((N_COND - 1 - c_ref.shape[0], c_ref.shape[1]), F32)
    cond = jnp.concatenate([cctx_ref[...], c_ref[...], pad], axis=0)
    a = _silu(cond).astype(BF16)
    w = w_ref[...].astype(BF16)
    o_ref[:, 0, :] += jnp.dot(a, w, preferred_element_type=F32)


def _adaln(c_ctx, c, w_ada, b_ada):
    d3 = w_ada.shape[1]
    return pl.pallas_call(
        _adaln_body,
        out_shape=jax.ShapeDtypeStruct((N_COND, 1, d3), F32),
        grid=(D_MODEL // ADA_TK,),
        in_specs=[
            pl.BlockSpec((1, ADA_TK), lambda k: (0, k)),
            pl.BlockSpec((c.shape[0], ADA_TK), lambda k: (0, k)),
            pl.BlockSpec((ADA_TK, d3), lambda k: (k, 0)),
            pl.BlockSpec((1, d3), lambda k: (0, 0)),
        ],
        out_specs=pl.BlockSpec((N_COND, 1, d3), lambda k: (0, 0, 0)),
        compiler_params=_params(1),
        name="adaln",
    )(c_ctx, c, w_ada, b_ada)


def _rope_a(x, c, s_up, s_dn):
    return x * c + pltpu.roll(x, 96, 1) * s_up + pltpu.roll(x, 32, 1) * s_dn


def _rope_b(x, c, s):
    return x * c + pltpu.roll(x, 64, 1) * s


def _inproj_body(rope, x_ref, mod_ref, gn_ref, w_in_ref, qn_ref, kn_ref, *rest):
    if rope:
        ca_ref, sau_ref, sad_ref, cb_ref, sb_ref = rest[:5]
        rest = rest[5:]
    qa_o, ka_o, va_o, ga_o, qb_o, kb_o, vb_o, gb_o = rest[:8]
    cache_layout = not rope
    if rope:
        weights = w_in_ref
    else:
        wbf_hbm, weights, w_stage, sem, sem_out = rest[8:]
        handoff = _stage_weights(w_in_ref, wbf_hbm, weights, w_stage, sem, sem_out)
    tm = x_ref.shape[0]

    x = x_ref[...]
    shift = mod_ref[:, 0:D_MODEL]
    scale = mod_ref[:, D_MODEL:2 * D_MODEL]
    h = ((_rms(x) * gn_ref[...]) * (1.0 + scale) + shift).astype(BF16)

    if rope:
        ca, sau, sad = ca_ref[...], sau_ref[...], sad_ref[...]
        cb, sb = cb_ref[...], sb_ref[...]

    def rope_a(t):
        return _rope_a(t, ca, sau, sad) if rope else t

    def rope_b(t):
        return _rope_b(t, cb, sb) if rope else t

    ident = lambda t: t
    regions = (
        (OFF_AQ, A_Q, qa_o, lambda t: rope_a(t) * QSCALE_A, False),
        (OFF_AK, A_K, ka_o, rope_a, True),
        (OFF_AG, A_G, ga_o, _silu, False),
        (OFF_BQ, B_Q, qb_o, lambda t: rope_b(_rms(t) * qn_ref[...]) * QSCALE_B, False),
        (OFF_BK, B_K, kb_o, lambda t: rope_b(_rms(t) * kn_ref[...]), True),
        (OFF_BG, B_G, gb_o, _silu, False),
        (OFF_AV, A_V, va_o, ident, True),
        (OFF_BV, B_V, vb_o, ident, True),
    )
    for start, width, o_ref, epi, is_kv in regions:
        n_heads = width // HEAD_W
        for c0 in range(0, width, W_CHUNK):
            z = jnp.dot(h, weights[(start + c0) // W_CHUNK], preferred_element_type=F32)
            for h0 in range(0, W_CHUNK, HEAD_W):
                hd = (c0 + h0) // HEAD_W
                t = epi(z[:, h0:h0 + HEAD_W]).astype(o_ref.dtype)
                if is_kv and cache_layout:
                    o_ref[pl.ds(hd, tm, stride=n_heads), :] = t
                else:
                    o_ref[:, hd * HEAD_W:(hd + 1) * HEAD_W] = t

    if not rope:
        pl.when(pl.program_id(0) == pl.num_programs(0) - 1)(handoff.wait)


def _inproj(x2d, mod3, row_of_tile, g_norm, w_in, q_norm_g, k_norm_g, rope_tabs):
    t = x2d.shape[0]
    tm = INPROJ_TM
    rope = rope_tabs is not None
    row_spec = lambda w: pl.BlockSpec((tm, w), lambda i: (i, 0))
    const = lambda shape: pl.BlockSpec(shape, lambda i: (0,) * len(shape))
    in_specs = [
        row_spec(D_MODEL),
        pl.BlockSpec((None, 1, 3 * D_MODEL), lambda i: (row_of_tile(i), 0, 0)),
        const((1, D_MODEL)),
        _weight_spec(D_IN) if rope else pl.BlockSpec(memory_space=pl.ANY),
        const((1, HEAD_W)),
        const((1, HEAD_W)),
    ]
    args = [x2d, mod3, g_norm, w_in, q_norm_g, k_norm_g]
    if rope:
        n_pos_tiles = rope_tabs[0].shape[0] // tm
        in_specs += [pl.BlockSpec((tm, HEAD_W), lambda i: (i % n_pos_tiles, 0))] * 5
        args += list(rope_tabs)
    widths = (A_Q, A_K, A_V, A_G, B_Q, B_K, B_V, B_G)
    is_kv = (False, True, True, False, False, True, True, False)
    out_shape, out_specs = [], []
    for w, kv in zip(widths, is_kv):
        if kv and not rope:
            n_heads = w // HEAD_W
            out_shape.append(jax.ShapeDtypeStruct((t * n_heads, HEAD_W), F32))
            out_specs.append(pl.BlockSpec((tm * n_heads, HEAD_W), lambda i: (i, 0)))
        else:
            out_shape.append(jax.ShapeDtypeStruct((t, w), BF16))
            out_specs.append(row_spec(w))
    if not rope:
        out_shape.append(jax.ShapeDtypeStruct((D_IN // W_CHUNK, D_MODEL, W_CHUNK), BF16))
        out_specs.append(pl.BlockSpec(memory_space=pl.ANY))
    return pl.pallas_call(
        functools.partial(_inproj_body, rope),
        out_shape=out_shape,
        grid=(t // tm,),
        in_specs=in_specs,
        out_specs=out_specs,
        scratch_shapes=[] if rope else _weight_scratch(D_IN),
        compiler_params=_params(1),
        name="inproj_rope" if rope else "inproj",
    )(*args)


def _diff_lambda_col(lq1_ref, lk1_ref, lq2_ref, lk2_ref, lam_init):
    s1 = jnp.sum(lq1_ref[...] * lk1_ref[...], axis=-1, keepdims=True)
    s2 = jnp.sum(lq2_ref[...] * lk2_ref[...], axis=-1, keepdims=True)
    return jnp.exp(s1) - jnp.exp(s2) + lam_init


def _cache_head(ref, head, n_tok, n_heads):
    return ref[pl.ds(head, n_tok, stride=n_heads), :].astype(BF16)


def _scores(q, k):
    return lax.dot_general(q, k, (((1,), (1,)), ((), ())), preferred_element_type=F32)


def _softmax_pv(s, v):
    m = jnp.max(s, axis=-1, keepdims=True)
    e = jnp.exp2(s - m)
    v1 = jnp.concatenate([v, jnp.ones_like(v)], axis=1)
    ol = jnp.dot(e.astype(BF16), v1, preferred_element_type=F32)
    return ol[:, :HEAD_W], ol[:, HEAD_W:]


def _attend(lam_col, subln, qa_ref, ga_ref, qb_ref, gb_ref, ka, va, kb, vb, o_ref):
    tq = qa_ref.shape[0]
    lane = lax.broadcasted_iota(jnp.int32, (tq, HEAD_W), 1)
    first = lane < DK_A

    def diff_scores(hd):
        q = qa_ref[:, hd * HEAD_W:(hd + 1) * HEAD_W]
        zero = jnp.zeros_like(q)
        q2 = jnp.concatenate([jnp.where(first, q, zero), jnp.where(first, zero, q)], axis=0)
        return _scores(q2, ka(hd))

    def diff_finish(hd, s):
        cols = slice(hd * HEAD_W, (hd + 1) * HEAD_W)
        o2, l2 = _softmax_pv(s, va(hd))
        r2 = 1.0 / l2
        lam = lam_col[hd:hd + 1, :]
        o = o2[:tq] * r2[:tq] - o2[tq:] * (lam * r2[tq:])
        o_ref[:, cols] = (_rms(o) * subln * ga_ref[:, cols].astype(F32)).astype(o_ref.dtype)

    def gqa_scores(h0):
        q = jnp.concatenate(
            [qb_ref[:, (h0 + g) * HEAD_W:(h0 + g + 1) * HEAD_W] for g in range(GQA_STACK)], axis=0)
        return _scores(q, kb(h0 // G_B))

    def gqa_finish(h0, s):
        o, l = _softmax_pv(s, vb(h0 // G_B))
        o = o * (1.0 / l)
        for g in range(GQA_STACK):
            cols = slice((h0 + g) * HEAD_W, (h0 + g + 1) * HEAD_W)
            gate = gb_ref[:, cols].astype(F32)
            o_ref[:, A_V + (h0 + g) * HEAD_W:A_V + (h0 + g + 1) * HEAD_W] = (
                o[g * tq:(g + 1) * tq, :] * gate).astype(o_ref.dtype)

    units = [(diff_scores, diff_finish, hd) for hd in range(H_A)]
    units += [(gqa_scores, gqa_finish, h0) for h0 in range(0, H_B, GQA_STACK)]
    scores = []
    for u, (_, finish, arg) in enumerate(units):
        while len(scores) < min(len(units), u + 1 + SCORE_LOOKAHEAD):
            issue, _, issue_arg = units[len(scores)]
            scores.append(issue(issue_arg))
        finish(arg, scores[u])
        scores[u] = None


def _project_out(o, weights, x_ref, mod_ref, gf_ref, y_ref):
    m = jnp.concatenate(
        [jnp.dot(o, weights[c], preferred_element_type=F32) for c in range(D_MODEL // W_CHUNK)],
        axis=1)
    gate = mod_ref[:, 2 * D_MODEL:3 * D_MODEL]
    y_ref[...] = _rms(x_ref[...] + gate * m) * gf_ref[...]


def _attn_ctx_body(lam_init, lq1, lk1, lq2, lk2, subln_ref,
                   qa_ref, ka_ref, va_ref, ga_ref, qb_ref, kb_ref, vb_ref, gb_ref,
                   w_out_hbm, x_ref, mod_ref, gf_ref, y_ref, wbf_hbm,
                   w_scr, w_stage, sem, sem_out, o_scr):
    handoff = _stage_weights(w_out_hbm, wbf_hbm, w_scr, w_stage, sem, sem_out)
    lam_col = _diff_lambda_col(lq1, lk1, lq2, lk2, lam_init)
    seq = qa_ref.shape[0]
    head = lambda ref, n_heads: (lambda i: _cache_head(ref, i, seq, n_heads))
    _attend(lam_col, subln_ref[...] * (1.0 - lam_init), qa_ref, ga_ref, qb_ref, gb_ref,
            head(ka_ref, H_A), head(va_ref, H_A), head(kb_ref, KV_B), head(vb_ref, KV_B), o_scr)
    _project_out(o_scr[...], w_scr, x_ref, mod_ref, gf_ref, y_ref)
    pl.when(pl.program_id(0) == pl.num_programs(0) - 1)(handoff.wait)


def _attn_ctx(lam_init, lam_params, subln_g, qa, ka, va, ga, qb, kb, vb, gb, w_out, x2d, mod3,
              g_final, seq):
    t = qa.shape[0]
    const = lambda shape: pl.BlockSpec(shape, lambda b: (0,) * len(shape))
    row_spec = lambda w: pl.BlockSpec((seq, w), lambda b: (b, 0))
    cache_spec = lambda n_heads: pl.BlockSpec((seq * n_heads, HEAD_W), lambda b: (b, 0))
    return pl.pallas_call(
        functools.partial(_attn_ctx_body, lam_init),
        out_shape=[jax.ShapeDtypeStruct((t, D_MODEL), F32),
                   jax.ShapeDtypeStruct((D_MODEL // W_CHUNK, D_MIX, W_CHUNK), BF16)],
        grid=(t // seq,),
        in_specs=[const((H_A, DK_A))] * 4 + [const((1, DV_A))] + [
            row_spec(A_Q), cache_spec(H_A), cache_spec(H_A), row_spec(A_G),
            row_spec(B_Q), cache_spec(KV_B), cache_spec(KV_B), row_spec(B_G),
            pl.BlockSpec(memory_space=pl.ANY),
            row_spec(D_MODEL),
            pl.BlockSpec((None, 1, 3 * D_MODEL), lambda b: (0, 0, 0)),
            const((1, D_MODEL))],
        out_specs=[row_spec(D_MODEL), pl.BlockSpec(memory_space=pl.ANY)],
        scratch_shapes=_weight_scratch(D_MODEL) + [pltpu.VMEM((seq, D_MIX), BF16)],
        compiler_params=_params(1),
        name="attn_ctx",
    )(*lam_params, subln_g, qa, ka, va, ga, qb, kb, vb, gb, w_out, x2d, mod3, g_final)


def _attn_lat_body(lam_init, n_lat, lq1, lk1, lq2, lk2, subln_ref,
                   qa_ref, ka_ref, va_ref, cka_ref, cva_ref, ga_ref,
                   qb_ref, kb_ref, vb_ref, ckb_ref, cvb_ref, gb_ref, o_ref,
                   ka_all, va_all, kb_all, vb_all):
    @pl.when(pl.program_id(1) == 0)
    def _():
        for new_ref, cache_ref, all_ref in ((ka_ref, cka_ref, ka_all), (va_ref, cva_ref, va_all),
                                            (kb_ref, ckb_ref, kb_all), (vb_ref, cvb_ref, vb_all)):
            n_heads = new_ref.shape[1] // HEAD_W
            n_past = cache_ref.shape[0] // n_heads
            all_ref[0:n_lat, :] = new_ref[...]
            for hd in range(n_heads):
                all_ref[n_lat:, hd * HEAD_W:(hd + 1) * HEAD_W] = _cache_head(cache_ref, hd, n_past, n_heads)

    lam_col = _diff_lambda_col(lq1, lk1, lq2, lk2, lam_init)
    head = lambda ref: (lambda i: ref[:, i * HEAD_W:(i + 1) * HEAD_W])
    _attend(lam_col, subln_ref[...] * (1.0 - lam_init), qa_ref, ga_ref, qb_ref, gb_ref,
            head(ka_all), head(va_all), head(kb_all), head(vb_all), o_ref)


def _attn_lat(lam_init, lam_params, subln_g, qa, ka, va, cka, cva, ga, qb, kb, vb, ckb, cvb, gb,
              n_lat, n_past):
    t = qa.shape[0]
    tq = ATTN_TQ
    nq = n_lat // tq
    const = lambda shape: pl.BlockSpec(shape, lambda b, i: (0,) * len(shape))
    q_spec = lambda w: pl.BlockSpec((tq, w), lambda b, i: (b * nq + i, 0))
    new_spec = lambda w: pl.BlockSpec((n_lat, w), lambda b, i: (b, 0))
    cache_spec = lambda w: pl.BlockSpec((n_past * (w // HEAD_W), HEAD_W), lambda b, i: (b, 0))
    n_all = n_lat + n_past
    return pl.pallas_call(
        functools.partial(_attn_lat_body, lam_init, n_lat),
        out_shape=jax.ShapeDtypeStruct((t, D_MIX), BF16),
        grid=(t // n_lat, nq),
        in_specs=[const((H_A, DK_A))] * 4 + [const((1, DV_A))] + [
            q_spec(A_Q), new_spec(A_K), new_spec(A_V), cache_spec(A_K), cache_spec(A_V), q_spec(A_G),
            q_spec(B_Q), new_spec(B_K), new_spec(B_V), cache_spec(B_K), cache_spec(B_V), q_spec(B_G)],
        out_specs=q_spec(D_MIX),
        scratch_shapes=[pltpu.VMEM((n_all, A_K), BF16), pltpu.VMEM((n_all, A_V), BF16),
                        pltpu.VMEM((n_all, B_K), BF16), pltpu.VMEM((n_all, B_V), BF16)],
        compiler_params=_params(2),
        name="attn_lat",
    )(*lam_params, subln_g, qa, ka, va, cka, cva, ga, qb, kb, vb, ckb, cvb, gb)


def _outproj_body(o_ref, w_ref, x_ref, mod_ref, gf_ref, y_ref):
    _project_out(o_ref[...], w_ref, x_ref, mod_ref, gf_ref, y_ref)


def _outproj(o, w_out_bf16, x2d, mod3, row_of_tile, g_final):
    t = x2d.shape[0]
    tm = OUTPROJ_TM
    return pl.pallas_call(
        _outproj_body,
        out_shape=jax.ShapeDtypeStruct((t, D_MODEL), F32),
        grid=(t // tm,),
        in_specs=[
            pl.BlockSpec((tm, D_MIX), lambda i: (i, 0)),
            _weight_spec(D_MODEL),
            pl.BlockSpec((tm, D_MODEL), lambda i: (i, 0)),
            pl.BlockSpec((None, 1, 3 * D_MODEL), lambda i: (row_of_tile(i), 0, 0)),
            pl.BlockSpec((1, D_MODEL), lambda i: (0, 0)),
        ],
        out_specs=pl.BlockSpec((tm, D_MODEL), lambda i: (i, 0)),
        compiler_params=_params(1),
        name="outproj",
    )(o, w_out_bf16, x2d, mod3, g_final)


def _rope_tables(n_tok):
    n_rows = n_tok // GRID_W
    rows = np.repeat(np.arange(n_rows, dtype=np.float32), GRID_W)
    cols = np.tile(np.arange(GRID_W, dtype=np.float32), n_rows)

    def cos_sin(rot_dim):
        quarter = rot_dim // 4
        inv_freq = (1.0 / (np.float32(ROPE_THETA) ** (np.arange(quarter, dtype=np.float32) / quarter))
                    ).astype(np.float32)
        ang = np.concatenate([rows[:, None] * inv_freq, cols[:, None] * inv_freq], axis=-1)
        return np.cos(ang).astype(np.float32), np.sin(ang).astype(np.float32)

    ca, sa = cos_sin(DK_A)
    za = np.zeros_like(sa)
    cb, sb = cos_sin(DH_B)
    tabs = (
        np.concatenate([ca, ca, ca, ca], axis=-1),
        np.concatenate([-sa, za, -sa, za], axis=-1),
        np.concatenate([za, sa, za, sa], axis=-1),
        np.concatenate([cb, cb], axis=-1),
        np.concatenate([-sb, sb], axis=-1),
    )
    return tuple(jnp.asarray(t) for t in tabs)


def kernel(x_prompt, x_sample, cache_diff_k, cache_diff_v, cache_gqa_k, cache_gqa_v, c, c_ctx,
           w_ada, b_ada, g_norm, w_in, lam_q1, lam_k1, lam_q2, lam_k2, subln_g, q_norm_g, k_norm_g,
           w_out, g_final):
    bp, n_ctx, d = x_prompt.shape
    bs, n_lat, _ = x_sample.shape
    depth = w_in.shape[0]
    n_past = cache_diff_k.shape[2]
    assert depth == 1 and d == D_MODEL and bs + 1 <= N_COND
    assert n_lat % INPROJ_TM == 0 and n_lat % OUTPROJ_TM == 0 and n_lat % ATTN_TQ == 0
    l = 0
    lam_init = 0.8 - 0.6 * math.exp(-0.3 * l)

    mod3 = _adaln(c_ctx[None, :], c, w_ada[l], b_ada[l][None, :])

    gn = g_norm[l][None, :]
    qn = q_norm_g[l][None, :]
    kn = k_norm_g[l][None, :]
    lam_params = (lam_q1[l], lam_k1[l], lam_q2[l], lam_k2[l])
    subln = subln_g[l][None, :]
    gf = g_final[None, :]

    xp2 = x_prompt.reshape(bp * n_ctx, d)
    qa, ka, va, ga, qb, kb, vb, gb, w_in_bf16 = _inproj(
        xp2, mod3, lambda i: 0, gn, w_in[l], qn, kn, None)
    y_prompt, w_out_bf16 = _attn_ctx(lam_init, lam_params, subln, qa, ka, va, ga, qb, kb, vb, gb,
                                     w_out[l], xp2, mod3, gf, n_ctx)
    y_prompt = y_prompt.reshape(bp, n_ctx, d)
    new_diff_k = ka.reshape(bp, 1, n_ctx, H_A, 2 * DK_A)
    new_diff_v = va.reshape(bp, 1, n_ctx, H_A, DV_A)
    new_gqa_k = kb.reshape(bp, 1, n_ctx, KV_B, DH_B)
    new_gqa_v = vb.reshape(bp, 1, n_ctx, KV_B, DH_B)

    xs2 = x_sample.reshape(bs * n_lat, d)
    in_tiles = n_lat // INPROJ_TM
    qa, ka, va, ga, qb, kb, vb, gb = _inproj(
        xs2, mod3, lambda i: 1 + i // in_tiles, gn, w_in_bf16, qn, kn, _rope_tables(n_lat))
    cka = cache_diff_k[:, l].reshape(bs * n_past * H_A, HEAD_W)
    cva = cache_diff_v[:, l].reshape(bs * n_past * H_A, HEAD_W)
    ckb = cache_gqa_k[:, l].reshape(bs * n_past * KV_B, HEAD_W)
    cvb = cache_gqa_v[:, l].reshape(bs * n_past * KV_B, HEAD_W)
    o = _attn_lat(lam_init, lam_params, subln, qa, ka, va, cka, cva, ga, qb, kb, vb, ckb, cvb, gb,
                  n_lat, n_past)
    out_tiles = n_lat // OUTPROJ_TM
    y_sample = _outproj(o, w_out_bf16, xs2, mod3, lambda i: 1 + i // out_tiles, gf)
    y_sample = y_sample.reshape(bs, n_lat, d)

    return (y_prompt, y_sample, new_diff_k, new_diff_v, new_gqa_k, new_gqa_v)
```

```python
import functools
import math

import jax
import jax.numpy as jnp
import numpy as np
from jax import lax
from jax.experimental import pallas as pl
from jax.experimental.pallas import tpu as pltpu

D_MODEL = 2048
GRID_W = 64
ROPE_THETA = 10000.0
EPS = 1e-6
H_A = 8
DK_A = 64
DV_A = 2 * DK_A
H_B = 8
KV_B = 2
DH_B = 128
G_B = H_B // KV_B
HEAD_W = 128
A_Q = H_A * 2 * DK_A
A_K = H_A * 2 * DK_A
A_V = H_A * DV_A
A_G = H_A * DV_A
B_Q = H_B * DH_B
B_K = KV_B * DH_B
B_V = KV_B * DH_B
B_G = H_B * DH_B
D_IN = A_Q + A_K + A_V + A_G + B_Q + B_K + B_V + B_G
D_MIX = A_V + B_Q
OFF_AQ = 0
OFF_AK = OFF_AQ + A_Q
OFF_AV = OFF_AK + A_K
OFF_AG = OFF_AV + A_V
OFF_BQ = OFF_AG + A_G
OFF_BK = OFF_BQ + B_Q
OFF_BV = OFF_BK + B_K
OFF_BG = OFF_BV + B_V

N_COND = 8
ADA_TK = 256
INPROJ_TM = 256
W_CHUNK = 256
STAGE_BYTES = 2 * 1024 * 1024
STAGE_SLOTS = 4
OUTPROJ_TM = 512
ATTN_TQ = 256
GQA_STACK = 2
SCORE_LOOKAHEAD = 2
CTX_REQS = 2
CTX_STAGE_SLOTS = 3
CTX_STAGE_BYTES = 1024 * 1024
QSCALE_A = math.log2(math.e) / math.sqrt(DK_A)
QSCALE_B = math.log2(math.e) / math.sqrt(DH_B)
VMEM_LIMIT = 56 * 1024 * 1024

BF16 = jnp.bfloat16
F32 = jnp.float32


def _params(n_grid_axes):
    return pltpu.CompilerParams(
        dimension_semantics=("arbitrary",) * n_grid_axes,
        vmem_limit_bytes=VMEM_LIMIT,
    )


def _silu(x):
    return x * jax.nn.sigmoid(x)


def _rms(x):
    return x * lax.rsqrt(jnp.mean(x * x, axis=-1, keepdims=True) + EPS)


def _weight_scratch(n_cols, slots=STAGE_SLOTS, slab_bytes=STAGE_BYTES):
    slab_rows = 1 << int(math.log2(slab_bytes // (n_cols * 4)))
    return [pltpu.VMEM((n_cols // W_CHUNK, D_MODEL, W_CHUNK), BF16),
            pltpu.VMEM((slots, slab_rows, n_cols), F32),
            pltpu.SemaphoreType.DMA((slots,)),
            pltpu.SemaphoreType.DMA(())]


def _slab_copy(w_hbm, w_stage, sem, k):
    slot = k % w_stage.shape[0]
    n_rows = w_stage.shape[1]
    return pltpu.make_async_copy(
        w_hbm.at[pl.ds(k * n_rows, n_rows), :], w_stage.at[slot], sem.at[slot])


def _stage_weights(w_hbm, wbf_hbm, w_scr, w_stage, sem, sem_out):
    n_chunks = w_scr.shape[0]
    n_slots, slab_rows = w_stage.shape[:2]
    n_slabs = w_scr.shape[1] // slab_rows
    copy = functools.partial(_slab_copy, w_hbm, w_stage, sem)
    handoff = pltpu.make_async_copy(w_scr, wbf_hbm, sem_out)

    @pl.when(pl.program_id(0) == 0)
    def _():
        for k in range(n_slots - 1):
            copy(k).start()
        for k in range(n_slabs):
            if k + n_slots - 1 < n_slabs:
                copy(k + n_slots - 1).start()
            copy(k).wait()
            rows = slice(k * slab_rows, (k + 1) * slab_rows)
            for c in range(n_chunks):
                w_scr[c, rows, :] = w_stage[
                    k % n_slots, :, c * W_CHUNK:(c + 1) * W_CHUNK].astype(BF16)
        handoff.start()

    return handoff


def _weight_spec(n_cols):
    return pl.BlockSpec((n_cols // W_CHUNK, D_MODEL, W_CHUNK), lambda i: (0, 0, 0),
                        pipeline_mode=pl.Buffered(1))


def _adaln_body(cctx_ref, c_ref, w_ref, b_ref, o_ref):
    @pl.when(pl.program_id(0) == 0)
    def _():
        o_ref[:, 0, :] = jnp.broadcast_to(b_ref[...], (N_COND, b_ref.shape[1]))

    pad = jnp.zeros((N_COND - 1 - c_ref.shape[0], c_ref.shape[1]), F32)
    cond = jnp.concatenate([cctx_ref[...], c_ref[...], pad], axis=0)
    a = _silu(cond).astype(BF16)
    w = w_ref[...].astype(BF16)
    o_ref[:, 0, :] += jnp.dot(a, w, preferred_element_type=F32)


def _adaln(c_ctx, c, w_ada, b_ada):
    d3 = w_ada.shape[1]
    return pl.pallas_call(
        _adaln_body,
        out_shape=jax.ShapeDtypeStruct((N_COND, 1, d3), F32),
        grid=(D_MODEL // ADA_TK,),
        in_specs=[
            pl.BlockSpec((1, ADA_TK), lambda k: (0, k)),
            pl.BlockSpec((c.shape[0], ADA_TK), lambda k: (0, k)),
            pl.BlockSpec((ADA_TK, d3), lambda k: (k, 0)),
            pl.BlockSpec((1, d3), lambda k: (0, 0)),
        ],
        out_specs=pl.BlockSpec((N_COND, 1, d3), lambda k: (0, 0, 0)),
        compiler_params=_params(1),
        name="adaln",
    )(c_ctx, c, w_ada, b_ada)


def _rope_a(x, c, s_up, s_dn):
    return x * c + pltpu.roll(x, 96, 1) * s_up + pltpu.roll(x, 32, 1) * s_dn


def _rope_b(x, c, s):
    return x * c + pltpu.roll(x, 64, 1) * s


def _inproj_body(rope, x_ref, mod_ref, gn_ref, w_in_ref, qn_ref, kn_ref, *rest):
    if rope:
        ca_ref, sau_ref, sad_ref, cb_ref, sb_ref = rest[:5]
        rest = rest[5:]
    qa_o, ka_o, va_o, ga_o, qb_o, kb_o, vb_o, gb_o = rest[:8]
    cache_layout = not rope
    if rope:
        weights = w_in_ref
    else:
        wbf_hbm, weights, w_stage, sem, sem_out = rest[8:]
        handoff = _stage_weights(w_in_ref, wbf_hbm, weights, w_stage, sem, sem_out)
    tm = x_ref.shape[0]

    x = x_ref[...]
    shift = mod_ref[:, 0:D_MODEL]
    scale = mod_ref[:, D_MODEL:2 * D_MODEL]
    h = ((_rms(x) * gn_ref[...]) * (1.0 + scale) + shift).astype(BF16)

    if rope:
        ca, sau, sad = ca_ref[...], sau_ref[...], sad_ref[...]
        cb, sb = cb_ref[...], sb_ref[...]

    def rope_a(t):
        return _rope_a(t, ca, sau, sad) if rope else t

    def rope_b(t):
        return _rope_b(t, cb, sb) if rope else t

    ident = lambda t: t
    regions = (
        (OFF_AQ, A_Q, qa_o, lambda t: rope_a(t) * QSCALE_A, False),
        (OFF_AK, A_K, ka_o, rope_a, True),
        (OFF_AG, A_G, ga_o, _silu, False),
        (OFF_BQ, B_Q, qb_o, lambda t: rope_b(_rms(t) * qn_ref[...]) * QSCALE_B, False),
        (OFF_BK, B_K, kb_o, lambda t: rope_b(_rms(t) * kn_ref[...]), True),
        (OFF_BG, B_G, gb_o, _silu, False),
        (OFF_AV, A_V, va_o, ident, True),
        (OFF_BV, B_V, vb_o, ident, True),
    )
    for start, width, o_ref, epi, is_kv in regions:
        n_heads = width // HEAD_W
        for c0 in range(0, width, W_CHUNK):
            z = jnp.dot(h, weights[(start + c0) // W_CHUNK], preferred_element_type=F32)
            for h0 in range(0, W_CHUNK, HEAD_W):
                hd = (c0 + h0) // HEAD_W
                t = epi(z[:, h0:h0 + HEAD_W]).astype(o_ref.dtype)
                if is_kv and cache_layout:
                    o_ref[pl.ds(hd, tm, stride=n_heads), :] = t
                else:
                    o_ref[:, hd * HEAD_W:(hd + 1) * HEAD_W] = t

    if not rope:
        pl.when(pl.program_id(0) == pl.num_programs(0) - 1)(handoff.wait)


def _inproj(x2d, mod3, row_of_tile, g_norm, w_in, q_norm_g, k_norm_g, rope_tabs):
    t = x2d.shape[0]
    tm = INPROJ_TM
    rope = rope_tabs is not None
    row_spec = lambda w: pl.BlockSpec((tm, w), lambda i: (i, 0))
    const = lambda shape: pl.BlockSpec(shape, lambda i: (0,) * len(shape))
    in_specs = [
        row_spec(D_MODEL),
        pl.BlockSpec((None, 1, 3 * D_MODEL), lambda i: (row_of_tile(i), 0, 0)),
        const((1, D_MODEL)),
        _weight_spec(D_IN) if rope else pl.BlockSpec(memory_space=pl.ANY),
        const((1, HEAD_W)),
        const((1, HEAD_W)),
    ]
    args = [x2d, mod3, g_norm, w_in, q_norm_g, k_norm_g]
    if rope:
        n_pos_tiles = rope_tabs[0].shape[0] // tm
        in_specs += [pl.BlockSpec((tm, HEAD_W), lambda i: (i % n_pos_tiles, 0))] * 5
        args += list(rope_tabs)
    widths = (A_Q, A_K, A_V, A_G, B_Q, B_K, B_V, B_G)
    is_kv = (False, True, True, False, False, True, True, False)
    out_shape, out_specs = [], []
    for w, kv in zip(widths, is_kv):
        if kv and not rope:
            n_heads = w // HEAD_W
            out_shape.append(jax.ShapeDtypeStruct((t * n_heads, HEAD_W), F32))
            out_specs.append(pl.BlockSpec((tm * n_heads, HEAD_W), lambda i: (i, 0)))
        else:
            out_shape.append(jax.ShapeDtypeStruct((t, w), BF16))
            out_specs.append(row_spec(w))
    if not rope:
        out_shape.append(jax.ShapeDtypeStruct((D_IN // W_CHUNK, D_MODEL, W_CHUNK), BF16))
        out_specs.append(pl.BlockSpec(memory_space=pl.ANY))
    return pl.pallas_call(
        functools.partial(_inproj_body, rope),
        out_shape=out_shape,
        grid=(t // tm,),
        in_specs=in_specs,
        out_specs=out_specs,
        scratch_shapes=[] if rope else _weight_scratch(D_IN),
        compiler_params=_params(1),
        name="inproj_rope" if rope else "inproj",
    )(*args)


def _diff_lambda_col(lq1_ref, lk1_ref, lq2_ref, lk2_ref, lam_init):
    s1 = jnp.sum(lq1_ref[...] * lk1_ref[...], axis=-1, keepdims=True)
    s2 = jnp.sum(lq2_ref[...] * lk2_ref[...], axis=-1, keepdims=True)
    return jnp.exp(s1) - jnp.exp(s2) + lam_init


def _cache_head(ref, head, n_tok, n_heads):
    return ref[pl.ds(head, n_tok, stride=n_heads), :].astype(BF16)


def _scores(q, k):
    return lax.dot_general(q, k, (((1,), (1,)), ((), ())), preferred_element_type=F32)


def _softmax_pv(s, v):
    m = jnp.max(s, axis=-1, keepdims=True)
    e = jnp.exp2(s - m)
    v1 = jnp.concatenate([v, jnp.ones_like(v)], axis=1)
    ol = jnp.dot(e.astype(BF16), v1, preferred_element_type=F32)
    return ol[:, :HEAD_W], ol[:, HEAD_W:]


def _attention_units(lam_col, subln, qa_ref, ga_ref, qb_ref, gb_ref, ka, va, kb, vb, o_ref):
    tq = qa_ref.shape[0]
    lane = lax.broadcasted_iota(jnp.int32, (tq, HEAD_W), 1)
    first = lane < DK_A

    def diff_scores(hd):
        q = qa_ref[:, hd * HEAD_W:(hd + 1) * HEAD_W]
        zero = jnp.zeros_like(q)
        q2 = jnp.concatenate([jnp.where(first, q, zero), jnp.where(first, zero, q)], axis=0)
        return _scores(q2, ka(hd))

    def diff_finish(hd, s):
        cols = slice(hd * HEAD_W, (hd + 1) * HEAD_W)
        o2, l2 = _softmax_pv(s, va(hd))
        r2 = 1.0 / l2
        lam = lam_col[hd:hd + 1, :]
        o = o2[:tq] * r2[:tq] - o2[tq:] * (lam * r2[tq:])
        o_ref[:, cols] = (_rms(o) * subln * ga_ref[:, cols].astype(F32)).astype(o_ref.dtype)

    def gqa_scores(h0):
        q = jnp.concatenate(
            [qb_ref[:, (h0 + g) * HEAD_W:(h0 + g + 1) * HEAD_W] for g in range(GQA_STACK)], axis=0)
        return _scores(q, kb(h0 // G_B))

    def gqa_finish(h0, s):
        o, l = _softmax_pv(s, vb(h0 // G_B))
        o = o * (1.0 / l)
        for g in range(GQA_STACK):
            cols = slice((h0 + g) * HEAD_W, (h0 + g + 1) * HEAD_W)
            gate = gb_ref[:, cols].astype(F32)
            o_ref[:, A_V + (h0 + g) * HEAD_W:A_V + (h0 + g + 1) * HEAD_W] = (
                o[g * tq:(g + 1) * tq, :] * gate).astype(o_ref.dtype)

    units = [(diff_scores, diff_finish, hd) for hd in range(H_A)]
    units += [(gqa_scores, gqa_finish, h0) for h0 in range(0, H_B, GQA_STACK)]
    return units


def _run_units(units, after=None):
    scores = []
    for u, (_, finish, arg) in enumerate(units):
        while len(scores) < min(len(units), u + 1 + SCORE_LOOKAHEAD):
            issue, _, issue_arg = units[len(scores)]
            scores.append(issue(issue_arg))
        finish(arg, scores[u])
        scores[u] = None
        for thunk in (after or {}).get(u, ()):
            thunk()


def _project_out_steps(o_ref, weights, x_ref, mod_ref, gf_ref, y_ref):
    parts = []

    def chunk(c):
        parts.append(jnp.dot(o_ref[...], weights[c], preferred_element_type=F32))

    def finish():
        gate = mod_ref[:, 2 * D_MODEL:3 * D_MODEL]
        y_ref[...] = _rms(x_ref[...] + gate * jnp.concatenate(parts, axis=1)) * gf_ref[...]

    return [functools.partial(chunk, c) for c in range(D_MODEL // W_CHUNK)] + [finish]


def _attn_ctx_body(lam_init, seq, lq1, lk1, lq2, lk2, subln_ref,
                   qa_ref, ka_ref, va_ref, ga_ref, qb_ref, kb_ref, vb_ref, gb_ref,
                   w_out_hbm, x_ref, mod_ref, gf_ref, y_ref, wbf_hbm,
                   w_scr, w_stage, sem, sem_out, o_scr):
    handoff = _stage_weights(w_out_hbm, wbf_hbm, w_scr, w_stage, sem, sem_out)
    lam_col = _diff_lambda_col(lq1, lk1, lq2, lk2, lam_init)
    subln = subln_ref[...] * (1.0 - lam_init)
    units, after, project = [], {}, []
    for r in range(CTX_REQS):
        rows = lambda ref, n=1, r=r: ref.at[pl.ds(r * seq * n, seq * n), :]
        head = lambda ref, n_heads, rows=rows: (
            lambda i: _cache_head(rows(ref, n_heads), i, seq, n_heads))
        for k, step in enumerate(project):
            after[len(units) + k] = [step]
        units += _attention_units(
            lam_col, subln, rows(qa_ref), rows(ga_ref), rows(qb_ref), rows(gb_ref),
            head(ka_ref, H_A), head(va_ref, H_A), head(kb_ref, KV_B), head(vb_ref, KV_B),
            rows(o_scr))
        project = _project_out_steps(rows(o_scr), w_scr, rows(x_ref), mod_ref, gf_ref, rows(y_ref))
    _run_units(units, after)
    for step in project:
        step()
    pl.when(pl.program_id(0) == pl.num_programs(0) - 1)(handoff.wait)


def _attn_ctx(lam_init, lam_params, subln_g, qa, ka, va, ga, qb, kb, vb, gb, w_out, x2d, mod3,
              g_final, seq):
    t = qa.shape[0]
    rows = CTX_REQS * seq
    const = lambda shape: pl.BlockSpec(shape, lambda b: (0,) * len(shape))
    row_spec = lambda w: pl.BlockSpec((rows, w), lambda b: (b, 0))
    cache_spec = lambda n_heads: pl.BlockSpec((rows * n_heads, HEAD_W), lambda b: (b, 0))
    return pl.pallas_call(
        functools.partial(_attn_ctx_body, lam_init, seq),
        out_shape=[jax.ShapeDtypeStruct((t, D_MODEL), F32),
                   jax.ShapeDtypeStruct((D_MODEL // W_CHUNK, D_MIX, W_CHUNK), BF16)],
        grid=(t // rows,),
        in_specs=[const((H_A, DK_A))] * 4 + [const((1, DV_A))] + [
            row_spec(A_Q), cache_spec(H_A), cache_spec(H_A), row_spec(A_G),
            row_spec(B_Q), cache_spec(KV_B), cache_spec(KV_B), row_spec(B_G),
            pl.BlockSpec(memory_space=pl.ANY),
            row_spec(D_MODEL),
            pl.BlockSpec((None, 1, 3 * D_MODEL), lambda b: (0, 0, 0)),
            const((1, D_MODEL))],
        out_specs=[row_spec(D_MODEL), pl.BlockSpec(memory_space=pl.ANY)],
        scratch_shapes=_weight_scratch(D_MODEL, CTX_STAGE_SLOTS, CTX_STAGE_BYTES)
        + [pltpu.VMEM((rows, D_MIX), BF16)],
        compiler_params=_params(1),
        name="attn_ctx",
    )(*lam_params, subln_g, qa, ka, va, ga, qb, kb, vb, gb, w_out, x2d, mod3, g_final)


def _attn_lat_body(lam_init, n_lat, lq1, lk1, lq2, lk2, subln_ref,
                   qa_ref, ka_ref, va_ref, cka_ref, cva_ref, ga_ref,
                   qb_ref, kb_ref, vb_ref, ckb_ref, cvb_ref, gb_ref, o_ref,
                   ka_all, va_all, kb_all, vb_all):
    @pl.when(pl.program_id(1) == 0)
    def _():
        for new_ref, cache_ref, all_ref in ((ka_ref, cka_ref, ka_all), (va_ref, cva_ref, va_all),
                                            (kb_ref, ckb_ref, kb_all), (vb_ref, cvb_ref, vb_all)):
            n_heads = new_ref.shape[1] // HEAD_W
            n_past = cache_ref.shape[0] // n_heads
            all_ref[0:n_lat, :] = new_ref[...]
            for hd in range(n_heads):
                all_ref[n_lat:, hd * HEAD_W:(hd + 1) * HEAD_W] = _cache_head(cache_ref, hd, n_past, n_heads)

    lam_col = _diff_lambda_col(lq1, lk1, lq2, lk2, lam_init)
    head = lambda ref: (lambda i: ref[:, i * HEAD_W:(i + 1) * HEAD_W])
    _run_units(_attention_units(
        lam_col, subln_ref[...] * (1.0 - lam_init), qa_ref, ga_ref, qb_ref, gb_ref,
        head(ka_all), head(va_all), head(kb_all), head(vb_all), o_ref))


def _attn_lat(lam_init, lam_params, subln_g, qa, ka, va, cka, cva, ga, qb, kb, vb, ckb, cvb, gb,
              n_lat, n_past):
    t = qa.shape[0]
    tq = ATTN_TQ
    nq = n_lat // tq
    const = lambda shape: pl.BlockSpec(shape, lambda b, i: (0,) * len(shape))
    q_spec = lambda w: pl.BlockSpec((tq, w), lambda b, i: (b * nq + i, 0))
    new_spec = lambda w: pl.BlockSpec((n_lat, w), lambda b, i: (b, 0))
    cache_spec = lambda w: pl.BlockSpec((n_past * (w // HEAD_W), HEAD_W), lambda b, i: (b, 0))
    n_all = n_lat + n_past
    return pl.pallas_call(
        functools.partial(_attn_lat_body, lam_init, n_lat),
        out_shape=jax.ShapeDtypeStruct((t, D_MIX), BF16),
        grid=(t // n_lat, nq),
        in_specs=[const((H_A, DK_A))] * 4 + [const((1, DV_A))] + [
            q_spec(A_Q), new_spec(A_K), new_spec(A_V), cache_spec(A_K), cache_spec(A_V), q_spec(A_G),
            q_spec(B_Q), new_spec(B_K), new_spec(B_V), cache_spec(B_K), cache_spec(B_V), q_spec(B_G)],
        out_specs=q_spec(D_MIX),
        scratch_shapes=[pltpu.VMEM((n_all, A_K), BF16), pltpu.VMEM((n_all, A_V), BF16),
                        pltpu.VMEM((n_all, B_K), BF16), pltpu.VMEM((n_all, B_V), BF16)],
        compiler_params=_params(2),
        name="attn_lat",
    )(*lam_params, subln_g, qa, ka, va, cka, cva, ga, qb, kb, vb, ckb, cvb, gb)


def _outproj_body(o_ref, w_ref, x_ref, mod_ref, gf_ref, y_ref):
    for step in _project_out_steps(o_ref, w_ref, x_ref, mod_ref, gf_ref, y_ref):
        step()


def _outproj(o, w_out_bf16, x2d, mod3, row_of_tile, g_final):
    t = x2d.shape[0]
    tm = OUTPROJ_TM
    return pl.pallas_call(
        _outproj_body,
        out_shape=jax.ShapeDtypeStruct((t, D_MODEL), F32),
        grid=(t // tm,),
        in_specs=[
            pl.BlockSpec((tm, D_MIX), lambda i: (i, 0)),
            _weight_spec(D_MODEL),
            pl.BlockSpec((tm, D_MODEL), lambda i: (i, 0)),
            pl.BlockSpec((None, 1, 3 * D_MODEL), lambda i: (row_of_tile(i), 0, 0)),
            pl.BlockSpec((1, D_MODEL), lambda i: (0, 0)),
        ],
        out_specs=pl.BlockSpec((tm, D_MODEL), lambda i: (i, 0)),
        compiler_params=_params(1),
        name="outproj",
    )(o, w_out_bf16, x2d, mod3, g_final)


def _rope_tables(n_tok):
    n_rows = n_tok // GRID_W
    rows = np.repeat(np.arange(n_rows, dtype=np.float32), GRID_W)
    cols = np.tile(np.arange(GRID_W, dtype=np.float32), n_rows)

    def cos_sin(rot_dim):
        quarter = rot_dim // 4
        inv_freq = (1.0 / (np.float32(ROPE_THETA) ** (np.arange(quarter, dtype=np.float32) / quarter))
                    ).astype(np.float32)
        ang = np.concatenate([rows[:, None] * inv_freq, cols[:, None] * inv_freq], axis=-1)
        return np.cos(ang).astype(np.float32), np.sin(ang).astype(np.float32)

    ca, sa = cos_sin(DK_A)
    za = np.zeros_like(sa)
    cb, sb = cos_sin(DH_B)
    tabs = (
        np.concatenate([ca, ca, ca, ca], axis=-1),
        np.concatenate([-sa, za, -sa, za], axis=-1),
        np.concatenate([za, sa, za, sa], axis=-1),
        np.concatenate([cb, cb], axis=-1),
        np.concatenate([-sb, sb], axis=-1),
    )
    return tuple(jnp.asarray(t) for t in tabs)


def kernel(x_prompt, x_sample, cache_diff_k, cache_diff_v, cache_gqa_k, cache_gqa_v, c, c_ctx,
           w_ada, b_ada, g_norm, w_in, lam_q1, lam_k1, lam_q2, lam_k2, subln_g, q_norm_g, k_norm_g,
           w_out, g_final):
    bp, n_ctx, d = x_prompt.shape
    bs, n_lat, _ = x_sample.shape
    depth = w_in.shape[0]
    n_past = cache_diff_k.shape[2]
    assert depth == 1 and d == D_MODEL and bs + 1 <= N_COND
    assert n_lat % INPROJ_TM == 0 and n_lat % OUTPROJ_TM == 0 and n_lat % ATTN_TQ == 0
    l = 0
    lam_init = 0.8 - 0.6 * math.exp(-0.3 * l)

    mod3 = _adaln(c_ctx[None, :], c, w_ada[l], b_ada[l][None, :])

    gn = g_norm[l][None, :]
    qn = q_norm_g[l][None, :]
    kn = k_norm_g[l][None, :]
    lam_params = (lam_q1[l], lam_k1[l], lam_q2[l], lam_k2[l])
    subln = subln_g[l][None, :]
    gf = g_final[None, :]

    xp2 = x_prompt.reshape(bp * n_ctx, d)
    qa, ka, va, ga, qb, kb, vb, gb, w_in_bf16 = _inproj(
        xp2, mod3, lambda i: 0, gn, w_in[l], qn, kn, None)
    y_prompt, w_out_bf16 = _attn_ctx(lam_init, lam_params, subln, qa, ka, va, ga, qb, kb, vb, gb,
                                     w_out[l], xp2, mod3, gf, n_ctx)
    y_prompt = y_prompt.reshape(bp, n_ctx, d)
    new_diff_k = ka.reshape(bp, 1, n_ctx, H_A, 2 * DK_A)
    new_diff_v = va.reshape(bp, 1, n_ctx, H_A, DV_A)
    new_gqa_k = kb.reshape(bp, 1, n_ctx, KV_B, DH_B)
    new_gqa_v = vb.reshape(bp, 1, n_ctx, KV_B, DH_B)

    xs2 = x_sample.reshape(bs * n_lat, d)
    in_tiles = n_lat // INPROJ_TM
    qa, ka, va, ga, qb, kb, vb, gb = _inproj(
        xs2, mod3, lambda i: 1 + i // in_tiles, gn, w_in_bf16, qn, kn, _rope_tables(n_lat))
    cka = cache_diff_k[:, l].reshape(bs * n_past * H_A, HEAD_W)
    cva = cache_diff_v[:, l].reshape(bs * n_past * H_A, HEAD_W)
    ckb = cache_gqa_k[:, l].reshape(bs * n_past * KV_B, HEAD_W)
    cvb = cache_gqa_v[:, l].reshape(bs * n_past * KV_B, HEAD_W)
    o = _attn_lat(lam_init, lam_params, subln, qa, ka, va, cka, cva, ga, qb, kb, vb, ckb, cvb, gb,
                  n_lat, n_past)
    out_tiles = n_lat // OUTPROJ_TM
    y_sample = _outproj(o, w_out_bf16, xs2, mod3, lambda i: 1 + i // out_tiles, gf)
    y_sample = y_sample.reshape(bs, n_lat, d)

    return (y_prompt, y_sample, new_diff_k, new_diff_v, new_gqa_k, new_gqa_v)
```

```python
import functools
import math

import jax
import jax.numpy as jnp
import numpy as np
from jax import lax
from jax.experimental import pallas as pl
from jax.experimental.pallas import tpu as pltpu

D_MODEL = 2048
GRID_W = 64
ROPE_THETA = 10000.0
EPS = 1e-6
H_A = 8
DK_A = 64
DV_A = 2 * DK_A
H_B = 8
KV_B = 2
DH_B = 128
G_B = H_B // KV_B
HEAD_W = 128
A_Q = H_A * 2 * DK_A
A_K = H_A * 2 * DK_A
A_V = H_A * DV_A
A_G = H_A * DV_A
B_Q = H_B * DH_B
B_K = KV_B * DH_B
B_V = KV_B * DH_B
B_G = H_B * DH_B
D_IN = A_Q + A_K + A_V + A_G + B_Q + B_K + B_V + B_G
D_MIX = A_V + B_Q
OFF_AQ = 0
OFF_AK = OFF_AQ + A_Q
OFF_AV = OFF_AK + A_K
OFF_AG = OFF_AV + A_V
OFF_BQ = OFF_AG + A_G
OFF_BK = OFF_BQ + B_Q
OFF_BV = OFF_BK + B_K
OFF_BG = OFF_BV + B_V

N_COND = 8
ADA_TK = 256
INPROJ_TM = 256
W_CHUNK = 256
STAGE_BYTES = 2 * 1024 * 1024
STAGE_BYTES_IN = 1024 * 1024
STAGE_SLOTS = 4
OUTPROJ_TM = 512
ATTN_TQ = 256
GQA_STACK = 2
SCORE_LOOKAHEAD = 2
QSCALE_A = math.log2(math.e) / math.sqrt(DK_A)
QSCALE_B = math.log2(math.e) / math.sqrt(DH_B)
VMEM_LIMIT = 56 * 1024 * 1024

BF16 = jnp.bfloat16
F32 = jnp.float32


def _params(n_grid_axes):
    return pltpu.CompilerParams(
        dimension_semantics=("arbitrary",) * n_grid_axes,
        vmem_limit_bytes=VMEM_LIMIT,
    )


def _silu(x):
    return x * jax.nn.sigmoid(x)


def _rms(x):
    return x * lax.rsqrt(jnp.mean(x * x, axis=-1, keepdims=True) + EPS)


def _stage_rows(n_cols, stage_bytes):
    return 1 << int(math.log2(stage_bytes // (n_cols * 4)))


def _weight_scratch(n_cols, stage_bytes):
    return [pltpu.VMEM((n_cols // W_CHUNK, D_MODEL, W_CHUNK), BF16),
            pltpu.VMEM((STAGE_SLOTS, _stage_rows(n_cols, stage_bytes), n_cols), F32),
            pltpu.SemaphoreType.DMA((STAGE_SLOTS,)),
            pltpu.SemaphoreType.DMA(())]


def _slab_copy(w_hbm, w_stage, sem, k):
    slot = k % STAGE_SLOTS
    n_rows = w_stage.shape[1]
    return pltpu.make_async_copy(
        w_hbm.at[pl.ds(k * n_rows, n_rows), :], w_stage.at[slot], sem.at[slot])


def _stage_weights(w_hbm, wbf_hbm, w_scr, w_stage, sem, sem_out):
    n_chunks = w_scr.shape[0]
    slab_rows = w_stage.shape[1]
    n_slabs = w_scr.shape[1] // slab_rows
    copy = functools.partial(_slab_copy, w_hbm, w_stage, sem)
    handoff = None if wbf_hbm is None else pltpu.make_async_copy(w_scr, wbf_hbm, sem_out)

    @pl.when(pl.program_id(0) == 0)
    def _():
        for k in range(STAGE_SLOTS - 1):
            copy(k).start()
        for k in range(n_slabs):
            if k + STAGE_SLOTS - 1 < n_slabs:
                copy(k + STAGE_SLOTS - 1).start()
            copy(k).wait()
            rows = slice(k * slab_rows, (k + 1) * slab_rows)
            for c in range(n_chunks):
                w_scr[c, rows, :] = w_stage[
                    k % STAGE_SLOTS, :, c * W_CHUNK:(c + 1) * W_CHUNK].astype(BF16)
        if handoff is not None:
            handoff.start()

    return handoff


def _weight_spec(n_cols):
    return pl.BlockSpec((n_cols // W_CHUNK, D_MODEL, W_CHUNK), lambda i: (0, 0, 0),
                        pipeline_mode=pl.Buffered(1))


def _adaln_body(cctx_ref, c_ref, w_ref, b_ref, o_ref):
    @pl.when(pl.program_id(0) == 0)
    def _():
        o_ref[:, 0, :] = jnp.broadcast_to(b_ref[...], (N_COND, b_ref.shape[1]))

    pad = jnp.zeros((N_COND - 1 - c_ref.shape[0], c_ref.shape[1]), F32)
    cond = jnp.concatenate([cctx_ref[...], c_ref[...], pad], axis=0)
    a = _silu(cond).astype(BF16)
    w = w_ref[...].astype(BF16)
    o_ref[:, 0, :] += jnp.dot(a, w, preferred_element_type=F32)


def _adaln(c_ctx, c, w_ada, b_ada):
    d3 = w_ada.shape[1]
    return pl.pallas_call(
        _adaln_body,
        out_shape=jax.ShapeDtypeStruct((N_COND, 1, d3), F32),
        grid=(D_MODEL // ADA_TK,),
        in_specs=[
            pl.BlockSpec((1, ADA_TK), lambda k: (0, k)),
            pl.BlockSpec((c.shape[0], ADA_TK), lambda k: (0, k)),
            pl.BlockSpec((ADA_TK, d3), lambda k: (k, 0)),
            pl.BlockSpec((1, d3), lambda k: (0, 0)),
        ],
        out_specs=pl.BlockSpec((N_COND, 1, d3), lambda k: (0, 0, 0)),
        compiler_params=_params(1),
        name="adaln",
    )(c_ctx, c, w_ada, b_ada)


def _rope_a(x, c, s_up, s_dn):
    return x * c + pltpu.roll(x, 96, 1) * s_up + pltpu.roll(x, 32, 1) * s_dn


def _rope_b(x, c, s):
    return x * c + pltpu.roll(x, 64, 1) * s


def _project_in(rope_tabs, x_ref, mod_ref, gn_ref, weights, qn_ref, kn_ref, outs):
    rope = rope_tabs is not None
    qa_o, ka_o, va_o, ga_o, qb_o, kb_o, vb_o, gb_o = outs
    tm = x_ref.shape[0]

    x = x_ref[...]
    shift = mod_ref[:, 0:D_MODEL]
    scale = mod_ref[:, D_MODEL:2 * D_MODEL]
    h = ((_rms(x) * gn_ref[...]) * (1.0 + scale) + shift).astype(BF16)

    if rope:
        ca, sau, sad, cb, sb = (r[...] for r in rope_tabs)

    def rope_a(t):
        return _rope_a(t, ca, sau, sad) if rope else t

    def rope_b(t):
        return _rope_b(t, cb, sb) if rope else t

    ident = lambda t: t
    regions = (
        (OFF_AQ, A_Q, qa_o, lambda t: rope_a(t) * QSCALE_A, False),
        (OFF_AK, A_K, ka_o, rope_a, True),
        (OFF_AG, A_G, ga_o, _silu, False),
        (OFF_BQ, B_Q, qb_o, lambda t: rope_b(_rms(t) * qn_ref[...]) * QSCALE_B, False),
        (OFF_BK, B_K, kb_o, lambda t: rope_b(_rms(t) * kn_ref[...]), True),
        (OFF_BG, B_G, gb_o, _silu, False),
        (OFF_AV, A_V, va_o, ident, True),
        (OFF_BV, B_V, vb_o, ident, True),
    )
    for start, width, o_ref, epi, is_kv in regions:
        n_heads = width // HEAD_W
        for c0 in range(0, width, W_CHUNK):
            z = jnp.dot(h, weights[(start + c0) // W_CHUNK], preferred_element_type=F32)
            for h0 in range(0, W_CHUNK, HEAD_W):
                hd = (c0 + h0) // HEAD_W
                t = epi(z[:, h0:h0 + HEAD_W]).astype(o_ref.dtype)
                if is_kv and not rope:
                    o_ref[pl.ds(hd, tm, stride=n_heads), :] = t
                else:
                    o_ref[:, hd * HEAD_W:(hd + 1) * HEAD_W] = t


def _inproj_body(n_ctx_tiles, xc_ref, xl_ref, mod_ref, gn_ref, w_in_hbm, qn_ref, kn_ref,
                 ca_ref, sau_ref, sad_ref, cb_ref, sb_ref,
                 qa_o, ga_o, qb_o, gb_o, cka_o, cva_o, ckb_o, cvb_o, lka_o, lva_o, lkb_o, lvb_o,
                 weights, w_stage, sem, unused_sem):
    _stage_weights(w_in_hbm, None, weights, w_stage, sem, None)
    is_ctx = pl.program_id(0) < n_ctx_tiles

    @pl.when(is_ctx)
    def _():
        _project_in(None, xc_ref, mod_ref, gn_ref, weights, qn_ref, kn_ref,
                    (qa_o, cka_o, cva_o, ga_o, qb_o, ckb_o, cvb_o, gb_o))

    @pl.when(jnp.logical_not(is_ctx))
    def _():
        _project_in((ca_ref, sau_ref, sad_ref, cb_ref, sb_ref), xl_ref, mod_ref, gn_ref, weights,
                    qn_ref, kn_ref, (qa_o, lka_o, lva_o, ga_o, qb_o, lkb_o, lvb_o, gb_o))


def _inproj(x_ctx, x_lat, mod3, lat_tiles_per_request, g_norm, w_in, q_norm_g, k_norm_g, rope_tabs):
    tm = INPROJ_TM
    tc, tl = x_ctx.shape[0], x_lat.shape[0]
    nc = tc // tm
    ctx_tile = lambda i: jnp.minimum(i, nc - 1)
    lat_tile = lambda i: jnp.maximum(i - nc, 0)
    const = lambda shape: pl.BlockSpec(shape, lambda i: (0,) * len(shape))
    n_pos_tiles = rope_tabs[0].shape[0] // tm
    in_specs = [
        pl.BlockSpec((tm, D_MODEL), lambda i: (ctx_tile(i), 0)),
        pl.BlockSpec((tm, D_MODEL), lambda i: (lat_tile(i), 0)),
        pl.BlockSpec((None, 1, 3 * D_MODEL), lambda i: (
            jnp.where(i < nc, 0, 1 + lat_tile(i) // lat_tiles_per_request), 0, 0)),
        const((1, D_MODEL)),
        pl.BlockSpec(memory_space=pl.ANY),
        const((1, HEAD_W)),
        const((1, HEAD_W)),
    ] + [pl.BlockSpec((tm, HEAD_W), lambda i: (lat_tile(i) % n_pos_tiles, 0))] * 5
    out_shape = [jax.ShapeDtypeStruct((tc + tl, w), BF16) for w in (A_Q, A_G, B_Q, B_G)]
    out_specs = [pl.BlockSpec((tm, w), lambda i: (i, 0)) for w in (A_Q, A_G, B_Q, B_G)]
    for w in (A_K, A_V, B_K, B_V):
        n_heads = w // HEAD_W
        out_shape.append(jax.ShapeDtypeStruct((tc * n_heads, HEAD_W), F32))
        out_specs.append(pl.BlockSpec((tm * n_heads, HEAD_W), lambda i: (ctx_tile(i), 0)))
    for w in (A_K, A_V, B_K, B_V):
        out_shape.append(jax.ShapeDtypeStruct((tl, w), BF16))
        out_specs.append(pl.BlockSpec((tm, w), lambda i: (lat_tile(i), 0)))
    return pl.pallas_call(
        functools.partial(_inproj_body, nc),
        out_shape=out_shape,
        grid=((tc + tl) // tm,),
        in_specs=in_specs,
        out_specs=out_specs,
        scratch_shapes=_weight_scratch(D_IN, STAGE_BYTES_IN),
        compiler_params=_params(1),
        name="inproj",
    )(x_ctx, x_lat, mod3, g_norm, w_in, q_norm_g, k_norm_g, *rope_tabs)


def _diff_lambda_col(lq1_ref, lk1_ref, lq2_ref, lk2_ref, lam_init):
    s1 = jnp.sum(lq1_ref[...] * lk1_ref[...], axis=-1, keepdims=True)
    s2 = jnp.sum(lq2_ref[...] * lk2_ref[...], axis=-1, keepdims=True)
    return jnp.exp(s1) - jnp.exp(s2) + lam_init


def _cache_head(ref, head, n_tok, n_heads):
    return ref[pl.ds(head, n_tok, stride=n_heads), :].astype(BF16)


def _scores(q, k):
    return lax.dot_general(q, k, (((1,), (1,)), ((), ())), preferred_element_type=F32)


def _softmax_pv(s, v):
    m = jnp.max(s, axis=-1, keepdims=True)
    e = jnp.exp2(s - m)
    v1 = jnp.concatenate([v, jnp.ones_like(v)], axis=1)
    ol = jnp.dot(e.astype(BF16), v1, preferred_element_type=F32)
    return ol[:, :HEAD_W], ol[:, HEAD_W:]


def _attend(lam_col, subln, qa_ref, ga_ref, qb_ref, gb_ref, ka, va, kb, vb, o_ref):
    tq = qa_ref.shape[0]
    lane = lax.broadcasted_iota(jnp.int32, (tq, HEAD_W), 1)
    first = lane < DK_A

    def diff_scores(hd):
        q = qa_ref[:, hd * HEAD_W:(hd + 1) * HEAD_W]
        zero = jnp.zeros_like(q)
        q2 = jnp.concatenate([jnp.where(first, q, zero), jnp.where(first, zero, q)], axis=0)
        return _scores(q2, ka(hd))

    def diff_finish(hd, s):
        cols = slice(hd * HEAD_W, (hd + 1) * HEAD_W)
        o2, l2 = _softmax_pv(s, va(hd))
        r2 = 1.0 / l2
        lam = lam_col[hd:hd + 1, :]
        o = o2[:tq] * r2[:tq] - o2[tq:] * (lam * r2[tq:])
        o_ref[:, cols] = (_rms(o) * subln * ga_ref[:, cols].astype(F32)).astype(o_ref.dtype)

    def gqa_scores(h0):
        q = jnp.concatenate(
            [qb_ref[:, (h0 + g) * HEAD_W:(h0 + g + 1) * HEAD_W] for g in range(GQA_STACK)], axis=0)
        return _scores(q, kb(h0 // G_B))

    def gqa_finish(h0, s):
        o, l = _softmax_pv(s, vb(h0 // G_B))
        o = o * (1.0 / l)
        for g in range(GQA_STACK):
            cols = slice((h0 + g) * HEAD_W, (h0 + g + 1) * HEAD_W)
            gate = gb_ref[:, cols].astype(F32)
            o_ref[:, A_V + (h0 + g) * HEAD_W:A_V + (h0 + g + 1) * HEAD_W] = (
                o[g * tq:(g + 1) * tq, :] * gate).astype(o_ref.dtype)

    units = [(diff_scores, diff_finish, hd) for hd in range(H_A)]
    units += [(gqa_scores, gqa_finish, h0) for h0 in range(0, H_B, GQA_STACK)]
    scores = []
    for u, (_, finish, arg) in enumerate(units):
        while len(scores) < min(len(units), u + 1 + SCORE_LOOKAHEAD):
            issue, _, issue_arg = units[len(scores)]
            scores.append(issue(issue_arg))
        finish(arg, scores[u])
        scores[u] = None


def _project_out(o, weights, x_ref, mod_ref, gf_ref, y_ref):
    m = jnp.concatenate(
        [jnp.dot(o, weights[c], preferred_element_type=F32) for c in range(D_MODEL // W_CHUNK)],
        axis=1)
    gate = mod_ref[:, 2 * D_MODEL:3 * D_MODEL]
    y_ref[...] = _rms(x_ref[...] + gate * m) * gf_ref[...]


def _attn_ctx_body(lam_init, lq1, lk1, lq2, lk2, subln_ref,
                   qa_ref, ka_ref, va_ref, ga_ref, qb_ref, kb_ref, vb_ref, gb_ref,
                   w_out_hbm, x_ref, mod_ref, gf_ref, y_ref, wbf_hbm,
                   w_scr, w_stage, sem, sem_out, o_scr):
    handoff = _stage_weights(w_out_hbm, wbf_hbm, w_scr, w_stage, sem, sem_out)
    lam_col = _diff_lambda_col(lq1, lk1, lq2, lk2, lam_init)
    seq = qa_ref.shape[0]
    head = lambda ref, n_heads: (lambda i: _cache_head(ref, i, seq, n_heads))
    _attend(lam_col, subln_ref[...] * (1.0 - lam_init), qa_ref, ga_ref, qb_ref, gb_ref,
            head(ka_ref, H_A), head(va_ref, H_A), head(kb_ref, KV_B), head(vb_ref, KV_B), o_scr)
    _project_out(o_scr[...], w_scr, x_ref, mod_ref, gf_ref, y_ref)
    pl.when(pl.program_id(0) == pl.num_programs(0) - 1)(handoff.wait)


def _attn_ctx(lam_init, lam_params, subln_g, qa, ka, va, ga, qb, kb, vb, gb, w_out, x2d, mod3,
              g_final, seq):
    t = x2d.shape[0]
    const = lambda shape: pl.BlockSpec(shape, lambda b: (0,) * len(shape))
    row_spec = lambda w: pl.BlockSpec((seq, w), lambda b: (b, 0))
    cache_spec = lambda n_heads: pl.BlockSpec((seq * n_heads, HEAD_W), lambda b: (b, 0))
    return pl.pallas_call(
        functools.partial(_attn_ctx_body, lam_init),
        out_shape=[jax.ShapeDtypeStruct((t, D_MODEL), F32),
                   jax.ShapeDtypeStruct((D_MODEL // W_CHUNK, D_MIX, W_CHUNK), BF16)],
        grid=(t // seq,),
        in_specs=[const((H_A, DK_A))] * 4 + [const((1, DV_A))] + [
            row_spec(A_Q), cache_spec(H_A), cache_spec(H_A), row_spec(A_G),
            row_spec(B_Q), cache_spec(KV_B), cache_spec(KV_B), row_spec(B_G),
            pl.BlockSpec(memory_space=pl.ANY),
            row_spec(D_MODEL),
            pl.BlockSpec((None, 1, 3 * D_MODEL), lambda b: (0, 0, 0)),
            const((1, D_MODEL))],
        out_specs=[row_spec(D_MODEL), pl.BlockSpec(memory_space=pl.ANY)],
        scratch_shapes=_weight_scratch(D_MODEL, STAGE_BYTES) + [pltpu.VMEM((seq, D_MIX), BF16)],
        compiler_params=_params(1),
        name="attn_ctx",
    )(*lam_params, subln_g, qa, ka, va, ga, qb, kb, vb, gb, w_out, x2d, mod3, g_final)


def _attn_lat_body(lam_init, n_lat, lq1, lk1, lq2, lk2, subln_ref,
                   qa_ref, ka_ref, va_ref, cka_ref, cva_ref, ga_ref,
                   qb_ref, kb_ref, vb_ref, ckb_ref, cvb_ref, gb_ref, o_ref,
                   ka_all, va_all, kb_all, vb_all):
    @pl.when(pl.program_id(1) == 0)
    def _():
        for new_ref, cache_ref, all_ref in ((ka_ref, cka_ref, ka_all), (va_ref, cva_ref, va_all),
                                            (kb_ref, ckb_ref, kb_all), (vb_ref, cvb_ref, vb_all)):
            n_heads = new_ref.shape[1] // HEAD_W
            n_past = cache_ref.shape[0] // n_heads
            all_ref[0:n_lat, :] = new_ref[...]
            for hd in range(n_heads):
                all_ref[n_lat:, hd * HEAD_W:(hd + 1) * HEAD_W] = _cache_head(cache_ref, hd, n_past, n_heads)

    lam_col = _diff_lambda_col(lq1, lk1, lq2, lk2, lam_init)
    head = lambda ref: (lambda i: ref[:, i * HEAD_W:(i + 1) * HEAD_W])
    _attend(lam_col, subln_ref[...] * (1.0 - lam_init), qa_ref, ga_ref, qb_ref, gb_ref,
            head(ka_all), head(va_all), head(kb_all), head(vb_all), o_ref)


def _attn_lat(lam_init, lam_params, subln_g, qa, ka, va, cka, cva, ga, qb, kb, vb, ckb, cvb, gb,
              n_lat, n_past, q_row0):
    t = ka.shape[0]
    tq = ATTN_TQ
    nq = n_lat // tq
    q_tile0 = q_row0 // tq
    const = lambda shape: pl.BlockSpec(shape, lambda b, i: (0,) * len(shape))
    q_spec = lambda w: pl.BlockSpec((tq, w), lambda b, i: (q_tile0 + b * nq + i, 0))
    new_spec = lambda w: pl.BlockSpec((n_lat, w), lambda b, i: (b, 0))
    cache_spec = lambda w: pl.BlockSpec((n_past * (w // HEAD_W), HEAD_W), lambda b, i: (b, 0))
    n_all = n_lat + n_past
    return pl.pallas_call(
        functools.partial(_attn_lat_body, lam_init, n_lat),
        out_shape=jax.ShapeDtypeStruct((t, D_MIX), BF16),
        grid=(t // n_lat, nq),
        in_specs=[const((H_A, DK_A))] * 4 + [const((1, DV_A))] + [
            q_spec(A_Q), new_spec(A_K), new_spec(A_V), cache_spec(A_K), cache_spec(A_V), q_spec(A_G),
            q_spec(B_Q), new_spec(B_K), new_spec(B_V), cache_spec(B_K), cache_spec(B_V), q_spec(B_G)],
        out_specs=pl.BlockSpec((tq, D_MIX), lambda b, i: (b * nq + i, 0)),
        scratch_shapes=[pltpu.VMEM((n_all, A_K), BF16), pltpu.VMEM((n_all, A_V), BF16),
                        pltpu.VMEM((n_all, B_K), BF16), pltpu.VMEM((n_all, B_V), BF16)],
        compiler_params=_params(2),
        name="attn_lat",
    )(*lam_params, subln_g, qa, ka, va, cka, cva, ga, qb, kb, vb, ckb, cvb, gb)


def _outproj_body(o_ref, w_ref, x_ref, mod_ref, gf_ref, y_ref):
    _project_out(o_ref[...], w_ref, x_ref, mod_ref, gf_ref, y_ref)


def _outproj(o, w_out_bf16, x2d, mod3, row_of_tile, g_final):
    t = x2d.shape[0]
    tm = OUTPROJ_TM
    return pl.pallas_call(
        _outproj_body,
        out_shape=jax.ShapeDtypeStruct((t, D_MODEL), F32),
        grid=(t // tm,),
        in_specs=[
            pl.BlockSpec((tm, D_MIX), lambda i: (i, 0)),
            _weight_spec(D_MODEL),
            pl.BlockSpec((tm, D_MODEL), lambda i: (i, 0)),
            pl.BlockSpec((None, 1, 3 * D_MODEL), lambda i: (row_of_tile(i), 0, 0)),
            pl.BlockSpec((1, D_MODEL), lambda i: (0, 0)),
        ],
        out_specs=pl.BlockSpec((tm, D_MODEL), lambda i: (i, 0)),
        compiler_params=_params(1),
        name="outproj",
    )(o, w_out_bf16, x2d, mod3, g_final)


def _rope_tables(n_tok):
    n_rows = n_tok // GRID_W
    rows = np.repeat(np.arange(n_rows, dtype=np.float32), GRID_W)
    cols = np.tile(np.arange(GRID_W, dtype=np.float32), n_rows)

    def cos_sin(rot_dim):
        quarter = rot_dim // 4
        inv_freq = (1.0 / (np.float32(ROPE_THETA) ** (np.arange(quarter, dtype=np.float32) / quarter))
                    ).astype(np.float32)
        ang = np.concatenate([rows[:, None] * inv_freq, cols[:, None] * inv_freq], axis=-1)
        return np.cos(ang).astype(np.float32), np.sin(ang).astype(np.float32)

    ca, sa = cos_sin(DK_A)
    za = np.zeros_like(sa)
    cb, sb = cos_sin(DH_B)
    tabs = (
        np.concatenate([ca, ca, ca, ca], axis=-1),
        np.concatenate([-sa, za, -sa, za], axis=-1),
        np.concatenate([za, sa, za, sa], axis=-1),
        np.concatenate([cb, cb], axis=-1),
        np.concatenate([-sb, sb], axis=-1),
    )
    return tuple(jnp.asarray(t) for t in tabs)


def kernel(x_prompt, x_sample, cache_diff_k, cache_diff_v, cache_gqa_k, cache_gqa_v, c, c_ctx,
           w_ada, b_ada, g_norm, w_in, lam_q1, lam_k1, lam_q2, lam_k2, subln_g, q_norm_g, k_norm_g,
           w_out, g_final):
    bp, n_ctx, d = x_prompt.shape
    bs, n_lat, _ = x_sample.shape
    depth = w_in.shape[0]
    n_past = cache_diff_k.shape[2]
    assert depth == 1 and d == D_MODEL and bs + 1 <= N_COND
    assert n_lat % INPROJ_TM == 0 and n_lat % OUTPROJ_TM == 0 and n_lat % ATTN_TQ == 0
    l = 0
    lam_init = 0.8 - 0.6 * math.exp(-0.3 * l)

    mod3 = _adaln(c_ctx[None, :], c, w_ada[l], b_ada[l][None, :])

    gn = g_norm[l][None, :]
    qn = q_norm_g[l][None, :]
    kn = k_norm_g[l][None, :]
    lam_params = (lam_q1[l], lam_k1[l], lam_q2[l], lam_k2[l])
    subln = subln_g[l][None, :]
    gf = g_final[None, :]

    xp2 = x_prompt.reshape(bp * n_ctx, d)
    xs2 = x_sample.reshape(bs * n_lat, d)
    assert (bp * n_ctx) % INPROJ_TM == 0 and (bp * n_ctx) % ATTN_TQ == 0
    (qa, ga, qb, gb, ka_ctx, va_ctx, kb_ctx, vb_ctx, ka_lat, va_lat, kb_lat, vb_lat) = _inproj(
        xp2, xs2, mod3, n_lat // INPROJ_TM, gn, w_in[l], qn, kn, _rope_tables(n_lat))

    y_prompt, w_out_bf16 = _attn_ctx(lam_init, lam_params, subln, qa, ka_ctx, va_ctx, ga,
                                     qb, kb_ctx, vb_ctx, gb, w_out[l], xp2, mod3, gf, n_ctx)
    y_prompt = y_prompt.reshape(bp, n_ctx, d)
    new_diff_k = ka_ctx.reshape(bp, 1, n_ctx, H_A, 2 * DK_A)
    new_diff_v = va_ctx.reshape(bp, 1, n_ctx, H_A, DV_A)
    new_gqa_k = kb_ctx.reshape(bp, 1, n_ctx, KV_B, DH_B)
    new_gqa_v = vb_ctx.reshape(bp, 1, n_ctx, KV_B, DH_B)

    cka = cache_diff_k[:, l].reshape(bs * n_past * H_A, HEAD_W)
    cva = cache_diff_v[:, l].reshape(bs * n_past * H_A, HEAD_W)
    ckb = cache_gqa_k[:, l].reshape(bs * n_past * KV_B, HEAD_W)
    cvb = cache_gqa_v[:, l].reshape(bs * n_past * KV_B, HEAD_W)
    o = _attn_lat(lam_init, lam_params, subln, qa, ka_lat, va_lat, cka, cva, ga,
                  qb, kb_lat, vb_lat, ckb, cvb, gb, n_lat, n_past, bp * n_ctx)
    out_tiles = n_lat // OUTPROJ_TM
    y_sample = _outproj(o, w_out_bf16, xs2, mod3, lambda i: 1 + i // out_tiles, gf)
    y_sample = y_sample.reshape(bs, n_lat, d)

    return (y_prompt, y_sample, new_diff_k, new_diff_v, new_gqa_k, new_gqa_v)
```

```python
import functools
import math

import jax
import jax.numpy as jnp
import numpy as np
from jax import lax
from jax.experimental import pallas as pl
from jax.experimental.pallas import tpu as pltpu

D_MODEL = 2048
GRID_W = 64
ROPE_THETA = 10000.0
EPS = 1e-6
H_A = 8
DK_A = 64
DV_A = 2 * DK_A
H_B = 8
KV_B = 2
DH_B = 128
G_B = H_B // KV_B
HEAD_W = 128
A_Q = H_A * 2 * DK_A
A_K = H_A * 2 * DK_A
A_V = H_A * DV_A
A_G = H_A * DV_A
B_Q = H_B * DH_B
B_K = KV_B * DH_B
B_V = KV_B * DH_B
B_G = H_B * DH_B
D_IN = A_Q + A_K + A_V + A_G + B_Q + B_K + B_V + B_G
D_MIX = A_V + B_Q
OFF_AQ = 0
OFF_AK = OFF_AQ + A_Q
OFF_AV = OFF_AK + A_K
OFF_AG = OFF_AV + A_V
OFF_BQ = OFF_AG + A_G
OFF_BK = OFF_BQ + B_Q
OFF_BV = OFF_BK + B_K
OFF_BG = OFF_BV + B_V

N_COND = 8
ADA_TK = 256
INPROJ_TM = 256
W_CHUNK = 256
STAGE_BYTES = 2 * 1024 * 1024
STAGE_BYTES_IN = 1024 * 1024
STAGE_SLOTS = 4
STAGE_SLOTS_IN = 6
OUTPROJ_TM = 512
ATTN_TQ = 256
GQA_STACK = 2
SCORE_LOOKAHEAD = 2
QSCALE_A = math.log2(math.e) / math.sqrt(DK_A)
QSCALE_B = math.log2(math.e) / math.sqrt(DH_B)
VMEM_LIMIT = 56 * 1024 * 1024

BF16 = jnp.bfloat16
F32 = jnp.float32


def _params(n_grid_axes):
    return pltpu.CompilerParams(
        dimension_semantics=("arbitrary",) * n_grid_axes,
        vmem_limit_bytes=VMEM_LIMIT,
    )


def _silu(x):
    return x * jax.nn.sigmoid(x)


def _rms(x):
    return x * lax.rsqrt(jnp.mean(x * x, axis=-1, keepdims=True) + EPS)


def _stage_rows(n_cols, stage_bytes):
    return 1 << int(math.log2(stage_bytes // (n_cols * 4)))


def _weight_scratch(n_cols, stage_bytes, n_slots):
    return [pltpu.VMEM((n_cols // W_CHUNK, D_MODEL, W_CHUNK), BF16),
            pltpu.VMEM((n_slots, _stage_rows(n_cols, stage_bytes), n_cols), F32),
            pltpu.SemaphoreType.DMA((n_slots,)),
            pltpu.SemaphoreType.DMA(())]


def _slab_copy(w_hbm, w_stage, sem, k):
    slot = k % w_stage.shape[0]
    n_rows = w_stage.shape[1]
    return pltpu.make_async_copy(
        w_hbm.at[pl.ds(k * n_rows, n_rows), :], w_stage.at[slot], sem.at[slot])


def _stage_weights(w_hbm, wbf_hbm, w_scr, w_stage, sem, sem_out):
    n_chunks = w_scr.shape[0]
    n_slots, slab_rows = w_stage.shape[:2]
    n_slabs = w_scr.shape[1] // slab_rows
    copy = functools.partial(_slab_copy, w_hbm, w_stage, sem)
    handoff = None if wbf_hbm is None else pltpu.make_async_copy(w_scr, wbf_hbm, sem_out)

    @pl.when(pl.program_id(0) == 0)
    def _():
        for k in range(n_slots - 1):
            copy(k).start()
        for k in range(n_slabs):
            if k + n_slots - 1 < n_slabs:
                copy(k + n_slots - 1).start()
            copy(k).wait()
            rows = slice(k * slab_rows, (k + 1) * slab_rows)
            for c in range(n_chunks):
                w_scr[c, rows, :] = w_stage[
                    k % n_slots, :, c * W_CHUNK:(c + 1) * W_CHUNK].astype(BF16)
        if handoff is not None:
            handoff.start()

    return handoff


def _weight_spec(n_cols):
    return pl.BlockSpec((n_cols // W_CHUNK, D_MODEL, W_CHUNK), lambda i: (0, 0, 0),
                        pipeline_mode=pl.Buffered(1))


def _adaln_body(cctx_ref, c_ref, w_ref, b_ref, o_ref):
    @pl.when(pl.program_id(0) == 0)
    def _():
        o_ref[:, 0, :] = jnp.broadcast_to(b_ref[...], (N_COND, b_ref.shape[1]))

    pad = jnp.zeros((N_COND - 1 - c_ref.shape[0], c_ref.shape[1]), F32)
    cond = jnp.concatenate([cctx_ref[...], c_ref[...], pad], axis=0)
    a = _silu(cond).astype(BF16)
    w = w_ref[...].astype(BF16)
    o_ref[:, 0, :] += jnp.dot(a, w, preferred_element_type=F32)


def _adaln(c_ctx, c, w_ada, b_ada):
    d3 = w_ada.shape[1]
    return pl.pallas_call(
        _adaln_body,
        out_shape=jax.ShapeDtypeStruct((N_COND, 1, d3), F32),
        grid=(D_MODEL // ADA_TK,),
        in_specs=[
            pl.BlockSpec((1, ADA_TK), lambda k: (0, k)),
            pl.BlockSpec((c.shape[0], ADA_TK), lambda k: (0, k)),
            pl.BlockSpec((ADA_TK, d3), lambda k: (k, 0)),
            pl.BlockSpec((1, d3), lambda k: (0, 0)),
        ],
        out_specs=pl.BlockSpec((N_COND, 1, d3), lambda k: (0, 0, 0)),
        compiler_params=_params(1),
        name="adaln",
    )(c_ctx, c, w_ada, b_ada)


def _rope_a(x, c, s_up, s_dn):
    return x * c + pltpu.roll(x, 96, 1) * s_up + pltpu.roll(x, 32, 1) * s_dn


def _rope_b(x, c, s):
    return x * c + pltpu.roll(x, 64, 1) * s


def _project_in(rope_tabs, x_ref, mod_ref, gn_ref, weights, qn_ref, kn_ref, outs):
    rope = rope_tabs is not None
    qa_o, ka_o, va_o, ga_o, qb_o, kb_o, vb_o, gb_o = outs
    tm = x_ref.shape[0]

    x = x_ref[...]
    shift = mod_ref[:, 0:D_MODEL]
    scale = mod_ref[:, D_MODEL:2 * D_MODEL]
    h = ((_rms(x) * gn_ref[...]) * (1.0 + scale) + shift).astype(BF16)

    if rope:
        ca, sau, sad, cb, sb = (r[...] for r in rope_tabs)

    def rope_a(t):
        return _rope_a(t, ca, sau, sad) if rope else t

    def rope_b(t):
        return _rope_b(t, cb, sb) if rope else t

    ident = lambda t: t
    regions = (
        (OFF_AQ, A_Q, qa_o, lambda t: rope_a(t) * QSCALE_A, False),
        (OFF_AK, A_K, ka_o, rope_a, True),
        (OFF_AG, A_G, ga_o, _silu, False),
        (OFF_BQ, B_Q, qb_o, lambda t: rope_b(_rms(t) * qn_ref[...]) * QSCALE_B, False),
        (OFF_BK, B_K, kb_o, lambda t: rope_b(_rms(t) * kn_ref[...]), True),
        (OFF_BG, B_G, gb_o, _silu, False),
        (OFF_AV, A_V, va_o, ident, True),
        (OFF_BV, B_V, vb_o, ident, True),
    )
    for start, width, o_ref, epi, is_kv in regions:
        n_heads = width // HEAD_W
        for c0 in range(0, width, W_CHUNK):
            z = jnp.dot(h, weights[(start + c0) // W_CHUNK], preferred_element_type=F32)
            for h0 in range(0, W_CHUNK, HEAD_W):
                hd = (c0 + h0) // HEAD_W
                t = epi(z[:, h0:h0 + HEAD_W]).astype(o_ref.dtype)
                if is_kv and not rope:
                    o_ref[pl.ds(hd, tm, stride=n_heads), :] = t
                else:
                    o_ref[:, hd * HEAD_W:(hd + 1) * HEAD_W] = t


def _inproj_body(n_ctx_tiles, xc_ref, xl_ref, mod_ref, gn_ref, w_in_hbm, qn_ref, kn_ref,
                 ca_ref, sau_ref, sad_ref, cb_ref, sb_ref,
                 qa_o, ga_o, qb_o, gb_o, cka_o, cva_o, ckb_o, cvb_o, lka_o, lva_o, lkb_o, lvb_o,
                 weights, w_stage, sem, unused_sem):
    _stage_weights(w_in_hbm, None, weights, w_stage, sem, None)
    is_ctx = pl.program_id(0) < n_ctx_tiles

    @pl.when(is_ctx)
    def _():
        _project_in(None, xc_ref, mod_ref, gn_ref, weights, qn_ref, kn_ref,
                    (qa_o, cka_o, cva_o, ga_o, qb_o, ckb_o, cvb_o, gb_o))

    @pl.when(jnp.logical_not(is_ctx))
    def _():
        _project_in((ca_ref, sau_ref, sad_ref, cb_ref, sb_ref), xl_ref, mod_ref, gn_ref, weights,
                    qn_ref, kn_ref, (qa_o, lka_o, lva_o, ga_o, qb_o, lkb_o, lvb_o, gb_o))


def _inproj(x_ctx, x_lat, mod3, lat_tiles_per_request, g_norm, w_in, q_norm_g, k_norm_g, rope_tabs):
    tm = INPROJ_TM
    tc, tl = x_ctx.shape[0], x_lat.shape[0]
    nc = tc // tm
    ctx_tile = lambda i: jnp.minimum(i, nc - 1)
    lat_tile = lambda i: jnp.maximum(i - nc, 0)
    const = lambda shape: pl.BlockSpec(shape, lambda i: (0,) * len(shape))
    n_pos_tiles = rope_tabs[0].shape[0] // tm
    in_specs = [
        pl.BlockSpec((tm, D_MODEL), lambda i: (ctx_tile(i), 0)),
        pl.BlockSpec((tm, D_MODEL), lambda i: (lat_tile(i), 0)),
        pl.BlockSpec((None, 1, 3 * D_MODEL), lambda i: (
            jnp.where(i < nc, 0, 1 + lat_tile(i) // lat_tiles_per_request), 0, 0)),
        const((1, D_MODEL)),
        pl.BlockSpec(memory_space=pl.ANY),
        const((1, HEAD_W)),
        const((1, HEAD_W)),
    ] + [pl.BlockSpec((tm, HEAD_W), lambda i: (lat_tile(i) % n_pos_tiles, 0))] * 5
    out_shape = [jax.ShapeDtypeStruct((tc + tl, w), BF16) for w in (A_Q, A_G, B_Q, B_G)]
    out_specs = [pl.BlockSpec((tm, w), lambda i: (i, 0)) for w in (A_Q, A_G, B_Q, B_G)]
    for w in (A_K, A_V, B_K, B_V):
        n_heads = w // HEAD_W
        out_shape.append(jax.ShapeDtypeStruct((tc * n_heads, HEAD_W), F32))
        out_specs.append(pl.BlockSpec((tm * n_heads, HEAD_W), lambda i: (ctx_tile(i), 0)))
    for w in (A_K, A_V, B_K, B_V):
        out_shape.append(jax.ShapeDtypeStruct((tl, w), BF16))
        out_specs.append(pl.BlockSpec((tm, w), lambda i: (lat_tile(i), 0)))
    return pl.pallas_call(
        functools.partial(_inproj_body, nc),
        out_shape=out_shape,
        grid=((tc + tl) // tm,),
        in_specs=in_specs,
        out_specs=out_specs,
        scratch_shapes=_weight_scratch(D_IN, STAGE_BYTES_IN, STAGE_SLOTS_IN),
        compiler_params=_params(1),
        name="inproj",
    )(x_ctx, x_lat, mod3, g_norm, w_in, q_norm_g, k_norm_g, *rope_tabs)


def _diff_lambda_col(lq1_ref, lk1_ref, lq2_ref, lk2_ref, lam_init):
    s1 = jnp.sum(lq1_ref[...] * lk1_ref[...], axis=-1, keepdims=True)
    s2 = jnp.sum(lq2_ref[...] * lk2_ref[...], axis=-1, keepdims=True)
    return jnp.exp(s1) - jnp.exp(s2) + lam_init


def _cache_head(ref, head, n_tok, n_heads):
    return ref[pl.ds(head, n_tok, stride=n_heads), :].astype(BF16)


def _scores(q, k):
    return lax.dot_general(q, k, (((1,), (1,)), ((), ())), preferred_element_type=F32)


def _softmax_pv(s, v):
    m = jnp.max(s, axis=-1, keepdims=True)
    e = jnp.exp2(s - m)
    v1 = jnp.concatenate([v, jnp.ones_like(v)], axis=1)
    ol = jnp.dot(e.astype(BF16), v1, preferred_element_type=F32)
    return ol[:, :HEAD_W], ol[:, HEAD_W:]


def _attend(lam_col, subln, qa_ref, ga_ref, qb_ref, gb_ref, ka, va, kb, vb, o_ref):
    tq = qa_ref.shape[0]
    lane = lax.broadcasted_iota(jnp.int32, (tq, HEAD_W), 1)
    first = lane < DK_A

    def diff_scores(hd):
        q = qa_ref[:, hd * HEAD_W:(hd + 1) * HEAD_W]
        zero = jnp.zeros_like(q)
        q2 = jnp.concatenate([jnp.where(first, q, zero), jnp.where(first, zero, q)], axis=0)
        return _scores(q2, ka(hd))

    def diff_finish(hd, s):
        cols = slice(hd * HEAD_W, (hd + 1) * HEAD_W)
        o2, l2 = _softmax_pv(s, va(hd))
        r2 = 1.0 / l2
        lam = lam_col[hd:hd + 1, :]
        o = o2[:tq] * r2[:tq] - o2[tq:] * (lam * r2[tq:])
        o_ref[:, cols] = (_rms(o) * subln * ga_ref[:, cols].astype(F32)).astype(o_ref.dtype)

    def gqa_scores(h0):
        q = jnp.concatenate(
            [qb_ref[:, (h0 + g) * HEAD_W:(h0 + g + 1) * HEAD_W] for g in range(GQA_STACK)], axis=0)
        return _scores(q, kb(h0 // G_B))

    def gqa_finish(h0, s):
        o, l = _softmax_pv(s, vb(h0 // G_B))
        o = o * (1.0 / l)
        for g in range(GQA_STACK):
            cols = slice((h0 + g) * HEAD_W, (h0 + g + 1) * HEAD_W)
            gate = gb_ref[:, cols].astype(F32)
            o_ref[:, A_V + (h0 + g) * HEAD_W:A_V + (h0 + g + 1) * HEAD_W] = (
                o[g * tq:(g + 1) * tq, :] * gate).astype(o_ref.dtype)

    units = [(diff_scores, diff_finish, hd) for hd in range(H_A)]
    units += [(gqa_scores, gqa_finish, h0) for h0 in range(0, H_B, GQA_STACK)]
    scores = []
    for u, (_, finish, arg) in enumerate(units):
        while len(scores) < min(len(units), u + 1 + SCORE_LOOKAHEAD):
            issue, _, issue_arg = units[len(scores)]
            scores.append(issue(issue_arg))
        finish(arg, scores[u])
        scores[u] = None


def _project_out(o, weights, x_ref, mod_ref, gf_ref, y_ref):
    m = jnp.concatenate(
        [jnp.dot(o, weights[c], preferred_element_type=F32) for c in range(D_MODEL // W_CHUNK)],
        axis=1)
    gate = mod_ref[:, 2 * D_MODEL:3 * D_MODEL]
    y_ref[...] = _rms(x_ref[...] + gate * m) * gf_ref[...]


def _attn_ctx_body(lam_init, lq1, lk1, lq2, lk2, subln_ref,
                   qa_ref, ka_ref, va_ref, ga_ref, qb_ref, kb_ref, vb_ref, gb_ref,
                   w_out_hbm, x_ref, mod_ref, gf_ref, y_ref, wbf_hbm,
                   w_scr, w_stage, sem, sem_out, o_scr):
    handoff = _stage_weights(w_out_hbm, wbf_hbm, w_scr, w_stage, sem, sem_out)
    lam_col = _diff_lambda_col(lq1, lk1, lq2, lk2, lam_init)
    seq = qa_ref.shape[0]
    head = lambda ref, n_heads: (lambda i: _cache_head(ref, i, seq, n_heads))
    _attend(lam_col, subln_ref[...] * (1.0 - lam_init), qa_ref, ga_ref, qb_ref, gb_ref,
            head(ka_ref, H_A), head(va_ref, H_A), head(kb_ref, KV_B), head(vb_ref, KV_B), o_scr)
    _project_out(o_scr[...], w_scr, x_ref, mod_ref, gf_ref, y_ref)
    pl.when(pl.program_id(0) == pl.num_programs(0) - 1)(handoff.wait)


def _attn_ctx(lam_init, lam_params, subln_g, qa, ka, va, ga, qb, kb, vb, gb, w_out, x2d, mod3,
              g_final, seq):
    t = x2d.shape[0]
    const = lambda shape: pl.BlockSpec(shape, lambda b: (0,) * len(shape))
    row_spec = lambda w: pl.BlockSpec((seq, w), lambda b: (b, 0))
    cache_spec = lambda n_heads: pl.BlockSpec((seq * n_heads, HEAD_W), lambda b: (b, 0))
    return pl.pallas_call(
        functools.partial(_attn_ctx_body, lam_init),
        out_shape=[jax.ShapeDtypeStruct((t, D_MODEL), F32),
                   jax.ShapeDtypeStruct((D_MODEL // W_CHUNK, D_MIX, W_CHUNK), BF16)],
        grid=(t // seq,),
        in_specs=[const((H_A, DK_A))] * 4 + [const((1, DV_A))] + [
            row_spec(A_Q), cache_spec(H_A), cache_spec(H_A), row_spec(A_G),
            row_spec(B_Q), cache_spec(KV_B), cache_spec(KV_B), row_spec(B_G),
            pl.BlockSpec(memory_space=pl.ANY),
            row_spec(D_MODEL),
            pl.BlockSpec((None, 1, 3 * D_MODEL), lambda b: (0, 0, 0)),
            const((1, D_MODEL))],
        out_specs=[row_spec(D_MODEL), pl.BlockSpec(memory_space=pl.ANY)],
        scratch_shapes=_weight_scratch(D_MODEL, STAGE_BYTES, STAGE_SLOTS) + [pltpu.VMEM((seq, D_MIX), BF16)],
        compiler_params=_params(1),
        name="attn_ctx",
    )(*lam_params, subln_g, qa, ka, va, ga, qb, kb, vb, gb, w_out, x2d, mod3, g_final)


def _attn_lat_body(lam_init, n_lat, lq1, lk1, lq2, lk2, subln_ref,
                   qa_ref, ka_ref, va_ref, cka_ref, cva_ref, ga_ref,
                   qb_ref, kb_ref, vb_ref, ckb_ref, cvb_ref, gb_ref, o_ref,
                   ka_all, va_all, kb_all, vb_all):
    @pl.when(pl.program_id(1) == 0)
    def _():
        for new_ref, cache_ref, all_ref in ((ka_ref, cka_ref, ka_all), (va_ref, cva_ref, va_all),
                                            (kb_ref, ckb_ref, kb_all), (vb_ref, cvb_ref, vb_all)):
            n_heads = new_ref.shape[1] // HEAD_W
            n_past = cache_ref.shape[0] // n_heads
            all_ref[0:n_lat, :] = new_ref[...]
            for hd in range(n_heads):
                all_ref[n_lat:, hd * HEAD_W:(hd + 1) * HEAD_W] = _cache_head(cache_ref, hd, n_past, n_heads)

    lam_col = _diff_lambda_col(lq1, lk1, lq2, lk2, lam_init)
    head = lambda ref: (lambda i: ref[:, i * HEAD_W:(i + 1) * HEAD_W])
    _attend(lam_col, subln_ref[...] * (1.0 - lam_init), qa_ref, ga_ref, qb_ref, gb_ref,
            head(ka_all), head(va_all), head(kb_all), head(vb_all), o_ref)


def _attn_lat(lam_init, lam_params, subln_g, qa, ka, va, cka, cva, ga, qb, kb, vb, ckb, cvb, gb,
              n_lat, n_past, q_row0):
    t = ka.shape[0]
    tq = ATTN_TQ
    nq = n_lat // tq
    q_tile0 = q_row0 // tq
    const = lambda shape: pl.BlockSpec(shape, lambda b, i: (0,) * len(shape))
    q_spec = lambda w: pl.BlockSpec((tq, w), lambda b, i: (q_tile0 + b * nq + i, 0))
    new_spec = lambda w: pl.BlockSpec((n_lat, w), lambda b, i: (b, 0))
    cache_spec = lambda w: pl.BlockSpec((n_past * (w // HEAD_W), HEAD_W), lambda b, i: (b, 0))
    n_all = n_lat + n_past
    return pl.pallas_call(
        functools.partial(_attn_lat_body, lam_init, n_lat),
        out_shape=jax.ShapeDtypeStruct((t, D_MIX), BF16),
        grid=(t // n_lat, nq),
        in_specs=[const((H_A, DK_A))] * 4 + [const((1, DV_A))] + [
            q_spec(A_Q), new_spec(A_K), new_spec(A_V), cache_spec(A_K), cache_spec(A_V), q_spec(A_G),
            q_spec(B_Q), new_spec(B_K), new_spec(B_V), cache_spec(B_K), cache_spec(B_V), q_spec(B_G)],
        out_specs=pl.BlockSpec((tq, D_MIX), lambda b, i: (b * nq + i, 0)),
        scratch_shapes=[pltpu.VMEM((n_all, A_K), BF16), pltpu.VMEM((n_all, A_V), BF16),
                        pltpu.VMEM((n_all, B_K), BF16), pltpu.VMEM((n_all, B_V), BF16)],
        compiler_params=_params(2),
        name="attn_lat",
    )(*lam_params, subln_g, qa, ka, va, cka, cva, ga, qb, kb, vb, ckb, cvb, gb)


def _outproj_body(o_ref, w_ref, x_ref, mod_ref, gf_ref, y_ref):
    _project_out(o_ref[...], w_ref, x_ref, mod_ref, gf_ref, y_ref)


def _outproj(o, w_out_bf16, x2d, mod3, row_of_tile, g_final):
    t = x2d.shape[0]
    tm = OUTPROJ_TM
    return pl.pallas_call(
        _outproj_body,
        out_shape=jax.ShapeDtypeStruct((t, D_MODEL), F32),
        grid=(t // tm,),
        in_specs=[
            pl.BlockSpec((tm, D_MIX), lambda i: (i, 0)),
            _weight_spec(D_MODEL),
            pl.BlockSpec((tm, D_MODEL), lambda i: (i, 0)),
            pl.BlockSpec((None, 1, 3 * D_MODEL), lambda i: (row_of_tile(i), 0, 0)),
            pl.BlockSpec((1, D_MODEL), lambda i: (0, 0)),
        ],
        out_specs=pl.BlockSpec((tm, D_MODEL), lambda i: (i, 0)),
        compiler_params=_params(1),
        name="outproj",
    )(o, w_out_bf16, x2d, mod3, g_final)


def _rope_tables(n_tok):
    n_rows = n_tok // GRID_W
    rows = np.repeat(np.arange(n_rows, dtype=np.float32), GRID_W)
    cols = np.tile(np.arange(GRID_W, dtype=np.float32), n_rows)

    def cos_sin(rot_dim):
        quarter = rot_dim // 4
        inv_freq = (1.0 / (np.float32(ROPE_THETA) ** (np.arange(quarter, dtype=np.float32) / quarter))
                    ).astype(np.float32)
        ang = np.concatenate([rows[:, None] * inv_freq, cols[:, None] * inv_freq], axis=-1)
        return np.cos(ang).astype(np.float32), np.sin(ang).astype(np.float32)

    ca, sa = cos_sin(DK_A)
    za = np.zeros_like(sa)
    cb, sb = cos_sin(DH_B)
    tabs = (
        np.concatenate([ca, ca, ca, ca], axis=-1),
        np.concatenate([-sa, za, -sa, za], axis=-1),
        np.concatenate([za, sa, za, sa], axis=-1),
        np.concatenate([cb, cb], axis=-1),
        np.concatenate([-sb, sb], axis=-1),
    )
    return tuple(jnp.asarray(t) for t in tabs)


def kernel(x_prompt, x_sample, cache_diff_k, cache_diff_v, cache_gqa_k, cache_gqa_v, c, c_ctx,
           w_ada, b_ada, g_norm, w_in, lam_q1, lam_k1, lam_q2, lam_k2, subln_g, q_norm_g, k_norm_g,
           w_out, g_final):
    bp, n_ctx, d = x_prompt.shape
    bs, n_lat, _ = x_sample.shape
    depth = w_in.shape[0]
    n_past = cache_diff_k.shape[2]
    assert depth == 1 and d == D_MODEL and bs + 1 <= N_COND
    assert n_lat % INPROJ_TM == 0 and n_lat % OUTPROJ_TM == 0 and n_lat % ATTN_TQ == 0
    l = 0
    lam_init = 0.8 - 0.6 * math.exp(-0.3 * l)

    mod3 = _adaln(c_ctx[None, :], c, w_ada[l], b_ada[l][None, :])

    gn = g_norm[l][None, :]
    qn = q_norm_g[l][None, :]
    kn = k_norm_g[l][None, :]
    lam_params = (lam_q1[l], lam_k1[l], lam_q2[l], lam_k2[l])
    subln = subln_g[l][None, :]
    gf = g_final[None, :]

    xp2 = x_prompt.reshape(bp * n_ctx, d)
    xs2 = x_sample.reshape(bs * n_lat, d)
    assert (bp * n_ctx) % INPROJ_TM == 0 and (bp * n_ctx) % ATTN_TQ == 0
    (qa, ga, qb, gb, ka_ctx, va_ctx, kb_ctx, vb_ctx, ka_lat, va_lat, kb_lat, vb_lat) = _inproj(
        xp2, xs2, mod3, n_lat // INPROJ_TM, gn, w_in[l], qn, kn, _rope_tables(n_lat))

    y_prompt, w_out_bf16 = _attn_ctx(lam_init, lam_params, subln, qa, ka_ctx, va_ctx, ga,
                                     qb, kb_ctx, vb_ctx, gb, w_out[l], xp2, mod3, gf, n_ctx)
    y_prompt = y_prompt.reshape(bp, n_ctx, d)
    new_diff_k = ka_ctx.reshape(bp, 1, n_ctx, H_A, 2 * DK_A)
    new_diff_v = va_ctx.reshape(bp, 1, n_ctx, H_A, DV_A)
    new_gqa_k = kb_ctx.reshape(bp, 1, n_ctx, KV_B, DH_B)
    new_gqa_v = vb_ctx.reshape(bp, 1, n_ctx, KV_B, DH_B)

    cka = cache_diff_k[:, l].reshape(bs * n_past * H_A, HEAD_W)
    cva = cache_diff_v[:, l].reshape(bs * n_past * H_A, HEAD_W)
    ckb = cache_gqa_k[:, l].reshape(bs * n_past * KV_B, HEAD_W)
    cvb = cache_gqa_v[:, l].reshape(bs * n_past * KV_B, HEAD_W)
    o = _attn_lat(lam_init, lam_params, subln, qa, ka_lat, va_lat, cka, cva, ga,
                  qb, kb_lat, vb_lat, ckb, cvb, gb, n_lat, n_past, bp * n_ctx)
    out_tiles = n_lat // OUTPROJ_TM
    y_sample = _outproj(o, w_out_bf16, xs2, mod3, lambda i: 1 + i // out_tiles, gf)
    y_sample = y_sample.reshape(bs, n_lat, d)

    return (y_prompt, y_sample, new_diff_k, new_diff_v, new_gqa_k, new_gqa_v)
```

```python
import functools
import math

import jax
import jax.numpy as jnp
import numpy as np
from jax import lax
from jax.experimental import pallas as pl
from jax.experimental.pallas import tpu as pltpu

D_MODEL = 2048
GRID_W = 64
ROPE_THETA = 10000.0
EPS = 1e-6
H_A = 8
DK_A = 64
DV_A = 2 * DK_A
H_B = 8
KV_B = 2
DH_B = 128
G_B = H_B // KV_B
HEAD_W = 128
A_Q = H_A * 2 * DK_A
A_K = H_A * 2 * DK_A
A_V = H_A * DV_A
A_G = H_A * DV_A
B_Q = H_B * DH_B
B_K = KV_B * DH_B
B_V = KV_B * DH_B
B_G = H_B * DH_B
D_IN = A_Q + A_K + A_V + A_G + B_Q + B_K + B_V + B_G
D_MIX = A_V + B_Q
OFF_AQ = 0
OFF_AK = OFF_AQ + A_Q
OFF_AV = OFF_AK + A_K
OFF_AG = OFF_AV + A_V
OFF_BQ = OFF_AG + A_G
OFF_BK = OFF_BQ + B_Q
OFF_BV = OFF_BK + B_K
OFF_BG = OFF_BV + B_V

N_COND = 8
ADA_TK = 256
INPROJ_TM = 256
W_CHUNK = 256
STAGE_BYTES = 2 * 1024 * 1024
STAGE_BYTES_IN = 512 * 1024
STAGE_SLOTS = 4
STAGE_SLOTS_IN = 12
OUTPROJ_TM = 512
ATTN_TQ = 256
GQA_STACK = 2
SCORE_LOOKAHEAD = 2
QSCALE_A = math.log2(math.e) / math.sqrt(DK_A)
QSCALE_B = math.log2(math.e) / math.sqrt(DH_B)
VMEM_LIMIT = 56 * 1024 * 1024

BF16 = jnp.bfloat16
F32 = jnp.float32


def _params(n_grid_axes):
    return pltpu.CompilerParams(
        dimension_semantics=("arbitrary",) * n_grid_axes,
        vmem_limit_bytes=VMEM_LIMIT,
    )


def _silu(x):
    return x * jax.nn.sigmoid(x)


def _rms(x):
    return x * lax.rsqrt(jnp.mean(x * x, axis=-1, keepdims=True) + EPS)


def _stage_rows(n_cols, stage_bytes):
    return 1 << int(math.log2(stage_bytes // (n_cols * 4)))


def _weight_scratch(n_cols, stage_bytes, n_slots):
    return [pltpu.VMEM((n_cols // W_CHUNK, D_MODEL, W_CHUNK), BF16),
            pltpu.VMEM((n_slots, _stage_rows(n_cols, stage_bytes), n_cols), F32),
            pltpu.SemaphoreType.DMA((n_slots,)),
            pltpu.SemaphoreType.DMA(())]


def _slab_copy(w_hbm, w_stage, sem, k):
    slot = k % w_stage.shape[0]
    n_rows = w_stage.shape[1]
    return pltpu.make_async_copy(
        w_hbm.at[pl.ds(k * n_rows, n_rows), :], w_stage.at[slot], sem.at[slot])


def _stage_weights(w_hbm, wbf_hbm, w_scr, w_stage, sem, sem_out):
    n_chunks = w_scr.shape[0]
    n_slots, slab_rows = w_stage.shape[:2]
    n_slabs = w_scr.shape[1] // slab_rows
    copy = functools.partial(_slab_copy, w_hbm, w_stage, sem)
    handoff = None if wbf_hbm is None else pltpu.make_async_copy(w_scr, wbf_hbm, sem_out)

    @pl.when(pl.program_id(0) == 0)
    def _():
        for k in range(n_slots - 1):
            copy(k).start()
        for k in range(n_slabs):
            if k + n_slots - 1 < n_slabs:
                copy(k + n_slots - 1).start()
            copy(k).wait()
            rows = slice(k * slab_rows, (k + 1) * slab_rows)
            for c in range(n_chunks):
                w_scr[c, rows, :] = w_stage[
                    k % n_slots, :, c * W_CHUNK:(c + 1) * W_CHUNK].astype(BF16)
        if handoff is not None:
            handoff.start()

    return handoff


def _weight_spec(n_cols):
    return pl.BlockSpec((n_cols // W_CHUNK, D_MODEL, W_CHUNK), lambda i: (0, 0, 0),
                        pipeline_mode=pl.Buffered(1))


def _adaln_body(cctx_ref, c_ref, w_ref, b_ref, o_ref):
    @pl.when(pl.program_id(0) == 0)
    def _():
        o_ref[:, 0, :] = jnp.broadcast_to(b_ref[...], (N_COND, b_ref.shape[1]))

    pad = jnp.zeros((N_COND - 1 - c_ref.shape[0], c_ref.shape[1]), F32)
    cond = jnp.concatenate([cctx_ref[...], c_ref[...], pad], axis=0)
    a = _silu(cond).astype(BF16)
    w = w_ref[...].astype(BF16)
    o_ref[:, 0, :] += jnp.dot(a, w, preferred_element_type=F32)


def _adaln(c_ctx, c, w_ada, b_ada):
    d3 = w_ada.shape[1]
    return pl.pallas_call(
        _adaln_body,
        out_shape=jax.ShapeDtypeStruct((N_COND, 1, d3), F32),
        grid=(D_MODEL // ADA_TK,),
        in_specs=[
            pl.BlockSpec((1, ADA_TK), lambda k: (0, k)),
            pl.BlockSpec((c.shape[0], ADA_TK), lambda k: (0, k)),
            pl.BlockSpec((ADA_TK, d3), lambda k: (k, 0)),
            pl.BlockSpec((1, d3), lambda k: (0, 0)),
        ],
        out_specs=pl.BlockSpec((N_COND, 1, d3), lambda k: (0, 0, 0)),
        compiler_params=_params(1),
        name="adaln",
    )(c_ctx, c, w_ada, b_ada)


def _rope_a(x, c, s_up, s_dn):
    return x * c + pltpu.roll(x, 96, 1) * s_up + pltpu.roll(x, 32, 1) * s_dn


def _rope_b(x, c, s):
    return x * c + pltpu.roll(x, 64, 1) * s


def _project_in(rope_tabs, x_ref, mod_ref, gn_ref, weights, qn_ref, kn_ref, outs):
    rope = rope_tabs is not None
    qa_o, ka_o, va_o, ga_o, qb_o, kb_o, vb_o, gb_o = outs
    tm = x_ref.shape[0]

    x = x_ref[...]
    shift = mod_ref[:, 0:D_MODEL]
    scale = mod_ref[:, D_MODEL:2 * D_MODEL]
    h = ((_rms(x) * gn_ref[...]) * (1.0 + scale) + shift).astype(BF16)

    if rope:
        ca, sau, sad, cb, sb = (r[...] for r in rope_tabs)

    def rope_a(t):
        return _rope_a(t, ca, sau, sad) if rope else t

    def rope_b(t):
        return _rope_b(t, cb, sb) if rope else t

    ident = lambda t: t
    regions = (
        (OFF_AQ, A_Q, qa_o, lambda t: rope_a(t) * QSCALE_A, False),
        (OFF_AK, A_K, ka_o, rope_a, True),
        (OFF_AG, A_G, ga_o, _silu, False),
        (OFF_BQ, B_Q, qb_o, lambda t: rope_b(_rms(t) * qn_ref[...]) * QSCALE_B, False),
        (OFF_BK, B_K, kb_o, lambda t: rope_b(_rms(t) * kn_ref[...]), True),
        (OFF_BG, B_G, gb_o, _silu, False),
        (OFF_AV, A_V, va_o, ident, True),
        (OFF_BV, B_V, vb_o, ident, True),
    )
    for start, width, o_ref, epi, is_kv in regions:
        n_heads = width // HEAD_W
        for c0 in range(0, width, W_CHUNK):
            z = jnp.dot(h, weights[(start + c0) // W_CHUNK], preferred_element_type=F32)
            for h0 in range(0, W_CHUNK, HEAD_W):
                hd = (c0 + h0) // HEAD_W
                t = epi(z[:, h0:h0 + HEAD_W]).astype(o_ref.dtype)
                if is_kv and not rope:
                    o_ref[pl.ds(hd, tm, stride=n_heads), :] = t
                else:
                    o_ref[:, hd * HEAD_W:(hd + 1) * HEAD_W] = t


def _inproj_body(n_ctx_tiles, xc_ref, xl_ref, mod_ref, gn_ref, w_in_hbm, qn_ref, kn_ref,
                 ca_ref, sau_ref, sad_ref, cb_ref, sb_ref,
                 qa_o, ga_o, qb_o, gb_o, cka_o, cva_o, ckb_o, cvb_o, lka_o, lva_o, lkb_o, lvb_o,
                 weights, w_stage, sem, unused_sem):
    _stage_weights(w_in_hbm, None, weights, w_stage, sem, None)
    is_ctx = pl.program_id(0) < n_ctx_tiles

    @pl.when(is_ctx)
    def _():
        _project_in(None, xc_ref, mod_ref, gn_ref, weights, qn_ref, kn_ref,
                    (qa_o, cka_o, cva_o, ga_o, qb_o, ckb_o, cvb_o, gb_o))

    @pl.when(jnp.logical_not(is_ctx))
    def _():
        _project_in((ca_ref, sau_ref, sad_ref, cb_ref, sb_ref), xl_ref, mod_ref, gn_ref, weights,
                    qn_ref, kn_ref, (qa_o, lka_o, lva_o, ga_o, qb_o, lkb_o, lvb_o, gb_o))


def _inproj(x_ctx, x_lat, mod3, lat_tiles_per_request, g_norm, w_in, q_norm_g, k_norm_g, rope_tabs):
    tm = INPROJ_TM
    tc, tl = x_ctx.shape[0], x_lat.shape[0]
    nc = tc // tm
    ctx_tile = lambda i: jnp.minimum(i, nc - 1)
    lat_tile = lambda i: jnp.maximum(i - nc, 0)
    const = lambda shape: pl.BlockSpec(shape, lambda i: (0,) * len(shape))
    n_pos_tiles = rope_tabs[0].shape[0] // tm
    in_specs = [
        pl.BlockSpec((tm, D_MODEL), lambda i: (ctx_tile(i), 0)),
        pl.BlockSpec((tm, D_MODEL), lambda i: (lat_tile(i), 0)),
        pl.BlockSpec((None, 1, 3 * D_MODEL), lambda i: (
            jnp.where(i < nc, 0, 1 + lat_tile(i) // lat_tiles_per_request), 0, 0)),
        const((1, D_MODEL)),
        pl.BlockSpec(memory_space=pl.ANY),
        const((1, HEAD_W)),
        const((1, HEAD_W)),
    ] + [pl.BlockSpec((tm, HEAD_W), lambda i: (lat_tile(i) % n_pos_tiles, 0))] * 5
    out_shape = [jax.ShapeDtypeStruct((tc + tl, w), BF16) for w in (A_Q, A_G, B_Q, B_G)]
    out_specs = [pl.BlockSpec((tm, w), lambda i: (i, 0)) for w in (A_Q, A_G, B_Q, B_G)]
    for w in (A_K, A_V, B_K, B_V):
        n_heads = w // HEAD_W
        out_shape.append(jax.ShapeDtypeStruct((tc * n_heads, HEAD_W), F32))
        out_specs.append(pl.BlockSpec((tm * n_heads, HEAD_W), lambda i: (ctx_tile(i), 0)))
    for w in (A_K, A_V, B_K, B_V):
        out_shape.append(jax.ShapeDtypeStruct((tl, w), BF16))
        out_specs.append(pl.BlockSpec((tm, w), lambda i: (lat_tile(i), 0)))
    return pl.pallas_call(
        functools.partial(_inproj_body, nc),
        out_shape=out_shape,
        grid=((tc + tl) // tm,),
        in_specs=in_specs,
        out_specs=out_specs,
        scratch_shapes=_weight_scratch(D_IN, STAGE_BYTES_IN, STAGE_SLOTS_IN),
        compiler_params=_params(1),
        name="inproj",
    )(x_ctx, x_lat, mod3, g_norm, w_in, q_norm_g, k_norm_g, *rope_tabs)


def _diff_lambda_col(lq1_ref, lk1_ref, lq2_ref, lk2_ref, lam_init):
    s1 = jnp.sum(lq1_ref[...] * lk1_ref[...], axis=-1, keepdims=True)
    s2 = jnp.sum(lq2_ref[...] * lk2_ref[...], axis=-1, keepdims=True)
    return jnp.exp(s1) - jnp.exp(s2) + lam_init


def _cache_head(ref, head, n_tok, n_heads):
    return ref[pl.ds(head, n_tok, stride=n_heads), :].astype(BF16)


def _scores(q, k):
    return lax.dot_general(q, k, (((1,), (1,)), ((), ())), preferred_element_type=F32)


def _softmax_pv(s, v):
    m = jnp.max(s, axis=-1, keepdims=True)
    e = jnp.exp2(s - m)
    v1 = jnp.concatenate([v, jnp.ones_like(v)], axis=1)
    ol = jnp.dot(e.astype(BF16), v1, preferred_element_type=F32)
    return ol[:, :HEAD_W], ol[:, HEAD_W:]


def _attend(lam_col, subln, qa_ref, ga_ref, qb_ref, gb_ref, ka, va, kb, vb, o_ref):
    tq = qa_ref.shape[0]
    lane = lax.broadcasted_iota(jnp.int32, (tq, HEAD_W), 1)
    first = lane < DK_A

    def diff_scores(hd):
        q = qa_ref[:, hd * HEAD_W:(hd + 1) * HEAD_W]
        zero = jnp.zeros_like(q)
        q2 = jnp.concatenate([jnp.where(first, q, zero), jnp.where(first, zero, q)], axis=0)
        return _scores(q2, ka(hd))

    def diff_finish(hd, s):
        cols = slice(hd * HEAD_W, (hd + 1) * HEAD_W)
        o2, l2 = _softmax_pv(s, va(hd))
        r2 = 1.0 / l2
        lam = lam_col[hd:hd + 1, :]
        o = o2[:tq] * r2[:tq] - o2[tq:] * (lam * r2[tq:])
        o_ref[:, cols] = (_rms(o) * subln * ga_ref[:, cols].astype(F32)).astype(o_ref.dtype)

    def gqa_scores(h0):
        q = jnp.concatenate(
            [qb_ref[:, (h0 + g) * HEAD_W:(h0 + g + 1) * HEAD_W] for g in range(GQA_STACK)], axis=0)
        return _scores(q, kb(h0 // G_B))

    def gqa_finish(h0, s):
        o, l = _softmax_pv(s, vb(h0 // G_B))
        o = o * (1.0 / l)
        for g in range(GQA_STACK):
            cols = slice((h0 + g) * HEAD_W, (h0 + g + 1) * HEAD_W)
            gate = gb_ref[:, cols].astype(F32)
            o_ref[:, A_V + (h0 + g) * HEAD_W:A_V + (h0 + g + 1) * HEAD_W] = (
                o[g * tq:(g + 1) * tq, :] * gate).astype(o_ref.dtype)

    units = [(diff_scores, diff_finish, hd) for hd in range(H_A)]
    units += [(gqa_scores, gqa_finish, h0) for h0 in range(0, H_B, GQA_STACK)]
    scores = []
    for u, (_, finish, arg) in enumerate(units):
        while len(scores) < min(len(units), u + 1 + SCORE_LOOKAHEAD):
            issue, _, issue_arg = units[len(scores)]
            scores.append(issue(issue_arg))
        finish(arg, scores[u])
        scores[u] = None


def _project_out(o, weights, x_ref, mod_ref, gf_ref, y_ref):
    m = jnp.concatenate(
        [jnp.dot(o, weights[c], preferred_element_type=F32) for c in range(D_MODEL // W_CHUNK)],
        axis=1)
    gate = mod_ref[:, 2 * D_MODEL:3 * D_MODEL]
    y_ref[...] = _rms(x_ref[...] + gate * m) * gf_ref[...]


def _attn_ctx_body(lam_init, lq1, lk1, lq2, lk2, subln_ref,
                   qa_ref, ka_ref, va_ref, ga_ref, qb_ref, kb_ref, vb_ref, gb_ref,
                   w_out_hbm, x_ref, mod_ref, gf_ref, y_ref, wbf_hbm,
                   w_scr, w_stage, sem, sem_out, o_scr):
    handoff = _stage_weights(w_out_hbm, wbf_hbm, w_scr, w_stage, sem, sem_out)
    lam_col = _diff_lambda_col(lq1, lk1, lq2, lk2, lam_init)
    seq = qa_ref.shape[0]
    head = lambda ref, n_heads: (lambda i: _cache_head(ref, i, seq, n_heads))
    _attend(lam_col, subln_ref[...] * (1.0 - lam_init), qa_ref, ga_ref, qb_ref, gb_ref,
            head(ka_ref, H_A), head(va_ref, H_A), head(kb_ref, KV_B), head(vb_ref, KV_B), o_scr)
    _project_out(o_scr[...], w_scr, x_ref, mod_ref, gf_ref, y_ref)
    pl.when(pl.program_id(0) == pl.num_programs(0) - 1)(handoff.wait)


def _attn_ctx(lam_init, lam_params, subln_g, qa, ka, va, ga, qb, kb, vb, gb, w_out, x2d, mod3,
              g_final, seq):
    t = x2d.shape[0]
    const = lambda shape: pl.BlockSpec(shape, lambda b: (0,) * len(shape))
    row_spec = lambda w: pl.BlockSpec((seq, w), lambda b: (b, 0))
    cache_spec = lambda n_heads: pl.BlockSpec((seq * n_heads, HEAD_W), lambda b: (b, 0))
    return pl.pallas_call(
        functools.partial(_attn_ctx_body, lam_init),
        out_shape=[jax.ShapeDtypeStruct((t, D_MODEL), F32),
                   jax.ShapeDtypeStruct((D_MODEL // W_CHUNK, D_MIX, W_CHUNK), BF16)],
        grid=(t // seq,),
        in_specs=[const((H_A, DK_A))] * 4 + [const((1, DV_A))] + [
            row_spec(A_Q), cache_spec(H_A), cache_spec(H_A), row_spec(A_G),
            row_spec(B_Q), cache_spec(KV_B), cache_spec(KV_B), row_spec(B_G),
            pl.BlockSpec(memory_space=pl.ANY),
            row_spec(D_MODEL),
            pl.BlockSpec((None, 1, 3 * D_MODEL), lambda b: (0, 0, 0)),
            const((1, D_MODEL))],
        out_specs=[row_spec(D_MODEL), pl.BlockSpec(memory_space=pl.ANY)],
        scratch_shapes=_weight_scratch(D_MODEL, STAGE_BYTES, STAGE_SLOTS) + [pltpu.VMEM((seq, D_MIX), BF16)],
        compiler_params=_params(1),
        name="attn_ctx",
    )(*lam_params, subln_g, qa, ka, va, ga, qb, kb, vb, gb, w_out, x2d, mod3, g_final)


def _attn_lat_body(lam_init, n_lat, lq1, lk1, lq2, lk2, subln_ref,
                   qa_ref, ka_ref, va_ref, cka_ref, cva_ref, ga_ref,
                   qb_ref, kb_ref, vb_ref, ckb_ref, cvb_ref, gb_ref, o_ref,
                   ka_all, va_all, kb_all, vb_all):
    @pl.when(pl.program_id(1) == 0)
    def _():
        for new_ref, cache_ref, all_ref in ((ka_ref, cka_ref, ka_all), (va_ref, cva_ref, va_all),
                                            (kb_ref, ckb_ref, kb_all), (vb_ref, cvb_ref, vb_all)):
            n_heads = new_ref.shape[1] // HEAD_W
            n_past = cache_ref.shape[0] // n_heads
            all_ref[0:n_lat, :] = new_ref[...]
            for hd in range(n_heads):
                all_ref[n_lat:, hd * HEAD_W:(hd + 1) * HEAD_W] = _cache_head(cache_ref, hd, n_past, n_heads)

    lam_col = _diff_lambda_col(lq1, lk1, lq2, lk2, lam_init)
    head = lambda ref: (lambda i: ref[:, i * HEAD_W:(i + 1) * HEAD_W])
    _attend(lam_col, subln_ref[...] * (1.0 - lam_init), qa_ref, ga_ref, qb_ref, gb_ref,
            head(ka_all), head(va_all), head(kb_all), head(vb_all), o_ref)


def _attn_lat(lam_init, lam_params, subln_g, qa, ka, va, cka, cva, ga, qb, kb, vb, ckb, cvb, gb,
              n_lat, n_past, q_row0):
    t = ka.shape[0]
    tq = ATTN_TQ
    nq = n_lat // tq
    q_tile0 = q_row0 // tq
    const = lambda shape: pl.BlockSpec(shape, lambda b, i: (0,) * len(shape))
    q_spec = lambda w: pl.BlockSpec((tq, w), lambda b, i: (q_tile0 + b * nq + i, 0))
    new_spec = lambda w: pl.BlockSpec((n_lat, w), lambda b, i: (b, 0))
    cache_spec = lambda w: pl.BlockSpec((n_past * (w // HEAD_W), HEAD_W), lambda b, i: (b, 0))
    n_all = n_lat + n_past
    return pl.pallas_call(
        functools.partial(_attn_lat_body, lam_init, n_lat),
        out_shape=jax.ShapeDtypeStruct((t, D_MIX), BF16),
        grid=(t // n_lat, nq),
        in_specs=[const((H_A, DK_A))] * 4 + [const((1, DV_A))] + [
            q_spec(A_Q), new_spec(A_K), new_spec(A_V), cache_spec(A_K), cache_spec(A_V), q_spec(A_G),
            q_spec(B_Q), new_spec(B_K), new_spec(B_V), cache_spec(B_K), cache_spec(B_V), q_spec(B_G)],
        out_specs=pl.BlockSpec((tq, D_MIX), lambda b, i: (b * nq + i, 0)),
        scratch_shapes=[pltpu.VMEM((n_all, A_K), BF16), pltpu.VMEM((n_all, A_V), BF16),
                        pltpu.VMEM((n_all, B_K), BF16), pltpu.VMEM((n_all, B_V), BF16)],
        compiler_params=_params(2),
        name="attn_lat",
    )(*lam_params, subln_g, qa, ka, va, cka, cva, ga, qb, kb, vb, ckb, cvb, gb)


def _outproj_body(o_ref, w_ref, x_ref, mod_ref, gf_ref, y_ref):
    _project_out(o_ref[...], w_ref, x_ref, mod_ref, gf_ref, y_ref)


def _outproj(o, w_out_bf16, x2d, mod3, row_of_tile, g_final):
    t = x2d.shape[0]
    tm = OUTPROJ_TM
    return pl.pallas_call(
        _outproj_body,
        out_shape=jax.ShapeDtypeStruct((t, D_MODEL), F32),
        grid=(t // tm,),
        in_specs=[
            pl.BlockSpec((tm, D_MIX), lambda i: (i, 0)),
            _weight_spec(D_MODEL),
            pl.BlockSpec((tm, D_MODEL), lambda i: (i, 0)),
            pl.BlockSpec((None, 1, 3 * D_MODEL), lambda i: (row_of_tile(i), 0, 0)),
            pl.BlockSpec((1, D_MODEL), lambda i: (0, 0)),
        ],
        out_specs=pl.BlockSpec((tm, D_MODEL), lambda i: (i, 0)),
        compiler_params=_params(1),
        name="outproj",
    )(o, w_out_bf16, x2d, mod3, g_final)


def _rope_tables(n_tok):
    n_rows = n_tok // GRID_W
    rows = np.repeat(np.arange(n_rows, dtype=np.float32), GRID_W)
    cols = np.tile(np.arange(GRID_W, dtype=np.float32), n_rows)

    def cos_sin(rot_dim):
        quarter = rot_dim // 4
        inv_freq = (1.0 / (np.float32(ROPE_THETA) ** (np.arange(quarter, dtype=np.float32) / quarter))
                    ).astype(np.float32)
        ang = np.concatenate([rows[:, None] * inv_freq, cols[:, None] * inv_freq], axis=-1)
        return np.cos(ang).astype(np.float32), np.sin(ang).astype(np.float32)

    ca, sa = cos_sin(DK_A)
    za = np.zeros_like(sa)
    cb, sb = cos_sin(DH_B)
    tabs = (
        np.concatenate([ca, ca, ca, ca], axis=-1),
        np.concatenate([-sa, za, -sa, za], axis=-1),
        np.concatenate([za, sa, za, sa], axis=-1),
        np.concatenate([cb, cb], axis=-1),
        np.concatenate([-sb, sb], axis=-1),
    )
    return tuple(jnp.asarray(t) for t in tabs)


def kernel(x_prompt, x_sample, cache_diff_k, cache_diff_v, cache_gqa_k, cache_gqa_v, c, c_ctx,
           w_ada, b_ada, g_norm, w_in, lam_q1, lam_k1, lam_q2, lam_k2, subln_g, q_norm_g, k_norm_g,
           w_out, g_final):
    bp, n_ctx, d = x_prompt.shape
    bs, n_lat, _ = x_sample.shape
    depth = w_in.shape[0]
    n_past = cache_diff_k.shape[2]
    assert depth == 1 and d == D_MODEL and bs + 1 <= N_COND
    assert n_lat % INPROJ_TM == 0 and n_lat % OUTPROJ_TM == 0 and n_lat % ATTN_TQ == 0
    l = 0
    lam_init = 0.8 - 0.6 * math.exp(-0.3 * l)

    mod3 = _adaln(c_ctx[None, :], c, w_ada[l], b_ada[l][None, :])

    gn = g_norm[l][None, :]
    qn = q_norm_g[l][None, :]
    kn = k_norm_g[l][None, :]
    lam_params = (lam_q1[l], lam_k1[l], lam_q2[l], lam_k2[l])
    subln = subln_g[l][None, :]
    gf = g_final[None, :]

    xp2 = x_prompt.reshape(bp * n_ctx, d)
    xs2 = x_sample.reshape(bs * n_lat, d)
    assert (bp * n_ctx) % INPROJ_TM == 0 and (bp * n_ctx) % ATTN_TQ == 0
    (qa, ga, qb, gb, ka_ctx, va_ctx, kb_ctx, vb_ctx, ka_lat, va_lat, kb_lat, vb_lat) = _inproj(
        xp2, xs2, mod3, n_lat // INPROJ_TM, gn, w_in[l], qn, kn, _rope_tables(n_lat))

    y_prompt, w_out_bf16 = _attn_ctx(lam_init, lam_params, subln, qa, ka_ctx, va_ctx, ga,
                                     qb, kb_ctx, vb_ctx, gb, w_out[l], xp2, mod3, gf, n_ctx)
    y_prompt = y_prompt.reshape(bp, n_ctx, d)
    new_diff_k = ka_ctx.reshape(bp, 1, n_ctx, H_A, 2 * DK_A)
    new_diff_v = va_ctx.reshape(bp, 1, n_ctx, H_A, DV_A)
    new_gqa_k = kb_ctx.reshape(bp, 1, n_ctx, KV_B, DH_B)
    new_gqa_v = vb_ctx.reshape(bp, 1, n_ctx, KV_B, DH_B)

    cka = cache_diff_k[:, l].reshape(bs * n_past * H_A, HEAD_W)
    cva = cache_diff_v[:, l].reshape(bs * n_past * H_A, HEAD_W)
    ckb = cache_gqa_k[:, l].reshape(bs * n_past * KV_B, HEAD_W)
    cvb = cache_gqa_v[:, l].reshape(bs * n_past * KV_B, HEAD_W)
    o = _attn_lat(lam_init, lam_params, subln, qa, ka_lat, va_lat, cka, cva, ga,
                  qb, kb_lat, vb_lat, ckb, cvb, gb, n_lat, n_past, bp * n_ctx)
    out_tiles = n_lat // OUTPROJ_TM
    y_sample = _outproj(o, w_out_bf16, xs2, mod3, lambda i: 1 + i // out_tiles, gf)
    y_sample = y_sample.reshape(bs, n_lat, d)

    return (y_prompt, y_sample, new_diff_k, new_diff_v, new_gqa_k, new_gqa_v)
```

```python
import functools
import math

import jax
import jax.numpy as jnp
import numpy as np
from jax import lax
from jax.experimental import pallas as pl
from jax.experimental.pallas import tpu as pltpu

D_MODEL = 2048
GRID_W = 64
ROPE_THETA = 10000.0
EPS = 1e-6
H_A = 8
DK_A = 64
DV_A = 2 * DK_A
H_B = 8
KV_B = 2
DH_B = 128
G_B = H_B // KV_B
HEAD_W = 128
A_Q = H_A * 2 * DK_A
A_K = H_A * 2 * DK_A
A_V = H_A * DV_A
A_G = H_A * DV_A
B_Q = H_B * DH_B
B_K = KV_B * DH_B
B_V = KV_B * DH_B
B_G = H_B * DH_B
D_IN = A_Q + A_K + A_V + A_G + B_Q + B_K + B_V + B_G
D_MIX = A_V + B_Q
OFF_AQ = 0
OFF_AK = OFF_AQ + A_Q
OFF_AV = OFF_AK + A_K
OFF_AG = OFF_AV + A_V
OFF_BQ = OFF_AG + A_G
OFF_BK = OFF_BQ + B_Q
OFF_BV = OFF_BK + B_K
OFF_BG = OFF_BV + B_V

N_COND = 8
ADA_TK = 256
INPROJ_TM = 256
W_CHUNK = 256
STAGE_BYTES = 512 * 1024
STAGE_BYTES_IN = 512 * 1024
STAGE_SLOTS = 16
STAGE_SLOTS_IN = 12
OUTPROJ_TM = 512
ATTN_TQ = 256
GQA_STACK = 2
SCORE_LOOKAHEAD = 2
QSCALE_A = math.log2(math.e) / math.sqrt(DK_A)
QSCALE_B = math.log2(math.e) / math.sqrt(DH_B)
VMEM_LIMIT = 56 * 1024 * 1024

BF16 = jnp.bfloat16
F32 = jnp.float32


def _params(n_grid_axes):
    return pltpu.CompilerParams(
        dimension_semantics=("arbitrary",) * n_grid_axes,
        vmem_limit_bytes=VMEM_LIMIT,
    )


def _silu(x):
    return x * jax.nn.sigmoid(x)


def _rms(x):
    return x * lax.rsqrt(jnp.mean(x * x, axis=-1, keepdims=True) + EPS)


def _stage_rows(n_cols, stage_bytes):
    return 1 << int(math.log2(stage_bytes // (n_cols * 4)))


def _weight_scratch(n_cols, stage_bytes, n_slots):
    return [pltpu.VMEM((n_cols // W_CHUNK, D_MODEL, W_CHUNK), BF16),
            pltpu.VMEM((n_slots, _stage_rows(n_cols, stage_bytes), n_cols), F32),
            pltpu.SemaphoreType.DMA((n_slots,)),
            pltpu.SemaphoreType.DMA(())]


def _slab_copy(w_hbm, w_stage, sem, k):
    slot = k % w_stage.shape[0]
    n_rows = w_stage.shape[1]
    return pltpu.make_async_copy(
        w_hbm.at[pl.ds(k * n_rows, n_rows), :], w_stage.at[slot], sem.at[slot])


def _stage_weights(w_hbm, wbf_hbm, w_scr, w_stage, sem, sem_out):
    n_chunks = w_scr.shape[0]
    n_slots, slab_rows = w_stage.shape[:2]
    n_slabs = w_scr.shape[1] // slab_rows
    copy = functools.partial(_slab_copy, w_hbm, w_stage, sem)
    handoff = None if wbf_hbm is None else pltpu.make_async_copy(w_scr, wbf_hbm, sem_out)

    @pl.when(pl.program_id(0) == 0)
    def _():
        for k in range(n_slots - 1):
            copy(k).start()
        for k in range(n_slabs):
            if k + n_slots - 1 < n_slabs:
                copy(k + n_slots - 1).start()
            copy(k).wait()
            rows = slice(k * slab_rows, (k + 1) * slab_rows)
            for c in range(n_chunks):
                w_scr[c, rows, :] = w_stage[
                    k % n_slots, :, c * W_CHUNK:(c + 1) * W_CHUNK].astype(BF16)
        if handoff is not None:
            handoff.start()

    return handoff


def _weight_spec(n_cols):
    return pl.BlockSpec((n_cols // W_CHUNK, D_MODEL, W_CHUNK), lambda i: (0, 0, 0),
                        pipeline_mode=pl.Buffered(1))


def _adaln_body(cctx_ref, c_ref, w_ref, b_ref, o_ref):
    @pl.when(pl.program_id(0) == 0)
    def _():
        o_ref[:, 0, :] = jnp.broadcast_to(b_ref[...], (N_COND, b_ref.shape[1]))

    pad = jnp.zeros((N_COND - 1 - c_ref.shape[0], c_ref.shape[1]), F32)
    cond = jnp.concatenate([cctx_ref[...], c_ref[...], pad], axis=0)
    a = _silu(cond).astype(BF16)
    w = w_ref[...].astype(BF16)
    o_ref[:, 0, :] += jnp.dot(a, w, preferred_element_type=F32)


def _adaln(c_ctx, c, w_ada, b_ada):
    d3 = w_ada.shape[1]
    return pl.pallas_call(
        _adaln_body,
        out_shape=jax.ShapeDtypeStruct((N_COND, 1, d3), F32),
        grid=(D_MODEL // ADA_TK,),
        in_specs=[
            pl.BlockSpec((1, ADA_TK), lambda k: (0, k)),
            pl.BlockSpec((c.shape[0], ADA_TK), lambda k: (0, k)),
            pl.BlockSpec((ADA_TK, d3), lambda k: (k, 0)),
            pl.BlockSpec((1, d3), lambda k: (0, 0)),
        ],
        out_specs=pl.BlockSpec((N_COND, 1, d3), lambda k: (0, 0, 0)),
        compiler_params=_params(1),
        name="adaln",
    )(c_ctx, c, w_ada, b_ada)


def _rope_a(x, c, s_up, s_dn):
    return x * c + pltpu.roll(x, 96, 1) * s_up + pltpu.roll(x, 32, 1) * s_dn


def _rope_b(x, c, s):
    return x * c + pltpu.roll(x, 64, 1) * s


def _project_in(rope_tabs, x_ref, mod_ref, gn_ref, weights, qn_ref, kn_ref, outs):
    rope = rope_tabs is not None
    qa_o, ka_o, va_o, ga_o, qb_o, kb_o, vb_o, gb_o = outs
    tm = x_ref.shape[0]

    x = x_ref[...]
    shift = mod_ref[:, 0:D_MODEL]
    scale = mod_ref[:, D_MODEL:2 * D_MODEL]
    h = ((_rms(x) * gn_ref[...]) * (1.0 + scale) + shift).astype(BF16)

    if rope:
        ca, sau, sad, cb, sb = (r[...] for r in rope_tabs)

    def rope_a(t):
        return _rope_a(t, ca, sau, sad) if rope else t

    def rope_b(t):
        return _rope_b(t, cb, sb) if rope else t

    ident = lambda t: t
    regions = (
        (OFF_AQ, A_Q, qa_o, lambda t: rope_a(t) * QSCALE_A, False),
        (OFF_AK, A_K, ka_o, rope_a, True),
        (OFF_AG, A_G, ga_o, _silu, False),
        (OFF_BQ, B_Q, qb_o, lambda t: rope_b(_rms(t) * qn_ref[...]) * QSCALE_B, False),
        (OFF_BK, B_K, kb_o, lambda t: rope_b(_rms(t) * kn_ref[...]), True),
        (OFF_BG, B_G, gb_o, _silu, False),
        (OFF_AV, A_V, va_o, ident, True),
        (OFF_BV, B_V, vb_o, ident, True),
    )
    for start, width, o_ref, epi, is_kv in regions:
        n_heads = width // HEAD_W
        for c0 in range(0, width, W_CHUNK):
            z = jnp.dot(h, weights[(start + c0) // W_CHUNK], preferred_element_type=F32)
            for h0 in range(0, W_CHUNK, HEAD_W):
                hd = (c0 + h0) // HEAD_W
                t = epi(z[:, h0:h0 + HEAD_W]).astype(o_ref.dtype)
                if is_kv and not rope:
                    o_ref[pl.ds(hd, tm, stride=n_heads), :] = t
                else:
                    o_ref[:, hd * HEAD_W:(hd + 1) * HEAD_W] = t


def _inproj_body(n_ctx_tiles, xc_ref, xl_ref, mod_ref, gn_ref, w_in_hbm, qn_ref, kn_ref,
                 ca_ref, sau_ref, sad_ref, cb_ref, sb_ref,
                 qa_o, ga_o, qb_o, gb_o, cka_o, cva_o, ckb_o, cvb_o, lka_o, lva_o, lkb_o, lvb_o,
                 weights, w_stage, sem, unused_sem):
    _stage_weights(w_in_hbm, None, weights, w_stage, sem, None)
    is_ctx = pl.program_id(0) < n_ctx_tiles

    @pl.when(is_ctx)
    def _():
        _project_in(None, xc_ref, mod_ref, gn_ref, weights, qn_ref, kn_ref,
                    (qa_o, cka_o, cva_o, ga_o, qb_o, ckb_o, cvb_o, gb_o))

    @pl.when(jnp.logical_not(is_ctx))
    def _():
        _project_in((ca_ref, sau_ref, sad_ref, cb_ref, sb_ref), xl_ref, mod_ref, gn_ref, weights,
                    qn_ref, kn_ref, (qa_o, lka_o, lva_o, ga_o, qb_o, lkb_o, lvb_o, gb_o))


def _inproj(x_ctx, x_lat, mod3, lat_tiles_per_request, g_norm, w_in, q_norm_g, k_norm_g, rope_tabs):
    tm = INPROJ_TM
    tc, tl = x_ctx.shape[0], x_lat.shape[0]
    nc = tc // tm
    ctx_tile = lambda i: jnp.minimum(i, nc - 1)
    lat_tile = lambda i: jnp.maximum(i - nc, 0)
    const = lambda shape: pl.BlockSpec(shape, lambda i: (0,) * len(shape))
    n_pos_tiles = rope_tabs[0].shape[0] // tm
    in_specs = [
        pl.BlockSpec((tm, D_MODEL), lambda i: (ctx_tile(i), 0)),
        pl.BlockSpec((tm, D_MODEL), lambda i: (lat_tile(i), 0)),
        pl.BlockSpec((None, 1, 3 * D_MODEL), lambda i: (
            jnp.where(i < nc, 0, 1 + lat_tile(i) // lat_tiles_per_request), 0, 0)),
        const((1, D_MODEL)),
        pl.BlockSpec(memory_space=pl.ANY),
        const((1, HEAD_W)),
        const((1, HEAD_W)),
    ] + [pl.BlockSpec((tm, HEAD_W), lambda i: (lat_tile(i) % n_pos_tiles, 0))] * 5
    out_shape = [jax.ShapeDtypeStruct((tc + tl, w), BF16) for w in (A_Q, A_G, B_Q, B_G)]
    out_specs = [pl.BlockSpec((tm, w), lambda i: (i, 0)) for w in (A_Q, A_G, B_Q, B_G)]
    for w in (A_K, A_V, B_K, B_V):
        n_heads = w // HEAD_W
        out_shape.append(jax.ShapeDtypeStruct((tc * n_heads, HEAD_W), F32))
        out_specs.append(pl.BlockSpec((tm * n_heads, HEAD_W), lambda i: (ctx_tile(i), 0)))
    for w in (A_K, A_V, B_K, B_V):
        out_shape.append(jax.ShapeDtypeStruct((tl, w), BF16))
        out_specs.append(pl.BlockSpec((tm, w), lambda i: (lat_tile(i), 0)))
    return pl.pallas_call(
        functools.partial(_inproj_body, nc),
        out_shape=out_shape,
        grid=((tc + tl) // tm,),
        in_specs=in_specs,
        out_specs=out_specs,
        scratch_shapes=_weight_scratch(D_IN, STAGE_BYTES_IN, STAGE_SLOTS_IN),
        compiler_params=_params(1),
        name="inproj",
    )(x_ctx, x_lat, mod3, g_norm, w_in, q_norm_g, k_norm_g, *rope_tabs)


def _diff_lambda_col(lq1_ref, lk1_ref, lq2_ref, lk2_ref, lam_init):
    s1 = jnp.sum(lq1_ref[...] * lk1_ref[...], axis=-1, keepdims=True)
    s2 = jnp.sum(lq2_ref[...] * lk2_ref[...], axis=-1, keepdims=True)
    return jnp.exp(s1) - jnp.exp(s2) + lam_init


def _cache_head(ref, head, n_tok, n_heads):
    return ref[pl.ds(head, n_tok, stride=n_heads), :].astype(BF16)


def _scores(q, k):
    return lax.dot_general(q, k, (((1,), (1,)), ((), ())), preferred_element_type=F32)


def _softmax_pv(s, v):
    m = jnp.max(s, axis=-1, keepdims=True)
    e = jnp.exp2(s - m)
    v1 = jnp.concatenate([v, jnp.ones_like(v)], axis=1)
    ol = jnp.dot(e.astype(BF16), v1, preferred_element_type=F32)
    return ol[:, :HEAD_W], ol[:, HEAD_W:]


def _attend(lam_col, subln, qa_ref, ga_ref, qb_ref, gb_ref, ka, va, kb, vb, o_ref):
    tq = qa_ref.shape[0]
    lane = lax.broadcasted_iota(jnp.int32, (tq, HEAD_W), 1)
    first = lane < DK_A

    def diff_scores(hd):
        q = qa_ref[:, hd * HEAD_W:(hd + 1) * HEAD_W]
        zero = jnp.zeros_like(q)
        q2 = jnp.concatenate([jnp.where(first, q, zero), jnp.where(first, zero, q)], axis=0)
        return _scores(q2, ka(hd))

    def diff_finish(hd, s):
        cols = slice(hd * HEAD_W, (hd + 1) * HEAD_W)
        o2, l2 = _softmax_pv(s, va(hd))
        r2 = 1.0 / l2
        lam = lam_col[hd:hd + 1, :]
        o = o2[:tq] * r2[:tq] - o2[tq:] * (lam * r2[tq:])
        o_ref[:, cols] = (_rms(o) * subln * ga_ref[:, cols].astype(F32)).astype(o_ref.dtype)

    def gqa_scores(h0):
        q = jnp.concatenate(
            [qb_ref[:, (h0 + g) * HEAD_W:(h0 + g + 1) * HEAD_W] for g in range(GQA_STACK)], axis=0)
        return _scores(q, kb(h0 // G_B))

    def gqa_finish(h0, s):
        o, l = _softmax_pv(s, vb(h0 // G_B))
        o = o * (1.0 / l)
        for g in range(GQA_STACK):
            cols = slice((h0 + g) * HEAD_W, (h0 + g + 1) * HEAD_W)
            gate = gb_ref[:, cols].astype(F32)
            o_ref[:, A_V + (h0 + g) * HEAD_W:A_V + (h0 + g + 1) * HEAD_W] = (
                o[g * tq:(g + 1) * tq, :] * gate).astype(o_ref.dtype)

    units = [(diff_scores, diff_finish, hd) for hd in range(H_A)]
    units += [(gqa_scores, gqa_finish, h0) for h0 in range(0, H_B, GQA_STACK)]
    scores = []
    for u, (_, finish, arg) in enumerate(units):
        while len(scores) < min(len(units), u + 1 + SCORE_LOOKAHEAD):
            issue, _, issue_arg = units[len(scores)]
            scores.append(issue(issue_arg))
        finish(arg, scores[u])
        scores[u] = None


def _project_out(o, weights, x_ref, mod_ref, gf_ref, y_ref):
    m = jnp.concatenate(
        [jnp.dot(o, weights[c], preferred_element_type=F32) for c in range(D_MODEL // W_CHUNK)],
        axis=1)
    gate = mod_ref[:, 2 * D_MODEL:3 * D_MODEL]
    y_ref[...] = _rms(x_ref[...] + gate * m) * gf_ref[...]


def _attn_ctx_body(lam_init, lq1, lk1, lq2, lk2, subln_ref,
                   qa_ref, ka_ref, va_ref, ga_ref, qb_ref, kb_ref, vb_ref, gb_ref,
                   w_out_hbm, x_ref, mod_ref, gf_ref, y_ref, wbf_hbm,
                   w_scr, w_stage, sem, sem_out, o_scr):
    handoff = _stage_weights(w_out_hbm, wbf_hbm, w_scr, w_stage, sem, sem_out)
    lam_col = _diff_lambda_col(lq1, lk1, lq2, lk2, lam_init)
    seq = qa_ref.shape[0]
    head = lambda ref, n_heads: (lambda i: _cache_head(ref, i, seq, n_heads))
    _attend(lam_col, subln_ref[...] * (1.0 - lam_init), qa_ref, ga_ref, qb_ref, gb_ref,
            head(ka_ref, H_A), head(va_ref, H_A), head(kb_ref, KV_B), head(vb_ref, KV_B), o_scr)
    _project_out(o_scr[...], w_scr, x_ref, mod_ref, gf_ref, y_ref)
    pl.when(pl.program_id(0) == pl.num_programs(0) - 1)(handoff.wait)


def _attn_ctx(lam_init, lam_params, subln_g, qa, ka, va, ga, qb, kb, vb, gb, w_out, x2d, mod3,
              g_final, seq):
    t = x2d.shape[0]
    const = lambda shape: pl.BlockSpec(shape, lambda b: (0,) * len(shape))
    row_spec = lambda w: pl.BlockSpec((seq, w), lambda b: (b, 0))
    cache_spec = lambda n_heads: pl.BlockSpec((seq * n_heads, HEAD_W), lambda b: (b, 0))
    return pl.pallas_call(
        functools.partial(_attn_ctx_body, lam_init),
        out_shape=[jax.ShapeDtypeStruct((t, D_MODEL), F32),
                   jax.ShapeDtypeStruct((D_MODEL // W_CHUNK, D_MIX, W_CHUNK), BF16)],
        grid=(t // seq,),
        in_specs=[const((H_A, DK_A))] * 4 + [const((1, DV_A))] + [
            row_spec(A_Q), cache_spec(H_A), cache_spec(H_A), row_spec(A_G),
            row_spec(B_Q), cache_spec(KV_B), cache_spec(KV_B), row_spec(B_G),
            pl.BlockSpec(memory_space=pl.ANY),
            row_spec(D_MODEL),
            pl.BlockSpec((None, 1, 3 * D_MODEL), lambda b: (0, 0, 0)),
            const((1, D_MODEL))],
        out_specs=[row_spec(D_MODEL), pl.BlockSpec(memory_space=pl.ANY)],
        scratch_shapes=_weight_scratch(D_MODEL, STAGE_BYTES, STAGE_SLOTS) + [pltpu.VMEM((seq, D_MIX), BF16)],
        compiler_params=_params(1),
        name="attn_ctx",
    )(*lam_params, subln_g, qa, ka, va, ga, qb, kb, vb, gb, w_out, x2d, mod3, g_final)


def _attn_lat_body(lam_init, n_lat, lq1, lk1, lq2, lk2, subln_ref,
                   qa_ref, ka_ref, va_ref, cka_ref, cva_ref, ga_ref,
                   qb_ref, kb_ref, vb_ref, ckb_ref, cvb_ref, gb_ref, o_ref,
                   ka_all, va_all, kb_all, vb_all):
    @pl.when(pl.program_id(1) == 0)
    def _():
        for new_ref, cache_ref, all_ref in ((ka_ref, cka_ref, ka_all), (va_ref, cva_ref, va_all),
                                            (kb_ref, ckb_ref, kb_all), (vb_ref, cvb_ref, vb_all)):
            n_heads = new_ref.shape[1] // HEAD_W
            n_past = cache_ref.shape[0] // n_heads
            all_ref[0:n_lat, :] = new_ref[...]
            for hd in range(n_heads):
                all_ref[n_lat:, hd * HEAD_W:(hd + 1) * HEAD_W] = _cache_head(cache_ref, hd, n_past, n_heads)

    lam_col = _diff_lambda_col(lq1, lk1, lq2, lk2, lam_init)
    head = lambda ref: (lambda i: ref[:, i * HEAD_W:(i + 1) * HEAD_W])
    _attend(lam_col, subln_ref[...] * (1.0 - lam_init), qa_ref, ga_ref, qb_ref, gb_ref,
            head(ka_all), head(va_all), head(kb_all), head(vb_all), o_ref)


def _attn_lat(lam_init, lam_params, subln_g, qa, ka, va, cka, cva, ga, qb, kb, vb, ckb, cvb, gb,
              n_lat, n_past, q_row0):
    t = ka.shape[0]
    tq = ATTN_TQ
    nq = n_lat // tq
    q_tile0 = q_row0 // tq
    const = lambda shape: pl.BlockSpec(shape, lambda b, i: (0,) * len(shape))
    q_spec = lambda w: pl.BlockSpec((tq, w), lambda b, i: (q_tile0 + b * nq + i, 0))
    new_spec = lambda w: pl.BlockSpec((n_lat, w), lambda b, i: (b, 0))
    cache_spec = lambda w: pl.BlockSpec((n_past * (w // HEAD_W), HEAD_W), lambda b, i: (b, 0))
    n_all = n_lat + n_past
    return pl.pallas_call(
        functools.partial(_attn_lat_body, lam_init, n_lat),
        out_shape=jax.ShapeDtypeStruct((t, D_MIX), BF16),
        grid=(t // n_lat, nq),
        in_specs=[const((H_A, DK_A))] * 4 + [const((1, DV_A))] + [
            q_spec(A_Q), new_spec(A_K), new_spec(A_V), cache_spec(A_K), cache_spec(A_V), q_spec(A_G),
            q_spec(B_Q), new_spec(B_K), new_spec(B_V), cache_spec(B_K), cache_spec(B_V), q_spec(B_G)],
        out_specs=pl.BlockSpec((tq, D_MIX), lambda b, i: (b * nq + i, 0)),
        scratch_shapes=[pltpu.VMEM((n_all, A_K), BF16), pltpu.VMEM((n_all, A_V), BF16),
                        pltpu.VMEM((n_all, B_K), BF16), pltpu.VMEM((n_all, B_V), BF16)],
        compiler_params=_params(2),
        name="attn_lat",
    )(*lam_params, subln_g, qa, ka, va, cka, cva, ga, qb, kb, vb, ckb, cvb, gb)


def _outproj_body(o_ref, w_ref, x_ref, mod_ref, gf_ref, y_ref):
    _project_out(o_ref[...], w_ref, x_ref, mod_ref, gf_ref, y_ref)


def _outproj(o, w_out_bf16, x2d, mod3, row_of_tile, g_final):
    t = x2d.shape[0]
    tm = OUTPROJ_TM
    return pl.pallas_call(
        _outproj_body,
        out_shape=jax.ShapeDtypeStruct((t, D_MODEL), F32),
        grid=(t // tm,),
        in_specs=[
            pl.BlockSpec((tm, D_MIX), lambda i: (i, 0)),
            _weight_spec(D_MODEL),
            pl.BlockSpec((tm, D_MODEL), lambda i: (i, 0)),
            pl.BlockSpec((None, 1, 3 * D_MODEL), lambda i: (row_of_tile(i), 0, 0)),
            pl.BlockSpec((1, D_MODEL), lambda i: (0, 0)),
        ],
        out_specs=pl.BlockSpec((tm, D_MODEL), lambda i: (i, 0)),
        compiler_params=_params(1),
        name="outproj",
    )(o, w_out_bf16, x2d, mod3, g_final)


def _rope_tables(n_tok):
    n_rows = n_tok // GRID_W
    rows = np.repeat(np.arange(n_rows, dtype=np.float32), GRID_W)
    cols = np.tile(np.arange(GRID_W, dtype=np.float32), n_rows)

    def cos_sin(rot_dim):
        quarter = rot_dim // 4
        inv_freq = (1.0 / (np.float32(ROPE_THETA) ** (np.arange(quarter, dtype=np.float32) / quarter))
                    ).astype(np.float32)
        ang = np.concatenate([rows[:, None] * inv_freq, cols[:, None] * inv_freq], axis=-1)
        return np.cos(ang).astype(np.float32), np.sin(ang).astype(np.float32)

    ca, sa = cos_sin(DK_A)
    za = np.zeros_like(sa)
    cb, sb = cos_sin(DH_B)
    tabs = (
        np.concatenate([ca, ca, ca, ca], axis=-1),
        np.concatenate([-sa, za, -sa, za], axis=-1),
        np.concatenate([za, sa, za, sa], axis=-1),
        np.concatenate([cb, cb], axis=-1),
        np.concatenate([-sb, sb], axis=-1),
    )
    return tuple(jnp.asarray(t) for t in tabs)


def kernel(x_prompt, x_sample, cache_diff_k, cache_diff_v, cache_gqa_k, cache_gqa_v, c, c_ctx,
           w_ada, b_ada, g_norm, w_in, lam_q1, lam_k1, lam_q2, lam_k2, subln_g, q_norm_g, k_norm_g,
           w_out, g_final):
    bp, n_ctx, d = x_prompt.shape
    bs, n_lat, _ = x_sample.shape
    depth = w_in.shape[0]
    n_past = cache_diff_k.shape[2]
    assert depth == 1 and d == D_MODEL and bs + 1 <= N_COND
    assert n_lat % INPROJ_TM == 0 and n_lat % OUTPROJ_TM == 0 and n_lat % ATTN_TQ == 0
    l = 0
    lam_init = 0.8 - 0.6 * math.exp(-0.3 * l)

    mod3 = _adaln(c_ctx[None, :], c, w_ada[l], b_ada[l][None, :])

    gn = g_norm[l][None, :]
    qn = q_norm_g[l][None, :]
    kn = k_norm_g[l][None, :]
    lam_params = (lam_q1[l], lam_k1[l], lam_q2[l], lam_k2[l])
    subln = subln_g[l][None, :]
    gf = g_final[None, :]

    xp2 = x_prompt.reshape(bp * n_ctx, d)
    xs2 = x_sample.reshape(bs * n_lat, d)
    assert (bp * n_ctx) % INPROJ_TM == 0 and (bp * n_ctx) % ATTN_TQ == 0
    (qa, ga, qb, gb, ka_ctx, va_ctx, kb_ctx, vb_ctx, ka_lat, va_lat, kb_lat, vb_lat) = _inproj(
        xp2, xs2, mod3, n_lat // INPROJ_TM, gn, w_in[l], qn, kn, _rope_tables(n_lat))

    y_prompt, w_out_bf16 = _attn_ctx(lam_init, lam_params, subln, qa, ka_ctx, va_ctx, ga,
                                     qb, kb_ctx, vb_ctx, gb, w_out[l], xp2, mod3, gf, n_ctx)
    y_prompt = y_prompt.reshape(bp, n_ctx, d)
    new_diff_k = ka_ctx.reshape(bp, 1, n_ctx, H_A, 2 * DK_A)
    new_diff_v = va_ctx.reshape(bp, 1, n_ctx, H_A, DV_A)
    new_gqa_k = kb_ctx.reshape(bp, 1, n_ctx, KV_B, DH_B)
    new_gqa_v = vb_ctx.reshape(bp, 1, n_ctx, KV_B, DH_B)

    cka = cache_diff_k[:, l].reshape(bs * n_past * H_A, HEAD_W)
    cva = cache_diff_v[:, l].reshape(bs * n_past * H_A, HEAD_W)
    ckb = cache_gqa_k[:, l].reshape(bs * n_past * KV_B, HEAD_W)
    cvb = cache_gqa_v[:, l].reshape(bs * n_past * KV_B, HEAD_W)
    o = _attn_lat(lam_init, lam_params, subln, qa, ka_lat, va_lat, cka, cva, ga,
                  qb, kb_lat, vb_lat, ckb, cvb, gb, n_lat, n_past, bp * n_ctx)
    out_tiles = n_lat // OUTPROJ_TM
    y_sample = _outproj(o, w_out_bf16, xs2, mod3, lambda i: 1 + i // out_tiles, gf)
    y_sample = y_sample.reshape(bs, n_lat, d)

    return (y_prompt, y_sample, new_diff_k, new_diff_v, new_gqa_k, new_gqa_v)
```

```python
import functools
import math

import jax
import jax.numpy as jnp
import numpy as np
from jax import lax
from jax.experimental import pallas as pl
from jax.experimental.pallas import tpu as pltpu

D_MODEL = 2048
GRID_W = 64
ROPE_THETA = 10000.0
EPS = 1e-6
H_A = 8
DK_A = 64
DV_A = 2 * DK_A
H_B = 8
KV_B = 2
DH_B = 128
G_B = H_B // KV_B
HEAD_W = 128
A_Q = H_A * 2 * DK_A
A_K = H_A * 2 * DK_A
A_V = H_A * DV_A
A_G = H_A * DV_A
B_Q = H_B * DH_B
B_K = KV_B * DH_B
B_V = KV_B * DH_B
B_G = H_B * DH_B
D_IN = A_Q + A_K + A_V + A_G + B_Q + B_K + B_V + B_G
D_MIX = A_V + B_Q
OFF_AQ = 0
OFF_AK = OFF_AQ + A_Q
OFF_AV = OFF_AK + A_K
OFF_AG = OFF_AV + A_V
OFF_BQ = OFF_AG + A_G
OFF_BK = OFF_BQ + B_Q
OFF_BV = OFF_BK + B_K
OFF_BG = OFF_BV + B_V

N_COND = 8
ADA_TK = 256
INPROJ_TM = 256
W_CHUNK = 256
STAGE_BYTES = 2 * 1024 * 1024
STAGE_BYTES_IN = 512 * 1024
STAGE_SLOTS = 4
STAGE_SLOTS_IN = 12
OUTPROJ_TM = 512
ATTN_TQ = 256
GQA_STACK = 2
SCORE_LOOKAHEAD = 2
QSCALE_A = math.log2(math.e) / math.sqrt(DK_A)
QSCALE_B = math.log2(math.e) / math.sqrt(DH_B)
VMEM_LIMIT = 56 * 1024 * 1024

BF16 = jnp.bfloat16
F32 = jnp.float32


def _params(n_grid_axes):
    return pltpu.CompilerParams(
        dimension_semantics=("arbitrary",) * n_grid_axes,
        vmem_limit_bytes=VMEM_LIMIT,
    )


def _silu(x):
    return x * jax.nn.sigmoid(x)


def _rms(x):
    return x * lax.rsqrt(jnp.mean(x * x, axis=-1, keepdims=True) + EPS)


def _stage_rows(n_cols, stage_bytes):
    return 1 << int(math.log2(stage_bytes // (n_cols * 4)))


def _weight_scratch(n_cols, stage_bytes, n_slots):
    return [pltpu.VMEM((n_cols // W_CHUNK, D_MODEL, W_CHUNK), BF16),
            pltpu.VMEM((n_slots, _stage_rows(n_cols, stage_bytes), n_cols), F32),
            pltpu.SemaphoreType.DMA((n_slots,))]


def _slab_copy(w_hbm, w_stage, sem, k):
    slot = k % w_stage.shape[0]
    n_rows = w_stage.shape[1]
    return pltpu.make_async_copy(
        w_hbm.at[pl.ds(k * n_rows, n_rows), :], w_stage.at[slot], sem.at[slot])


def _stage_weights(w_hbm, wbf_hbm, w_scr, w_stage, sem, sem_out):
    n_chunks = w_scr.shape[0]
    n_slots, slab_rows = w_stage.shape[:2]
    n_slabs = w_scr.shape[1] // slab_rows
    copy = functools.partial(_slab_copy, w_hbm, w_stage, sem)
    handoff = None if wbf_hbm is None else pltpu.make_async_copy(w_scr, wbf_hbm, sem_out)

    @pl.when(pl.program_id(0) == 0)
    def _():
        for k in range(n_slots - 1):
            copy(k).start()
        for k in range(n_slabs):
            if k + n_slots - 1 < n_slabs:
                copy(k + n_slots - 1).start()
            copy(k).wait()
            rows = slice(k * slab_rows, (k + 1) * slab_rows)
            for c in range(n_chunks):
                w_scr[c, rows, :] = w_stage[
                    k % n_slots, :, c * W_CHUNK:(c + 1) * W_CHUNK].astype(BF16)
        if handoff is not None:
            handoff.start()

    return handoff


def _weight_spec(n_cols):
    return pl.BlockSpec((n_cols // W_CHUNK, D_MODEL, W_CHUNK), lambda i: (0, 0, 0),
                        pipeline_mode=pl.Buffered(1))


def _adaln_body(cctx_ref, c_ref, w_ref, b_ref, o_ref):
    @pl.when(pl.program_id(0) == 0)
    def _():
        o_ref[:, 0, :] = jnp.broadcast_to(b_ref[...], (N_COND, b_ref.shape[1]))

    pad = jnp.zeros((N_COND - 1 - c_ref.shape[0], c_ref.shape[1]), F32)
    cond = jnp.concatenate([cctx_ref[...], c_ref[...], pad], axis=0)
    a = _silu(cond).astype(BF16)
    w = w_ref[...].astype(BF16)
    o_ref[:, 0, :] += jnp.dot(a, w, preferred_element_type=F32)


def _adaln(c_ctx, c, w_ada, b_ada):
    d3 = w_ada.shape[1]
    return pl.pallas_call(
        _adaln_body,
        out_shape=jax.ShapeDtypeStruct((N_COND, 1, d3), F32),
        grid=(D_MODEL // ADA_TK,),
        in_specs=[
            pl.BlockSpec((1, ADA_TK), lambda k: (0, k)),
            pl.BlockSpec((c.shape[0], ADA_TK), lambda k: (0, k)),
            pl.BlockSpec((ADA_TK, d3), lambda k: (k, 0)),
            pl.BlockSpec((1, d3), lambda k: (0, 0)),
        ],
        out_specs=pl.BlockSpec((N_COND, 1, d3), lambda k: (0, 0, 0)),
        compiler_params=_params(1),
        name="adaln",
    )(c_ctx, c, w_ada, b_ada)


def _rope_a(x, c, s_up, s_dn):
    return x * c + pltpu.roll(x, 96, 1) * s_up + pltpu.roll(x, 32, 1) * s_dn


def _rope_b(x, c, s):
    return x * c + pltpu.roll(x, 64, 1) * s


def _project_in(rope_tabs, x_ref, mod_ref, gn_ref, weights, qn_ref, kn_ref, outs):
    rope = rope_tabs is not None
    qa_o, ka_o, va_o, ga_o, qb_o, kb_o, vb_o, gb_o = outs
    tm = x_ref.shape[0]

    x = x_ref[...]
    shift = mod_ref[:, 0:D_MODEL]
    scale = mod_ref[:, D_MODEL:2 * D_MODEL]
    h = ((_rms(x) * gn_ref[...]) * (1.0 + scale) + shift).astype(BF16)

    if rope:
        ca, sau, sad, cb, sb = (r[...] for r in rope_tabs)

    def rope_a(t):
        return _rope_a(t, ca, sau, sad) if rope else t

    def rope_b(t):
        return _rope_b(t, cb, sb) if rope else t

    ident = lambda t: t
    regions = (
        (OFF_AQ, A_Q, qa_o, lambda t: rope_a(t) * QSCALE_A, False),
        (OFF_AK, A_K, ka_o, rope_a, True),
        (OFF_AG, A_G, ga_o, _silu, False),
        (OFF_BQ, B_Q, qb_o, lambda t: rope_b(_rms(t) * qn_ref[...]) * QSCALE_B, False),
        (OFF_BK, B_K, kb_o, lambda t: rope_b(_rms(t) * kn_ref[...]), True),
        (OFF_BG, B_G, gb_o, _silu, False),
        (OFF_AV, A_V, va_o, ident, True),
        (OFF_BV, B_V, vb_o, ident, True),
    )
    for start, width, o_ref, epi, is_kv in regions:
        n_heads = width // HEAD_W
        for c0 in range(0, width, W_CHUNK):
            z = jnp.dot(h, weights[(start + c0) // W_CHUNK], preferred_element_type=F32)
            for h0 in range(0, W_CHUNK, HEAD_W):
                hd = (c0 + h0) // HEAD_W
                t = epi(z[:, h0:h0 + HEAD_W]).astype(o_ref.dtype)
                if is_kv and not rope:
                    o_ref[pl.ds(hd, tm, stride=n_heads), :] = t
                else:
                    o_ref[:, hd * HEAD_W:(hd + 1) * HEAD_W] = t


def _inproj_body(n_ctx_tiles, xc_ref, xl_ref, mod_ref, gn_ref, w_in_hbm, qn_ref, kn_ref,
                 ca_ref, sau_ref, sad_ref, cb_ref, sb_ref,
                 qa_o, ga_o, qb_o, gb_o, cka_o, cva_o, ckb_o, cvb_o, lka_o, lva_o, lkb_o, lvb_o,
                 weights, w_stage, sem):
    _stage_weights(w_in_hbm, None, weights, w_stage, sem, None)
    is_ctx = pl.program_id(0) < n_ctx_tiles

    @pl.when(is_ctx)
    def _():
        _project_in(None, xc_ref, mod_ref, gn_ref, weights, qn_ref, kn_ref,
                    (qa_o, cka_o, cva_o, ga_o, qb_o, ckb_o, cvb_o, gb_o))

    @pl.when(jnp.logical_not(is_ctx))
    def _():
        _project_in((ca_ref, sau_ref, sad_ref, cb_ref, sb_ref), xl_ref, mod_ref, gn_ref, weights,
                    qn_ref, kn_ref, (qa_o, lka_o, lva_o, ga_o, qb_o, lkb_o, lvb_o, gb_o))


def _inproj(x_ctx, x_lat, mod3, lat_tiles_per_request, g_norm, w_in, q_norm_g, k_norm_g, rope_tabs):
    tm = INPROJ_TM
    tc, tl = x_ctx.shape[0], x_lat.shape[0]
    nc = tc // tm
    ctx_tile = lambda i: jnp.minimum(i, nc - 1)
    lat_tile = lambda i: jnp.maximum(i - nc, 0)
    const = lambda shape: pl.BlockSpec(shape, lambda i: (0,) * len(shape))
    n_pos_tiles = rope_tabs[0].shape[0] // tm
    in_specs = [
        pl.BlockSpec((tm, D_MODEL), lambda i: (ctx_tile(i), 0)),
        pl.BlockSpec((tm, D_MODEL), lambda i: (lat_tile(i), 0)),
        pl.BlockSpec((None, 1, 3 * D_MODEL), lambda i: (
            jnp.where(i < nc, 0, 1 + lat_tile(i) // lat_tiles_per_request), 0, 0)),
        const((1, D_MODEL)),
        pl.BlockSpec(memory_space=pl.ANY),
        const((1, HEAD_W)),
        const((1, HEAD_W)),
    ] + [pl.BlockSpec((tm, HEAD_W), lambda i: (lat_tile(i) % n_pos_tiles, 0))] * 5
    out_shape = [jax.ShapeDtypeStruct((tc + tl, w), BF16) for w in (A_Q, A_G, B_Q, B_G)]
    out_specs = [pl.BlockSpec((tm, w), lambda i: (i, 0)) for w in (A_Q, A_G, B_Q, B_G)]
    for w in (A_K, A_V, B_K, B_V):
        n_heads = w // HEAD_W
        out_shape.append(jax.ShapeDtypeStruct((tc * n_heads, HEAD_W), F32))
        out_specs.append(pl.BlockSpec((tm * n_heads, HEAD_W), lambda i: (ctx_tile(i), 0)))
    for w in (A_K, A_V, B_K, B_V):
        out_shape.append(jax.ShapeDtypeStruct((tl, w), BF16))
        out_specs.append(pl.BlockSpec((tm, w), lambda i: (lat_tile(i), 0)))
    return pl.pallas_call(
        functools.partial(_inproj_body, nc),
        out_shape=out_shape,
        grid=((tc + tl) // tm,),
        in_specs=in_specs,
        out_specs=out_specs,
        scratch_shapes=_weight_scratch(D_IN, STAGE_BYTES_IN, STAGE_SLOTS_IN),
        compiler_params=_params(1),
        name="inproj",
    )(x_ctx, x_lat, mod3, g_norm, w_in, q_norm_g, k_norm_g, *rope_tabs)


def _diff_lambda_col(lq1_ref, lk1_ref, lq2_ref, lk2_ref, lam_init):
    s1 = jnp.sum(lq1_ref[...] * lk1_ref[...], axis=-1, keepdims=True)
    s2 = jnp.sum(lq2_ref[...] * lk2_ref[...], axis=-1, keepdims=True)
    return jnp.exp(s1) - jnp.exp(s2) + lam_init


def _cache_head(ref, head, n_tok, n_heads):
    return ref[pl.ds(head, n_tok, stride=n_heads), :].astype(BF16)


def _scores(q, k):
    return lax.dot_general(q, k, (((1,), (1,)), ((), ())), preferred_element_type=F32)


def _softmax_pv(s, v):
    m = jnp.max(s, axis=-1, keepdims=True)
    e = jnp.exp2(s - m)
    v1 = jnp.concatenate([v, jnp.ones_like(v)], axis=1)
    ol = jnp.dot(e.astype(BF16), v1, preferred_element_type=F32)
    return ol[:, :HEAD_W], ol[:, HEAD_W:]


def _attend(lam_col, subln, qa_ref, ga_ref, qb_ref, gb_ref, ka, va, kb, vb, o_ref):
    tq = qa_ref.shape[0]
    lane = lax.broadcasted_iota(jnp.int32, (tq, HEAD_W), 1)
    first = lane < DK_A

    def diff_scores(hd):
        q = qa_ref[:, hd * HEAD_W:(hd + 1) * HEAD_W]
        zero = jnp.zeros_like(q)
        q2 = jnp.concatenate([jnp.where(first, q, zero), jnp.where(first, zero, q)], axis=0)
        return _scores(q2, ka(hd))

    def diff_finish(hd, s):
        cols = slice(hd * HEAD_W, (hd + 1) * HEAD_W)
        o2, l2 = _softmax_pv(s, va(hd))
        r2 = 1.0 / l2
        lam = lam_col[hd:hd + 1, :]
        o = o2[:tq] * r2[:tq] - o2[tq:] * (lam * r2[tq:])
        o_ref[:, cols] = (_rms(o) * subln * ga_ref[:, cols].astype(F32)).astype(o_ref.dtype)

    def gqa_scores(h0):
        q = jnp.concatenate(
            [qb_ref[:, (h0 + g) * HEAD_W:(h0 + g + 1) * HEAD_W] for g in range(GQA_STACK)], axis=0)
        return _scores(q, kb(h0 // G_B))

    def gqa_finish(h0, s):
        o, l = _softmax_pv(s, vb(h0 // G_B))
        o = o * (1.0 / l)
        for g in range(GQA_STACK):
            cols = slice((h0 + g) * HEAD_W, (h0 + g + 1) * HEAD_W)
            gate = gb_ref[:, cols].astype(F32)
            o_ref[:, A_V + (h0 + g) * HEAD_W:A_V + (h0 + g + 1) * HEAD_W] = (
                o[g * tq:(g + 1) * tq, :] * gate).astype(o_ref.dtype)

    units = [(diff_scores, diff_finish, hd) for hd in range(H_A)]
    units += [(gqa_scores, gqa_finish, h0) for h0 in range(0, H_B, GQA_STACK)]
    scores = []
    for u, (_, finish, arg) in enumerate(units):
        while len(scores) < min(len(units), u + 1 + SCORE_LOOKAHEAD):
            issue, _, issue_arg = units[len(scores)]
            scores.append(issue(issue_arg))
        finish(arg, scores[u])
        scores[u] = None


def _project_out(o, weights, x_ref, mod_ref, gf_ref, y_ref):
    m = jnp.concatenate(
        [jnp.dot(o, weights[c], preferred_element_type=F32) for c in range(D_MODEL // W_CHUNK)],
        axis=1)
    gate = mod_ref[:, 2 * D_MODEL:3 * D_MODEL]
    y_ref[...] = _rms(x_ref[...] + gate * m) * gf_ref[...]


def _attn_ctx_body(lam_init, lq1, lk1, lq2, lk2, subln_ref,
                   qa_ref, ka_ref, va_ref, ga_ref, qb_ref, kb_ref, vb_ref, gb_ref,
                   w_out_hbm, x_ref, mod_ref, gf_ref, y_ref, wbf_hbm,
                   w_scr, w_stage, sem, o_scr, sem_out):
    handoff = _stage_weights(w_out_hbm, wbf_hbm, w_scr, w_stage, sem, sem_out)
    lam_col = _diff_lambda_col(lq1, lk1, lq2, lk2, lam_init)
    seq = qa_ref.shape[0]
    head = lambda ref, n_heads: (lambda i: _cache_head(ref, i, seq, n_heads))
    _attend(lam_col, subln_ref[...] * (1.0 - lam_init), qa_ref, ga_ref, qb_ref, gb_ref,
            head(ka_ref, H_A), head(va_ref, H_A), head(kb_ref, KV_B), head(vb_ref, KV_B), o_scr)
    _project_out(o_scr[...], w_scr, x_ref, mod_ref, gf_ref, y_ref)
    pl.when(pl.program_id(0) == pl.num_programs(0) - 1)(handoff.wait)


def _attn_ctx(lam_init, lam_params, subln_g, qa, ka, va, ga, qb, kb, vb, gb, w_out, x2d, mod3,
              g_final, seq):
    t = x2d.shape[0]
    const = lambda shape: pl.BlockSpec(shape, lambda b: (0,) * len(shape))
    row_spec = lambda w: pl.BlockSpec((seq, w), lambda b: (b, 0))
    cache_spec = lambda n_heads: pl.BlockSpec((seq * n_heads, HEAD_W), lambda b: (b, 0))
    return pl.pallas_call(
        functools.partial(_attn_ctx_body, lam_init),
        out_shape=[jax.ShapeDtypeStruct((t, D_MODEL), F32),
                   jax.ShapeDtypeStruct((D_MODEL // W_CHUNK, D_MIX, W_CHUNK), BF16)],
        grid=(t // seq,),
        in_specs=[const((H_A, DK_A))] * 4 + [const((1, DV_A))] + [
            row_spec(A_Q), cache_spec(H_A), cache_spec(H_A), row_spec(A_G),
            row_spec(B_Q), cache_spec(KV_B), cache_spec(KV_B), row_spec(B_G),
            pl.BlockSpec(memory_space=pl.ANY),
            row_spec(D_MODEL),
            pl.BlockSpec((None, 1, 3 * D_MODEL), lambda b: (0, 0, 0)),
            const((1, D_MODEL))],
        out_specs=[row_spec(D_MODEL), pl.BlockSpec(memory_space=pl.ANY)],
        scratch_shapes=_weight_scratch(D_MODEL, STAGE_BYTES, STAGE_SLOTS) + [
            pltpu.VMEM((seq, D_MIX), BF16), pltpu.SemaphoreType.DMA(())],
        compiler_params=_params(1),
        name="attn_ctx",
    )(*lam_params, subln_g, qa, ka, va, ga, qb, kb, vb, gb, w_out, x2d, mod3, g_final)


def _attn_lat_body(lam_init, n_lat, lq1, lk1, lq2, lk2, subln_ref,
                   qa_ref, ka_ref, va_ref, cka_ref, cva_ref, ga_ref,
                   qb_ref, kb_ref, vb_ref, ckb_ref, cvb_ref, gb_ref, o_ref,
                   ka_all, va_all, kb_all, vb_all):
    @pl.when(pl.program_id(1) == 0)
    def _():
        for new_ref, cache_ref, all_ref in ((ka_ref, cka_ref, ka_all), (va_ref, cva_ref, va_all),
                                            (kb_ref, ckb_ref, kb_all), (vb_ref, cvb_ref, vb_all)):
            n_heads = new_ref.shape[1] // HEAD_W
            n_past = cache_ref.shape[0] // n_heads
            all_ref[0:n_lat, :] = new_ref[...]
            for hd in range(n_heads):
                all_ref[n_lat:, hd * HEAD_W:(hd + 1) * HEAD_W] = _cache_head(cache_ref, hd, n_past, n_heads)

    lam_col = _diff_lambda_col(lq1, lk1, lq2, lk2, lam_init)
    head = lambda ref: (lambda i: ref[:, i * HEAD_W:(i + 1) * HEAD_W])
    _attend(lam_col, subln_ref[...] * (1.0 - lam_init), qa_ref, ga_ref, qb_ref, gb_ref,
            head(ka_all), head(va_all), head(kb_all), head(vb_all), o_ref)


def _attn_lat(lam_init, lam_params, subln_g, qa, ka, va, cka, cva, ga, qb, kb, vb, ckb, cvb, gb,
              n_lat, n_past, q_row0):
    t = ka.shape[0]
    tq = ATTN_TQ
    nq = n_lat // tq
    q_tile0 = q_row0 // tq
    const = lambda shape: pl.BlockSpec(shape, lambda b, i: (0,) * len(shape))
    q_spec = lambda w: pl.BlockSpec((tq, w), lambda b, i: (q_tile0 + b * nq + i, 0))
    new_spec = lambda w: pl.BlockSpec((n_lat, w), lambda b, i: (b, 0))
    cache_spec = lambda w: pl.BlockSpec((n_past * (w // HEAD_W), HEAD_W), lambda b, i: (b, 0))
    n_all = n_lat + n_past
    return pl.pallas_call(
        functools.partial(_attn_lat_body, lam_init, n_lat),
        out_shape=jax.ShapeDtypeStruct((t, D_MIX), BF16),
        grid=(t // n_lat, nq),
        in_specs=[const((H_A, DK_A))] * 4 + [const((1, DV_A))] + [
            q_spec(A_Q), new_spec(A_K), new_spec(A_V), cache_spec(A_K), cache_spec(A_V), q_spec(A_G),
            q_spec(B_Q), new_spec(B_K), new_spec(B_V), cache_spec(B_K), cache_spec(B_V), q_spec(B_G)],
        out_specs=pl.BlockSpec((tq, D_MIX), lambda b, i: (b * nq + i, 0)),
        scratch_shapes=[pltpu.VMEM((n_all, A_K), BF16), pltpu.VMEM((n_all, A_V), BF16),
                        pltpu.VMEM((n_all, B_K), BF16), pltpu.VMEM((n_all, B_V), BF16)],
        compiler_params=_params(2),
        name="attn_lat",
    )(*lam_params, subln_g, qa, ka, va, cka, cva, ga, qb, kb, vb, ckb, cvb, gb)


def _outproj_body(o_ref, w_ref, x_ref, mod_ref, gf_ref, y_ref):
    _project_out(o_ref[...], w_ref, x_ref, mod_ref, gf_ref, y_ref)


def _outproj(o, w_out_bf16, x2d, mod3, row_of_tile, g_final):
    t = x2d.shape[0]
    tm = OUTPROJ_TM
    return pl.pallas_call(
        _outproj_body,
        out_shape=jax.ShapeDtypeStruct((t, D_MODEL), F32),
        grid=(t // tm,),
        in_specs=[
            pl.BlockSpec((tm, D_MIX), lambda i: (i, 0)),
            _weight_spec(D_MODEL),
            pl.BlockSpec((tm, D_MODEL), lambda i: (i, 0)),
            pl.BlockSpec((None, 1, 3 * D_MODEL), lambda i: (row_of_tile(i), 0, 0)),
            pl.BlockSpec((1, D_MODEL), lambda i: (0, 0)),
        ],
        out_specs=pl.BlockSpec((tm, D_MODEL), lambda i: (i, 0)),
        compiler_params=_params(1),
        name="outproj",
    )(o, w_out_bf16, x2d, mod3, g_final)


def _rope_tables(n_tok):
    n_rows = n_tok // GRID_W
    rows = np.repeat(np.arange(n_rows, dtype=np.float32), GRID_W)
    cols = np.tile(np.arange(GRID_W, dtype=np.float32), n_rows)

    def cos_sin(rot_dim):
        quarter = rot_dim // 4
        inv_freq = (1.0 / (np.float32(ROPE_THETA) ** (np.arange(quarter, dtype=np.float32) / quarter))
                    ).astype(np.float32)
        ang = np.concatenate([rows[:, None] * inv_freq, cols[:, None] * inv_freq], axis=-1)
        return np.cos(ang).astype(np.float32), np.sin(ang).astype(np.float32)

    ca, sa = cos_sin(DK_A)
    za = np.zeros_like(sa)
    cb, sb = cos_sin(DH_B)
    tabs = (
        np.concatenate([ca, ca, ca, ca], axis=-1),
        np.concatenate([-sa, za, -sa, za], axis=-1),
        np.concatenate([za, sa, za, sa], axis=-1),
        np.concatenate([cb, cb], axis=-1),
        np.concatenate([-sb, sb], axis=-1),
    )
    return tuple(jnp.asarray(t) for t in tabs)


def kernel(x_prompt, x_sample, cache_diff_k, cache_diff_v, cache_gqa_k, cache_gqa_v, c, c_ctx,
           w_ada, b_ada, g_norm, w_in, lam_q1, lam_k1, lam_q2, lam_k2, subln_g, q_norm_g, k_norm_g,
           w_out, g_final):
    bp, n_ctx, d = x_prompt.shape
    bs, n_lat, _ = x_sample.shape
    depth = w_in.shape[0]
    n_past = cache_diff_k.shape[2]
    assert depth == 1 and d == D_MODEL and bs + 1 <= N_COND
    assert n_lat % INPROJ_TM == 0 and n_lat % OUTPROJ_TM == 0 and n_lat % ATTN_TQ == 0
    l = 0
    lam_init = 0.8 - 0.6 * math.exp(-0.3 * l)

    mod3 = _adaln(c_ctx[None, :], c, w_ada[l], b_ada[l][None, :])

    gn = g_norm[l][None, :]
    qn = q_norm_g[l][None, :]
    kn = k_norm_g[l][None, :]
    lam_params = (lam_q1[l], lam_k1[l], lam_q2[l], lam_k2[l])
    subln = subln_g[l][None, :]
    gf = g_final[None, :]

    xp2 = x_prompt.reshape(bp * n_ctx, d)
    xs2 = x_sample.reshape(bs * n_lat, d)
    assert (bp * n_ctx) % INPROJ_TM == 0 and (bp * n_ctx) % ATTN_TQ == 0
    (qa, ga, qb, gb, ka_ctx, va_ctx, kb_ctx, vb_ctx, ka_lat, va_lat, kb_lat, vb_lat) = _inproj(
        xp2, xs2, mod3, n_lat // INPROJ_TM, gn, w_in[l], qn, kn, _rope_tables(n_lat))

    y_prompt, w_out_bf16 = _attn_ctx(lam_init, lam_params, subln, qa, ka_ctx, va_ctx, ga,
                                     qb, kb_ctx, vb_ctx, gb, w_out[l], xp2, mod3, gf, n_ctx)
    y_prompt = y_prompt.reshape(bp, n_ctx, d)
    new_diff_k = ka_ctx.reshape(bp, 1, n_ctx, H_A, 2 * DK_A)
    new_diff_v = va_ctx.reshape(bp, 1, n_ctx, H_A, DV_A)
    new_gqa_k = kb_ctx.reshape(bp, 1, n_ctx, KV_B, DH_B)
    new_gqa_v = vb_ctx.reshape(bp, 1, n_ctx, KV_B, DH_B)

    cka = cache_diff_k[:, l].reshape(bs * n_past * H_A, HEAD_W)
    cva = cache_diff_v[:, l].reshape(bs * n_past * H_A, HEAD_W)
    ckb = cache_gqa_k[:, l].reshape(bs * n_past * KV_B, HEAD_W)
    cvb = cache_gqa_v[:, l].reshape(bs * n_past * KV_B, HEAD_W)
    o = _attn_lat(lam_init, lam_params, subln, qa, ka_lat, va_lat, cka, cva, ga,
                  qb, kb_lat, vb_lat, ckb, cvb, gb, n_lat, n_past, bp * n_ctx)
    out_tiles = n_lat // OUTPROJ_TM
    y_sample = _outproj(o, w_out_bf16, xs2, mod3, lambda i: 1 + i // out_tiles, gf)
    y_sample = y_sample.reshape(bs, n_lat, d)

    return (y_prompt, y_sample, new_diff_k, new_diff_v, new_gqa_k, new_gqa_v)
```
